```python
import math
import jax, jax.numpy as jnp
from jax import lax
import numpy as np

D_MODEL = 2048
BATCH = 2
SEQ = 4096
DEPTH = 1

HEAD_DIM = 64
ATTN_WIDTH = D_MODEL // 2
N_Q_HEADS = ATTN_WIDTH // HEAD_DIM
N_KV_HEADS = N_Q_HEADS // 4
KV_WIDTH = N_KV_HEADS * HEAD_DIM
CONV_WIDTH = D_MODEL - ATTN_WIDTH
CONV_K = 3
WINDOW = 128
Q_BLOCK = 128
IN_WIDTH = ATTN_WIDTH + 2 * KV_WIDTH + 3 * CONV_WIDTH
N_EXPERTS = 64
TOP_K = 8
N_GROUPS = 8
TOPK_GROUPS = 4
D_EXPERT = 512
ROUTED_SCALE = 2.5
EXPERT_BLOCK = 128
ALPHA = (2.0 * DEPTH) ** 0.25
BETA = (8.0 * DEPTH) ** -0.25
LN_EPS = 1e-5

kernel_name = "hymba_conv_swa_moe_deepnorm_adaln"


def layer_norm(x, g, b):
    xf = x.astype(jnp.float32)
    mu = jnp.mean(xf, axis=-1, keepdims=True)
    var = jnp.mean(jnp.square(xf - mu), axis=-1, keepdims=True)
    y = (xf - mu) * lax.rsqrt(var + LN_EPS) * g.astype(jnp.float32) + b.astype(jnp.float32)
    return y.astype(x.dtype)


def alibi_slopes(n_heads):
    return 2.0 ** (-8.0 * jnp.arange(1, n_heads + 1, dtype=jnp.float32) / n_heads)


def swa_sink_attention(q, k, v, sinks):
    b, s = q.shape[0], q.shape[1]
    nb = s // Q_BLOCK
    g = N_Q_HEADS // N_KV_HEADS
    qb = q.reshape(b, nb, Q_BLOCK, N_KV_HEADS, g, HEAD_DIM)

    def band(t):
        tb = t.reshape(b, nb, Q_BLOCK, N_KV_HEADS, HEAD_DIM)
        prev = jnp.pad(tb, ((0, 0), (1, 0), (0, 0), (0, 0), (0, 0)))[:, :-1]
        return jnp.concatenate([prev, tb], axis=2)

    kk, vv = band(k), band(v)
    scores = jnp.einsum('bnqhgd,bnkhd->bnhgqk', qb, kk,
                        preferred_element_type=jnp.float32) * (HEAD_DIM ** -0.5)
    qi = jnp.arange(Q_BLOCK)[:, None]
    kj = jnp.arange(2 * Q_BLOCK)[None, :]
    dist = qi + Q_BLOCK - kj
    valid = (dist >= 0) & (dist < WINDOW)
    valid = valid[None] & ((jnp.arange(nb)[:, None, None] > 0) | (kj[None] >= Q_BLOCK))
    slopes = alibi_slopes(N_Q_HEADS).reshape(N_KV_HEADS, g)
    scores = scores - slopes[:, :, None, None] * dist.astype(jnp.float32)
    scores = jnp.where(valid[None, :, None, None], scores, -jnp.inf)
    sink = sinks.astype(jnp.float32).reshape(N_KV_HEADS, g)[None, None, :, :, None, None]
    m = jnp.maximum(jnp.max(scores, axis=-1, keepdims=True), sink)
    p = jnp.exp(scores - m)
    denom = jnp.sum(p, axis=-1, keepdims=True) + jnp.exp(sink - m)
    p = (p / denom).astype(v.dtype)
    out = jnp.einsum('bnhgqk,bnkhd->bnqhgd', p, vv)
    return out.reshape(b, s, ATTN_WIDTH)


def short_gated_conv(gate_b, gate_c, hc, conv_w):
    u = gate_c * hc
    up = jnp.pad(u, ((0, 0), (CONV_K - 1, 0), (0, 0)))
    s = u.shape[1]
    y = sum(conv_w[j] * up[:, j:j + s] for j in range(CONV_K))
    return gate_b * y


def swiglu(h, wg, wu, wd):
    return (jax.nn.silu(h @ wg) * (h @ wu)) @ wd


def routed_moe(h, w_router, router_bias, w_gate, w_up, w_down):
    n, d = h.shape
    scores = jax.nn.sigmoid((h @ w_router).astype(jnp.float32))
    sel = scores + router_bias.astype(jnp.float32)
    grp = sel.reshape(n, N_GROUPS, N_EXPERTS // N_GROUPS)
    gscore = jnp.sum(lax.top_k(grp, 2)[0], axis=-1)
    _, gidx = lax.top_k(gscore, TOPK_GROUPS)
    gmask = jnp.sum(jax.nn.one_hot(gidx, N_GROUPS, dtype=jnp.int32), axis=1) > 0
    emask = jnp.repeat(gmask, N_EXPERTS // N_GROUPS, axis=1)
    _, eidx = lax.top_k(jnp.where(emask, sel, -jnp.inf), TOP_K)
    w = jnp.take_along_axis(scores, eidx, axis=1)
    w = (w / jnp.sum(w, axis=-1, keepdims=True) * ROUTED_SCALE).astype(h.dtype)

    a = n * TOP_K
    flat_e = eidx.reshape(-1).astype(jnp.int32)
    flat_w = w.reshape(-1)
    order = jnp.argsort(flat_e)
    sorted_e = flat_e[order]
    counts = jnp.bincount(flat_e, length=N_EXPERTS)
    offsets = jnp.cumsum(counts) - counts
    rank = jnp.arange(a, dtype=jnp.int32) - offsets[sorted_e]
    padded = (counts + EXPERT_BLOCK - 1) // EXPERT_BLOCK * EXPERT_BLOCK
    pad_end = jnp.cumsum(padded)
    dest = (pad_end - padded)[sorted_e] + rank
    n_rows = (a + EXPERT_BLOCK - 1) // EXPERT_BLOCK * EXPERT_BLOCK + N_EXPERTS * EXPERT_BLOCK
    n_blk = n_rows // EXPERT_BLOCK
    row_tok = jnp.zeros((n_rows,), jnp.int32).at[dest].set((order // TOP_K).astype(jnp.int32))
    row_w = jnp.zeros((n_rows,), h.dtype).at[dest].set(flat_w[order])
    block_e = jnp.minimum(jnp.searchsorted(pad_end, jnp.arange(n_blk) * EXPERT_BLOCK, side='right'),
                          N_EXPERTS - 1).astype(jnp.int32)

    def expert_block(args):
        tok, wr, e = args
        xb = h[tok]
        return swiglu(xb, w_gate[e], w_up[e], w_down[e]) * wr[:, None]

    ys = lax.map(expert_block, (row_tok.reshape(n_blk, EXPERT_BLOCK),
                                row_w.reshape(n_blk, EXPERT_BLOCK), block_e))
    return jnp.zeros_like(h).at[row_tok].add(ys.reshape(n_rows, d))


def setup_inputs(seed: int = 0) -> dict:
    key = jax.random.key(seed)
    ks = jax.random.split(key, 20)
    f32 = jnp.float32
    D, L = D_MODEL, DEPTH
    nrm = lambda k, shp, s: jax.random.normal(k, shp, f32) * s
    col_scale = jnp.concatenate([
        jnp.ones((ATTN_WIDTH + KV_WIDTH,), f32), jnp.full((KV_WIDTH,), BETA, f32),
        jnp.ones((2 * CONV_WIDTH,), f32), jnp.full((CONV_WIDTH,), BETA, f32)])
    return {
        "x": jax.random.normal(ks[0], (BATCH, SEQ, D), f32),
        "c": jax.random.normal(ks[1], (BATCH, D), f32),
        "w_mod": nrm(ks[2], (L, D, 6 * D), 0.5 * D ** -0.5),
        "b_mod": nrm(ks[3], (L, 6 * D), 0.02),
        "w_in": nrm(ks[4], (L, D, IN_WIDTH), D ** -0.5) * col_scale,
        "conv_w": nrm(ks[5], (L, CONV_K, CONV_WIDTH), CONV_K ** -0.5),
        "attn_sinks": nrm(ks[6], (L, N_Q_HEADS), 0.5),
        "w_out": nrm(ks[7], (L, D, D), BETA * D ** -0.5),
        "ln1_g": 1.0 + nrm(ks[8], (L, D), 0.02),
        "ln1_b": nrm(ks[9], (L, D), 0.02),
        "w_router": nrm(ks[10], (L, D, N_EXPERTS), D ** -0.5),
        "router_bias": nrm(ks[11], (L, N_EXPERTS), 0.01),
        "w_gate": nrm(ks[12], (L, N_EXPERTS, D, D_EXPERT), BETA * D ** -0.5),
        "w_up": nrm(ks[13], (L, N_EXPERTS, D, D_EXPERT), BETA * D ** -0.5),
        "w_down": nrm(ks[14], (L, N_EXPERTS, D_EXPERT, D), BETA * D_EXPERT ** -0.5),
        "ws_gate": nrm(ks[15], (L, D, D_EXPERT), BETA * D ** -0.5),
        "ws_up": nrm(ks[16], (L, D, D_EXPERT), BETA * D ** -0.5),
        "ws_down": nrm(ks[17], (L, D_EXPERT, D), BETA * D_EXPERT ** -0.5),
        "ln2_g": 1.0 + nrm(ks[18], (L, D), 0.02),
        "ln2_b": nrm(ks[19], (L, D), 0.02),
    }


def reference(x, c, w_mod, b_mod, w_in, conv_w, attn_sinks, w_out, ln1_g, ln1_b,
              w_router, router_bias, w_gate, w_up, w_down, ws_gate, ws_up, ws_down,
              ln2_g, ln2_b):
    b, s, d = x.shape
    cs = jax.nn.silu(c)
    splits = [ATTN_WIDTH, ATTN_WIDTH + KV_WIDTH, ATTN_WIDTH + 2 * KV_WIDTH,
              ATTN_WIDTH + 2 * KV_WIDTH + CONV_WIDTH, ATTN_WIDTH + 2 * KV_WIDTH + 2 * CONV_WIDTH]
    for l in range(DEPTH):
        mod = (cs @ w_mod[l] + b_mod[l])[:, None, :]
        sh1, sc1, g1, sh2, sc2, g2 = jnp.split(mod, 6, axis=-1)
        h = x * (1.0 + sc1) + sh1
        proj = h @ w_in[l]
        q, k, v, cb, cc, ch = jnp.split(proj, splits, axis=-1)
        attn = swa_sink_attention(q.reshape(b, s, N_Q_HEADS, HEAD_DIM),
                                  k.reshape(b, s, N_KV_HEADS, HEAD_DIM),
                                  v.reshape(b, s, N_KV_HEADS, HEAD_DIM), attn_sinks[l])
        conv = short_gated_conv(cb, cc, ch, conv_w[l])
        mix = jnp.concatenate([attn, conv], axis=-1) @ w_out[l]
        x = layer_norm(ALPHA * x + (1.0 + g1) * mix, ln1_g[l], ln1_b[l])
        h = (x * (1.0 + sc2) + sh2).reshape(b * s, d)
        ffn = routed_moe(h, w_router[l], router_bias[l], w_gate[l], w_up[l], w_down[l]) \
            + swiglu(h, ws_gate[l], ws_up[l], ws_down[l])
        x = layer_norm(ALPHA * x + (1.0 + g2) * ffn.reshape(b, s, d), ln2_g[l], ln2_b[l])
    return x
```

```python
import functools

import jax
import jax.numpy as jnp
from jax import lax
from jax.experimental import pallas as pl
from jax.experimental.pallas import tpu as pltpu

HEAD_DIM = 64
N_Q_HEADS = 16
N_KV_HEADS = 4
GQA = N_Q_HEADS // N_KV_HEADS
CONV_K = 3
WINDOW = 128
Q_BLOCK = 128
N_EXPERTS = 64
TOP_K = 8
N_GROUPS = 8
GROUP_SIZE = N_EXPERTS // N_GROUPS
TOPK_GROUPS = 4
ROUTED_SCALE = 2.5
DEPTH = 1
ALPHA = (2.0 * DEPTH) ** 0.25
LN_EPS = 1e-5

LANES = 128
SUBLANES = 8
TOKEN_TILE = (SUBLANES, 2 * LANES)
VMEM_LIMIT = 56 * 1024 * 1024

MOD_TN = 1024
INPROJ_TM = 512
INPROJ_TN = 1536
OUTPROJ_TM = 256
ROUTE_TM = 512
MOE_TM = 256
FINAL_TM = 128
IDX_REC = 1024

F32 = jnp.float32
BF16 = jnp.bfloat16


def _cparams(sem):
    return pltpu.CompilerParams(dimension_semantics=sem, vmem_limit_bytes=VMEM_LIMIT)


def _silu(v):
    return v * jax.nn.sigmoid(v)


def _layer_norm(y, g, b):
    mu = jnp.mean(y, axis=-1, keepdims=True)
    yc = y - mu
    var = jnp.mean(yc * yc, axis=-1, keepdims=True)
    return yc * lax.rsqrt(var + LN_EPS) * g + b


def _mod_kernel(c_ref, w_ref, b_ref, o_ref):
    cs = _silu(c_ref[...]).astype(BF16)
    o_ref[...] = jnp.dot(cs, w_ref[...].astype(BF16), preferred_element_type=F32) + b_ref[...]


def _mod(c8, w_mod, b_mod):
    d, n = w_mod.shape
    return pl.pallas_call(
        _mod_kernel,
        grid=(n // MOD_TN,),
        in_specs=[pl.BlockSpec((SUBLANES, d), lambda j: (0, 0)),
                  pl.BlockSpec((d, MOD_TN), lambda j: (0, j)),
                  pl.BlockSpec((1, MOD_TN), lambda j: (0, j))],
        out_specs=pl.BlockSpec((SUBLANES, MOD_TN), lambda j: (0, j)),
        out_shape=jax.ShapeDtypeStruct((SUBLANES, n), F32),
        compiler_params=_cparams(("arbitrary",)),
        name="mod",
    )(c8, w_mod, b_mod)


def _inproj_kernel(x_ref, mod_ref, w_ref, o_ref, h_ref):
    j = pl.program_id(1)

    @pl.when(j == 0)
    def _():
        m = mod_ref[0]
        h_ref[...] = (x_ref[...] * (1.0 + m[1:2]) + m[0:1]).astype(BF16)

    o_ref[...] = jnp.dot(h_ref[...], w_ref[0], preferred_element_type=F32).astype(BF16)


def _inproj(x2, mod3, w_in3, seq):
    n, d = x2.shape
    nj, _, tn = w_in3.shape
    tm = INPROJ_TM
    return pl.pallas_call(
        _inproj_kernel,
        grid=(n // tm, nj),
        in_specs=[pl.BlockSpec((tm, d), lambda i, j: (i, 0)),
                  pl.BlockSpec((1, 6, d), lambda i, j: (i * tm // seq, 0, 0)),
                  pl.BlockSpec((1, d, tn), lambda i, j: (j, 0, 0))],
        out_specs=pl.BlockSpec((tm, tn), lambda i, j: (i, j)),
        out_shape=jax.ShapeDtypeStruct((n, nj * tn), BF16),
        scratch_shapes=[pltpu.VMEM((tm, d), BF16)],
        compiler_params=_cparams(("arbitrary", "arbitrary")),
        name="inproj",
    )(x2, mod3, w_in3)


def _mixer_kernel(cur_ref, pk_ref, pv_ref, prow_ref, sink_ref, cw_ref, o_ref, *, attn_w, kv_w, conv_w):
    nblk = pl.program_id(1)
    has_prev = nblk > 0
    qb = Q_BLOCK
    cur = cur_ref[...]
    k_cur = cur[:, attn_w:attn_w + kv_w]
    v_cur = cur[:, attn_w + kv_w:attn_w + 2 * kv_w]
    k_all = jnp.concatenate([pk_ref[...], k_cur], axis=0)
    v_all = jnp.concatenate([pv_ref[...], v_cur], axis=0)

    rows = GQA * qb
    qi = lax.broadcasted_iota(jnp.int32, (rows, 2 * qb), 0) % qb
    kj = lax.broadcasted_iota(jnp.int32, (rows, 2 * qb), 1)
    dist = qi + qb - kj
    kmin = jnp.where(has_prev, 0, qb)
    valid = (dist >= 0) & (dist < WINDOW) & (kj >= kmin)
    distf = dist.astype(F32)
    head_in_group = lax.broadcasted_iota(jnp.int32, (rows, 1), 0) // qb
    sinks = sink_ref[...]

    outs = []
    for g in range(N_KV_HEADS):
        q4 = jnp.concatenate(
            [cur[:, (g * GQA + j) * HEAD_DIM:(g * GQA + j + 1) * HEAD_DIM] for j in range(GQA)], axis=0)
        kg = k_all[:, g * HEAD_DIM:(g + 1) * HEAD_DIM]
        vg = v_all[:, g * HEAD_DIM:(g + 1) * HEAD_DIM]
        s = lax.dot_general(q4, kg, (((1,), (1,)), ((), ())), preferred_element_type=F32)
        s = s * (HEAD_DIM ** -0.5)
        slope = jnp.zeros((rows, 1), F32)
        sink = jnp.zeros((rows, 1), F32)
        for j in range(GQA):
            h = g * GQA + j
            sel = head_in_group == j
            slope = jnp.where(sel, 2.0 ** (-8.0 * (h + 1) / N_Q_HEADS), slope)
            sink = jnp.where(sel, sinks[:, h:h + 1], sink)
        s = jnp.where(valid, s - slope * distf, -jnp.inf)
        m = jnp.maximum(jnp.max(s, axis=-1, keepdims=True), sink)
        p = jnp.exp(s - m)
        denom = jnp.sum(p, axis=-1, keepdims=True) + jnp.exp(sink - m)
        o4 = jnp.dot(p.astype(BF16), vg, preferred_element_type=F32) / denom
        outs.extend(o4[j * qb:(j + 1) * qb] for j in range(GQA))
    attn = jnp.concatenate(outs, axis=-1)

    c0 = attn_w + 2 * kv_w
    cb = cur[:, c0:c0 + conv_w].astype(F32)
    u = cur[:, c0 + conv_w:c0 + 2 * conv_w].astype(F32) * cur[:, c0 + 2 * conv_w:c0 + 3 * conv_w].astype(F32)
    prow = prow_ref[...]
    up = prow[:, c0 + conv_w:c0 + 2 * conv_w].astype(F32) * prow[:, c0 + 2 * conv_w:c0 + 3 * conv_w].astype(F32)
    up = up * jnp.where(has_prev, 1.0, 0.0)
    pm1 = up[15:16]
    pm2 = up[14:15]
    ri = lax.broadcasted_iota(jnp.int32, u.shape, 0)
    u1 = jnp.where(ri == 0, pm1, pltpu.roll(u, 1, 0))
    u2 = jnp.where(ri == 0, pm2, jnp.where(ri == 1, pm1, pltpu.roll(u, 2, 0)))
    cw = cw_ref[...]
    conv = cb * (cw[0:1] * u2 + cw[1:2] * u1 + cw[2:3] * u)
    o_ref[...] = jnp.concatenate([attn, conv], axis=-1).astype(BF16)


def _mixer(proj, sinks2, conv_w, batch, seq, attn_w, kv_w, conv_wd):
    n, in_w = proj.shape
    nb = seq // Q_BLOCK
    kv_blk0 = attn_w // kv_w
    sub16 = Q_BLOCK // 16

    def cur_map(b, i):
        return (b * nb + i, 0)

    def prev_map(col):
        return lambda b, i: (b * nb + jnp.maximum(i - 1, 0), col)

    def prow_map(b, i):
        return (jnp.maximum((b * nb + i) * sub16 - 1, 0), 0)

    kern = functools.partial(_mixer_kernel, attn_w=attn_w, kv_w=kv_w, conv_w=conv_wd)
    return pl.pallas_call(
        kern,
        grid=(batch, nb),
        in_specs=[pl.BlockSpec((Q_BLOCK, in_w), cur_map),
                  pl.BlockSpec((Q_BLOCK, kv_w), prev_map(kv_blk0)),
                  pl.BlockSpec((Q_BLOCK, kv_w), prev_map(kv_blk0 + 1)),
                  pl.BlockSpec((16, in_w), prow_map),
                  pl.BlockSpec((1, N_Q_HEADS), lambda b, i: (0, 0)),
                  pl.BlockSpec((CONV_K, conv_wd), lambda b, i: (0, 0))],
        out_specs=pl.BlockSpec((Q_BLOCK, attn_w + conv_wd), cur_map),
        out_shape=jax.ShapeDtypeStruct((n, attn_w + conv_wd), BF16),
        compiler_params=_cparams(("arbitrary", "arbitrary")),
        name="mixer",
    )(proj, proj, proj, proj, sinks2, conv_w)


def _split_bf16(v):
    hi = v.astype(BF16)
    lo = (v - hi.astype(F32)).astype(BF16)
    return hi, lo


def _outproj_kernel(mix_ref, x_ref, mod_ref, w_ref, g_ref, b_ref, wr_ref, x1_ref, h2_ref, lg_ref):
    m = mod_ref[0]
    mix = jnp.dot(mix_ref[...], w_ref[...], preferred_element_type=F32)
    x1 = _layer_norm(ALPHA * x_ref[...] + (1.0 + m[2:3]) * mix, g_ref[...], b_ref[...])
    x1_ref[...] = x1
    h2 = x1 * (1.0 + m[4:5]) + m[3:4]
    nslab = h2.shape[1] // TOKEN_TILE[1]
    for s in range(nslab):
        h2_ref[:, s, :] = h2[:, s * TOKEN_TILE[1]:(s + 1) * TOKEN_TILE[1]]
    h_hi, h_lo = _split_bf16(h2)
    w_hi, w_lo = _split_bf16(wr_ref[...])
    lg_ref[...] = (jnp.dot(h_hi, w_hi, preferred_element_type=F32)
                   + (jnp.dot(h_hi, w_lo, preferred_element_type=F32)
                      + jnp.dot(h_lo, w_hi, preferred_element_type=F32)))


def _outproj(mix, x2, mod3, w_out_bf, ln_g, ln_b, w_router, seq):
    n, d = x2.shape
    tm = OUTPROJ_TM
    ne = w_router.shape[1]
    ts, tl = TOKEN_TILE
    row = lambda i: (i, 0)
    const = lambda i: (0, 0)
    return pl.pallas_call(
        _outproj_kernel,
        grid=(n // tm,),
        in_specs=[pl.BlockSpec((tm, d), row),
                  pl.BlockSpec((tm, d), row),
                  pl.BlockSpec((1, 6, d), lambda i: (i * tm // seq, 0, 0)),
                  pl.BlockSpec((d, d), const),
                  pl.BlockSpec((1, d), const),
                  pl.BlockSpec((1, d), const),
                  pl.BlockSpec((d, ne), const)],
        out_specs=[pl.BlockSpec((tm, d), row),
                   pl.BlockSpec((tm, ts, tl), lambda i: (i, 0, 0)),
                   pl.BlockSpec((tm, ne), row)],
        out_shape=[jax.ShapeDtypeStruct((n, d), F32),
                   jax.ShapeDtypeStruct((n, ts, tl), F32),
                   jax.ShapeDtypeStruct((n, ne), F32)],
        compiler_params=_cparams(("arbitrary",)),
        name="outproj",
    )(mix, x2, mod3, w_out_bf, ln_g, ln_b, w_router)


def _first_argmax(v, lane_f):
    m = jnp.max(v, axis=-1, keepdims=True)
    idx = jnp.min(jnp.where(v == m, lane_f, float(N_EXPERTS)), axis=-1, keepdims=True)
    return m, idx


def _route_kernel(lg_ref, bias_ref, eidx_ref, w_ref):
    scores = jax.nn.sigmoid(lg_ref[...])
    sel = scores + bias_ref[...]
    tm = sel.shape[0]
    lane = lax.broadcasted_iota(jnp.int32, (tm, N_EXPERTS), 1)
    lane_f = lane.astype(F32)
    grp = lane // GROUP_SIZE
    neg = -jnp.inf
    gs = []
    for g in range(N_GROUPS):
        vg = jnp.where(grp == g, sel, neg)
        m1, i1 = _first_argmax(vg, lane_f)
        m2 = jnp.max(jnp.where(lane_f == i1, neg, vg), axis=-1, keepdims=True)
        gs.append(m1 + m2)
    keep = jnp.zeros((tm, N_EXPERTS), F32)
    for g in range(N_GROUPS):
        rank = jnp.zeros((tm, 1), F32)
        for o in range(N_GROUPS):
            if o == g:
                continue
            ahead = (gs[o] >= gs[g]) if o < g else (gs[o] > gs[g])
            rank = rank + jnp.where(ahead, 1.0, 0.0)
        keep = jnp.where(grp == g, jnp.where(rank < TOPK_GROUPS, 1.0, 0.0), keep)
    cand = jnp.where(keep > 0.5, sel, neg)
    idxs, ws = [], []
    for _ in range(TOP_K):
        _, ik = _first_argmax(cand, lane_f)
        hit = lane_f == ik
        ws.append(jnp.sum(jnp.where(hit, scores, 0.0), axis=-1, keepdims=True))
        idxs.append(ik)
        cand = jnp.where(hit, neg, cand)
    wsum = ws[0]
    for k in range(1, TOP_K):
        wsum = wsum + ws[k]
    col = lax.broadcasted_iota(jnp.int32, (tm, TOP_K), 1)
    eidx = jnp.zeros((tm, TOP_K), F32)
    wout = jnp.zeros((tm, TOP_K), F32)
    for k in range(TOP_K):
        eidx = jnp.where(col == k, idxs[k], eidx)
        wout = jnp.where(col == k, ws[k] / wsum * ROUTED_SCALE, wout)
    eidx_ref[...] = eidx.astype(jnp.int32)
    w_ref[...] = wout


def _route(logits, bias2):
    n, ne = logits.shape
    tm = ROUTE_TM
    row = lambda i: (i, 0)
    return pl.pallas_call(
        _route_kernel,
        grid=(n // tm,),
        in_specs=[pl.BlockSpec((tm, ne), row), pl.BlockSpec((1, ne), lambda i: (0, 0))],
        out_specs=[pl.BlockSpec((tm, TOP_K), row), pl.BlockSpec((tm, TOP_K), row)],
        out_shape=[jax.ShapeDtypeStruct((n, TOP_K), jnp.int32), jax.ShapeDtypeStruct((n, TOP_K), F32)],
        compiler_params=_cparams(("arbitrary",)),
        name="route",
    )(logits, bias2)


def _moe_kernel(be_ref, new_ref, nused_ref, idx_hbm, roww_ref, h_hbm, wg_ref, wu_ref, wd_ref, ys_hbm,
                idx_s, xbuf, ybuf, wg_s, wu_s, wd_s, sem_i, sem_g, sem_s):
    blk = pl.program_id(0)
    tm = MOE_TM
    nslab, slab_w = TOKEN_TILE

    @pl.when(blk < nused_ref[0])
    def _():
        rec = pltpu.make_async_copy(idx_hbm.at[pl.ds(pl.multiple_of(blk * IDX_REC, IDX_REC), IDX_REC)],
                                    idx_s, sem_i)
        rec.start()
        rec.wait()
        n_valid = idx_s[2 * tm]

        @pl.when(blk == 0)
        def _():
            xbuf[...] = jnp.zeros(xbuf.shape, F32)

        def gather(r, carry):
            pltpu.make_async_copy(h_hbm.at[idx_s[r]], xbuf.at[r], sem_g).start()
            return carry

        lax.fori_loop(0, n_valid, gather, 0)

        @pl.when(new_ref[blk] == 1)
        def _():
            wg_s[...] = wg_ref[0].astype(BF16)
            wu_s[...] = wu_ref[0].astype(BF16)
            wd_s[...] = wd_ref[0].astype(BF16)

        pltpu.make_async_copy(h_hbm.at[pl.ds(0, n_valid)], xbuf.at[pl.ds(0, n_valid)], sem_g).wait()

        x = jnp.concatenate([xbuf[:, s, :] for s in range(nslab)], axis=-1).astype(BF16)
        gate = jnp.dot(x, wg_s[...], preferred_element_type=F32)
        up = jnp.dot(x, wu_s[...], preferred_element_type=F32)
        act = (_silu(gate) * up).astype(BF16)
        y = jnp.dot(act, wd_s[...], preferred_element_type=F32) * roww_ref[...]
        for s in range(nslab):
            ybuf[:, s, :] = y[:, s * slab_w:(s + 1) * slab_w]

        def scatter(r, carry):
            pltpu.make_async_copy(ybuf.at[r], ys_hbm.at[idx_s[tm + r]], sem_s).start()
            return carry

        lax.fori_loop(0, n_valid, scatter, 0)
        pltpu.make_async_copy(ybuf.at[pl.ds(0, n_valid)], ys_hbm.at[pl.ds(0, n_valid)], sem_s).wait()


def _moe(block_e, block_new, n_used, idx_rec, row_w, h3, w_gate, w_up, w_down):
    n, ts, tl = h3.shape
    d = ts * tl
    de = w_gate.shape[2]
    tm = MOE_TM
    n_blk = block_e.shape[0]
    wmap = lambda i, be, nw, nu: (be[i], 0, 0)
    grid_spec = pltpu.PrefetchScalarGridSpec(
        num_scalar_prefetch=3,
        grid=(n_blk,),
        in_specs=[pl.BlockSpec(memory_space=pl.ANY),
                  pl.BlockSpec((tm, 1), lambda i, be, nw, nu: (i, 0)),
                  pl.BlockSpec(memory_space=pl.ANY),
                  pl.BlockSpec((1, d, de), wmap),
                  pl.BlockSpec((1, d, de), wmap),
                  pl.BlockSpec((1, de, d), wmap)],
        out_specs=pl.BlockSpec(memory_space=pl.ANY),
        scratch_shapes=[pltpu.SMEM((IDX_REC,), jnp.int32),
                        pltpu.VMEM((tm, ts, tl), F32),
                        pltpu.VMEM((tm, ts, tl), F32),
                        pltpu.VMEM((d, de), BF16),
                        pltpu.VMEM((d, de), BF16),
                        pltpu.VMEM((de, d), BF16),
                        pltpu.SemaphoreType.DMA,
                        pltpu.SemaphoreType.DMA,
                        pltpu.SemaphoreType.DMA])
    return pl.pallas_call(
        _moe_kernel,
        grid_spec=grid_spec,
        out_shape=jax.ShapeDtypeStruct((TOP_K * n, ts, tl), F32),
        compiler_params=_cparams(("arbitrary",)),
        name="moe",
    )(block_e, block_new, n_used, idx_rec, row_w, h3, w_gate, w_up, w_down)


def _final_kernel(ys_ref, h_ref, x1_ref, mod_ref, wg_ref, wu_ref, wd_ref, g_ref, b_ref, o_ref, acc_ref):
    nslab = TOKEN_TILE[0]
    acc = ys_ref[0]
    for k in range(1, TOP_K):
        acc = acc + ys_ref[k]
    acc_ref[...] = acc
    moe = jnp.concatenate([acc_ref[:, s, :] for s in range(nslab)], axis=-1)
    h = jnp.concatenate([h_ref[:, s, :] for s in range(nslab)], axis=-1).astype(BF16)
    gate = jnp.dot(h, wg_ref[...], preferred_element_type=F32)
    up = jnp.dot(h, wu_ref[...], preferred_element_type=F32)
    shared = jnp.dot((_silu(gate) * up).astype(BF16), wd_ref[...], preferred_element_type=F32)
    m = mod_ref[0]
    y = ALPHA * x1_ref[...] + (1.0 + m[5:6]) * (moe + shared)
    o_ref[...] = _layer_norm(y, g_ref[...], b_ref[...])


def _final(ys4, h3, x1, mod3, wsg, wsu, wsd, ln_g, ln_b, seq):
    n, d = x1.shape
    ts, tl = TOKEN_TILE
    de = wsg.shape[1]
    tm = FINAL_TM
    row = lambda i: (i, 0)
    const = lambda i: (0, 0)
    return pl.pallas_call(
        _final_kernel,
        grid=(n // tm,),
        in_specs=[pl.BlockSpec((TOP_K, tm, ts, tl), lambda i: (0, i, 0, 0)),
                  pl.BlockSpec((tm, ts, tl), lambda i: (i, 0, 0)),
                  pl.BlockSpec((tm, d), row),
                  pl.BlockSpec((1, 6, d), lambda i: (i * tm // seq, 0, 0)),
                  pl.BlockSpec((d, de), const),
                  pl.BlockSpec((d, de), const),
                  pl.BlockSpec((de, d), const),
                  pl.BlockSpec((1, d), const),
                  pl.BlockSpec((1, d), const)],
        out_specs=pl.BlockSpec((tm, d), row),
        out_shape=jax.ShapeDtypeStruct((n, d), F32),
        scratch_shapes=[pltpu.VMEM((tm, ts, tl), F32)],
        compiler_params=_cparams(("arbitrary",)),
        name="final",
    )(ys4, h3, x1, mod3, wsg, wsu, wsd, ln_g, ln_b)


def _dispatch_tables(eidx, gate_w, n):
    tm = MOE_TM
    a = n * TOP_K
    flat_e = eidx.reshape(-1)
    order = jnp.argsort(flat_e).astype(jnp.int32)
    counts = jnp.bincount(flat_e, length=N_EXPERTS).astype(jnp.int32)
    offsets = jnp.cumsum(counts) - counts
    padded = (counts + tm - 1) // tm * tm
    pad_end = jnp.cumsum(padded)
    starts = pad_end - padded
    n_rows = a + N_EXPERTS * tm
    n_blk = n_rows // tm
    blk_start = jnp.arange(n_blk, dtype=jnp.int32) * tm
    n_used = (pad_end[-1] // tm).astype(jnp.int32)
    block_e = jnp.minimum(jnp.searchsorted(pad_end, blk_start, side="right"), N_EXPERTS - 1).astype(jnp.int32)
    last_e = block_e[jnp.maximum(n_used - 1, 0)]
    block_e = jnp.where(jnp.arange(n_blk) < n_used, block_e, last_e)
    block_new = jnp.concatenate([jnp.ones((1,), jnp.int32), (block_e[1:] != block_e[:-1]).astype(jnp.int32)])
    pos = jnp.arange(n_rows, dtype=jnp.int32)
    e_of = block_e[pos // tm]
    j = pos - starts[e_of]
    valid = (j < counts[e_of]) & (pos < pad_end[-1])
    flat_id = order[jnp.clip(offsets[e_of] + j, 0, a - 1)]
    tok = jnp.where(valid, flat_id // TOP_K, 0)
    tgt = jnp.where(valid, (flat_id % TOP_K) * n + flat_id // TOP_K, 0)
    row_w = jnp.where(valid, gate_w.reshape(-1)[flat_id], 0.0).reshape(n_rows, 1)
    n_valid = jnp.sum(valid.reshape(n_blk, tm).astype(jnp.int32), axis=1, keepdims=True)
    rec = jnp.concatenate([tok.reshape(n_blk, tm), tgt.reshape(n_blk, tm), n_valid,
                           jnp.zeros((n_blk, IDX_REC - 2 * tm - 1), jnp.int32)], axis=1)
    return block_e, block_new, n_used.reshape(1), rec.reshape(-1), row_w


def kernel(x, c, w_mod, b_mod, w_in, conv_w, attn_sinks, w_out, ln1_g, ln1_b, w_router, router_bias,
           w_gate, w_up, w_down, ws_gate, ws_up, ws_down, ln2_g, ln2_b):
    b, s, d = x.shape
    n = b * s
    attn_w = N_Q_HEADS * HEAD_DIM
    kv_w = N_KV_HEADS * HEAD_DIM
    conv_wd = d - attn_w
    in_w = attn_w + 2 * kv_w + 3 * conv_wd
    ts, tl = TOKEN_TILE
    x2 = x.reshape(n, d)
    c8 = jnp.zeros((SUBLANES, d), F32).at[:b].set(c)
    for l in range(DEPTH):
        mod = _mod(c8, w_mod[l], b_mod[l].reshape(1, -1))[:b]
        mod3 = mod.reshape(b, 6, d)
        w_in3 = w_in[l].astype(BF16).reshape(d, in_w // INPROJ_TN, INPROJ_TN).transpose(1, 0, 2)
        proj = _inproj(x2, mod3, w_in3, s)
        mix = _mixer(proj, attn_sinks[l].reshape(1, -1), conv_w[l], b, s, attn_w, kv_w, conv_wd)
        x1, h3, logits = _outproj(mix, x2, mod3, w_out[l].astype(BF16), ln1_g[l].reshape(1, -1),
                                  ln1_b[l].reshape(1, -1), w_router[l], s)
        eidx, gate_w = _route(logits, router_bias[l].reshape(1, -1))
        block_e, block_new, n_used, idx_rec, row_w = _dispatch_tables(eidx, gate_w, n)
        ys = _moe(block_e, block_new, n_used, idx_rec, row_w, h3, w_gate[l], w_up[l], w_down[l])
        ys4 = ys.reshape(TOP_K, n, ts, tl)
        x2 = _final(ys4, h3, x1, mod3, ws_gate[l].astype(BF16), ws_up[l].astype(BF16), ws_down[l].astype(BF16),
                    ln2_g[l].reshape(1, -1), ln2_b[l].reshape(1, -1), s)
    return x2.reshape(b, s, d)
```

```python
import functools

import jax
import jax.numpy as jnp
from jax import lax
from jax.experimental import pallas as pl
from jax.experimental.pallas import tpu as pltpu

HEAD_DIM = 64
N_Q_HEADS = 16
N_KV_HEADS = 4
GQA = N_Q_HEADS // N_KV_HEADS
CONV_K = 3
WINDOW = 128
Q_BLOCK = 128
N_EXPERTS = 64
TOP_K = 8
N_GROUPS = 8
GROUP_SIZE = N_EXPERTS // N_GROUPS
TOPK_GROUPS = 4
ROUTED_SCALE = 2.5
DEPTH = 1
ALPHA = (2.0 * DEPTH) ** 0.25
LN_EPS = 1e-5

LANES = 128
SUBLANES = 8
TOKEN_ROWS = 16
SLAB_PITCH = 24
VMEM_LIMIT = 56 * 1024 * 1024

MOD_TN = 1024
INPROJ_TM = 512
INPROJ_TN = 1536
OUTPROJ_TM = 256
ROUTE_TM = 512
MOE_TM = 256
FINAL_TM = 128
IDX_CHUNK = 1024
ISSUE_UNROLL = 8

F32 = jnp.float32
BF16 = jnp.bfloat16


def _cparams(sem):
    return pltpu.CompilerParams(dimension_semantics=sem, vmem_limit_bytes=VMEM_LIMIT)


def _silu(v):
    return v * jax.nn.sigmoid(v)


def _layer_norm(y, g, b):
    mu = jnp.mean(y, axis=-1, keepdims=True)
    yc = y - mu
    var = jnp.mean(yc * yc, axis=-1, keepdims=True)
    return yc * lax.rsqrt(var + LN_EPS) * g + b


def _mod_kernel(c_ref, w_ref, b_ref, o_ref):
    cs = _silu(c_ref[...]).astype(BF16)
    o_ref[...] = jnp.dot(cs, w_ref[...].astype(BF16), preferred_element_type=F32) + b_ref[...]


def _mod(c8, w_mod, b_mod):
    d, n = w_mod.shape
    return pl.pallas_call(
        _mod_kernel,
        grid=(n // MOD_TN,),
        in_specs=[pl.BlockSpec((SUBLANES, d), lambda j: (0, 0)),
                  pl.BlockSpec((d, MOD_TN), lambda j: (0, j)),
                  pl.BlockSpec((1, MOD_TN), lambda j: (0, j))],
        out_specs=pl.BlockSpec((SUBLANES, MOD_TN), lambda j: (0, j)),
        out_shape=jax.ShapeDtypeStruct((SUBLANES, n), F32),
        compiler_params=_cparams(("arbitrary",)),
        name="mod",
    )(c8, w_mod, b_mod)


def _inproj_kernel(x_ref, mod_ref, w_ref, o_ref, h_ref):
    j = pl.program_id(1)

    @pl.when(j == 0)
    def _():
        m = mod_ref[0]
        h_ref[...] = (x_ref[...] * (1.0 + m[1:2]) + m[0:1]).astype(BF16)

    o_ref[...] = jnp.dot(h_ref[...], w_ref[0], preferred_element_type=F32).astype(BF16)


def _inproj(x2, mod3, w_in3, seq):
    n, d = x2.shape
    nj, _, tn = w_in3.shape
    tm = INPROJ_TM
    return pl.pallas_call(
        _inproj_kernel,
        grid=(n // tm, nj),
        in_specs=[pl.BlockSpec((tm, d), lambda i, j: (i, 0)),
                  pl.BlockSpec((1, 6, d), lambda i, j: (i * tm // seq, 0, 0)),
                  pl.BlockSpec((1, d, tn), lambda i, j: (j, 0, 0))],
        out_specs=pl.BlockSpec((tm, tn), lambda i, j: (i, j)),
        out_shape=jax.ShapeDtypeStruct((n, nj * tn), BF16),
        scratch_shapes=[pltpu.VMEM((tm, d), BF16)],
        compiler_params=_cparams(("arbitrary", "arbitrary")),
        name="inproj",
    )(x2, mod3, w_in3)


def _mixer_kernel(cur_ref, pk_ref, pv_ref, prow_ref, sink_ref, cw_ref, o_ref, *, attn_w, kv_w, conv_w):
    nblk = pl.program_id(1)
    has_prev = nblk > 0
    qb = Q_BLOCK
    cur = cur_ref[...]
    k_cur = cur[:, attn_w:attn_w + kv_w]
    v_cur = cur[:, attn_w + kv_w:attn_w + 2 * kv_w]
    k_all = jnp.concatenate([pk_ref[...], k_cur], axis=0)
    v_all = jnp.concatenate([pv_ref[...], v_cur], axis=0)

    rows = GQA * qb
    qi = lax.broadcasted_iota(jnp.int32, (rows, 2 * qb), 0) % qb
    kj = lax.broadcasted_iota(jnp.int32, (rows, 2 * qb), 1)
    dist = qi + qb - kj
    kmin = jnp.where(has_prev, 0, qb)
    valid = (dist >= 0) & (dist < WINDOW) & (kj >= kmin)
    distf = dist.astype(F32)
    head_in_group = lax.broadcasted_iota(jnp.int32, (rows, 1), 0) // qb
    sinks = sink_ref[...]

    outs = []
    for g in range(N_KV_HEADS):
        q4 = jnp.concatenate(
            [cur[:, (g * GQA + j) * HEAD_DIM:(g * GQA + j + 1) * HEAD_DIM] for j in range(GQA)], axis=0)
        kg = k_all[:, g * HEAD_DIM:(g + 1) * HEAD_DIM]
        vg = v_all[:, g * HEAD_DIM:(g + 1) * HEAD_DIM]
        s = lax.dot_general(q4, kg, (((1,), (1,)), ((), ())), preferred_element_type=F32)
        s = s * (HEAD_DIM ** -0.5)
        slope = jnp.zeros((rows, 1), F32)
        sink = jnp.zeros((rows, 1), F32)
        for j in range(GQA):
            h = g * GQA + j
            sel = head_in_group == j
            slope = jnp.where(sel, 2.0 ** (-8.0 * (h + 1) / N_Q_HEADS), slope)
            sink = jnp.where(sel, sinks[:, h:h + 1], sink)
        s = jnp.where(valid, s - slope * distf, -jnp.inf)
        m = jnp.maximum(jnp.max(s, axis=-1, keepdims=True), sink)
        p = jnp.exp(s - m)
        denom = jnp.sum(p, axis=-1, keepdims=True) + jnp.exp(sink - m)
        o4 = jnp.dot(p.astype(BF16), vg, preferred_element_type=F32) / denom
        outs.extend(o4[j * qb:(j + 1) * qb] for j in range(GQA))
    attn = jnp.concatenate(outs, axis=-1)

    c0 = attn_w + 2 * kv_w
    cb = cur[:, c0:c0 + conv_w].astype(F32)
    u = cur[:, c0 + conv_w:c0 + 2 * conv_w].astype(F32) * cur[:, c0 + 2 * conv_w:c0 + 3 * conv_w].astype(F32)
    prow = prow_ref[...]
    up = prow[:, c0 + conv_w:c0 + 2 * conv_w].astype(F32) * prow[:, c0 + 2 * conv_w:c0 + 3 * conv_w].astype(F32)
    up = up * jnp.where(has_prev, 1.0, 0.0)
    pm1 = up[15:16]
    pm2 = up[14:15]
    ri = lax.broadcasted_iota(jnp.int32, u.shape, 0)
    u1 = jnp.where(ri == 0, pm1, pltpu.roll(u, 1, 0))
    u2 = jnp.where(ri == 0, pm2, jnp.where(ri == 1, pm1, pltpu.roll(u, 2, 0)))
    cw = cw_ref[...]
    conv = cb * (cw[0:1] * u2 + cw[1:2] * u1 + cw[2:3] * u)
    o_ref[...] = jnp.concatenate([attn, conv], axis=-1).astype(BF16)


def _mixer(proj, sinks2, conv_w, batch, seq, attn_w, kv_w, conv_wd):
    n, in_w = proj.shape
    nb = seq // Q_BLOCK
    kv_blk0 = attn_w // kv_w
    sub16 = Q_BLOCK // 16

    def cur_map(b, i):
        return (b * nb + i, 0)

    def prev_map(col):
        return lambda b, i: (b * nb + jnp.maximum(i - 1, 0), col)

    def prow_map(b, i):
        return (jnp.maximum((b * nb + i) * sub16 - 1, 0), 0)

    kern = functools.partial(_mixer_kernel, attn_w=attn_w, kv_w=kv_w, conv_w=conv_wd)
    return pl.pallas_call(
        kern,
        grid=(batch, nb),
        in_specs=[pl.BlockSpec((Q_BLOCK, in_w), cur_map),
                  pl.BlockSpec((Q_BLOCK, kv_w), prev_map(kv_blk0)),
                  pl.BlockSpec((Q_BLOCK, kv_w), prev_map(kv_blk0 + 1)),
                  pl.BlockSpec((16, in_w), prow_map),
                  pl.BlockSpec((1, N_Q_HEADS), lambda b, i: (0, 0)),
                  pl.BlockSpec((CONV_K, conv_wd), lambda b, i: (0, 0))],
        out_specs=pl.BlockSpec((Q_BLOCK, attn_w + conv_wd), cur_map),
        out_shape=jax.ShapeDtypeStruct((n, attn_w + conv_wd), BF16),
        compiler_params=_cparams(("arbitrary", "arbitrary")),
        name="mixer",
    )(proj, proj, proj, proj, sinks2, conv_w)


def _split_bf16(v):
    hi = v.astype(BF16)
    lo = (v - hi.astype(F32)).astype(BF16)
    return hi, lo


def _outproj_kernel(mix_ref, x_ref, mod_ref, w_ref, g_ref, b_ref, wr_ref, x1_ref, h2_ref, lg_ref):
    m = mod_ref[0]
    mix = jnp.dot(mix_ref[...], w_ref[...], preferred_element_type=F32)
    x1 = _layer_norm(ALPHA * x_ref[...] + (1.0 + m[2:3]) * mix, g_ref[...], b_ref[...])
    x1_ref[...] = x1
    h2 = x1 * (1.0 + m[4:5]) + m[3:4]
    tm = h2.shape[0]
    for s in range(TOKEN_ROWS):
        h2_ref[pl.ds(s, tm, stride=TOKEN_ROWS), :] = h2[:, s * LANES:(s + 1) * LANES]
    h_hi, h_lo = _split_bf16(h2)
    w_hi, w_lo = _split_bf16(wr_ref[...])
    lg_ref[...] = (jnp.dot(h_hi, w_hi, preferred_element_type=F32)
                   + (jnp.dot(h_hi, w_lo, preferred_element_type=F32)
                      + jnp.dot(h_lo, w_hi, preferred_element_type=F32)))


def _outproj(mix, x2, mod3, w_out_bf, ln_g, ln_b, w_router, seq):
    n, d = x2.shape
    tm = OUTPROJ_TM
    ne = w_router.shape[1]
    row = lambda i: (i, 0)
    const = lambda i: (0, 0)
    return pl.pallas_call(
        _outproj_kernel,
        grid=(n // tm,),
        in_specs=[pl.BlockSpec((tm, d), row),
                  pl.BlockSpec((tm, d), row),
                  pl.BlockSpec((1, 6, d), lambda i: (i * tm // seq, 0, 0)),
                  pl.BlockSpec((d, d), const),
                  pl.BlockSpec((1, d), const),
                  pl.BlockSpec((1, d), const),
                  pl.BlockSpec((d, ne), const)],
        out_specs=[pl.BlockSpec((tm, d), row),
                   pl.BlockSpec((tm * TOKEN_ROWS, LANES), row),
                   pl.BlockSpec((tm, ne), row)],
        out_shape=[jax.ShapeDtypeStruct((n, d), F32),
                   jax.ShapeDtypeStruct((n * TOKEN_ROWS, LANES), F32),
                   jax.ShapeDtypeStruct((n, ne), F32)],
        compiler_params=_cparams(("arbitrary",)),
        name="outproj",
    )(mix, x2, mod3, w_out_bf, ln_g, ln_b, w_router)


def _first_argmax(v, lane_f):
    m = jnp.max(v, axis=-1, keepdims=True)
    idx = jnp.min(jnp.where(v == m, lane_f, float(N_EXPERTS)), axis=-1, keepdims=True)
    return m, idx


def _route_kernel(lg_ref, bias_ref, eidx_ref, w_ref, cnt_ref):
    scores = jax.nn.sigmoid(lg_ref[...])
    sel = scores + bias_ref[...]
    tm = sel.shape[0]
    lane = lax.broadcasted_iota(jnp.int32, (tm, N_EXPERTS), 1)
    lane_f = lane.astype(F32)
    grp = lane // GROUP_SIZE
    neg = -jnp.inf
    gs = []
    for g in range(N_GROUPS):
        vg = jnp.where(grp == g, sel, neg)
        m1, i1 = _first_argmax(vg, lane_f)
        m2 = jnp.max(jnp.where(lane_f == i1, neg, vg), axis=-1, keepdims=True)
        gs.append(m1 + m2)
    keep = jnp.zeros((tm, N_EXPERTS), F32)
    for g in range(N_GROUPS):
        rank = jnp.zeros((tm, 1), F32)
        for o in range(N_GROUPS):
            if o == g:
                continue
            ahead = (gs[o] >= gs[g]) if o < g else (gs[o] > gs[g])
            rank = rank + jnp.where(ahead, 1.0, 0.0)
        keep = jnp.where(grp == g, jnp.where(rank < TOPK_GROUPS, 1.0, 0.0), keep)
    cand = jnp.where(keep > 0.5, sel, neg)
    idxs, ws = [], []
    chosen = jnp.zeros((tm, N_EXPERTS), F32)
    for _ in range(TOP_K):
        _, ik = _first_argmax(cand, lane_f)
        hit = lane_f == ik
        ws.append(jnp.sum(jnp.where(hit, scores, 0.0), axis=-1, keepdims=True))
        idxs.append(ik)
        cand = jnp.where(hit, neg, cand)
        chosen = jnp.where(hit, 1.0, chosen)
    cnt_ref[0] = jnp.sum(chosen, axis=0, keepdims=True)
    wsum = ws[0]
    for k in range(1, TOP_K):
        wsum = wsum + ws[k]
    col = lax.broadcasted_iota(jnp.int32, (tm, TOP_K), 1)
    eidx = jnp.zeros((tm, TOP_K), F32)
    wout = jnp.zeros((tm, TOP_K), F32)
    for k in range(TOP_K):
        eidx = jnp.where(col == k, idxs[k], eidx)
        wout = jnp.where(col == k, ws[k] / wsum * ROUTED_SCALE, wout)
    eidx_ref[...] = eidx.astype(jnp.int32)
    w_ref[...] = wout


def _route(logits, bias2):
    n, ne = logits.shape
    tm = ROUTE_TM
    row = lambda i: (i, 0)
    return pl.pallas_call(
        _route_kernel,
        grid=(n // tm,),
        in_specs=[pl.BlockSpec((tm, ne), row), pl.BlockSpec((1, ne), lambda i: (0, 0))],
        out_specs=[pl.BlockSpec((tm, TOP_K), row), pl.BlockSpec((tm, TOP_K), row),
                   pl.BlockSpec((1, 1, ne), lambda i: (i, 0, 0))],
        out_shape=[jax.ShapeDtypeStruct((n, TOP_K), jnp.int32), jax.ShapeDtypeStruct((n, TOP_K), F32),
                   jax.ShapeDtypeStruct((n // tm, 1, ne), F32)],
        compiler_params=_cparams(("arbitrary",)),
        name="route",
    )(logits, bias2)


def _issue_rows(n_rows, issue_one):
    n_full = n_rows // ISSUE_UNROLL

    def chunk(c, carry):
        for u in range(ISSUE_UNROLL):
            issue_one(c * ISSUE_UNROLL + u)
        return carry

    def tail(r, carry):
        issue_one(r)
        return carry

    lax.fori_loop(0, n_full, chunk, 0)
    lax.fori_loop(n_full * ISSUE_UNROLL, n_rows, tail, 0)


def _moe_kernel(be_ref, new_ref, nv_ref, nused_ref, ids_hbm, roww_ref, h_hbm, wg_ref, wu_ref, wd_ref, ys_hbm,
                idx_s, xbuf, ybuf, wg_s, wu_s, wd_s, sem_i, sem_g, sem_s, *, n_tok):
    blk = pl.program_id(0)
    tm = MOE_TM
    per = IDX_CHUNK // tm
    per_log2 = per.bit_length() - 1
    topk_log2 = TOP_K.bit_length() - 1
    n_used = nused_ref[0]
    slab = TOKEN_ROWS

    def slab_at(buf, slot, r):
        return buf.at[slot, pl.ds(pl.multiple_of(r * SLAB_PITCH, SUBLANES), slab), :]

    def hbm_slab(ref, row):
        return ref.at[pl.ds(pl.multiple_of(row * slab, slab), slab), :]

    def ids_copy(b):
        c = b >> per_log2
        return pltpu.make_async_copy(ids_hbm.at[pl.ds(pl.multiple_of(c * IDX_CHUNK, IDX_CHUNK), IDX_CHUNK)],
                                     idx_s.at[c & 1], sem_i)

    def row_id(b, r):
        return idx_s[(b >> per_log2) & 1, (b & (per - 1)) * tm + r]

    def gather_block(b, slot):
        def one(r):
            tok = row_id(b, r) >> topk_log2
            pltpu.make_async_copy(hbm_slab(h_hbm, tok), slab_at(xbuf, slot, r), sem_g.at[slot]).start()
        _issue_rows(nv_ref[b], one)

    def scatter_block(b, slot):
        def one(r):
            rid = row_id(b, r)
            tgt = (rid & (TOP_K - 1)) * n_tok + (rid >> topk_log2)
            pltpu.make_async_copy(slab_at(ybuf, slot, r), hbm_slab(ys_hbm, tgt), sem_s.at[slot]).start()
        _issue_rows(nv_ref[b], one)

    def wait_rows(hbm, buf, sem, b, slot, to_hbm):
        rows = nv_ref[b] * slab
        v = buf.at[slot, pl.ds(0, rows), :]
        hv = hbm.at[pl.ds(0, rows), :]
        (pltpu.make_async_copy(v, hv, sem.at[slot]) if to_hbm else pltpu.make_async_copy(hv, v, sem.at[slot])).wait()

    @pl.when(blk < n_used)
    def _():
        slot = blk & 1

        @pl.when(blk == 0)
        def _():
            xbuf[...] = jnp.zeros(xbuf.shape, F32)
            first = ids_copy(0)
            first.start()
            first.wait()
            gather_block(0, 0)

        @pl.when(blk + 1 < n_used)
        def _():
            @pl.when(((blk + 1) & (per - 1)) == 0)
            def _():
                nxt = ids_copy(blk + 1)
                nxt.start()
                nxt.wait()
            gather_block(blk + 1, 1 - slot)

        @pl.when(new_ref[blk] == 1)
        def _():
            wg_s[...] = wg_ref[0].astype(BF16)
            wu_s[...] = wu_ref[0].astype(BF16)
            wd_s[...] = wd_ref[0].astype(BF16)

        wait_rows(h_hbm, xbuf, sem_g, blk, slot, to_hbm=False)

        @pl.when(blk >= 2)
        def _():
            wait_rows(ys_hbm, ybuf, sem_s, blk - 2, slot, to_hbm=True)

        x = jnp.concatenate([xbuf[slot, pl.ds(s, tm, stride=SLAB_PITCH), :] for s in range(slab)],
                            axis=-1).astype(BF16)
        gate = jnp.dot(x, wg_s[...], preferred_element_type=F32)
        up = jnp.dot(x, wu_s[...], preferred_element_type=F32)
        act = (_silu(gate) * up).astype(BF16)
        y = jnp.dot(act, wd_s[...], preferred_element_type=F32) * roww_ref[...]
        for s in range(slab):
            ybuf[slot, pl.ds(s, tm, stride=SLAB_PITCH), :] = y[:, s * LANES:(s + 1) * LANES]
        scatter_block(blk, slot)

        @pl.when(blk == n_used - 1)
        def _():
            @pl.when(blk >= 1)
            def _():
                wait_rows(ys_hbm, ybuf, sem_s, blk - 1, 1 - slot, to_hbm=True)
            wait_rows(ys_hbm, ybuf, sem_s, blk, slot, to_hbm=True)


def _moe(block_e, block_new, block_nv, n_used, row_ids, row_w, h2d, w_gate, w_up, w_down):
    n = h2d.shape[0] // TOKEN_ROWS
    d = TOKEN_ROWS * LANES
    de = w_gate.shape[2]
    tm = MOE_TM
    n_blk = block_e.shape[0]
    wmap = lambda i, be, nw, nv, nu: (be[i], 0, 0)
    grid_spec = pltpu.PrefetchScalarGridSpec(
        num_scalar_prefetch=4,
        grid=(n_blk,),
        in_specs=[pl.BlockSpec(memory_space=pl.ANY),
                  pl.BlockSpec((tm, 1), lambda i, be, nw, nv, nu: (i, 0)),
                  pl.BlockSpec(memory_space=pl.ANY),
                  pl.BlockSpec((1, d, de), wmap),
                  pl.BlockSpec((1, d, de), wmap),
                  pl.BlockSpec((1, de, d), wmap)],
        out_specs=pl.BlockSpec(memory_space=pl.ANY),
        scratch_shapes=[pltpu.SMEM((2, IDX_CHUNK), jnp.int32),
                        pltpu.VMEM((2, tm * SLAB_PITCH, LANES), F32),
                        pltpu.VMEM((2, tm * SLAB_PITCH, LANES), F32),
                        pltpu.VMEM((d, de), BF16),
                        pltpu.VMEM((d, de), BF16),
                        pltpu.VMEM((de, d), BF16),
                        pltpu.SemaphoreType.DMA,
                        pltpu.SemaphoreType.DMA((2,)),
                        pltpu.SemaphoreType.DMA((2,))])
    return pl.pallas_call(
        functools.partial(_moe_kernel, n_tok=n),
        grid_spec=grid_spec,
        out_shape=jax.ShapeDtypeStruct((TOP_K * n * TOKEN_ROWS, LANES), F32),
        compiler_params=_cparams(("arbitrary",)),
        name="moe",
    )(block_e, block_new, block_nv, n_used, row_ids, row_w, h2d, w_gate, w_up, w_down)


def _final_kernel(ys_ref, h_ref, x1_ref, mod_ref, wg_ref, wu_ref, wd_ref, g_ref, b_ref, o_ref, acc_ref):
    tm = x1_ref.shape[0]
    acc = ys_ref[0]
    for k in range(1, TOP_K):
        acc = acc + ys_ref[k]
    acc_ref[...] = acc
    rows = lambda ref, s: ref[pl.ds(s, tm, stride=TOKEN_ROWS), :]
    moe = jnp.concatenate([rows(acc_ref, s) for s in range(TOKEN_ROWS)], axis=-1)
    h = jnp.concatenate([rows(h_ref, s) for s in range(TOKEN_ROWS)], axis=-1).astype(BF16)
    gate = jnp.dot(h, wg_ref[...], preferred_element_type=F32)
    up = jnp.dot(h, wu_ref[...], preferred_element_type=F32)
    shared = jnp.dot((_silu(gate) * up).astype(BF16), wd_ref[...], preferred_element_type=F32)
    m = mod_ref[0]
    y = ALPHA * x1_ref[...] + (1.0 + m[5:6]) * (moe + shared)
    o_ref[...] = _layer_norm(y, g_ref[...], b_ref[...])


def _final(ys3, h2d, x1, mod3, wsg, wsu, wsd, ln_g, ln_b, seq):
    n, d = x1.shape
    de = wsg.shape[1]
    tm = FINAL_TM
    row = lambda i: (i, 0)
    const = lambda i: (0, 0)
    return pl.pallas_call(
        _final_kernel,
        grid=(n // tm,),
        in_specs=[pl.BlockSpec((TOP_K, tm * TOKEN_ROWS, LANES), lambda i: (0, i, 0)),
                  pl.BlockSpec((tm * TOKEN_ROWS, LANES), row),
                  pl.BlockSpec((tm, d), row),
                  pl.BlockSpec((1, 6, d), lambda i: (i * tm // seq, 0, 0)),
                  pl.BlockSpec((d, de), const),
                  pl.BlockSpec((d, de), const),
                  pl.BlockSpec((de, d), const),
                  pl.BlockSpec((1, d), const),
                  pl.BlockSpec((1, d), const)],
        out_specs=pl.BlockSpec((tm, d), row),
        out_shape=jax.ShapeDtypeStruct((n, d), F32),
        scratch_shapes=[pltpu.VMEM((tm * TOKEN_ROWS, LANES), F32)],
        compiler_params=_cparams(("arbitrary",)),
        name="final",
    )(ys3, h2d, x1, mod3, wsg, wsu, wsd, ln_g, ln_b)


def _dispatch_tables(eidx, gate_w, tile_counts, n):
    tm = MOE_TM
    a = n * TOP_K
    i32 = jnp.int32
    experts = jnp.arange(N_EXPERTS, dtype=i32)
    counts = jnp.sum(tile_counts, axis=(0, 1)).astype(i32)
    padded = (counts + tm - 1) // tm * tm
    pad_end = jnp.cumsum(padded)
    starts = pad_end - padded
    n_blk = a // tm + N_EXPERTS
    blk_start = jnp.arange(n_blk, dtype=i32) * tm
    n_used = pad_end[-1] // tm
    in_use = jnp.arange(n_blk) < n_used
    raw_e = jnp.minimum(jnp.sum((pad_end[None, :] <= blk_start[:, None]).astype(i32), axis=1), N_EXPERTS - 1)
    last_e = jnp.sum(jnp.where(jnp.arange(n_blk) == n_used - 1, raw_e, 0))
    block_e = jnp.where(in_use, raw_e, last_e)
    onehot = block_e[:, None] == experts[None, :]
    cnt_b = jnp.sum(jnp.where(onehot, counts[None, :], 0), axis=1)
    start_b = jnp.sum(jnp.where(onehot, starts[None, :], 0), axis=1)
    block_nv = jnp.where(in_use, jnp.clip(cnt_b - (blk_start - start_b), 0, tm), 0).astype(i32)
    block_new = jnp.concatenate([jnp.ones((1,), i32), (block_e[1:] != block_e[:-1]).astype(i32)])
    dummy_keys = jnp.where(jnp.arange(tm, dtype=i32)[None, :] < (padded - counts)[:, None],
                           experts[:, None], N_EXPERTS).reshape(-1)
    keys = jnp.concatenate([eidx.reshape(-1), dummy_keys])
    ids = jnp.concatenate([jnp.arange(a, dtype=i32), jnp.zeros((N_EXPERTS * tm,), i32)])
    wts = jnp.concatenate([gate_w.reshape(-1), jnp.zeros((N_EXPERTS * tm,), F32)])
    _, row_ids, row_w = lax.sort((keys, ids, wts), num_keys=1, is_stable=True)
    return (block_e.astype(i32), block_new, block_nv, n_used.astype(i32).reshape(1), row_ids,
            row_w.reshape(-1, 1))


def kernel(x, c, w_mod, b_mod, w_in, conv_w, attn_sinks, w_out, ln1_g, ln1_b, w_router, router_bias,
           w_gate, w_up, w_down, ws_gate, ws_up, ws_down, ln2_g, ln2_b):
    b, s, d = x.shape
    n = b * s
    attn_w = N_Q_HEADS * HEAD_DIM
    kv_w = N_KV_HEADS * HEAD_DIM
    conv_wd = d - attn_w
    in_w = attn_w + 2 * kv_w + 3 * conv_wd
    x2 = x.reshape(n, d)
    c8 = jnp.zeros((SUBLANES, d), F32).at[:b].set(c)
    for l in range(DEPTH):
        mod = _mod(c8, w_mod[l], b_mod[l].reshape(1, -1))[:b]
        mod3 = mod.reshape(b, 6, d)
        w_in3 = w_in[l].astype(BF16).reshape(d, in_w // INPROJ_TN, INPROJ_TN).transpose(1, 0, 2)
        proj = _inproj(x2, mod3, w_in3, s)
        mix = _mixer(proj, attn_sinks[l].reshape(1, -1), conv_w[l], b, s, attn_w, kv_w, conv_wd)
        x1, h2d, logits = _outproj(mix, x2, mod3, w_out[l].astype(BF16), ln1_g[l].reshape(1, -1),
                                   ln1_b[l].reshape(1, -1), w_router[l], s)
        eidx, gate_w, tile_counts = _route(logits, router_bias[l].reshape(1, -1))
        block_e, block_new, block_nv, n_used, row_ids, row_w = _dispatch_tables(eidx, gate_w, tile_counts, n)
        ys = _moe(block_e, block_new, block_nv, n_used, row_ids, row_w, h2d, w_gate[l], w_up[l], w_down[l])
        ys3 = ys.reshape(TOP_K, n * TOKEN_ROWS, LANES)
        x2 = _final(ys3, h2d, x1, mod3, ws_gate[l].astype(BF16), ws_up[l].astype(BF16), ws_down[l].astype(BF16),
                    ln2_g[l].reshape(1, -1), ln2_b[l].reshape(1, -1), s)
    return x2.reshape(b, s, d)
```

```python
import functools

import jax
import jax.numpy as jnp
from jax import lax
from jax.experimental import pallas as pl
from jax.experimental.pallas import tpu as pltpu

HEAD_DIM = 64
N_Q_HEADS = 16
N_KV_HEADS = 4
GQA = N_Q_HEADS // N_KV_HEADS
CONV_K = 3
WINDOW = 128
Q_BLOCK = 128
N_EXPERTS = 64
TOP_K = 8
N_GROUPS = 8
GROUP_SIZE = N_EXPERTS // N_GROUPS
TOPK_GROUPS = 4
ROUTED_SCALE = 2.5
DEPTH = 1
ALPHA = (2.0 * DEPTH) ** 0.25
LN_EPS = 1e-5

LANES = 128
SUBLANES = 8
TOKEN_ROWS = 16
SLAB_PITCH = 24
VMEM_LIMIT = 56 * 1024 * 1024

MOD_TN = 1024
INPROJ_TM = 512
INPROJ_TN = 1536
OUTPROJ_TM = 256
ROUTE_TM = 512
MOE_TM = 256
FINAL_TM = 128
IDX_CHUNK = 1024
ISSUE_UNROLL = 8

F32 = jnp.float32
BF16 = jnp.bfloat16


def _cparams(sem):
    return pltpu.CompilerParams(dimension_semantics=sem, vmem_limit_bytes=VMEM_LIMIT)


def _silu(v):
    return v * jax.nn.sigmoid(v)


def _layer_norm(y, g, b):
    mu = jnp.mean(y, axis=-1, keepdims=True)
    yc = y - mu
    var = jnp.mean(yc * yc, axis=-1, keepdims=True)
    return yc * lax.rsqrt(var + LN_EPS) * g + b


def _mod_kernel(c_ref, w_ref, b_ref, o_ref):
    cs = _silu(c_ref[...]).astype(BF16)
    o_ref[...] = jnp.dot(cs, w_ref[...].astype(BF16), preferred_element_type=F32) + b_ref[...]


def _mod(c8, w_mod, b_mod):
    d, n = w_mod.shape
    return pl.pallas_call(
        _mod_kernel,
        grid=(n // MOD_TN,),
        in_specs=[pl.BlockSpec((SUBLANES, d), lambda j: (0, 0)),
                  pl.BlockSpec((d, MOD_TN), lambda j: (0, j)),
                  pl.BlockSpec((1, MOD_TN), lambda j: (0, j))],
        out_specs=pl.BlockSpec((SUBLANES, MOD_TN), lambda j: (0, j)),
        out_shape=jax.ShapeDtypeStruct((SUBLANES, n), F32),
        compiler_params=_cparams(("arbitrary",)),
        name="mod",
    )(c8, w_mod, b_mod)


def _inproj_kernel(x_ref, mod_ref, w_ref, o_ref, h_ref):
    j = pl.program_id(1)

    @pl.when(j == 0)
    def _():
        m = mod_ref[0]
        h_ref[...] = (x_ref[...] * (1.0 + m[1:2]) + m[0:1]).astype(BF16)

    o_ref[...] = jnp.dot(h_ref[...], w_ref[0], preferred_element_type=F32).astype(BF16)


def _inproj(x2, mod3, w_in3, seq):
    n, d = x2.shape
    nj, _, tn = w_in3.shape
    tm = INPROJ_TM
    return pl.pallas_call(
        _inproj_kernel,
        grid=(n // tm, nj),
        in_specs=[pl.BlockSpec((tm, d), lambda i, j: (i, 0)),
                  pl.BlockSpec((1, 6, d), lambda i, j: (i * tm // seq, 0, 0)),
                  pl.BlockSpec((1, d, tn), lambda i, j: (j, 0, 0))],
        out_specs=pl.BlockSpec((tm, tn), lambda i, j: (i, j)),
        out_shape=jax.ShapeDtypeStruct((n, nj * tn), BF16),
        scratch_shapes=[pltpu.VMEM((tm, d), BF16)],
        compiler_params=_cparams(("arbitrary", "arbitrary")),
        name="inproj",
    )(x2, mod3, w_in3)


def _mixer_kernel(cur_ref, pk_ref, pv_ref, prow_ref, sink_ref, cw_ref, o_ref, *, attn_w, kv_w, conv_w):
    nblk = pl.program_id(1)
    has_prev = nblk > 0
    qb = Q_BLOCK
    cur = cur_ref[...]
    k_cur = cur[:, attn_w:attn_w + kv_w]
    v_cur = cur[:, attn_w + kv_w:attn_w + 2 * kv_w]
    k_all = jnp.concatenate([pk_ref[...], k_cur], axis=0)
    v_all = jnp.concatenate([pv_ref[...], v_cur], axis=0)

    rows = GQA * qb
    qi = lax.broadcasted_iota(jnp.int32, (rows, 2 * qb), 0) % qb
    kj = lax.broadcasted_iota(jnp.int32, (rows, 2 * qb), 1)
    dist = qi + qb - kj
    kmin = jnp.where(has_prev, 0, qb)
    valid = (dist >= 0) & (dist < WINDOW) & (kj >= kmin)
    distf = dist.astype(F32)
    head_in_group = lax.broadcasted_iota(jnp.int32, (rows, 1), 0) // qb
    sinks = sink_ref[...]

    outs = []
    for g in range(N_KV_HEADS):
        q4 = jnp.concatenate(
            [cur[:, (g * GQA + j) * HEAD_DIM:(g * GQA + j + 1) * HEAD_DIM] for j in range(GQA)], axis=0)
        kg = k_all[:, g * HEAD_DIM:(g + 1) * HEAD_DIM]
        vg = v_all[:, g * HEAD_DIM:(g + 1) * HEAD_DIM]
        s = lax.dot_general(q4, kg, (((1,), (1,)), ((), ())), preferred_element_type=F32)
        s = s * (HEAD_DIM ** -0.5)
        slope = jnp.zeros((rows, 1), F32)
        sink = jnp.zeros((rows, 1), F32)
        for j in range(GQA):
            h = g * GQA + j
            sel = head_in_group == j
            slope = jnp.where(sel, 2.0 ** (-8.0 * (h + 1) / N_Q_HEADS), slope)
            sink = jnp.where(sel, sinks[:, h:h + 1], sink)
        s = jnp.where(valid, s - slope * distf, -jnp.inf)
        m = jnp.maximum(jnp.max(s, axis=-1, keepdims=True), sink)
        p = jnp.exp(s - m)
        denom = jnp.sum(p, axis=-1, keepdims=True) + jnp.exp(sink - m)
        o4 = jnp.dot(p.astype(BF16), vg, preferred_element_type=F32) / denom
        outs.extend(o4[j * qb:(j + 1) * qb] for j in range(GQA))
    attn = jnp.concatenate(outs, axis=-1)

    c0 = attn_w + 2 * kv_w
    cb = cur[:, c0:c0 + conv_w].astype(F32)
    u = cur[:, c0 + conv_w:c0 + 2 * conv_w].astype(F32) * cur[:, c0 + 2 * conv_w:c0 + 3 * conv_w].astype(F32)
    prow = prow_ref[...]
    up = prow[:, c0 + conv_w:c0 + 2 * conv_w].astype(F32) * prow[:, c0 + 2 * conv_w:c0 + 3 * conv_w].astype(F32)
    up = up * jnp.where(has_prev, 1.0, 0.0)
    pm1 = up[15:16]
    pm2 = up[14:15]
    ri = lax.broadcasted_iota(jnp.int32, u.shape, 0)
    u1 = jnp.where(ri == 0, pm1, pltpu.roll(u, 1, 0))
    u2 = jnp.where(ri == 0, pm2, jnp.where(ri == 1, pm1, pltpu.roll(u, 2, 0)))
    cw = cw_ref[...]
    conv = cb * (cw[0:1] * u2 + cw[1:2] * u1 + cw[2:3] * u)
    o_ref[...] = jnp.concatenate([attn, conv], axis=-1).astype(BF16)


def _mixer(proj, sinks2, conv_w, batch, seq, attn_w, kv_w, conv_wd):
    n, in_w = proj.shape
    nb = seq // Q_BLOCK
    kv_blk0 = attn_w // kv_w
    sub16 = Q_BLOCK // 16

    def cur_map(b, i):
        return (b * nb + i, 0)

    def prev_map(col):
        return lambda b, i: (b * nb + jnp.maximum(i - 1, 0), col)

    def prow_map(b, i):
        return (jnp.maximum((b * nb + i) * sub16 - 1, 0), 0)

    kern = functools.partial(_mixer_kernel, attn_w=attn_w, kv_w=kv_w, conv_w=conv_wd)
    return pl.pallas_call(
        kern,
        grid=(batch, nb),
        in_specs=[pl.BlockSpec((Q_BLOCK, in_w), cur_map),
                  pl.BlockSpec((Q_BLOCK, kv_w), prev_map(kv_blk0)),
                  pl.BlockSpec((Q_BLOCK, kv_w), prev_map(kv_blk0 + 1)),
                  pl.BlockSpec((16, in_w), prow_map),
                  pl.BlockSpec((1, N_Q_HEADS), lambda b, i: (0, 0)),
                  pl.BlockSpec((CONV_K, conv_wd), lambda b, i: (0, 0))],
        out_specs=pl.BlockSpec((Q_BLOCK, attn_w + conv_wd), cur_map),
        out_shape=jax.ShapeDtypeStruct((n, attn_w + conv_wd), BF16),
        compiler_params=_cparams(("arbitrary", "arbitrary")),
        name="mixer",
    )(proj, proj, proj, proj, sinks2, conv_w)


def _split_bf16(v):
    hi = v.astype(BF16)
    lo = (v - hi.astype(F32)).astype(BF16)
    return hi, lo


def _outproj_kernel(mix_ref, x_ref, mod_ref, w_ref, g_ref, b_ref, wr_ref, x1_ref, h2_ref, lg_ref):
    m = mod_ref[0]
    mix = jnp.dot(mix_ref[...], w_ref[...], preferred_element_type=F32)
    x1 = _layer_norm(ALPHA * x_ref[...] + (1.0 + m[2:3]) * mix, g_ref[...], b_ref[...])
    x1_ref[...] = x1
    h2 = x1 * (1.0 + m[4:5]) + m[3:4]
    tm = h2.shape[0]
    for s in range(TOKEN_ROWS):
        h2_ref[pl.ds(s, tm, stride=TOKEN_ROWS), :] = h2[:, s * LANES:(s + 1) * LANES]
    h_hi, h_lo = _split_bf16(h2)
    w_hi, w_lo = _split_bf16(wr_ref[...])
    lg_ref[...] = (jnp.dot(h_hi, w_hi, preferred_element_type=F32)
                   + (jnp.dot(h_hi, w_lo, preferred_element_type=F32)
                      + jnp.dot(h_lo, w_hi, preferred_element_type=F32)))


def _outproj(mix, x2, mod3, w_out_bf, ln_g, ln_b, w_router, seq):
    n, d = x2.shape
    tm = OUTPROJ_TM
    ne = w_router.shape[1]
    row = lambda i: (i, 0)
    const = lambda i: (0, 0)
    return pl.pallas_call(
        _outproj_kernel,
        grid=(n // tm,),
        in_specs=[pl.BlockSpec((tm, d), row),
                  pl.BlockSpec((tm, d), row),
                  pl.BlockSpec((1, 6, d), lambda i: (i * tm // seq, 0, 0)),
                  pl.BlockSpec((d, d), const),
                  pl.BlockSpec((1, d), const),
                  pl.BlockSpec((1, d), const),
                  pl.BlockSpec((d, ne), const)],
        out_specs=[pl.BlockSpec((tm, d), row),
                   pl.BlockSpec((tm * TOKEN_ROWS, LANES), row),
                   pl.BlockSpec((tm, ne), row)],
        out_shape=[jax.ShapeDtypeStruct((n, d), F32),
                   jax.ShapeDtypeStruct((n * TOKEN_ROWS, LANES), F32),
                   jax.ShapeDtypeStruct((n, ne), F32)],
        compiler_params=_cparams(("arbitrary",)),
        name="outproj",
    )(mix, x2, mod3, w_out_bf, ln_g, ln_b, w_router)


def _first_argmax(v, lane_f):
    m = jnp.max(v, axis=-1, keepdims=True)
    idx = jnp.min(jnp.where(v == m, lane_f, float(N_EXPERTS)), axis=-1, keepdims=True)
    return m, idx


def _route_kernel(lg_ref, bias_ref, eidx_ref, w_ref, cnt_ref):
    scores = jax.nn.sigmoid(lg_ref[...])
    sel = scores + bias_ref[...]
    tm = sel.shape[0]
    lane = lax.broadcasted_iota(jnp.int32, (tm, N_EXPERTS), 1)
    lane_f = lane.astype(F32)
    grp = lane // GROUP_SIZE
    neg = -jnp.inf
    gs = []
    for g in range(N_GROUPS):
        vg = jnp.where(grp == g, sel, neg)
        m1, i1 = _first_argmax(vg, lane_f)
        m2 = jnp.max(jnp.where(lane_f == i1, neg, vg), axis=-1, keepdims=True)
        gs.append(m1 + m2)
    keep = jnp.zeros((tm, N_EXPERTS), F32)
    for g in range(N_GROUPS):
        rank = jnp.zeros((tm, 1), F32)
        for o in range(N_GROUPS):
            if o == g:
                continue
            ahead = (gs[o] >= gs[g]) if o < g else (gs[o] > gs[g])
            rank = rank + jnp.where(ahead, 1.0, 0.0)
        keep = jnp.where(grp == g, jnp.where(rank < TOPK_GROUPS, 1.0, 0.0), keep)
    cand = jnp.where(keep > 0.5, sel, neg)
    idxs, ws = [], []
    chosen = jnp.zeros((tm, N_EXPERTS), F32)
    for _ in range(TOP_K):
        _, ik = _first_argmax(cand, lane_f)
        hit = lane_f == ik
        ws.append(jnp.sum(jnp.where(hit, scores, 0.0), axis=-1, keepdims=True))
        idxs.append(ik)
        cand = jnp.where(hit, neg, cand)
        chosen = jnp.where(hit, 1.0, chosen)
    cnt_ref[0] = jnp.sum(chosen, axis=0, keepdims=True)
    wsum = ws[0]
    for k in range(1, TOP_K):
        wsum = wsum + ws[k]
    col = lax.broadcasted_iota(jnp.int32, (tm, TOP_K), 1)
    eidx = jnp.zeros((tm, TOP_K), F32)
    wout = jnp.zeros((tm, TOP_K), F32)
    for k in range(TOP_K):
        eidx = jnp.where(col == k, idxs[k], eidx)
        wout = jnp.where(col == k, ws[k] / wsum * ROUTED_SCALE, wout)
    eidx_ref[...] = eidx.astype(jnp.int32)
    w_ref[...] = wout


def _route(logits, bias2):
    n, ne = logits.shape
    tm = ROUTE_TM
    row = lambda i: (i, 0)
    return pl.pallas_call(
        _route_kernel,
        grid=(n // tm,),
        in_specs=[pl.BlockSpec((tm, ne), row), pl.BlockSpec((1, ne), lambda i: (0, 0))],
        out_specs=[pl.BlockSpec((tm, TOP_K), row), pl.BlockSpec((tm, TOP_K), row),
                   pl.BlockSpec((1, 1, ne), lambda i: (i, 0, 0))],
        out_shape=[jax.ShapeDtypeStruct((n, TOP_K), jnp.int32), jax.ShapeDtypeStruct((n, TOP_K), F32),
                   jax.ShapeDtypeStruct((n // tm, 1, ne), F32)],
        compiler_params=_cparams(("arbitrary",)),
        name="route",
    )(logits, bias2)


def _issue_rows(n_rows, issue_one):
    n_full = n_rows // ISSUE_UNROLL

    def chunk(c, carry):
        for u in range(ISSUE_UNROLL):
            issue_one(c * ISSUE_UNROLL + u)
        return carry

    def tail(r, carry):
        issue_one(r)
        return carry

    lax.fori_loop(0, n_full, chunk, 0)
    lax.fori_loop(n_full * ISSUE_UNROLL, n_rows, tail, 0)


def _moe_kernel(be_ref, new_ref, nv_ref, nused_ref, ids_hbm, roww_ref, h_hbm, wg_ref, wu_ref, wd_ref, ys_hbm,
                idx_s, xbuf, ybuf, wg_s, wu_s, wd_s, sem_i, sem_g, sem_s, *, n_tok):
    blk = pl.program_id(0)
    tm = MOE_TM
    per = IDX_CHUNK // tm
    per_log2 = per.bit_length() - 1
    topk_log2 = TOP_K.bit_length() - 1
    n_used = nused_ref[0]
    slab = TOKEN_ROWS

    buf_rows = tm * SLAB_PITCH

    def slab_at(buf, base, r):
        return buf.at[pl.ds(pl.multiple_of(base + r * SLAB_PITCH, SUBLANES), slab), :]

    def hbm_slab(ref, row):
        return ref.at[pl.ds(pl.multiple_of(row * slab, slab), slab), :]

    def ids_copy(b):
        c = b >> per_log2
        return pltpu.make_async_copy(
            ids_hbm.at[pl.ds(pl.multiple_of(c * IDX_CHUNK, IDX_CHUNK), IDX_CHUNK)],
            idx_s.at[pl.ds(pl.multiple_of((c & 1) * IDX_CHUNK, IDX_CHUNK), IDX_CHUNK)], sem_i)

    def ids_base(b):
        return ((b >> per_log2) & 1) * IDX_CHUNK + (b & (per - 1)) * tm

    def gather_block(b, slot):
        ibase = ids_base(b)
        xbase = slot * buf_rows

        def one(r):
            tok = idx_s[ibase + r] >> topk_log2
            pltpu.make_async_copy(hbm_slab(h_hbm, tok), slab_at(xbuf, xbase, r), sem_g.at[slot]).start()
        _issue_rows(nv_ref[b], one)

    def scatter_block(b, slot):
        ibase = ids_base(b)
        ybase = slot * buf_rows

        def one(r):
            rid = idx_s[ibase + r]
            tgt = (rid & (TOP_K - 1)) * n_tok + (rid >> topk_log2)
            pltpu.make_async_copy(slab_at(ybuf, ybase, r), hbm_slab(ys_hbm, tgt), sem_s.at[slot]).start()
        _issue_rows(nv_ref[b], one)

    def wait_rows(hbm, buf, sem, b, slot, to_hbm):
        rows = nv_ref[b] * slab
        v = buf.at[pl.ds(pl.multiple_of(slot * buf_rows, SUBLANES), rows), :]
        hv = hbm.at[pl.ds(0, rows), :]
        (pltpu.make_async_copy(v, hv, sem.at[slot]) if to_hbm else pltpu.make_async_copy(hv, v, sem.at[slot])).wait()

    @pl.when(blk < n_used)
    def _():
        slot = blk & 1

        @pl.when(blk == 0)
        def _():
            xbuf[...] = jnp.zeros(xbuf.shape, F32)
            first = ids_copy(0)
            first.start()
            first.wait()
            gather_block(0, 0)

        @pl.when(blk + 1 < n_used)
        def _():
            @pl.when(((blk + 1) & (per - 1)) == 0)
            def _():
                nxt = ids_copy(blk + 1)
                nxt.start()
                nxt.wait()
            gather_block(blk + 1, 1 - slot)

        @pl.when(new_ref[blk] == 1)
        def _():
            wg_s[...] = wg_ref[0].astype(BF16)
            wu_s[...] = wu_ref[0].astype(BF16)
            wd_s[...] = wd_ref[0].astype(BF16)

        wait_rows(h_hbm, xbuf, sem_g, blk, slot, to_hbm=False)

        @pl.when(blk >= 2)
        def _():
            wait_rows(ys_hbm, ybuf, sem_s, blk - 2, slot, to_hbm=True)

        base = slot * buf_rows
        x = jnp.concatenate([xbuf[pl.ds(base + s, tm, stride=SLAB_PITCH), :] for s in range(slab)],
                            axis=-1).astype(BF16)
        gate = jnp.dot(x, wg_s[...], preferred_element_type=F32)
        up = jnp.dot(x, wu_s[...], preferred_element_type=F32)
        act = (_silu(gate) * up).astype(BF16)
        y = jnp.dot(act, wd_s[...], preferred_element_type=F32) * roww_ref[...]
        for s in range(slab):
            ybuf[pl.ds(base + s, tm, stride=SLAB_PITCH), :] = y[:, s * LANES:(s + 1) * LANES]
        scatter_block(blk, slot)

        @pl.when(blk == n_used - 1)
        def _():
            @pl.when(blk >= 1)
            def _():
                wait_rows(ys_hbm, ybuf, sem_s, blk - 1, 1 - slot, to_hbm=True)
            wait_rows(ys_hbm, ybuf, sem_s, blk, slot, to_hbm=True)


def _moe(block_e, block_new, block_nv, n_used, row_ids, row_w, h2d, w_gate, w_up, w_down):
    n = h2d.shape[0] // TOKEN_ROWS
    d = TOKEN_ROWS * LANES
    de = w_gate.shape[2]
    tm = MOE_TM
    n_blk = block_e.shape[0]
    wmap = lambda i, be, nw, nv, nu: (be[i], 0, 0)
    grid_spec = pltpu.PrefetchScalarGridSpec(
        num_scalar_prefetch=4,
        grid=(n_blk,),
        in_specs=[pl.BlockSpec(memory_space=pl.ANY),
                  pl.BlockSpec((tm, 1), lambda i, be, nw, nv, nu: (i, 0)),
                  pl.BlockSpec(memory_space=pl.ANY),
                  pl.BlockSpec((1, d, de), wmap),
                  pl.BlockSpec((1, d, de), wmap),
                  pl.BlockSpec((1, de, d), wmap)],
        out_specs=pl.BlockSpec(memory_space=pl.ANY),
        scratch_shapes=[pltpu.SMEM((2 * IDX_CHUNK,), jnp.int32),
                        pltpu.VMEM((2 * tm * SLAB_PITCH, LANES), F32),
                        pltpu.VMEM((2 * tm * SLAB_PITCH, LANES), F32),
                        pltpu.VMEM((d, de), BF16),
                        pltpu.VMEM((d, de), BF16),
                        pltpu.VMEM((de, d), BF16),
                        pltpu.SemaphoreType.DMA,
                        pltpu.SemaphoreType.DMA((2,)),
                        pltpu.SemaphoreType.DMA((2,))])
    return pl.pallas_call(
        functools.partial(_moe_kernel, n_tok=n),
        grid_spec=grid_spec,
        out_shape=jax.ShapeDtypeStruct((TOP_K * n * TOKEN_ROWS, LANES), F32),
        compiler_params=_cparams(("arbitrary",)),
        name="moe",
    )(block_e, block_new, block_nv, n_used, row_ids, row_w, h2d, w_gate, w_up, w_down)


def _final_kernel(ys_ref, h_ref, x1_ref, mod_ref, wg_ref, wu_ref, wd_ref, g_ref, b_ref, o_ref, acc_ref):
    tm = x1_ref.shape[0]
    acc = ys_ref[0]
    for k in range(1, TOP_K):
        acc = acc + ys_ref[k]
    acc_ref[...] = acc
    rows = lambda ref, s: ref[pl.ds(s, tm, stride=TOKEN_ROWS), :]
    moe = jnp.concatenate([rows(acc_ref, s) for s in range(TOKEN_ROWS)], axis=-1)
    h = jnp.concatenate([rows(h_ref, s) for s in range(TOKEN_ROWS)], axis=-1).astype(BF16)
    gate = jnp.dot(h, wg_ref[...], preferred_element_type=F32)
    up = jnp.dot(h, wu_ref[...], preferred_element_type=F32)
    shared = jnp.dot((_silu(gate) * up).astype(BF16), wd_ref[...], preferred_element_type=F32)
    m = mod_ref[0]
    y = ALPHA * x1_ref[...] + (1.0 + m[5:6]) * (moe + shared)
    o_ref[...] = _layer_norm(y, g_ref[...], b_ref[...])


def _final(ys3, h2d, x1, mod3, wsg, wsu, wsd, ln_g, ln_b, seq):
    n, d = x1.shape
    de = wsg.shape[1]
    tm = FINAL_TM
    row = lambda i: (i, 0)
    const = lambda i: (0, 0)
    return pl.pallas_call(
        _final_kernel,
        grid=(n // tm,),
        in_specs=[pl.BlockSpec((TOP_K, tm * TOKEN_ROWS, LANES), lambda i: (0, i, 0)),
                  pl.BlockSpec((tm * TOKEN_ROWS, LANES), row),
                  pl.BlockSpec((tm, d), row),
                  pl.BlockSpec((1, 6, d), lambda i: (i * tm // seq, 0, 0)),
                  pl.BlockSpec((d, de), const),
                  pl.BlockSpec((d, de), const),
                  pl.BlockSpec((de, d), const),
                  pl.BlockSpec((1, d), const),
                  pl.BlockSpec((1, d), const)],
        out_specs=pl.BlockSpec((tm, d), row),
        out_shape=jax.ShapeDtypeStruct((n, d), F32),
        scratch_shapes=[pltpu.VMEM((tm * TOKEN_ROWS, LANES), F32)],
        compiler_params=_cparams(("arbitrary",)),
        name="final",
    )(ys3, h2d, x1, mod3, wsg, wsu, wsd, ln_g, ln_b)


def _dispatch_tables(eidx, gate_w, tile_counts, n):
    tm = MOE_TM
    a = n * TOP_K
    i32 = jnp.int32
    experts = jnp.arange(N_EXPERTS, dtype=i32)
    counts = jnp.sum(tile_counts, axis=(0, 1)).astype(i32)
    padded = (counts + tm - 1) // tm * tm
    pad_end = jnp.cumsum(padded)
    starts = pad_end - padded
    n_blk = a // tm + N_EXPERTS
    blk_start = jnp.arange(n_blk, dtype=i32) * tm
    n_used = pad_end[-1] // tm
    in_use = jnp.arange(n_blk) < n_used
    raw_e = jnp.minimum(jnp.sum((pad_end[None, :] <= blk_start[:, None]).astype(i32), axis=1), N_EXPERTS - 1)
    last_e = jnp.sum(jnp.where(jnp.arange(n_blk) == n_used - 1, raw_e, 0))
    block_e = jnp.where(in_use, raw_e, last_e)
    onehot = block_e[:, None] == experts[None, :]
    cnt_b = jnp.sum(jnp.where(onehot, counts[None, :], 0), axis=1)
    start_b = jnp.sum(jnp.where(onehot, starts[None, :], 0), axis=1)
    block_nv = jnp.where(in_use, jnp.clip(cnt_b - (blk_start - start_b), 0, tm), 0).astype(i32)
    block_new = jnp.concatenate([jnp.ones((1,), i32), (block_e[1:] != block_e[:-1]).astype(i32)])
    dummy_keys = jnp.where(jnp.arange(tm, dtype=i32)[None, :] < (padded - counts)[:, None],
                           experts[:, None], N_EXPERTS).reshape(-1)
    keys = jnp.concatenate([eidx.reshape(-1), dummy_keys])
    ids = jnp.concatenate([jnp.arange(a, dtype=i32), jnp.zeros((N_EXPERTS * tm,), i32)])
    wts = jnp.concatenate([gate_w.reshape(-1), jnp.zeros((N_EXPERTS * tm,), F32)])
    _, row_ids, row_w = lax.sort((keys, ids, wts), num_keys=1, is_stable=True)
    return (block_e.astype(i32), block_new, block_nv, n_used.astype(i32).reshape(1), row_ids,
            row_w.reshape(-1, 1))


def kernel(x, c, w_mod, b_mod, w_in, conv_w, attn_sinks, w_out, ln1_g, ln1_b, w_router, router_bias,
           w_gate, w_up, w_down, ws_gate, ws_up, ws_down, ln2_g, ln2_b):
    b, s, d = x.shape
    n = b * s
    attn_w = N_Q_HEADS * HEAD_DIM
    kv_w = N_KV_HEADS * HEAD_DIM
    conv_wd = d - attn_w
    in_w = attn_w + 2 * kv_w + 3 * conv_wd
    x2 = x.reshape(n, d)
    c8 = jnp.zeros((SUBLANES, d), F32).at[:b].set(c)
    for l in range(DEPTH):
        mod = _mod(c8, w_mod[l], b_mod[l].reshape(1, -1))[:b]
        mod3 = mod.reshape(b, 6, d)
        w_in3 = w_in[l].astype(BF16).reshape(d, in_w // INPROJ_TN, INPROJ_TN).transpose(1, 0, 2)
        proj = _inproj(x2, mod3, w_in3, s)
        mix = _mixer(proj, attn_sinks[l].reshape(1, -1), conv_w[l], b, s, attn_w, kv_w, conv_wd)
        x1, h2d, logits = _outproj(mix, x2, mod3, w_out[l].astype(BF16), ln1_g[l].reshape(1, -1),
                                   ln1_b[l].reshape(1, -1), w_router[l], s)
        eidx, gate_w, tile_counts = _route(logits, router_bias[l].reshape(1, -1))
        block_e, block_new, block_nv, n_used, row_ids, row_w = _dispatch_tables(eidx, gate_w, tile_counts, n)
        ys = _moe(block_e, block_new, block_nv, n_used, row_ids, row_w, h2d, w_gate[l], w_up[l], w_down[l])
        ys3 = ys.reshape(TOP_K, n * TOKEN_ROWS, LANES)
        x2 = _final(ys3, h2d, x1, mod3, ws_gate[l].astype(BF16), ws_up[l].astype(BF16), ws_down[l].astype(BF16),
                    ln2_g[l].reshape(1, -1), ln2_b[l].reshape(1, -1), s)
    return x2.reshape(b, s, d)
```

```python
import functools

import jax
import jax.numpy as jnp
from jax import lax
from jax.experimental import pallas as pl
from jax.experimental.pallas import tpu as pltpu

HEAD_DIM = 64
N_Q_HEADS = 16
N_KV_HEADS = 4
GQA = N_Q_HEADS // N_KV_HEADS
CONV_K = 3
WINDOW = 128
Q_BLOCK = 128
N_EXPERTS = 64
TOP_K = 8
N_GROUPS = 8
GROUP_SIZE = N_EXPERTS // N_GROUPS
TOPK_GROUPS = 4
ROUTED_SCALE = 2.5
DEPTH = 1
ALPHA = (2.0 * DEPTH) ** 0.25
LN_EPS = 1e-5

LANES = 128
SUBLANES = 8
TOKEN_ROWS = 16
SLAB_PITCH = 24
VMEM_LIMIT = 56 * 1024 * 1024

MOD_TN = 1024
INPROJ_TM = 512
INPROJ_TN = 1536
OUTPROJ_TM = 256
ROUTE_TM = 512
MOE_TM = 256
FINAL_TM = 128
IDX_CHUNK = 1024
ISSUE_UNROLL = 8

F32 = jnp.float32
BF16 = jnp.bfloat16


def _cparams(sem):
    return pltpu.CompilerParams(dimension_semantics=sem, vmem_limit_bytes=VMEM_LIMIT)


def _silu(v):
    return v * jax.nn.sigmoid(v)


def _layer_norm(y, g, b):
    mu = jnp.mean(y, axis=-1, keepdims=True)
    yc = y - mu
    var = jnp.mean(yc * yc, axis=-1, keepdims=True)
    return yc * lax.rsqrt(var + LN_EPS) * g + b


def _mod_kernel(c_ref, w_ref, b_ref, o_ref):
    cs = _silu(c_ref[...]).astype(BF16)
    o_ref[...] = jnp.dot(cs, w_ref[...].astype(BF16), preferred_element_type=F32) + b_ref[...]


def _mod(c8, w_mod, b_mod):
    d, n = w_mod.shape
    return pl.pallas_call(
        _mod_kernel,
        grid=(n // MOD_TN,),
        in_specs=[pl.BlockSpec((SUBLANES, d), lambda j: (0, 0)),
                  pl.BlockSpec((d, MOD_TN), lambda j: (0, j)),
                  pl.BlockSpec((1, MOD_TN), lambda j: (0, j))],
        out_specs=pl.BlockSpec((SUBLANES, MOD_TN), lambda j: (0, j)),
        out_shape=jax.ShapeDtypeStruct((SUBLANES, n), F32),
        compiler_params=_cparams(("arbitrary",)),
        name="mod",
    )(c8, w_mod, b_mod)


def _inproj_kernel(x_ref, mod_ref, w_ref, o_ref, h_ref):
    j = pl.program_id(1)

    @pl.when(j == 0)
    def _():
        m = mod_ref[0]
        h_ref[...] = (x_ref[...] * (1.0 + m[1:2]) + m[0:1]).astype(BF16)

    o_ref[...] = jnp.dot(h_ref[...], w_ref[0], preferred_element_type=F32).astype(BF16)


def _inproj(x2, mod3, w_in3, seq):
    n, d = x2.shape
    nj, _, tn = w_in3.shape
    tm = INPROJ_TM
    return pl.pallas_call(
        _inproj_kernel,
        grid=(n // tm, nj),
        in_specs=[pl.BlockSpec((tm, d), lambda i, j: (i, 0)),
                  pl.BlockSpec((1, 6, d), lambda i, j: (i * tm // seq, 0, 0)),
                  pl.BlockSpec((1, d, tn), lambda i, j: (j, 0, 0))],
        out_specs=pl.BlockSpec((tm, tn), lambda i, j: (i, j)),
        out_shape=jax.ShapeDtypeStruct((n, nj * tn), BF16),
        scratch_shapes=[pltpu.VMEM((tm, d), BF16)],
        compiler_params=_cparams(("arbitrary", "arbitrary")),
        name="inproj",
    )(x2, mod3, w_in3)


def _mixer_kernel(cur_ref, pk_ref, pv_ref, prow_ref, sink_ref, cw_ref, o_ref, *, attn_w, kv_w, conv_w):
    nblk = pl.program_id(1)
    has_prev = nblk > 0
    qb = Q_BLOCK
    cur = cur_ref[...]
    k_cur = cur[:, attn_w:attn_w + kv_w]
    v_cur = cur[:, attn_w + kv_w:attn_w + 2 * kv_w]
    k_all = jnp.concatenate([pk_ref[...], k_cur], axis=0)
    v_all = jnp.concatenate([pv_ref[...], v_cur], axis=0)

    rows = GQA * qb
    qi = lax.broadcasted_iota(jnp.int32, (rows, 2 * qb), 0) % qb
    kj = lax.broadcasted_iota(jnp.int32, (rows, 2 * qb), 1)
    dist = qi + qb - kj
    kmin = jnp.where(has_prev, 0, qb)
    valid = (dist >= 0) & (dist < WINDOW) & (kj >= kmin)
    distf = dist.astype(F32)
    head_in_group = lax.broadcasted_iota(jnp.int32, (rows, 1), 0) // qb
    sinks = sink_ref[...]

    outs = []
    for g in range(N_KV_HEADS):
        q4 = jnp.concatenate(
            [cur[:, (g * GQA + j) * HEAD_DIM:(g * GQA + j + 1) * HEAD_DIM] for j in range(GQA)], axis=0)
        kg = k_all[:, g * HEAD_DIM:(g + 1) * HEAD_DIM]
        vg = v_all[:, g * HEAD_DIM:(g + 1) * HEAD_DIM]
        s = lax.dot_general(q4, kg, (((1,), (1,)), ((), ())), preferred_element_type=F32)
        s = s * (HEAD_DIM ** -0.5)
        slope = jnp.zeros((rows, 1), F32)
        sink = jnp.zeros((rows, 1), F32)
        for j in range(GQA):
            h = g * GQA + j
            sel = head_in_group == j
            slope = jnp.where(sel, 2.0 ** (-8.0 * (h + 1) / N_Q_HEADS), slope)
            sink = jnp.where(sel, sinks[:, h:h + 1], sink)
        s = jnp.where(valid, s - slope * distf, -jnp.inf)
        m = jnp.maximum(jnp.max(s, axis=-1, keepdims=True), sink)
        p = jnp.exp(s - m)
        denom = jnp.sum(p, axis=-1, keepdims=True) + jnp.exp(sink - m)
        o4 = jnp.dot(p.astype(BF16), vg, preferred_element_type=F32) / denom
        outs.extend(o4[j * qb:(j + 1) * qb] for j in range(GQA))
    attn = jnp.concatenate(outs, axis=-1)

    c0 = attn_w + 2 * kv_w
    cb = cur[:, c0:c0 + conv_w].astype(F32)
    u = cur[:, c0 + conv_w:c0 + 2 * conv_w].astype(F32) * cur[:, c0 + 2 * conv_w:c0 + 3 * conv_w].astype(F32)
    prow = prow_ref[...]
    up = prow[:, c0 + conv_w:c0 + 2 * conv_w].astype(F32) * prow[:, c0 + 2 * conv_w:c0 + 3 * conv_w].astype(F32)
    up = up * jnp.where(has_prev, 1.0, 0.0)
    pm1 = up[15:16]
    pm2 = up[14:15]
    ri = lax.broadcasted_iota(jnp.int32, u.shape, 0)
    u1 = jnp.where(ri == 0, pm1, pltpu.roll(u, 1, 0))
    u2 = jnp.where(ri == 0, pm2, jnp.where(ri == 1, pm1, pltpu.roll(u, 2, 0)))
    cw = cw_ref[...]
    conv = cb * (cw[0:1] * u2 + cw[1:2] * u1 + cw[2:3] * u)
    o_ref[...] = jnp.concatenate([attn, conv], axis=-1).astype(BF16)


def _mixer(proj, sinks2, conv_w, batch, seq, attn_w, kv_w, conv_wd):
    n, in_w = proj.shape
    nb = seq // Q_BLOCK
    kv_blk0 = attn_w // kv_w
    sub16 = Q_BLOCK // 16

    def cur_map(b, i):
        return (b * nb + i, 0)

    def prev_map(col):
        return lambda b, i: (b * nb + jnp.maximum(i - 1, 0), col)

    def prow_map(b, i):
        return (jnp.maximum((b * nb + i) * sub16 - 1, 0), 0)

    kern = functools.partial(_mixer_kernel, attn_w=attn_w, kv_w=kv_w, conv_w=conv_wd)
    return pl.pallas_call(
        kern,
        grid=(batch, nb),
        in_specs=[pl.BlockSpec((Q_BLOCK, in_w), cur_map),
                  pl.BlockSpec((Q_BLOCK, kv_w), prev_map(kv_blk0)),
                  pl.BlockSpec((Q_BLOCK, kv_w), prev_map(kv_blk0 + 1)),
                  pl.BlockSpec((16, in_w), prow_map),
                  pl.BlockSpec((1, N_Q_HEADS), lambda b, i: (0, 0)),
                  pl.BlockSpec((CONV_K, conv_wd), lambda b, i: (0, 0))],
        out_specs=pl.BlockSpec((Q_BLOCK, attn_w + conv_wd), cur_map),
        out_shape=jax.ShapeDtypeStruct((n, attn_w + conv_wd), BF16),
        compiler_params=_cparams(("arbitrary", "arbitrary")),
        name="mixer",
    )(proj, proj, proj, proj, sinks2, conv_w)


def _split_bf16(v):
    hi = v.astype(BF16)
    lo = (v - hi.astype(F32)).astype(BF16)
    return hi, lo


def _outproj_kernel(mix_ref, x_ref, mod_ref, w_ref, g_ref, b_ref, wr_ref, x1_ref, h2_ref, lg_ref):
    m = mod_ref[0]
    mix = jnp.dot(mix_ref[...], w_ref[...], preferred_element_type=F32)
    x1 = _layer_norm(ALPHA * x_ref[...] + (1.0 + m[2:3]) * mix, g_ref[...], b_ref[...])
    x1_ref[...] = x1
    h2 = x1 * (1.0 + m[4:5]) + m[3:4]
    tm = h2.shape[0]
    for s in range(TOKEN_ROWS):
        h2_ref[pl.ds(s, tm, stride=TOKEN_ROWS), :] = h2[:, s * LANES:(s + 1) * LANES]
    h_hi, h_lo = _split_bf16(h2)
    w_hi, w_lo = _split_bf16(wr_ref[...])
    lg_ref[...] = (jnp.dot(h_hi, w_hi, preferred_element_type=F32)
                   + (jnp.dot(h_hi, w_lo, preferred_element_type=F32)
                      + jnp.dot(h_lo, w_hi, preferred_element_type=F32)))


def _outproj(mix, x2, mod3, w_out_bf, ln_g, ln_b, w_router, seq):
    n, d = x2.shape
    tm = OUTPROJ_TM
    ne = w_router.shape[1]
    row = lambda i: (i, 0)
    const = lambda i: (0, 0)
    return pl.pallas_call(
        _outproj_kernel,
        grid=(n // tm,),
        in_specs=[pl.BlockSpec((tm, d), row),
                  pl.BlockSpec((tm, d), row),
                  pl.BlockSpec((1, 6, d), lambda i: (i * tm // seq, 0, 0)),
                  pl.BlockSpec((d, d), const),
                  pl.BlockSpec((1, d), const),
                  pl.BlockSpec((1, d), const),
                  pl.BlockSpec((d, ne), const)],
        out_specs=[pl.BlockSpec((tm, d), row),
                   pl.BlockSpec((tm * TOKEN_ROWS, LANES), row),
                   pl.BlockSpec((tm, ne), row)],
        out_shape=[jax.ShapeDtypeStruct((n, d), F32),
                   jax.ShapeDtypeStruct((n * TOKEN_ROWS, LANES), F32),
                   jax.ShapeDtypeStruct((n, ne), F32)],
        compiler_params=_cparams(("arbitrary",)),
        name="outproj",
    )(mix, x2, mod3, w_out_bf, ln_g, ln_b, w_router)


def _first_argmax(v, lane_f):
    m = jnp.max(v, axis=-1, keepdims=True)
    idx = jnp.min(jnp.where(v == m, lane_f, float(N_EXPERTS)), axis=-1, keepdims=True)
    return m, idx


def _route_kernel(lg_ref, bias_ref, eidx_ref, w_ref, rank_ref, cnt_ref, carry_ref):
    scores = jax.nn.sigmoid(lg_ref[...])
    sel = scores + bias_ref[...]
    tm = sel.shape[0]
    lane = lax.broadcasted_iota(jnp.int32, (tm, N_EXPERTS), 1)
    lane_f = lane.astype(F32)
    grp = lane // GROUP_SIZE
    neg = -jnp.inf
    gs = []
    for g in range(N_GROUPS):
        vg = jnp.where(grp == g, sel, neg)
        m1, i1 = _first_argmax(vg, lane_f)
        m2 = jnp.max(jnp.where(lane_f == i1, neg, vg), axis=-1, keepdims=True)
        gs.append(m1 + m2)
    keep = jnp.zeros((tm, N_EXPERTS), F32)
    for g in range(N_GROUPS):
        rank = jnp.zeros((tm, 1), F32)
        for o in range(N_GROUPS):
            if o == g:
                continue
            ahead = (gs[o] >= gs[g]) if o < g else (gs[o] > gs[g])
            rank = rank + jnp.where(ahead, 1.0, 0.0)
        keep = jnp.where(grp == g, jnp.where(rank < TOPK_GROUPS, 1.0, 0.0), keep)
    cand = jnp.where(keep > 0.5, sel, neg)
    idxs, ws = [], []
    chosen = jnp.zeros((tm, N_EXPERTS), F32)
    for _ in range(TOP_K):
        _, ik = _first_argmax(cand, lane_f)
        hit = lane_f == ik
        ws.append(jnp.sum(jnp.where(hit, scores, 0.0), axis=-1, keepdims=True))
        idxs.append(ik)
        cand = jnp.where(hit, neg, cand)
        chosen = jnp.where(hit, 1.0, chosen)

    @pl.when(pl.program_id(0) == 0)
    def _():
        carry_ref[...] = jnp.zeros(carry_ref.shape, F32)

    ti = lax.broadcasted_iota(jnp.int32, (tm, tm), 0)
    tj = lax.broadcasted_iota(jnp.int32, (tm, tm), 1)
    earlier = jnp.where(tj < ti, 1.0, 0.0).astype(BF16)
    before = jnp.dot(earlier, chosen.astype(BF16), preferred_element_type=F32) + carry_ref[...]
    tile_cnt = jnp.sum(chosen, axis=0, keepdims=True)
    carry_ref[...] = carry_ref[...] + tile_cnt
    cnt_ref[0] = tile_cnt

    wsum = ws[0]
    for k in range(1, TOP_K):
        wsum = wsum + ws[k]
    col = lax.broadcasted_iota(jnp.int32, (tm, TOP_K), 1)
    eidx = jnp.zeros((tm, TOP_K), F32)
    wout = jnp.zeros((tm, TOP_K), F32)
    rank = jnp.zeros((tm, TOP_K), F32)
    for k in range(TOP_K):
        eidx = jnp.where(col == k, idxs[k], eidx)
        wout = jnp.where(col == k, ws[k] / wsum * ROUTED_SCALE, wout)
        rk = jnp.sum(jnp.where(lane_f == idxs[k], before, 0.0), axis=-1, keepdims=True)
        rank = jnp.where(col == k, rk, rank)
    eidx_ref[...] = eidx.astype(jnp.int32)
    w_ref[...] = wout
    rank_ref[...] = rank.astype(jnp.int32)


def _route(logits, bias2):
    n, ne = logits.shape
    tm = ROUTE_TM
    row = lambda i: (i, 0)
    return pl.pallas_call(
        _route_kernel,
        grid=(n // tm,),
        in_specs=[pl.BlockSpec((tm, ne), row), pl.BlockSpec((1, ne), lambda i: (0, 0))],
        out_specs=[pl.BlockSpec((tm, TOP_K), row), pl.BlockSpec((tm, TOP_K), row), pl.BlockSpec((tm, TOP_K), row),
                   pl.BlockSpec((1, 1, ne), lambda i: (i, 0, 0))],
        out_shape=[jax.ShapeDtypeStruct((n, TOP_K), jnp.int32), jax.ShapeDtypeStruct((n, TOP_K), F32),
                   jax.ShapeDtypeStruct((n, TOP_K), jnp.int32), jax.ShapeDtypeStruct((n // tm, 1, ne), F32)],
        scratch_shapes=[pltpu.VMEM((1, ne), F32)],
        compiler_params=_cparams(("arbitrary",)),
        name="route",
    )(logits, bias2)


def _issue_rows(lo, hi, issue_one):
    n_full = (hi - lo) // ISSUE_UNROLL

    def chunk(c, carry):
        for u in range(ISSUE_UNROLL):
            issue_one(lo + c * ISSUE_UNROLL + u)
        return carry

    def tail(r, carry):
        issue_one(r)
        return carry

    lax.fori_loop(0, n_full, chunk, 0)
    lax.fori_loop(lo + n_full * ISSUE_UNROLL, hi, tail, 0)


def _hbm_slab(ref, row):
    return ref.at[pl.ds(pl.multiple_of(row * TOKEN_ROWS, TOKEN_ROWS), TOKEN_ROWS), :]


def _moe_kernel(be_ref, new_ref, nv_ref, bo_ref, nused_ref, ids_hbm, h_hbm, wg_ref, wu_ref, wd_ref, ys_hbm,
                idx_s, xbuf, ybuf, wg_s, wu_s, wd_s, sem_i, sem_g, sem_s):
    blk = pl.program_id(0)
    tm = MOE_TM
    per = IDX_CHUNK // tm
    per_log2 = per.bit_length() - 1
    topk_log2 = TOP_K.bit_length() - 1
    n_used = nused_ref[0]
    slab = TOKEN_ROWS
    xrows = tm * SLAB_PITCH
    yrows = tm * slab

    def ids_copy(b):
        c = b >> per_log2
        return pltpu.make_async_copy(
            ids_hbm.at[pl.ds(pl.multiple_of(c * IDX_CHUNK, IDX_CHUNK), IDX_CHUNK)],
            idx_s.at[pl.ds(pl.multiple_of((c & 1) * IDX_CHUNK, IDX_CHUNK), IDX_CHUNK)], sem_i)

    def ids_base(b):
        return ((b >> per_log2) & 1) * IDX_CHUNK + (b & (per - 1)) * tm

    def gather_rows(b, slot, lo, hi):
        ibase = ids_base(b)
        xbase = slot * xrows

        def one(r):
            tok = idx_s[ibase + r] >> topk_log2
            dst = xbuf.at[pl.ds(pl.multiple_of(xbase + r * SLAB_PITCH, SUBLANES), slab), :]
            pltpu.make_async_copy(_hbm_slab(h_hbm, tok), dst, sem_g.at[slot]).start()
        _issue_rows(lo, hi, one)

    def wait_gather(b, slot):
        rows = nv_ref[b] * slab
        v = xbuf.at[pl.ds(pl.multiple_of(slot * xrows, SUBLANES), rows), :]
        pltpu.make_async_copy(h_hbm.at[pl.ds(0, rows), :], v, sem_g.at[slot]).wait()

    def result_copy(b, slot):
        rows = nv_ref[b] * slab
        src = ybuf.at[pl.ds(pl.multiple_of(slot * yrows, slab), rows), :]
        dst = ys_hbm.at[pl.ds(pl.multiple_of(bo_ref[b] * slab, slab), rows), :]
        return pltpu.make_async_copy(src, dst, sem_s.at[slot])

    @pl.when(blk < n_used)
    def _():
        slot = blk & 1
        has_next = blk + 1 < n_used
        nv_next = jnp.where(has_next, nv_ref[jnp.minimum(blk + 1, n_used - 1)], 0)
        half_next = (nv_next // (2 * ISSUE_UNROLL)) * ISSUE_UNROLL

        @pl.when(blk == 0)
        def _():
            xbuf[...] = jnp.zeros(xbuf.shape, F32)
            first = ids_copy(0)
            first.start()
            first.wait()
            gather_rows(0, 0, 0, nv_ref[0])

        @pl.when(has_next & (((blk + 1) & (per - 1)) == 0))
        def _():
            nxt = ids_copy(blk + 1)
            nxt.start()
            nxt.wait()

        gather_rows(blk + 1, 1 - slot, 0, half_next)

        @pl.when(new_ref[blk] == 1)
        def _():
            wg_s[...] = wg_ref[0].astype(BF16)
            wu_s[...] = wu_ref[0].astype(BF16)
            wd_s[...] = wd_ref[0].astype(BF16)

        wait_gather(blk, slot)
        xbase = slot * xrows
        x = jnp.concatenate([xbuf[pl.ds(xbase + s, tm, stride=SLAB_PITCH), :] for s in range(slab)],
                            axis=-1).astype(BF16)
        gate = jnp.dot(x, wg_s[...], preferred_element_type=F32)
        up = jnp.dot(x, wu_s[...], preferred_element_type=F32)
        act = (_silu(gate) * up).astype(BF16)

        gather_rows(blk + 1, 1 - slot, half_next, nv_next)

        @pl.when(blk >= 2)
        def _():
            result_copy(blk - 2, slot).wait()

        y = jnp.dot(act, wd_s[...], preferred_element_type=F32)
        ybase = slot * yrows
        for s in range(slab):
            ybuf[pl.ds(ybase + s, tm, stride=slab), :] = y[:, s * LANES:(s + 1) * LANES]
        result_copy(blk, slot).start()

        @pl.when(blk == n_used - 1)
        def _():
            @pl.when(blk >= 1)
            def _():
                result_copy(blk - 1, 1 - slot).wait()
            result_copy(blk, slot).wait()


def _moe(block_e, block_new, block_nv, block_out, n_used, row_ids, h2d, w_gate, w_up, w_down):
    n = h2d.shape[0] // TOKEN_ROWS
    d = TOKEN_ROWS * LANES
    de = w_gate.shape[2]
    tm = MOE_TM
    n_blk = block_e.shape[0]
    wmap = lambda i, be, nw, nv, bo, nu: (be[i], 0, 0)
    grid_spec = pltpu.PrefetchScalarGridSpec(
        num_scalar_prefetch=5,
        grid=(n_blk,),
        in_specs=[pl.BlockSpec(memory_space=pl.ANY),
                  pl.BlockSpec(memory_space=pl.ANY),
                  pl.BlockSpec((1, d, de), wmap),
                  pl.BlockSpec((1, d, de), wmap),
                  pl.BlockSpec((1, de, d), wmap)],
        out_specs=pl.BlockSpec(memory_space=pl.ANY),
        scratch_shapes=[pltpu.SMEM((2 * IDX_CHUNK,), jnp.int32),
                        pltpu.VMEM((2 * tm * SLAB_PITCH, LANES), F32),
                        pltpu.VMEM((2 * tm * TOKEN_ROWS, LANES), F32),
                        pltpu.VMEM((d, de), BF16),
                        pltpu.VMEM((d, de), BF16),
                        pltpu.VMEM((de, d), BF16),
                        pltpu.SemaphoreType.DMA,
                        pltpu.SemaphoreType.DMA((2,)),
                        pltpu.SemaphoreType.DMA((2,))])
    return pl.pallas_call(
        _moe_kernel,
        grid_spec=grid_spec,
        out_shape=jax.ShapeDtypeStruct((TOP_K * n * TOKEN_ROWS, LANES), F32),
        compiler_params=_cparams(("arbitrary",)),
        name="moe",
    )(block_e, block_new, block_nv, block_out, n_used, row_ids, h2d, w_gate, w_up, w_down)


def _final_kernel(pos_hbm, gw_hbm, ys_hbm, h_ref, x1_ref, mod_ref, wg_ref, wu_ref, wd_ref, g_ref, b_ref, o_ref,
                  pos_s, gw_s, gbuf, acc_ref, sem_i, sem_g):
    i = pl.program_id(0)
    tm = x1_ref.shape[0]
    per_step = tm * TOP_K
    grows = per_step * TOKEN_ROWS
    slot = i & 1
    has_next = i + 1 < pl.num_programs(0)

    def idx_copies(step):
        src = pl.ds(pl.multiple_of(step * per_step, per_step), per_step)
        dst = pl.ds(pl.multiple_of((step & 1) * per_step, per_step), per_step)
        return (pltpu.make_async_copy(pos_hbm.at[src], pos_s.at[dst], sem_i.at[0]),
                pltpu.make_async_copy(gw_hbm.at[src], gw_s.at[dst], sem_i.at[1]))

    def fetch_idx(step):
        for cp in idx_copies(step):
            cp.start()
        for cp in idx_copies(step):
            cp.wait()

    def gather(step, lo, hi):
        s2 = step & 1
        ibase = s2 * per_step
        gbase = s2 * grows

        def one(j):
            dst = gbuf.at[pl.ds(pl.multiple_of(gbase + j * TOKEN_ROWS, TOKEN_ROWS), TOKEN_ROWS), :]
            pltpu.make_async_copy(_hbm_slab(ys_hbm, pos_s[ibase + j]), dst, sem_g.at[s2]).start()
        _issue_rows(lo, hi, one)

    @pl.when(i == 0)
    def _():
        fetch_idx(0)
        gather(0, 0, per_step)

    @pl.when(has_next)
    def _():
        fetch_idx(i + 1)
        gather(i + 1, 0, per_step // 2)

    gbase = slot * grows
    pltpu.make_async_copy(ys_hbm.at[pl.ds(0, grows), :],
                          gbuf.at[pl.ds(pl.multiple_of(gbase, TOKEN_ROWS), grows), :], sem_g.at[slot]).wait()

    wbase = slot * per_step

    def combine(t, carry):
        acc = None
        for k in range(TOP_K):
            j = t * TOP_K + k
            row = gbuf[pl.ds(pl.multiple_of(gbase + j * TOKEN_ROWS, TOKEN_ROWS), TOKEN_ROWS), :]
            term = gw_s[wbase + j] * row
            acc = term if acc is None else acc + term
        acc_ref[pl.ds(pl.multiple_of(t * SLAB_PITCH, SUBLANES), TOKEN_ROWS), :] = acc
        return carry

    lax.fori_loop(0, tm, combine, 0, unroll=4)

    @pl.when(has_next)
    def _():
        gather(i + 1, per_step // 2, per_step)

    moe = jnp.concatenate([acc_ref[pl.ds(s, tm, stride=SLAB_PITCH), :] for s in range(TOKEN_ROWS)], axis=-1)
    h = jnp.concatenate([h_ref[pl.ds(s, tm, stride=TOKEN_ROWS), :] for s in range(TOKEN_ROWS)],
                        axis=-1).astype(BF16)
    gate = jnp.dot(h, wg_ref[...], preferred_element_type=F32)
    up = jnp.dot(h, wu_ref[...], preferred_element_type=F32)
    shared = jnp.dot((_silu(gate) * up).astype(BF16), wd_ref[...], preferred_element_type=F32)
    m = mod_ref[0]
    y = ALPHA * x1_ref[...] + (1.0 + m[5:6]) * (moe + shared)
    o_ref[...] = _layer_norm(y, g_ref[...], b_ref[...])


def _final(pos, gate_w, ys, h2d, x1, mod3, wsg, wsu, wsd, ln_g, ln_b, seq):
    n, d = x1.shape
    de = wsg.shape[1]
    tm = FINAL_TM
    per_step = tm * TOP_K
    assert per_step % IDX_CHUNK == 0
    row = lambda i: (i, 0)
    const = lambda i: (0, 0)
    return pl.pallas_call(
        _final_kernel,
        grid=(n // tm,),
        in_specs=[pl.BlockSpec(memory_space=pl.ANY),
                  pl.BlockSpec(memory_space=pl.ANY),
                  pl.BlockSpec(memory_space=pl.ANY),
                  pl.BlockSpec((tm * TOKEN_ROWS, LANES), row),
                  pl.BlockSpec((tm, d), row),
                  pl.BlockSpec((1, 6, d), lambda i: (i * tm // seq, 0, 0)),
                  pl.BlockSpec((d, de), const),
                  pl.BlockSpec((d, de), const),
                  pl.BlockSpec((de, d), const),
                  pl.BlockSpec((1, d), const),
                  pl.BlockSpec((1, d), const)],
        out_specs=pl.BlockSpec((tm, d), row),
        out_shape=jax.ShapeDtypeStruct((n, d), F32),
        scratch_shapes=[pltpu.SMEM((2 * per_step,), jnp.int32),
                        pltpu.SMEM((2 * per_step,), F32),
                        pltpu.VMEM((2 * per_step * TOKEN_ROWS, LANES), F32),
                        pltpu.VMEM((tm * SLAB_PITCH, LANES), F32),
                        pltpu.SemaphoreType.DMA((2,)),
                        pltpu.SemaphoreType.DMA((2,))],
        compiler_params=_cparams(("arbitrary",)),
        name="final",
    )(pos, gate_w, ys, h2d, x1, mod3, wsg, wsu, wsd, ln_g, ln_b)


def _dispatch_tables(eidx, rank, tile_counts, n):
    tm = MOE_TM
    a = n * TOP_K
    i32 = jnp.int32
    experts = jnp.arange(N_EXPERTS, dtype=i32)
    counts = jnp.sum(tile_counts, axis=(0, 1)).astype(i32)
    offsets = jnp.cumsum(counts) - counts
    padded = (counts + tm - 1) // tm * tm
    pad_end = jnp.cumsum(padded)
    starts = pad_end - padded
    n_blk = a // tm + N_EXPERTS
    blk_start = jnp.arange(n_blk, dtype=i32) * tm
    n_used = pad_end[-1] // tm
    in_use = jnp.arange(n_blk) < n_used
    raw_e = jnp.minimum(jnp.sum((pad_end[None, :] <= blk_start[:, None]).astype(i32), axis=1), N_EXPERTS - 1)
    last_e = jnp.sum(jnp.where(jnp.arange(n_blk) == n_used - 1, raw_e, 0))
    block_e = jnp.where(in_use, raw_e, last_e)
    onehot = block_e[:, None] == experts[None, :]
    cnt_b = jnp.sum(jnp.where(onehot, counts[None, :], 0), axis=1)
    start_b = jnp.sum(jnp.where(onehot, starts[None, :], 0), axis=1)
    off_b = jnp.sum(jnp.where(onehot, offsets[None, :], 0), axis=1)
    first_row = blk_start - start_b
    block_nv = jnp.where(in_use, jnp.clip(cnt_b - first_row, 0, tm), 0).astype(i32)
    block_out = jnp.where(in_use, off_b + first_row, 0).astype(i32)
    block_new = jnp.concatenate([jnp.ones((1,), i32), (block_e[1:] != block_e[:-1]).astype(i32)])
    dummy_keys = jnp.where(jnp.arange(tm, dtype=i32)[None, :] < (padded - counts)[:, None],
                           experts[:, None], N_EXPERTS).reshape(-1)
    keys = jnp.concatenate([eidx.reshape(-1), dummy_keys])
    ids = jnp.concatenate([jnp.arange(a, dtype=i32), jnp.zeros((N_EXPERTS * tm,), i32)])
    _, row_ids = lax.sort((keys, ids), num_keys=1, is_stable=True)
    pos = jnp.sum(jnp.where(eidx[:, :, None] == experts[None, None, :], offsets[None, None, :], 0), axis=-1) + rank
    return (block_e.astype(i32), block_new, block_nv, block_out, n_used.astype(i32).reshape(1), row_ids,
            pos.reshape(-1).astype(i32))


def kernel(x, c, w_mod, b_mod, w_in, conv_w, attn_sinks, w_out, ln1_g, ln1_b, w_router, router_bias,
           w_gate, w_up, w_down, ws_gate, ws_up, ws_down, ln2_g, ln2_b):
    b, s, d = x.shape
    n = b * s
    attn_w = N_Q_HEADS * HEAD_DIM
    kv_w = N_KV_HEADS * HEAD_DIM
    conv_wd = d - attn_w
    in_w = attn_w + 2 * kv_w + 3 * conv_wd
    x2 = x.reshape(n, d)
    c8 = jnp.zeros((SUBLANES, d), F32).at[:b].set(c)
    for l in range(DEPTH):
        mod = _mod(c8, w_mod[l], b_mod[l].reshape(1, -1))[:b]
        mod3 = mod.reshape(b, 6, d)
        w_in3 = w_in[l].astype(BF16).reshape(d, in_w // INPROJ_TN, INPROJ_TN).transpose(1, 0, 2)
        proj = _inproj(x2, mod3, w_in3, s)
        mix = _mixer(proj, attn_sinks[l].reshape(1, -1), conv_w[l], b, s, attn_w, kv_w, conv_wd)
        x1, h2d, logits = _outproj(mix, x2, mod3, w_out[l].astype(BF16), ln1_g[l].reshape(1, -1),
                                   ln1_b[l].reshape(1, -1), w_router[l], s)
        eidx, gate_w, rank, tile_counts = _route(logits, router_bias[l].reshape(1, -1))
        block_e, block_new, block_nv, block_out, n_used, row_ids, pos = _dispatch_tables(eidx, rank, tile_counts, n)
        ys = _moe(block_e, block_new, block_nv, block_out, n_used, row_ids, h2d, w_gate[l], w_up[l], w_down[l])
        x2 = _final(pos, gate_w.reshape(-1), ys, h2d, x1, mod3, ws_gate[l].astype(BF16), ws_up[l].astype(BF16),
                    ws_down[l].astype(BF16), ln2_g[l].reshape(1, -1), ln2_b[l].reshape(1, -1), s)
    return x2.reshape(b, s, d)
```

```python
import functools

import jax
import jax.numpy as jnp
from jax import lax
from jax.experimental import pallas as pl
from jax.experimental.pallas import tpu as pltpu

HEAD_DIM = 64
N_Q_HEADS = 16
N_KV_HEADS = 4
GQA = N_Q_HEADS // N_KV_HEADS
CONV_K = 3
WINDOW = 128
Q_BLOCK = 128
N_EXPERTS = 64
TOP_K = 8
N_GROUPS = 8
GROUP_SIZE = N_EXPERTS // N_GROUPS
TOPK_GROUPS = 4
ROUTED_SCALE = 2.5
DEPTH = 1
ALPHA = (2.0 * DEPTH) ** 0.25
LN_EPS = 1e-5

LANES = 128
SUBLANES = 8
TOKEN_ROWS = 16
SLAB_PITCH = 24
VMEM_LIMIT = 56 * 1024 * 1024

MOD_TN = 1024
INPROJ_TM = 512
INPROJ_TN = 1536
OUTPROJ_TM = 256
ROUTE_TM = 512
MOE_TM = 256
FINAL_TM = 128
IDX_CHUNK = 1024
ISSUE_UNROLL = 8

F32 = jnp.float32
BF16 = jnp.bfloat16


def _cparams(sem):
    return pltpu.CompilerParams(dimension_semantics=sem, vmem_limit_bytes=VMEM_LIMIT)


def _silu(v):
    return v * jax.nn.sigmoid(v)


def _layer_norm(y, g, b):
    mu = jnp.mean(y, axis=-1, keepdims=True)
    yc = y - mu
    var = jnp.mean(yc * yc, axis=-1, keepdims=True)
    return yc * lax.rsqrt(var + LN_EPS) * g + b


def _mod_kernel(c_ref, w_ref, b_ref, o_ref):
    cs = _silu(c_ref[...]).astype(BF16)
    o_ref[...] = jnp.dot(cs, w_ref[...].astype(BF16), preferred_element_type=F32) + b_ref[...]


def _mod(c8, w_mod, b_mod):
    d, n = w_mod.shape
    return pl.pallas_call(
        _mod_kernel,
        grid=(n // MOD_TN,),
        in_specs=[pl.BlockSpec((SUBLANES, d), lambda j: (0, 0)),
                  pl.BlockSpec((d, MOD_TN), lambda j: (0, j)),
                  pl.BlockSpec((1, MOD_TN), lambda j: (0, j))],
        out_specs=pl.BlockSpec((SUBLANES, MOD_TN), lambda j: (0, j)),
        out_shape=jax.ShapeDtypeStruct((SUBLANES, n), F32),
        compiler_params=_cparams(("arbitrary",)),
        name="mod",
    )(c8, w_mod, b_mod)


def _inproj_kernel(x_ref, mod_ref, w_ref, o_ref, h_ref):
    j = pl.program_id(1)

    @pl.when(j == 0)
    def _():
        m = mod_ref[0]
        h_ref[...] = (x_ref[...] * (1.0 + m[1:2]) + m[0:1]).astype(BF16)

    o_ref[...] = jnp.dot(h_ref[...], w_ref[0], preferred_element_type=F32).astype(BF16)


def _inproj(x2, mod3, w_in3, seq):
    n, d = x2.shape
    nj, _, tn = w_in3.shape
    tm = INPROJ_TM
    return pl.pallas_call(
        _inproj_kernel,
        grid=(n // tm, nj),
        in_specs=[pl.BlockSpec((tm, d), lambda i, j: (i, 0)),
                  pl.BlockSpec((1, 6, d), lambda i, j: (i * tm // seq, 0, 0)),
                  pl.BlockSpec((1, d, tn), lambda i, j: (j, 0, 0))],
        out_specs=pl.BlockSpec((tm, tn), lambda i, j: (i, j)),
        out_shape=jax.ShapeDtypeStruct((n, nj * tn), BF16),
        scratch_shapes=[pltpu.VMEM((tm, d), BF16)],
        compiler_params=_cparams(("arbitrary", "arbitrary")),
        name="inproj",
    )(x2, mod3, w_in3)


def _mixer_kernel(cur_ref, pk_ref, pv_ref, prow_ref, sink_ref, cw_ref, o_ref, *, attn_w, kv_w, conv_w):
    nblk = pl.program_id(1)
    has_prev = nblk > 0
    qb = Q_BLOCK
    cur = cur_ref[...]
    k_cur = cur[:, attn_w:attn_w + kv_w]
    v_cur = cur[:, attn_w + kv_w:attn_w + 2 * kv_w]
    k_all = jnp.concatenate([pk_ref[...], k_cur], axis=0)
    v_all = jnp.concatenate([pv_ref[...], v_cur], axis=0)

    rows = GQA * qb
    qi = lax.broadcasted_iota(jnp.int32, (rows, 2 * qb), 0) % qb
    kj = lax.broadcasted_iota(jnp.int32, (rows, 2 * qb), 1)
    dist = qi + qb - kj
    kmin = jnp.where(has_prev, 0, qb)
    valid = (dist >= 0) & (dist < WINDOW) & (kj >= kmin)
    distf = dist.astype(F32)
    head_in_group = lax.broadcasted_iota(jnp.int32, (rows, 1), 0) // qb
    sinks = sink_ref[...]

    outs = []
    for g in range(N_KV_HEADS):
        q4 = jnp.concatenate(
            [cur[:, (g * GQA + j) * HEAD_DIM:(g * GQA + j + 1) * HEAD_DIM] for j in range(GQA)], axis=0)
        kg = k_all[:, g * HEAD_DIM:(g + 1) * HEAD_DIM]
        vg = v_all[:, g * HEAD_DIM:(g + 1) * HEAD_DIM]
        s = lax.dot_general(q4, kg, (((1,), (1,)), ((), ())), preferred_element_type=F32)
        s = s * (HEAD_DIM ** -0.5)
        slope = jnp.zeros((rows, 1), F32)
        sink = jnp.zeros((rows, 1), F32)
        for j in range(GQA):
            h = g * GQA + j
            sel = head_in_group == j
            slope = jnp.where(sel, 2.0 ** (-8.0 * (h + 1) / N_Q_HEADS), slope)
            sink = jnp.where(sel, sinks[:, h:h + 1], sink)
        s = jnp.where(valid, s - slope * distf, -jnp.inf)
        m = jnp.maximum(jnp.max(s, axis=-1, keepdims=True), sink)
        p = jnp.exp(s - m)
        denom = jnp.sum(p, axis=-1, keepdims=True) + jnp.exp(sink - m)
        o4 = jnp.dot(p.astype(BF16), vg, preferred_element_type=F32) / denom
        outs.extend(o4[j * qb:(j + 1) * qb] for j in range(GQA))
    attn = jnp.concatenate(outs, axis=-1)

    c0 = attn_w + 2 * kv_w
    cb = cur[:, c0:c0 + conv_w].astype(F32)
    u = cur[:, c0 + conv_w:c0 + 2 * conv_w].astype(F32) * cur[:, c0 + 2 * conv_w:c0 + 3 * conv_w].astype(F32)
    prow = prow_ref[...]
    up = prow[:, c0 + conv_w:c0 + 2 * conv_w].astype(F32) * prow[:, c0 + 2 * conv_w:c0 + 3 * conv_w].astype(F32)
    up = up * jnp.where(has_prev, 1.0, 0.0)
    pm1 = up[15:16]
    pm2 = up[14:15]
    ri = lax.broadcasted_iota(jnp.int32, u.shape, 0)
    u1 = jnp.where(ri == 0, pm1, pltpu.roll(u, 1, 0))
    u2 = jnp.where(ri == 0, pm2, jnp.where(ri == 1, pm1, pltpu.roll(u, 2, 0)))
    cw = cw_ref[...]
    conv = cb * (cw[0:1] * u2 + cw[1:2] * u1 + cw[2:3] * u)
    o_ref[...] = jnp.concatenate([attn, conv], axis=-1).astype(BF16)


def _mixer(proj, sinks2, conv_w, batch, seq, attn_w, kv_w, conv_wd):
    n, in_w = proj.shape
    nb = seq // Q_BLOCK
    kv_blk0 = attn_w // kv_w
    sub16 = Q_BLOCK // 16

    def cur_map(b, i):
        return (b * nb + i, 0)

    def prev_map(col):
        return lambda b, i: (b * nb + jnp.maximum(i - 1, 0), col)

    def prow_map(b, i):
        return (jnp.maximum((b * nb + i) * sub16 - 1, 0), 0)

    kern = functools.partial(_mixer_kernel, attn_w=attn_w, kv_w=kv_w, conv_w=conv_wd)
    return pl.pallas_call(
        kern,
        grid=(batch, nb),
        in_specs=[pl.BlockSpec((Q_BLOCK, in_w), cur_map),
                  pl.BlockSpec((Q_BLOCK, kv_w), prev_map(kv_blk0)),
                  pl.BlockSpec((Q_BLOCK, kv_w), prev_map(kv_blk0 + 1)),
                  pl.BlockSpec((16, in_w), prow_map),
                  pl.BlockSpec((1, N_Q_HEADS), lambda b, i: (0, 0)),
                  pl.BlockSpec((CONV_K, conv_wd), lambda b, i: (0, 0))],
        out_specs=pl.BlockSpec((Q_BLOCK, attn_w + conv_wd), cur_map),
        out_shape=jax.ShapeDtypeStruct((n, attn_w + conv_wd), BF16),
        compiler_params=_cparams(("arbitrary", "arbitrary")),
        name="mixer",
    )(proj, proj, proj, proj, sinks2, conv_w)


def _split_bf16(v):
    hi = v.astype(BF16)
    lo = (v - hi.astype(F32)).astype(BF16)
    return hi, lo


def _outproj_kernel(mix_ref, x_ref, mod_ref, w_ref, g_ref, b_ref, wr_ref, x1_ref, h2_ref, lg_ref):
    m = mod_ref[0]
    mix = jnp.dot(mix_ref[...], w_ref[...], preferred_element_type=F32)
    x1 = _layer_norm(ALPHA * x_ref[...] + (1.0 + m[2:3]) * mix, g_ref[...], b_ref[...])
    x1_ref[...] = x1
    h2 = x1 * (1.0 + m[4:5]) + m[3:4]
    tm = h2.shape[0]
    for s in range(TOKEN_ROWS):
        h2_ref[pl.ds(s, tm, stride=TOKEN_ROWS), :] = h2[:, s * LANES:(s + 1) * LANES]
    h_hi, h_lo = _split_bf16(h2)
    w_hi, w_lo = _split_bf16(wr_ref[...])
    lg_ref[...] = (jnp.dot(h_hi, w_hi, preferred_element_type=F32)
                   + (jnp.dot(h_hi, w_lo, preferred_element_type=F32)
                      + jnp.dot(h_lo, w_hi, preferred_element_type=F32)))


def _outproj(mix, x2, mod3, w_out_bf, ln_g, ln_b, w_router, seq):
    n, d = x2.shape
    tm = OUTPROJ_TM
    ne = w_router.shape[1]
    row = lambda i: (i, 0)
    const = lambda i: (0, 0)
    return pl.pallas_call(
        _outproj_kernel,
        grid=(n // tm,),
        in_specs=[pl.BlockSpec((tm, d), row),
                  pl.BlockSpec((tm, d), row),
                  pl.BlockSpec((1, 6, d), lambda i: (i * tm // seq, 0, 0)),
                  pl.BlockSpec((d, d), const),
                  pl.BlockSpec((1, d), const),
                  pl.BlockSpec((1, d), const),
                  pl.BlockSpec((d, ne), const)],
        out_specs=[pl.BlockSpec((tm, d), row),
                   pl.BlockSpec((tm * TOKEN_ROWS, LANES), row),
                   pl.BlockSpec((tm, ne), row)],
        out_shape=[jax.ShapeDtypeStruct((n, d), F32),
                   jax.ShapeDtypeStruct((n * TOKEN_ROWS, LANES), F32),
                   jax.ShapeDtypeStruct((n, ne), F32)],
        compiler_params=_cparams(("arbitrary",)),
        name="outproj",
    )(mix, x2, mod3, w_out_bf, ln_g, ln_b, w_router)


def _first_argmax(v, lane_f):
    m = jnp.max(v, axis=-1, keepdims=True)
    idx = jnp.min(jnp.where(v == m, lane_f, float(N_EXPERTS)), axis=-1, keepdims=True)
    return m, idx


def _route_kernel(lg_ref, bias_ref, eidx_ref, w_ref, rank_ref, cnt_ref, carry_ref):
    scores = jax.nn.sigmoid(lg_ref[...])
    sel = scores + bias_ref[...]
    tm = sel.shape[0]
    lane = lax.broadcasted_iota(jnp.int32, (tm, N_EXPERTS), 1)
    lane_f = lane.astype(F32)
    grp = lane // GROUP_SIZE
    neg = -jnp.inf
    gs = []
    for g in range(N_GROUPS):
        vg = jnp.where(grp == g, sel, neg)
        m1, i1 = _first_argmax(vg, lane_f)
        m2 = jnp.max(jnp.where(lane_f == i1, neg, vg), axis=-1, keepdims=True)
        gs.append(m1 + m2)
    keep = jnp.zeros((tm, N_EXPERTS), F32)
    for g in range(N_GROUPS):
        rank = jnp.zeros((tm, 1), F32)
        for o in range(N_GROUPS):
            if o == g:
                continue
            ahead = (gs[o] >= gs[g]) if o < g else (gs[o] > gs[g])
            rank = rank + jnp.where(ahead, 1.0, 0.0)
        keep = jnp.where(grp == g, jnp.where(rank < TOPK_GROUPS, 1.0, 0.0), keep)
    cand = jnp.where(keep > 0.5, sel, neg)
    idxs, ws = [], []
    chosen = jnp.zeros((tm, N_EXPERTS), F32)
    for _ in range(TOP_K):
        _, ik = _first_argmax(cand, lane_f)
        hit = lane_f == ik
        ws.append(jnp.sum(jnp.where(hit, scores, 0.0), axis=-1, keepdims=True))
        idxs.append(ik)
        cand = jnp.where(hit, neg, cand)
        chosen = jnp.where(hit, 1.0, chosen)

    @pl.when(pl.program_id(0) == 0)
    def _():
        carry_ref[...] = jnp.zeros(carry_ref.shape, F32)

    ti = lax.broadcasted_iota(jnp.int32, (tm, tm), 0)
    tj = lax.broadcasted_iota(jnp.int32, (tm, tm), 1)
    earlier = jnp.where(tj < ti, 1.0, 0.0).astype(BF16)
    before = jnp.dot(earlier, chosen.astype(BF16), preferred_element_type=F32) + carry_ref[...]
    tile_cnt = jnp.sum(chosen, axis=0, keepdims=True)
    carry_ref[...] = carry_ref[...] + tile_cnt
    cnt_ref[0] = tile_cnt

    wsum = ws[0]
    for k in range(1, TOP_K):
        wsum = wsum + ws[k]
    col = lax.broadcasted_iota(jnp.int32, (tm, TOP_K), 1)
    eidx = jnp.zeros((tm, TOP_K), F32)
    wout = jnp.zeros((tm, TOP_K), F32)
    rank = jnp.zeros((tm, TOP_K), F32)
    for k in range(TOP_K):
        eidx = jnp.where(col == k, idxs[k], eidx)
        wout = jnp.where(col == k, ws[k] / wsum * ROUTED_SCALE, wout)
        rk = jnp.sum(jnp.where(lane_f == idxs[k], before, 0.0), axis=-1, keepdims=True)
        rank = jnp.where(col == k, rk, rank)
    eidx_ref[...] = eidx.astype(jnp.int32)
    w_ref[...] = wout
    rank_ref[...] = rank.astype(jnp.int32)


def _route(logits, bias2):
    n, ne = logits.shape
    tm = ROUTE_TM
    row = lambda i: (i, 0)
    return pl.pallas_call(
        _route_kernel,
        grid=(n // tm,),
        in_specs=[pl.BlockSpec((tm, ne), row), pl.BlockSpec((1, ne), lambda i: (0, 0))],
        out_specs=[pl.BlockSpec((tm, TOP_K), row), pl.BlockSpec((tm, TOP_K), row), pl.BlockSpec((tm, TOP_K), row),
                   pl.BlockSpec((1, 1, ne), lambda i: (i, 0, 0))],
        out_shape=[jax.ShapeDtypeStruct((n, TOP_K), jnp.int32), jax.ShapeDtypeStruct((n, TOP_K), F32),
                   jax.ShapeDtypeStruct((n, TOP_K), jnp.int32), jax.ShapeDtypeStruct((n // tm, 1, ne), F32)],
        scratch_shapes=[pltpu.VMEM((1, ne), F32)],
        compiler_params=_cparams(("arbitrary",)),
        name="route",
    )(logits, bias2)


def _issue_rows(lo, hi, issue_one):
    n_full = (hi - lo) // ISSUE_UNROLL

    def chunk(c, carry):
        for u in range(ISSUE_UNROLL):
            issue_one(lo + c * ISSUE_UNROLL + u)
        return carry

    def tail(r, carry):
        issue_one(r)
        return carry

    lax.fori_loop(0, n_full, chunk, 0)
    lax.fori_loop(lo + n_full * ISSUE_UNROLL, hi, tail, 0)


def _hbm_slab(ref, row):
    return ref.at[pl.ds(pl.multiple_of(row * TOKEN_ROWS, TOKEN_ROWS), TOKEN_ROWS), :]


def _dispatch_kernel(pos_hbm, h_ref, wg_ref, wu_ref, wd_ref, xs_hbm, sh_ref, pos_s, sem_i, sem_s):
    i = pl.program_id(0)
    tm = sh_ref.shape[0]
    per_step = tm * TOP_K
    cp = pltpu.make_async_copy(pos_hbm.at[pl.ds(pl.multiple_of(i * per_step, per_step), per_step)], pos_s, sem_i)
    cp.start()
    cp.wait()

    def chunk(t, carry):
        src = h_ref.at[pl.ds(pl.multiple_of(t * TOKEN_ROWS, TOKEN_ROWS), TOKEN_ROWS), :]
        for k in range(TOP_K):
            pltpu.make_async_copy(src, _hbm_slab(xs_hbm, pos_s[t * TOP_K + k]), sem_s).start()
        return carry

    lax.fori_loop(0, tm, chunk, 0)
    h = jnp.concatenate([h_ref[pl.ds(s, tm, stride=TOKEN_ROWS), :] for s in range(TOKEN_ROWS)],
                        axis=-1).astype(BF16)
    gate = jnp.dot(h, wg_ref[...], preferred_element_type=F32)
    up = jnp.dot(h, wu_ref[...], preferred_element_type=F32)
    sh_ref[...] = jnp.dot((_silu(gate) * up).astype(BF16), wd_ref[...], preferred_element_type=F32)
    for _ in range(TOP_K):
        pltpu.make_async_copy(h_ref, xs_hbm.at[pl.ds(0, tm * TOKEN_ROWS), :], sem_s).wait()


def _dispatch(pos, h2d, wsg, wsu, wsd, n_rows):
    n = h2d.shape[0] // TOKEN_ROWS
    d, de = wsg.shape
    tm = FINAL_TM
    per_step = tm * TOP_K
    assert per_step % IDX_CHUNK == 0
    row = lambda i: (i, 0)
    const = lambda i: (0, 0)
    return pl.pallas_call(
        _dispatch_kernel,
        grid=(n // tm,),
        in_specs=[pl.BlockSpec(memory_space=pl.ANY),
                  pl.BlockSpec((tm * TOKEN_ROWS, LANES), row),
                  pl.BlockSpec((d, de), const),
                  pl.BlockSpec((d, de), const),
                  pl.BlockSpec((de, d), const)],
        out_specs=[pl.BlockSpec(memory_space=pl.ANY),
                   pl.BlockSpec((tm, d), row)],
        out_shape=[jax.ShapeDtypeStruct((n_rows * TOKEN_ROWS, LANES), F32),
                   jax.ShapeDtypeStruct((n, d), F32)],
        scratch_shapes=[pltpu.SMEM((per_step,), jnp.int32),
                        pltpu.SemaphoreType.DMA,
                        pltpu.SemaphoreType.DMA],
        compiler_params=_cparams(("arbitrary",)),
        name="dispatch",
    )(pos, h2d, wsg, wsu, wsd)


def _moe_kernel(be_ref, new_ref, nv_ref, src_ref, nused_ref, ids_hbm, xs_hbm, wg_ref, wu_ref, wd_ref, ys_hbm,
                idx_s, xbuf, ybuf, wg_s, wu_s, wd_s, sem_i, sem_x, sem_s, *, n_tok):
    blk = pl.program_id(0)
    tm = MOE_TM
    per = IDX_CHUNK // tm
    per_log2 = per.bit_length() - 1
    topk_log2 = TOP_K.bit_length() - 1
    n_used = nused_ref[0]
    slab = TOKEN_ROWS
    xrows = tm * slab
    yrows = tm * SLAB_PITCH

    def x_copy(b, slot):
        rows = nv_ref[b] * slab
        src = xs_hbm.at[pl.ds(pl.multiple_of(src_ref[b] * slab, slab), rows), :]
        return pltpu.make_async_copy(src, xbuf.at[pl.ds(pl.multiple_of(slot * xrows, slab), rows), :],
                                     sem_x.at[slot])

    def ids_copy(b):
        c = b >> per_log2
        return pltpu.make_async_copy(
            ids_hbm.at[pl.ds(pl.multiple_of(c * IDX_CHUNK, IDX_CHUNK), IDX_CHUNK)],
            idx_s.at[pl.ds(pl.multiple_of((c & 1) * IDX_CHUNK, IDX_CHUNK), IDX_CHUNK)], sem_i)

    def scatter_block(b, slot):
        ibase = ((b >> per_log2) & 1) * IDX_CHUNK + (b & (per - 1)) * tm
        ybase = slot * yrows

        def one(r):
            rid = idx_s[ibase + r]
            tgt = (rid & (TOP_K - 1)) * n_tok + (rid >> topk_log2)
            src = ybuf.at[pl.ds(pl.multiple_of(ybase + r * SLAB_PITCH, SUBLANES), slab), :]
            pltpu.make_async_copy(src, _hbm_slab(ys_hbm, tgt), sem_s.at[slot]).start()
        _issue_rows(0, nv_ref[b], one)

    def wait_scatter(b, slot):
        rows = nv_ref[b] * slab
        v = ybuf.at[pl.ds(pl.multiple_of(slot * yrows, SUBLANES), rows), :]
        pltpu.make_async_copy(v, ys_hbm.at[pl.ds(0, rows), :], sem_s.at[slot]).wait()

    @pl.when(blk < n_used)
    def _():
        slot = blk & 1

        @pl.when(blk == 0)
        def _():
            xbuf[...] = jnp.zeros(xbuf.shape, F32)
            x_copy(0, 0).start()

        @pl.when(blk + 1 < n_used)
        def _():
            x_copy(blk + 1, 1 - slot).start()

        @pl.when((blk & (per - 1)) == 0)
        def _():
            ids = ids_copy(blk)
            ids.start()
            ids.wait()

        @pl.when(new_ref[blk] == 1)
        def _():
            wg_s[...] = wg_ref[0].astype(BF16)
            wu_s[...] = wu_ref[0].astype(BF16)
            wd_s[...] = wd_ref[0].astype(BF16)

        x_copy(blk, slot).wait()
        xbase = slot * xrows
        x = jnp.concatenate([xbuf[pl.ds(xbase + s, tm, stride=slab), :] for s in range(slab)],
                            axis=-1).astype(BF16)
        gate = jnp.dot(x, wg_s[...], preferred_element_type=F32)
        up = jnp.dot(x, wu_s[...], preferred_element_type=F32)
        act = (_silu(gate) * up).astype(BF16)

        @pl.when(blk >= 2)
        def _():
            wait_scatter(blk - 2, slot)

        y = jnp.dot(act, wd_s[...], preferred_element_type=F32)
        ybase = slot * yrows
        for s in range(slab):
            ybuf[pl.ds(ybase + s, tm, stride=SLAB_PITCH), :] = y[:, s * LANES:(s + 1) * LANES]
        scatter_block(blk, slot)

        @pl.when(blk == n_used - 1)
        def _():
            @pl.when(blk >= 1)
            def _():
                wait_scatter(blk - 1, 1 - slot)
            wait_scatter(blk, slot)


def _moe(block_e, block_new, block_nv, block_src, n_used, row_ids, xs, w_gate, w_up, w_down, n):
    d = TOKEN_ROWS * LANES
    de = w_gate.shape[2]
    tm = MOE_TM
    n_blk = block_e.shape[0]
    wmap = lambda i, be, nw, nv, bs, nu: (be[i], 0, 0)
    grid_spec = pltpu.PrefetchScalarGridSpec(
        num_scalar_prefetch=5,
        grid=(n_blk,),
        in_specs=[pl.BlockSpec(memory_space=pl.ANY),
                  pl.BlockSpec(memory_space=pl.ANY),
                  pl.BlockSpec((1, d, de), wmap),
                  pl.BlockSpec((1, d, de), wmap),
                  pl.BlockSpec((1, de, d), wmap)],
        out_specs=pl.BlockSpec(memory_space=pl.ANY),
        scratch_shapes=[pltpu.SMEM((2 * IDX_CHUNK,), jnp.int32),
                        pltpu.VMEM((2 * tm * TOKEN_ROWS, LANES), F32),
                        pltpu.VMEM((2 * tm * SLAB_PITCH, LANES), F32),
                        pltpu.VMEM((d, de), BF16),
                        pltpu.VMEM((d, de), BF16),
                        pltpu.VMEM((de, d), BF16),
                        pltpu.SemaphoreType.DMA,
                        pltpu.SemaphoreType.DMA((2,)),
                        pltpu.SemaphoreType.DMA((2,))])
    return pl.pallas_call(
        functools.partial(_moe_kernel, n_tok=n),
        grid_spec=grid_spec,
        out_shape=jax.ShapeDtypeStruct((TOP_K * n * TOKEN_ROWS, LANES), F32),
        compiler_params=_cparams(("arbitrary",)),
        name="moe",
    )(block_e, block_new, block_nv, block_src, n_used, row_ids, xs, w_gate, w_up, w_down)


def _final_kernel(gw_hbm, ys_ref, sh_ref, x1_ref, mod_ref, g_ref, b_ref, o_ref, gw_s, acc_ref, sem_i):
    i = pl.program_id(0)
    tm = x1_ref.shape[0]
    per_step = tm * TOP_K
    cp = pltpu.make_async_copy(gw_hbm.at[pl.ds(pl.multiple_of(i * per_step, per_step), per_step)], gw_s, sem_i)
    cp.start()
    cp.wait()

    def combine(t, carry):
        rows = pl.ds(pl.multiple_of(t * TOKEN_ROWS, TOKEN_ROWS), TOKEN_ROWS)
        acc = gw_s[t * TOP_K] * ys_ref[0, rows, :]
        for k in range(1, TOP_K):
            acc = acc + gw_s[t * TOP_K + k] * ys_ref[k, rows, :]
        acc_ref[pl.ds(pl.multiple_of(t * SLAB_PITCH, SUBLANES), TOKEN_ROWS), :] = acc
        return carry

    lax.fori_loop(0, tm, combine, 0, unroll=4)
    moe = jnp.concatenate([acc_ref[pl.ds(s, tm, stride=SLAB_PITCH), :] for s in range(TOKEN_ROWS)], axis=-1)
    m = mod_ref[0]
    y = ALPHA * x1_ref[...] + (1.0 + m[5:6]) * (moe + sh_ref[...])
    o_ref[...] = _layer_norm(y, g_ref[...], b_ref[...])


def _final(gate_w, ys3, shared, x1, mod3, ln_g, ln_b, seq):
    n, d = x1.shape
    tm = FINAL_TM
    per_step = tm * TOP_K
    assert per_step % IDX_CHUNK == 0
    row = lambda i: (i, 0)
    const = lambda i: (0, 0)
    return pl.pallas_call(
        _final_kernel,
        grid=(n // tm,),
        in_specs=[pl.BlockSpec(memory_space=pl.ANY),
                  pl.BlockSpec((TOP_K, tm * TOKEN_ROWS, LANES), lambda i: (0, i, 0)),
                  pl.BlockSpec((tm, d), row),
                  pl.BlockSpec((tm, d), row),
                  pl.BlockSpec((1, 6, d), lambda i: (i * tm // seq, 0, 0)),
                  pl.BlockSpec((1, d), const),
                  pl.BlockSpec((1, d), const)],
        out_specs=pl.BlockSpec((tm, d), row),
        out_shape=jax.ShapeDtypeStruct((n, d), F32),
        scratch_shapes=[pltpu.SMEM((per_step,), F32),
                        pltpu.VMEM((tm * SLAB_PITCH, LANES), F32),
                        pltpu.SemaphoreType.DMA],
        compiler_params=_cparams(("arbitrary",)),
        name="final",
    )(gate_w, ys3, shared, x1, mod3, ln_g, ln_b)


def _dispatch_tables(eidx, rank, tile_counts, n):
    tm = MOE_TM
    a = n * TOP_K
    i32 = jnp.int32
    experts = jnp.arange(N_EXPERTS, dtype=i32)
    counts = jnp.sum(tile_counts, axis=(0, 1)).astype(i32)
    offsets = jnp.cumsum(counts) - counts
    padded = (counts + tm - 1) // tm * tm
    pad_end = jnp.cumsum(padded)
    starts = pad_end - padded
    n_blk = a // tm + N_EXPERTS
    blk_start = jnp.arange(n_blk, dtype=i32) * tm
    n_used = pad_end[-1] // tm
    in_use = jnp.arange(n_blk) < n_used
    raw_e = jnp.minimum(jnp.sum((pad_end[None, :] <= blk_start[:, None]).astype(i32), axis=1), N_EXPERTS - 1)
    last_e = jnp.sum(jnp.where(jnp.arange(n_blk) == n_used - 1, raw_e, 0))
    block_e = jnp.where(in_use, raw_e, last_e)
    onehot = block_e[:, None] == experts[None, :]
    cnt_b = jnp.sum(jnp.where(onehot, counts[None, :], 0), axis=1)
    start_b = jnp.sum(jnp.where(onehot, starts[None, :], 0), axis=1)
    off_b = jnp.sum(jnp.where(onehot, offsets[None, :], 0), axis=1)
    first_row = blk_start - start_b
    block_nv = jnp.where(in_use, jnp.clip(cnt_b - first_row, 0, tm), 0).astype(i32)
    block_src = jnp.where(in_use, off_b + first_row, 0).astype(i32)
    block_new = jnp.concatenate([jnp.ones((1,), i32), (block_e[1:] != block_e[:-1]).astype(i32)])
    dummy_keys = jnp.where(jnp.arange(tm, dtype=i32)[None, :] < (padded - counts)[:, None],
                           experts[:, None], N_EXPERTS).reshape(-1)
    keys = jnp.concatenate([eidx.reshape(-1), dummy_keys])
    ids = jnp.concatenate([jnp.arange(a, dtype=i32), jnp.zeros((N_EXPERTS * tm,), i32)])
    _, row_ids = lax.sort((keys, ids), num_keys=1, is_stable=True)
    pos = jnp.sum(jnp.where(eidx[:, :, None] == experts[None, None, :], offsets[None, None, :], 0), axis=-1) + rank
    return (block_e.astype(i32), block_new, block_nv, block_src, n_used.astype(i32).reshape(1), row_ids,
            pos.reshape(-1).astype(i32))


def kernel(x, c, w_mod, b_mod, w_in, conv_w, attn_sinks, w_out, ln1_g, ln1_b, w_router, router_bias,
           w_gate, w_up, w_down, ws_gate, ws_up, ws_down, ln2_g, ln2_b):
    b, s, d = x.shape
    n = b * s
    attn_w = N_Q_HEADS * HEAD_DIM
    kv_w = N_KV_HEADS * HEAD_DIM
    conv_wd = d - attn_w
    in_w = attn_w + 2 * kv_w + 3 * conv_wd
    x2 = x.reshape(n, d)
    c8 = jnp.zeros((SUBLANES, d), F32).at[:b].set(c)
    for l in range(DEPTH):
        mod = _mod(c8, w_mod[l], b_mod[l].reshape(1, -1))[:b]
        mod3 = mod.reshape(b, 6, d)
        w_in3 = w_in[l].astype(BF16).reshape(d, in_w // INPROJ_TN, INPROJ_TN).transpose(1, 0, 2)
        proj = _inproj(x2, mod3, w_in3, s)
        mix = _mixer(proj, attn_sinks[l].reshape(1, -1), conv_w[l], b, s, attn_w, kv_w, conv_wd)
        x1, h2d, logits = _outproj(mix, x2, mod3, w_out[l].astype(BF16), ln1_g[l].reshape(1, -1),
                                   ln1_b[l].reshape(1, -1), w_router[l], s)
        eidx, gate_w, rank, tile_counts = _route(logits, router_bias[l].reshape(1, -1))
        block_e, block_new, block_nv, block_src, n_used, row_ids, pos = _dispatch_tables(eidx, rank, tile_counts, n)
        xs, shared = _dispatch(pos, h2d, ws_gate[l].astype(BF16), ws_up[l].astype(BF16), ws_down[l].astype(BF16),
                               n * TOP_K)
        ys = _moe(block_e, block_new, block_nv, block_src, n_used, row_ids, xs, w_gate[l], w_up[l], w_down[l], n)
        x2 = _final(gate_w.reshape(-1), ys.reshape(TOP_K, n * TOKEN_ROWS, LANES), shared, x1, mod3,
                    ln2_g[l].reshape(1, -1), ln2_b[l].reshape(1, -1), s)
    return x2.reshape(b, s, d)
```

```python
import functools

import jax
import jax.numpy as jnp
from jax import lax
from jax.experimental import pallas as pl
from jax.experimental.pallas import tpu as pltpu

HEAD_DIM = 64
N_Q_HEADS = 16
N_KV_HEADS = 4
GQA = N_Q_HEADS // N_KV_HEADS
CONV_K = 3
WINDOW = 128
Q_BLOCK = 128
N_EXPERTS = 64
TOP_K = 8
N_GROUPS = 8
GROUP_SIZE = N_EXPERTS // N_GROUPS
TOPK_GROUPS = 4
ROUTED_SCALE = 2.5
DEPTH = 1
ALPHA = (2.0 * DEPTH) ** 0.25
LN_EPS = 1e-5

LANES = 128
SUBLANES = 8
TOKEN_ROWS = 16
SLAB_PITCH = 24
VMEM_LIMIT = 56 * 1024 * 1024

MOD_TN = 1024
INPROJ_TM = 512
INPROJ_TN = 1536
OUTPROJ_TM = 256
ROUTE_TM = 512
MOE_TM = 256
FINAL_TM = 128
IDX_CHUNK = 1024
ISSUE_UNROLL = 8

F32 = jnp.float32
BF16 = jnp.bfloat16


def _cparams(sem):
    return pltpu.CompilerParams(dimension_semantics=sem, vmem_limit_bytes=VMEM_LIMIT)


def _silu(v):
    return v * jax.nn.sigmoid(v)


def _layer_norm(y, g, b):
    mu = jnp.mean(y, axis=-1, keepdims=True)
    yc = y - mu
    var = jnp.mean(yc * yc, axis=-1, keepdims=True)
    return yc * lax.rsqrt(var + LN_EPS) * g + b


def _mod_kernel(c_ref, w_ref, b_ref, o_ref):
    cs = _silu(c_ref[...]).astype(BF16)
    o_ref[...] = jnp.dot(cs, w_ref[...].astype(BF16), preferred_element_type=F32) + b_ref[...]


def _mod(c8, w_mod, b_mod):
    d, n = w_mod.shape
    return pl.pallas_call(
        _mod_kernel,
        grid=(n // MOD_TN,),
        in_specs=[pl.BlockSpec((SUBLANES, d), lambda j: (0, 0)),
                  pl.BlockSpec((d, MOD_TN), lambda j: (0, j)),
                  pl.BlockSpec((1, MOD_TN), lambda j: (0, j))],
        out_specs=pl.BlockSpec((SUBLANES, MOD_TN), lambda j: (0, j)),
        out_shape=jax.ShapeDtypeStruct((SUBLANES, n), F32),
        compiler_params=_cparams(("arbitrary",)),
        name="mod",
    )(c8, w_mod, b_mod)


def _inproj_kernel(x_ref, mod_ref, w_ref, o_ref, h_ref):
    j = pl.program_id(1)

    @pl.when(j == 0)
    def _():
        m = mod_ref[0]
        h_ref[...] = (x_ref[...] * (1.0 + m[1:2]) + m[0:1]).astype(BF16)

    o_ref[...] = jnp.dot(h_ref[...], w_ref[0], preferred_element_type=F32).astype(BF16)


def _inproj(x2, mod3, w_in3, seq):
    n, d = x2.shape
    nj, _, tn = w_in3.shape
    tm = INPROJ_TM
    return pl.pallas_call(
        _inproj_kernel,
        grid=(n // tm, nj),
        in_specs=[pl.BlockSpec((tm, d), lambda i, j: (i, 0)),
                  pl.BlockSpec((1, 6, d), lambda i, j: (i * tm // seq, 0, 0)),
                  pl.BlockSpec((1, d, tn), lambda i, j: (j, 0, 0))],
        out_specs=pl.BlockSpec((tm, tn), lambda i, j: (i, j)),
        out_shape=jax.ShapeDtypeStruct((n, nj * tn), BF16),
        scratch_shapes=[pltpu.VMEM((tm, d), BF16)],
        compiler_params=_cparams(("arbitrary", "arbitrary")),
        name="inproj",
    )(x2, mod3, w_in3)


def _mixer_kernel(cur_ref, pk_ref, pv_ref, prow_ref, sink_ref, cw_ref, o_ref, *, attn_w, kv_w, conv_w):
    nblk = pl.program_id(1)
    has_prev = nblk > 0
    qb = Q_BLOCK
    cur = cur_ref[...]
    k_cur = cur[:, attn_w:attn_w + kv_w]
    v_cur = cur[:, attn_w + kv_w:attn_w + 2 * kv_w]
    k_all = jnp.concatenate([pk_ref[...], k_cur], axis=0)
    v_all = jnp.concatenate([pv_ref[...], v_cur], axis=0)

    rows = GQA * qb
    qi = lax.broadcasted_iota(jnp.int32, (rows, 2 * qb), 0) % qb
    kj = lax.broadcasted_iota(jnp.int32, (rows, 2 * qb), 1)
    dist = qi + qb - kj
    kmin = jnp.where(has_prev, 0, qb)
    valid = (dist >= 0) & (dist < WINDOW) & (kj >= kmin)
    distf = dist.astype(F32)
    head_in_group = lax.broadcasted_iota(jnp.int32, (rows, 1), 0) // qb
    sinks = sink_ref[...]

    outs = []
    for g in range(N_KV_HEADS):
        q4 = jnp.concatenate(
            [cur[:, (g * GQA + j) * HEAD_DIM:(g * GQA + j + 1) * HEAD_DIM] for j in range(GQA)], axis=0)
        kg = k_all[:, g * HEAD_DIM:(g + 1) * HEAD_DIM]
        vg = v_all[:, g * HEAD_DIM:(g + 1) * HEAD_DIM]
        s = lax.dot_general(q4, kg, (((1,), (1,)), ((), ())), preferred_element_type=F32)
        s = s * (HEAD_DIM ** -0.5)
        slope = jnp.zeros((rows, 1), F32)
        sink = jnp.zeros((rows, 1), F32)
        for j in range(GQA):
            h = g * GQA + j
            sel = head_in_group == j
            slope = jnp.where(sel, 2.0 ** (-8.0 * (h + 1) / N_Q_HEADS), slope)
            sink = jnp.where(sel, sinks[:, h:h + 1], sink)
        s = jnp.where(valid, s - slope * distf, -jnp.inf)
        m = jnp.maximum(jnp.max(s, axis=-1, keepdims=True), sink)
        p = jnp.exp(s - m)
        denom = jnp.sum(p, axis=-1, keepdims=True) + jnp.exp(sink - m)
        o4 = jnp.dot(p.astype(BF16), vg, preferred_element_type=F32) / denom
        outs.extend(o4[j * qb:(j + 1) * qb] for j in range(GQA))
    attn = jnp.concatenate(outs, axis=-1)

    c0 = attn_w + 2 * kv_w
    cb = cur[:, c0:c0 + conv_w].astype(F32)
    u = cur[:, c0 + conv_w:c0 + 2 * conv_w].astype(F32) * cur[:, c0 + 2 * conv_w:c0 + 3 * conv_w].astype(F32)
    prow = prow_ref[...]
    up = prow[:, c0 + conv_w:c0 + 2 * conv_w].astype(F32) * prow[:, c0 + 2 * conv_w:c0 + 3 * conv_w].astype(F32)
    up = up * jnp.where(has_prev, 1.0, 0.0)
    pm1 = up[15:16]
    pm2 = up[14:15]
    ri = lax.broadcasted_iota(jnp.int32, u.shape, 0)
    u1 = jnp.where(ri == 0, pm1, pltpu.roll(u, 1, 0))
    u2 = jnp.where(ri == 0, pm2, jnp.where(ri == 1, pm1, pltpu.roll(u, 2, 0)))
    cw = cw_ref[...]
    conv = cb * (cw[0:1] * u2 + cw[1:2] * u1 + cw[2:3] * u)
    o_ref[...] = jnp.concatenate([attn, conv], axis=-1).astype(BF16)


def _mixer(proj, sinks2, conv_w, batch, seq, attn_w, kv_w, conv_wd):
    n, in_w = proj.shape
    nb = seq // Q_BLOCK
    kv_blk0 = attn_w // kv_w
    sub16 = Q_BLOCK // 16

    def cur_map(b, i):
        return (b * nb + i, 0)

    def prev_map(col):
        return lambda b, i: (b * nb + jnp.maximum(i - 1, 0), col)

    def prow_map(b, i):
        return (jnp.maximum((b * nb + i) * sub16 - 1, 0), 0)

    kern = functools.partial(_mixer_kernel, attn_w=attn_w, kv_w=kv_w, conv_w=conv_wd)
    return pl.pallas_call(
        kern,
        grid=(batch, nb),
        in_specs=[pl.BlockSpec((Q_BLOCK, in_w), cur_map),
                  pl.BlockSpec((Q_BLOCK, kv_w), prev_map(kv_blk0)),
                  pl.BlockSpec((Q_BLOCK, kv_w), prev_map(kv_blk0 + 1)),
                  pl.BlockSpec((16, in_w), prow_map),
                  pl.BlockSpec((1, N_Q_HEADS), lambda b, i: (0, 0)),
                  pl.BlockSpec((CONV_K, conv_wd), lambda b, i: (0, 0))],
        out_specs=pl.BlockSpec((Q_BLOCK, attn_w + conv_wd), cur_map),
        out_shape=jax.ShapeDtypeStruct((n, attn_w + conv_wd), BF16),
        compiler_params=_cparams(("arbitrary", "arbitrary")),
        name="mixer",
    )(proj, proj, proj, proj, sinks2, conv_w)


def _split_bf16(v):
    hi = v.astype(BF16)
    lo = (v - hi.astype(F32)).astype(BF16)
    return hi, lo


def _outproj_kernel(mix_ref, x_ref, mod_ref, w_ref, g_ref, b_ref, wr_ref, x1_ref, h2_ref, lg_ref):
    m = mod_ref[0]
    mix = jnp.dot(mix_ref[...], w_ref[...], preferred_element_type=F32)
    x1 = _layer_norm(ALPHA * x_ref[...] + (1.0 + m[2:3]) * mix, g_ref[...], b_ref[...])
    x1_ref[...] = x1
    h2 = x1 * (1.0 + m[4:5]) + m[3:4]
    tm = h2.shape[0]
    for s in range(TOKEN_ROWS):
        h2_ref[pl.ds(s, tm, stride=TOKEN_ROWS), :] = h2[:, s * LANES:(s + 1) * LANES]
    h_hi, h_lo = _split_bf16(h2)
    w_hi, w_lo = _split_bf16(wr_ref[...])
    lg_ref[...] = (jnp.dot(h_hi, w_hi, preferred_element_type=F32)
                   + (jnp.dot(h_hi, w_lo, preferred_element_type=F32)
                      + jnp.dot(h_lo, w_hi, preferred_element_type=F32)))


def _outproj(mix, x2, mod3, w_out_bf, ln_g, ln_b, w_router, seq):
    n, d = x2.shape
    tm = OUTPROJ_TM
    ne = w_router.shape[1]
    row = lambda i: (i, 0)
    const = lambda i: (0, 0)
    return pl.pallas_call(
        _outproj_kernel,
        grid=(n // tm,),
        in_specs=[pl.BlockSpec((tm, d), row),
                  pl.BlockSpec((tm, d), row),
                  pl.BlockSpec((1, 6, d), lambda i: (i * tm // seq, 0, 0)),
                  pl.BlockSpec((d, d), const),
                  pl.BlockSpec((1, d), const),
                  pl.BlockSpec((1, d), const),
                  pl.BlockSpec((d, ne), const)],
        out_specs=[pl.BlockSpec((tm, d), row),
                   pl.BlockSpec((tm * TOKEN_ROWS, LANES), row),
                   pl.BlockSpec((tm, ne), row)],
        out_shape=[jax.ShapeDtypeStruct((n, d), F32),
                   jax.ShapeDtypeStruct((n * TOKEN_ROWS, LANES), F32),
                   jax.ShapeDtypeStruct((n, ne), F32)],
        compiler_params=_cparams(("arbitrary",)),
        name="outproj",
    )(mix, x2, mod3, w_out_bf, ln_g, ln_b, w_router)


def _first_argmax(v, lane_f):
    m = jnp.max(v, axis=-1, keepdims=True)
    idx = jnp.min(jnp.where(v == m, lane_f, float(N_EXPERTS)), axis=-1, keepdims=True)
    return m, idx


def _route_kernel(lg_ref, bias_ref, eidx_ref, w_ref, cnt_ref):
    scores = jax.nn.sigmoid(lg_ref[...])
    sel = scores + bias_ref[...]
    tm = sel.shape[0]
    lane = lax.broadcasted_iota(jnp.int32, (tm, N_EXPERTS), 1)
    lane_f = lane.astype(F32)
    grp = lane // GROUP_SIZE
    neg = -jnp.inf
    gs = []
    for g in range(N_GROUPS):
        vg = jnp.where(grp == g, sel, neg)
        m1, i1 = _first_argmax(vg, lane_f)
        m2 = jnp.max(jnp.where(lane_f == i1, neg, vg), axis=-1, keepdims=True)
        gs.append(m1 + m2)
    keep = jnp.zeros((tm, N_EXPERTS), F32)
    for g in range(N_GROUPS):
        rank = jnp.zeros((tm, 1), F32)
        for o in range(N_GROUPS):
            if o == g:
                continue
            ahead = (gs[o] >= gs[g]) if o < g else (gs[o] > gs[g])
            rank = rank + jnp.where(ahead, 1.0, 0.0)
        keep = jnp.where(grp == g, jnp.where(rank < TOPK_GROUPS, 1.0, 0.0), keep)
    cand = jnp.where(keep > 0.5, sel, neg)
    idxs, ws = [], []
    chosen = jnp.zeros((tm, N_EXPERTS), F32)
    for _ in range(TOP_K):
        _, ik = _first_argmax(cand, lane_f)
        hit = lane_f == ik
        ws.append(jnp.sum(jnp.where(hit, scores, 0.0), axis=-1, keepdims=True))
        idxs.append(ik)
        cand = jnp.where(hit, neg, cand)
        chosen = jnp.where(hit, 1.0, chosen)
    cnt_ref[0] = jnp.sum(chosen, axis=0, keepdims=True)
    wsum = ws[0]
    for k in range(1, TOP_K):
        wsum = wsum + ws[k]
    col = lax.broadcasted_iota(jnp.int32, (tm, TOP_K), 1)
    eidx = jnp.zeros((tm, TOP_K), F32)
    wout = jnp.zeros((tm, TOP_K), F32)
    for k in range(TOP_K):
        eidx = jnp.where(col == k, idxs[k], eidx)
        wout = jnp.where(col == k, ws[k] / wsum * ROUTED_SCALE, wout)
    eidx_ref[...] = eidx.astype(jnp.int32)
    w_ref[...] = wout


def _route(logits, bias2):
    n, ne = logits.shape
    tm = ROUTE_TM
    row = lambda i: (i, 0)
    return pl.pallas_call(
        _route_kernel,
        grid=(n // tm,),
        in_specs=[pl.BlockSpec((tm, ne), row), pl.BlockSpec((1, ne), lambda i: (0, 0))],
        out_specs=[pl.BlockSpec((tm, TOP_K), row), pl.BlockSpec((tm, TOP_K), row),
                   pl.BlockSpec((1, 1, ne), lambda i: (i, 0, 0))],
        out_shape=[jax.ShapeDtypeStruct((n, TOP_K), jnp.int32), jax.ShapeDtypeStruct((n, TOP_K), F32),
                   jax.ShapeDtypeStruct((n // tm, 1, ne), F32)],
        compiler_params=_cparams(("arbitrary",)),
        name="route",
    )(logits, bias2)


def _issue_rows(lo, hi, issue_one):
    n_full = (hi - lo) // ISSUE_UNROLL

    def chunk(c, carry):
        for u in range(ISSUE_UNROLL):
            issue_one(lo + c * ISSUE_UNROLL + u)
        return carry

    def tail(r, carry):
        issue_one(r)
        return carry

    lax.fori_loop(0, n_full, chunk, 0)
    lax.fori_loop(lo + n_full * ISSUE_UNROLL, hi, tail, 0)


def _hbm_slab(ref, row):
    return ref.at[pl.ds(pl.multiple_of(row * TOKEN_ROWS, TOKEN_ROWS), TOKEN_ROWS), :]


def _moe_kernel(be_ref, new_ref, nv_ref, nused_ref, tgt_hbm, h_hbm, wg_ref, wu_ref, wd_ref, ys_hbm,
                idx_s, xbuf, ybuf, wg_s, wu_s, wd_s, sem_i, sem_g, sem_s, *, n_tok):
    blk = pl.program_id(0)
    tm = MOE_TM
    per = IDX_CHUNK // tm
    per_log2 = per.bit_length() - 1
    n_used = nused_ref[0]
    slab = TOKEN_ROWS
    buf_rows = tm * SLAB_PITCH

    def staged(buf, base, r):
        return buf.at[pl.ds(pl.multiple_of(base + r * SLAB_PITCH, SUBLANES), slab), :]

    def idx_copy(b):
        c = b >> per_log2
        return pltpu.make_async_copy(
            tgt_hbm.at[pl.ds(pl.multiple_of(c * IDX_CHUNK, IDX_CHUNK), IDX_CHUNK)],
            idx_s.at[pl.ds(pl.multiple_of((c & 1) * IDX_CHUNK, IDX_CHUNK), IDX_CHUNK)], sem_i)

    def idx_base(b):
        return ((b >> per_log2) & 1) * IDX_CHUNK + (b & (per - 1)) * tm

    def gather_block(b, slot):
        ibase = idx_base(b)
        xbase = slot * buf_rows

        def one(r):
            tok = idx_s[ibase + r] & (n_tok - 1)
            pltpu.make_async_copy(_hbm_slab(h_hbm, tok), staged(xbuf, xbase, r), sem_g.at[slot]).start()
        _issue_rows(0, nv_ref[b], one)

    def scatter_block(b, slot):
        ibase = idx_base(b)
        ybase = slot * buf_rows

        def one(r):
            pltpu.make_async_copy(staged(ybuf, ybase, r), _hbm_slab(ys_hbm, idx_s[ibase + r]),
                                  sem_s.at[slot]).start()
        _issue_rows(0, nv_ref[b], one)

    def wait_rows(hbm, buf, sem, b, slot, to_hbm):
        rows = nv_ref[b] * slab
        v = buf.at[pl.ds(pl.multiple_of(slot * buf_rows, SUBLANES), rows), :]
        hv = hbm.at[pl.ds(0, rows), :]
        (pltpu.make_async_copy(v, hv, sem.at[slot]) if to_hbm else pltpu.make_async_copy(hv, v, sem.at[slot])).wait()

    @pl.when(blk < n_used)
    def _():
        slot = blk & 1

        @pl.when(blk == 0)
        def _():
            xbuf[...] = jnp.zeros(xbuf.shape, F32)
            first = idx_copy(0)
            first.start()
            first.wait()
            gather_block(0, 0)

        @pl.when(blk + 1 < n_used)
        def _():
            @pl.when(((blk + 1) & (per - 1)) == 0)
            def _():
                nxt = idx_copy(blk + 1)
                nxt.start()
                nxt.wait()
            gather_block(blk + 1, 1 - slot)

        @pl.when(new_ref[blk] == 1)
        def _():
            wg_s[...] = wg_ref[0].astype(BF16)
            wu_s[...] = wu_ref[0].astype(BF16)
            wd_s[...] = wd_ref[0].astype(BF16)

        wait_rows(h_hbm, xbuf, sem_g, blk, slot, to_hbm=False)

        @pl.when(blk >= 2)
        def _():
            wait_rows(ys_hbm, ybuf, sem_s, blk - 2, slot, to_hbm=True)

        base = slot * buf_rows
        x = jnp.concatenate([xbuf[pl.ds(base + s, tm, stride=SLAB_PITCH), :] for s in range(slab)],
                            axis=-1).astype(BF16)
        gate = jnp.dot(x, wg_s[...], preferred_element_type=F32)
        up = jnp.dot(x, wu_s[...], preferred_element_type=F32)
        act = (_silu(gate) * up).astype(BF16)
        y = jnp.dot(act, wd_s[...], preferred_element_type=F32)
        for s in range(slab):
            ybuf[pl.ds(base + s, tm, stride=SLAB_PITCH), :] = y[:, s * LANES:(s + 1) * LANES]
        scatter_block(blk, slot)

        @pl.when(blk == n_used - 1)
        def _():
            @pl.when(blk >= 1)
            def _():
                wait_rows(ys_hbm, ybuf, sem_s, blk - 1, 1 - slot, to_hbm=True)
            wait_rows(ys_hbm, ybuf, sem_s, blk, slot, to_hbm=True)


def _moe(block_e, block_new, block_nv, n_used, row_tgt, h2d, w_gate, w_up, w_down):
    n = h2d.shape[0] // TOKEN_ROWS
    assert n & (n - 1) == 0
    d = TOKEN_ROWS * LANES
    de = w_gate.shape[2]
    tm = MOE_TM
    n_blk = block_e.shape[0]
    wmap = lambda i, be, nw, nv, nu: (be[i], 0, 0)
    grid_spec = pltpu.PrefetchScalarGridSpec(
        num_scalar_prefetch=4,
        grid=(n_blk,),
        in_specs=[pl.BlockSpec(memory_space=pl.ANY),
                  pl.BlockSpec(memory_space=pl.ANY),
                  pl.BlockSpec((1, d, de), wmap),
                  pl.BlockSpec((1, d, de), wmap),
                  pl.BlockSpec((1, de, d), wmap)],
        out_specs=pl.BlockSpec(memory_space=pl.ANY),
        scratch_shapes=[pltpu.SMEM((2 * IDX_CHUNK,), jnp.int32),
                        pltpu.VMEM((2 * tm * SLAB_PITCH, LANES), F32),
                        pltpu.VMEM((2 * tm * SLAB_PITCH, LANES), F32),
                        pltpu.VMEM((d, de), BF16),
                        pltpu.VMEM((d, de), BF16),
                        pltpu.VMEM((de, d), BF16),
                        pltpu.SemaphoreType.DMA,
                        pltpu.SemaphoreType.DMA((2,)),
                        pltpu.SemaphoreType.DMA((2,))])
    return pl.pallas_call(
        functools.partial(_moe_kernel, n_tok=n),
        grid_spec=grid_spec,
        out_shape=jax.ShapeDtypeStruct((TOP_K * n * TOKEN_ROWS, LANES), F32),
        compiler_params=_cparams(("arbitrary",)),
        name="moe",
    )(block_e, block_new, block_nv, n_used, row_tgt, h2d, w_gate, w_up, w_down)


def _final_kernel(gw_hbm, ys_ref, h_ref, x1_ref, mod_ref, wg_ref, wu_ref, wd_ref, g_ref, b_ref, o_ref,
                  gw_s, acc_ref, sem_i):
    i = pl.program_id(0)
    tm = x1_ref.shape[0]
    per_step = tm * TOP_K
    cp = pltpu.make_async_copy(gw_hbm.at[pl.ds(pl.multiple_of(i * per_step, per_step), per_step)], gw_s, sem_i)
    cp.start()
    cp.wait()

    def combine(t, carry):
        rows = pl.ds(pl.multiple_of(t * TOKEN_ROWS, TOKEN_ROWS), TOKEN_ROWS)
        acc = gw_s[t * TOP_K] * ys_ref[0, rows, :]
        for k in range(1, TOP_K):
            acc = acc + gw_s[t * TOP_K + k] * ys_ref[k, rows, :]
        acc_ref[pl.ds(pl.multiple_of(t * SLAB_PITCH, SUBLANES), TOKEN_ROWS), :] = acc
        return carry

    lax.fori_loop(0, tm, combine, 0, unroll=4)
    moe = jnp.concatenate([acc_ref[pl.ds(s, tm, stride=SLAB_PITCH), :] for s in range(TOKEN_ROWS)], axis=-1)
    h = jnp.concatenate([h_ref[pl.ds(s, tm, stride=TOKEN_ROWS), :] for s in range(TOKEN_ROWS)],
                        axis=-1).astype(BF16)
    gate = jnp.dot(h, wg_ref[...], preferred_element_type=F32)
    up = jnp.dot(h, wu_ref[...], preferred_element_type=F32)
    shared = jnp.dot((_silu(gate) * up).astype(BF16), wd_ref[...], preferred_element_type=F32)
    m = mod_ref[0]
    y = ALPHA * x1_ref[...] + (1.0 + m[5:6]) * (moe + shared)
    o_ref[...] = _layer_norm(y, g_ref[...], b_ref[...])


def _final(gate_w, ys3, h2d, x1, mod3, wsg, wsu, wsd, ln_g, ln_b, seq):
    n, d = x1.shape
    de = wsg.shape[1]
    tm = FINAL_TM
    per_step = tm * TOP_K
    assert per_step % IDX_CHUNK == 0
    row = lambda i: (i, 0)
    const = lambda i: (0, 0)
    return pl.pallas_call(
        _final_kernel,
        grid=(n // tm,),
        in_specs=[pl.BlockSpec(memory_space=pl.ANY),
                  pl.BlockSpec((TOP_K, tm * TOKEN_ROWS, LANES), lambda i: (0, i, 0)),
                  pl.BlockSpec((tm * TOKEN_ROWS, LANES), row),
                  pl.BlockSpec((tm, d), row),
                  pl.BlockSpec((1, 6, d), lambda i: (i * tm // seq, 0, 0)),
                  pl.BlockSpec((d, de), const),
                  pl.BlockSpec((d, de), const),
                  pl.BlockSpec((de, d), const),
                  pl.BlockSpec((1, d), const),
                  pl.BlockSpec((1, d), const)],
        out_specs=pl.BlockSpec((tm, d), row),
        out_shape=jax.ShapeDtypeStruct((n, d), F32),
        scratch_shapes=[pltpu.SMEM((per_step,), F32),
                        pltpu.VMEM((tm * SLAB_PITCH, LANES), F32),
                        pltpu.SemaphoreType.DMA],
        compiler_params=_cparams(("arbitrary",)),
        name="final",
    )(gate_w, ys3, h2d, x1, mod3, wsg, wsu, wsd, ln_g, ln_b)


def _dispatch_tables(eidx, tile_counts, n):
    tm = MOE_TM
    a = n * TOP_K
    i32 = jnp.int32
    experts = jnp.arange(N_EXPERTS, dtype=i32)
    counts = jnp.sum(tile_counts, axis=(0, 1)).astype(i32)
    padded = (counts + tm - 1) // tm * tm
    pad_end = jnp.cumsum(padded)
    starts = pad_end - padded
    n_blk = a // tm + N_EXPERTS
    blk_start = jnp.arange(n_blk, dtype=i32) * tm
    n_used = pad_end[-1] // tm
    in_use = jnp.arange(n_blk) < n_used
    raw_e = jnp.minimum(jnp.sum((pad_end[None, :] <= blk_start[:, None]).astype(i32), axis=1), N_EXPERTS - 1)
    last_e = jnp.sum(jnp.where(jnp.arange(n_blk) == n_used - 1, raw_e, 0))
    block_e = jnp.where(in_use, raw_e, last_e)
    onehot = block_e[:, None] == experts[None, :]
    cnt_b = jnp.sum(jnp.where(onehot, counts[None, :], 0), axis=1)
    start_b = jnp.sum(jnp.where(onehot, starts[None, :], 0), axis=1)
    block_nv = jnp.where(in_use, jnp.clip(cnt_b - (blk_start - start_b), 0, tm), 0).astype(i32)
    block_new = jnp.concatenate([jnp.ones((1,), i32), (block_e[1:] != block_e[:-1]).astype(i32)])
    dummy_keys = jnp.where(jnp.arange(tm, dtype=i32)[None, :] < (padded - counts)[:, None],
                           experts[:, None], N_EXPERTS).reshape(-1)
    keys = jnp.concatenate([eidx.reshape(-1), dummy_keys])
    tgt = (jnp.arange(TOP_K, dtype=i32)[None, :] * n + jnp.arange(n, dtype=i32)[:, None]).reshape(-1)
    payload = jnp.concatenate([tgt, jnp.zeros((N_EXPERTS * tm,), i32)])
    _, row_tgt = lax.sort((keys, payload), num_keys=1, is_stable=True)
    return block_e.astype(i32), block_new, block_nv, n_used.astype(i32).reshape(1), row_tgt


def kernel(x, c, w_mod, b_mod, w_in, conv_w, attn_sinks, w_out, ln1_g, ln1_b, w_router, router_bias,
           w_gate, w_up, w_down, ws_gate, ws_up, ws_down, ln2_g, ln2_b):
    b, s, d = x.shape
    n = b * s
    attn_w = N_Q_HEADS * HEAD_DIM
    kv_w = N_KV_HEADS * HEAD_DIM
    conv_wd = d - attn_w
    in_w = attn_w + 2 * kv_w + 3 * conv_wd
    x2 = x.reshape(n, d)
    c8 = jnp.zeros((SUBLANES, d), F32).at[:b].set(c)
    for l in range(DEPTH):
        mod = _mod(c8, w_mod[l], b_mod[l].reshape(1, -1))[:b]
        mod3 = mod.reshape(b, 6, d)
        w_in3 = w_in[l].astype(BF16).reshape(d, in_w // INPROJ_TN, INPROJ_TN).transpose(1, 0, 2)
        proj = _inproj(x2, mod3, w_in3, s)
        mix = _mixer(proj, attn_sinks[l].reshape(1, -1), conv_w[l], b, s, attn_w, kv_w, conv_wd)
        x1, h2d, logits = _outproj(mix, x2, mod3, w_out[l].astype(BF16), ln1_g[l].reshape(1, -1),
                                   ln1_b[l].reshape(1, -1), w_router[l], s)
        eidx, gate_w, tile_counts = _route(logits, router_bias[l].reshape(1, -1))
        block_e, block_new, block_nv, n_used, row_tgt = _dispatch_tables(eidx, tile_counts, n)
        ys = _moe(block_e, block_new, block_nv, n_used, row_tgt, h2d, w_gate[l], w_up[l], w_down[l])
        x2 = _final(gate_w.reshape(-1), ys.reshape(TOP_K, n * TOKEN_ROWS, LANES), h2d, x1, mod3,
                    ws_gate[l].astype(BF16), ws_up[l].astype(BF16), ws_down[l].astype(BF16),
                    ln2_g[l].reshape(1, -1), ln2_b[l].reshape(1, -1), s)
    return x2.reshape(b, s, d)
```

```python
import functools

import jax
import jax.numpy as jnp
from jax import lax
from jax.experimental import pallas as pl
from jax.experimental.pallas import tpu as pltpu

HEAD_DIM = 64
N_Q_HEADS = 16
N_KV_HEADS = 4
GQA = N_Q_HEADS // N_KV_HEADS
CONV_K = 3
WINDOW = 128
Q_BLOCK = 128
N_EXPERTS = 64
TOP_K = 8
N_GROUPS = 8
GROUP_SIZE = N_EXPERTS // N_GROUPS
TOPK_GROUPS = 4
ROUTED_SCALE = 2.5
DEPTH = 1
ALPHA = (2.0 * DEPTH) ** 0.25
LN_EPS = 1e-5

LANES = 128
SUBLANES = 8
TOKEN_ROWS = 16
SLAB_PITCH = 24
VMEM_LIMIT = 56 * 1024 * 1024

MOD_TN = 1024
INPROJ_TM = 512
INPROJ_TN = 1536
OUTPROJ_TM = 256
ROUTE_TM = 512
MOE_TM = 256
FINAL_TM = 128
IDX_CHUNK = 1024
ISSUE_UNROLL = 8

F32 = jnp.float32
BF16 = jnp.bfloat16


def _cparams(sem):
    return pltpu.CompilerParams(dimension_semantics=sem, vmem_limit_bytes=VMEM_LIMIT)


def _silu(v):
    return v * jax.nn.sigmoid(v)


def _layer_norm(y, g, b):
    mu = jnp.mean(y, axis=-1, keepdims=True)
    yc = y - mu
    var = jnp.mean(yc * yc, axis=-1, keepdims=True)
    return yc * lax.rsqrt(var + LN_EPS) * g + b


def _mod_kernel(c_ref, w_ref, b_ref, o_ref):
    cs = _silu(c_ref[...]).astype(BF16)
    o_ref[...] = jnp.dot(cs, w_ref[...].astype(BF16), preferred_element_type=F32) + b_ref[...]


def _mod(c8, w_mod, b_mod):
    d, n = w_mod.shape
    return pl.pallas_call(
        _mod_kernel,
        grid=(n // MOD_TN,),
        in_specs=[pl.BlockSpec((SUBLANES, d), lambda j: (0, 0)),
                  pl.BlockSpec((d, MOD_TN), lambda j: (0, j)),
                  pl.BlockSpec((1, MOD_TN), lambda j: (0, j))],
        out_specs=pl.BlockSpec((SUBLANES, MOD_TN), lambda j: (0, j)),
        out_shape=jax.ShapeDtypeStruct((SUBLANES, n), F32),
        compiler_params=_cparams(("arbitrary",)),
        name="mod",
    )(c8, w_mod, b_mod)


def _inproj_kernel(x_ref, mod_ref, w_ref, o_ref, h_ref):
    j = pl.program_id(1)

    @pl.when(j == 0)
    def _():
        m = mod_ref[0]
        h_ref[...] = (x_ref[...] * (1.0 + m[1:2]) + m[0:1]).astype(BF16)

    o_ref[...] = jnp.dot(h_ref[...], w_ref[0], preferred_element_type=F32).astype(BF16)


def _inproj(x2, mod3, w_in3, seq):
    n, d = x2.shape
    nj, _, tn = w_in3.shape
    tm = INPROJ_TM
    return pl.pallas_call(
        _inproj_kernel,
        grid=(n // tm, nj),
        in_specs=[pl.BlockSpec((tm, d), lambda i, j: (i, 0)),
                  pl.BlockSpec((1, 6, d), lambda i, j: (i * tm // seq, 0, 0)),
                  pl.BlockSpec((1, d, tn), lambda i, j: (j, 0, 0))],
        out_specs=pl.BlockSpec((tm, tn), lambda i, j: (i, j)),
        out_shape=jax.ShapeDtypeStruct((n, nj * tn), BF16),
        scratch_shapes=[pltpu.VMEM((tm, d), BF16)],
        compiler_params=_cparams(("arbitrary", "arbitrary")),
        name="inproj",
    )(x2, mod3, w_in3)


LOG2E = 1.4426950408889634
CONV_CHUNK = 256


def _mixer_kernel(cur_ref, pk_ref, pv_ref, prow_ref, sink_ref, cw_ref, o_ref, bias_ref, *, attn_w, kv_w, conv_w):
    first_step = (pl.program_id(0) == 0) & (pl.program_id(1) == 0)
    has_prev = pl.program_id(1) > 0
    qb = Q_BLOCK

    @pl.when(first_step)
    def _():
        qi = lax.broadcasted_iota(jnp.int32, (qb, 2 * qb), 0)
        kj = lax.broadcasted_iota(jnp.int32, (qb, 2 * qb), 1)
        dist = qi + qb - kj
        window = (dist >= 0) & (dist < WINDOW)
        distf = dist.astype(F32)
        for h in range(N_Q_HEADS):
            slope = 2.0 ** (-8.0 * (h + 1) / N_Q_HEADS)
            b1 = jnp.where(window, distf * (-slope * LOG2E), -jnp.inf)
            bias_ref[1, h] = b1
            bias_ref[0, h] = jnp.where(kj >= qb, b1, -jnp.inf)

    table = jnp.where(has_prev, 1, 0)
    sinks2 = sink_ref[...] * LOG2E
    for g in range(N_KV_HEADS):
        kcol = attn_w + g * HEAD_DIM
        vcol = attn_w + kv_w + g * HEAD_DIM
        kg = jnp.concatenate([pk_ref[:, g * HEAD_DIM:(g + 1) * HEAD_DIM], cur_ref[:, kcol:kcol + HEAD_DIM]], axis=0)
        vg = jnp.concatenate([pv_ref[:, g * HEAD_DIM:(g + 1) * HEAD_DIM], cur_ref[:, vcol:vcol + HEAD_DIM]], axis=0)
        for j in range(GQA):
            h = g * GQA + j
            q = cur_ref[:, h * HEAD_DIM:(h + 1) * HEAD_DIM]
            s = lax.dot_general(q, kg, (((1,), (1,)), ((), ())), preferred_element_type=F32)
            s2 = s * (HEAD_DIM ** -0.5 * LOG2E) + bias_ref[table, h]
            sink2 = sinks2[:, h:h + 1]
            m2 = jnp.maximum(jnp.max(s2, axis=-1, keepdims=True), sink2)
            p = jnp.exp2(s2 - m2)
            denom = jnp.sum(p, axis=-1, keepdims=True) + jnp.exp2(sink2 - m2)
            o = jnp.dot(p.astype(BF16), vg, preferred_element_type=F32) / denom
            o_ref[:, h * HEAD_DIM:(h + 1) * HEAD_DIM] = o.astype(BF16)

    c0 = attn_w + 2 * kv_w
    prev_on = jnp.where(has_prev, 1.0, 0.0)
    ri = lax.broadcasted_iota(jnp.int32, (qb, CONV_CHUNK), 0)
    for c in range(conv_w // CONV_CHUNK):
        lo = c * CONV_CHUNK
        cols = lambda base: slice(c0 + base + lo, c0 + base + lo + CONV_CHUNK)
        cb = cur_ref[:, cols(0)].astype(F32)
        u = cur_ref[:, cols(conv_w)].astype(F32) * cur_ref[:, cols(2 * conv_w)].astype(F32)
        up = prow_ref[:, cols(conv_w)].astype(F32) * prow_ref[:, cols(2 * conv_w)].astype(F32) * prev_on
        pm1 = up[15:16]
        pm2 = up[14:15]
        u1 = jnp.where(ri == 0, pm1, pltpu.roll(u, 1, 0))
        u2 = jnp.where(ri == 0, pm2, jnp.where(ri == 1, pm1, pltpu.roll(u, 2, 0)))
        cw = cw_ref[:, lo:lo + CONV_CHUNK]
        conv = cb * (cw[0:1] * u2 + cw[1:2] * u1 + cw[2:3] * u)
        o_ref[:, attn_w + lo:attn_w + lo + CONV_CHUNK] = conv.astype(BF16)


def _mixer(proj, sinks2, conv_w, batch, seq, attn_w, kv_w, conv_wd):
    n, in_w = proj.shape
    nb = seq // Q_BLOCK
    kv_blk0 = attn_w // kv_w
    sub16 = Q_BLOCK // 16

    def cur_map(b, i):
        return (b * nb + i, 0)

    def prev_map(col):
        return lambda b, i: (b * nb + jnp.maximum(i - 1, 0), col)

    def prow_map(b, i):
        return (jnp.maximum((b * nb + i) * sub16 - 1, 0), 0)

    kern = functools.partial(_mixer_kernel, attn_w=attn_w, kv_w=kv_w, conv_w=conv_wd)
    return pl.pallas_call(
        kern,
        grid=(batch, nb),
        in_specs=[pl.BlockSpec((Q_BLOCK, in_w), cur_map),
                  pl.BlockSpec((Q_BLOCK, kv_w), prev_map(kv_blk0)),
                  pl.BlockSpec((Q_BLOCK, kv_w), prev_map(kv_blk0 + 1)),
                  pl.BlockSpec((16, in_w), prow_map),
                  pl.BlockSpec((1, N_Q_HEADS), lambda b, i: (0, 0)),
                  pl.BlockSpec((CONV_K, conv_wd), lambda b, i: (0, 0))],
        out_specs=pl.BlockSpec((Q_BLOCK, attn_w + conv_wd), cur_map),
        out_shape=jax.ShapeDtypeStruct((n, attn_w + conv_wd), BF16),
        scratch_shapes=[pltpu.VMEM((2, N_Q_HEADS, Q_BLOCK, 2 * Q_BLOCK), F32)],
        compiler_params=_cparams(("arbitrary", "arbitrary")),
        name="mixer",
    )(proj, proj, proj, proj, sinks2, conv_w)


def _split_bf16(v):
    hi = v.astype(BF16)
    lo = (v - hi.astype(F32)).astype(BF16)
    return hi, lo


def _outproj_kernel(mix_ref, x_ref, mod_ref, w_ref, g_ref, b_ref, wr_ref, x1_ref, h2_ref, lg_ref):
    m = mod_ref[0]
    mix = jnp.dot(mix_ref[...], w_ref[...], preferred_element_type=F32)
    x1 = _layer_norm(ALPHA * x_ref[...] + (1.0 + m[2:3]) * mix, g_ref[...], b_ref[...])
    x1_ref[...] = x1
    h2 = x1 * (1.0 + m[4:5]) + m[3:4]
    tm = h2.shape[0]
    for s in range(TOKEN_ROWS):
        h2_ref[pl.ds(s, tm, stride=TOKEN_ROWS), :] = h2[:, s * LANES:(s + 1) * LANES]
    h_hi, h_lo = _split_bf16(h2)
    w_hi, w_lo = _split_bf16(wr_ref[...])
    lg_ref[...] = (jnp.dot(h_hi, w_hi, preferred_element_type=F32)
                   + (jnp.dot(h_hi, w_lo, preferred_element_type=F32)
                      + jnp.dot(h_lo, w_hi, preferred_element_type=F32)))


def _outproj(mix, x2, mod3, w_out_bf, ln_g, ln_b, w_router, seq):
    n, d = x2.shape
    tm = OUTPROJ_TM
    ne = w_router.shape[1]
    row = lambda i: (i, 0)
    const = lambda i: (0, 0)
    return pl.pallas_call(
        _outproj_kernel,
        grid=(n // tm,),
        in_specs=[pl.BlockSpec((tm, d), row),
                  pl.BlockSpec((tm, d), row),
                  pl.BlockSpec((1, 6, d), lambda i: (i * tm // seq, 0, 0)),
                  pl.BlockSpec((d, d), const),
                  pl.BlockSpec((1, d), const),
                  pl.BlockSpec((1, d), const),
                  pl.BlockSpec((d, ne), const)],
        out_specs=[pl.BlockSpec((tm, d), row),
                   pl.BlockSpec((tm * TOKEN_ROWS, LANES), row),
                   pl.BlockSpec((tm, ne), row)],
        out_shape=[jax.ShapeDtypeStruct((n, d), F32),
                   jax.ShapeDtypeStruct((n * TOKEN_ROWS, LANES), F32),
                   jax.ShapeDtypeStruct((n, ne), F32)],
        compiler_params=_cparams(("arbitrary",)),
        name="outproj",
    )(mix, x2, mod3, w_out_bf, ln_g, ln_b, w_router)


def _first_argmax(v, lane_f):
    m = jnp.max(v, axis=-1, keepdims=True)
    idx = jnp.min(jnp.where(v == m, lane_f, float(N_EXPERTS)), axis=-1, keepdims=True)
    return m, idx


def _route_kernel(lg_ref, bias_ref, eidx_ref, w_ref, cnt_ref):
    scores = jax.nn.sigmoid(lg_ref[...])
    sel = scores + bias_ref[...]
    tm = sel.shape[0]
    lane = lax.broadcasted_iota(jnp.int32, (tm, N_EXPERTS), 1)
    lane_f = lane.astype(F32)
    grp = lane // GROUP_SIZE
    neg = -jnp.inf
    gs = []
    for g in range(N_GROUPS):
        vg = jnp.where(grp == g, sel, neg)
        m1, i1 = _first_argmax(vg, lane_f)
        m2 = jnp.max(jnp.where(lane_f == i1, neg, vg), axis=-1, keepdims=True)
        gs.append(m1 + m2)
    keep = jnp.zeros((tm, N_EXPERTS), F32)
    for g in range(N_GROUPS):
        rank = jnp.zeros((tm, 1), F32)
        for o in range(N_GROUPS):
            if o == g:
                continue
            ahead = (gs[o] >= gs[g]) if o < g else (gs[o] > gs[g])
            rank = rank + jnp.where(ahead, 1.0, 0.0)
        keep = jnp.where(grp == g, jnp.where(rank < TOPK_GROUPS, 1.0, 0.0), keep)
    cand = jnp.where(keep > 0.5, sel, neg)
    idxs, ws = [], []
    chosen = jnp.zeros((tm, N_EXPERTS), F32)
    for _ in range(TOP_K):
        _, ik = _first_argmax(cand, lane_f)
        hit = lane_f == ik
        ws.append(jnp.sum(jnp.where(hit, scores, 0.0), axis=-1, keepdims=True))
        idxs.append(ik)
        cand = jnp.where(hit, neg, cand)
        chosen = jnp.where(hit, 1.0, chosen)
    cnt_ref[0] = jnp.sum(chosen, axis=0, keepdims=True)
    wsum = ws[0]
    for k in range(1, TOP_K):
        wsum = wsum + ws[k]
    col = lax.broadcasted_iota(jnp.int32, (tm, TOP_K), 1)
    eidx = jnp.zeros((tm, TOP_K), F32)
    wout = jnp.zeros((tm, TOP_K), F32)
    for k in range(TOP_K):
        eidx = jnp.where(col == k, idxs[k], eidx)
        wout = jnp.where(col == k, ws[k] / wsum * ROUTED_SCALE, wout)
    eidx_ref[...] = eidx.astype(jnp.int32)
    w_ref[...] = wout


def _route(logits, bias2):
    n, ne = logits.shape
    tm = ROUTE_TM
    row = lambda i: (i, 0)
    return pl.pallas_call(
        _route_kernel,
        grid=(n // tm,),
        in_specs=[pl.BlockSpec((tm, ne), row), pl.BlockSpec((1, ne), lambda i: (0, 0))],
        out_specs=[pl.BlockSpec((tm, TOP_K), row), pl.BlockSpec((tm, TOP_K), row),
                   pl.BlockSpec((1, 1, ne), lambda i: (i, 0, 0))],
        out_shape=[jax.ShapeDtypeStruct((n, TOP_K), jnp.int32), jax.ShapeDtypeStruct((n, TOP_K), F32),
                   jax.ShapeDtypeStruct((n // tm, 1, ne), F32)],
        compiler_params=_cparams(("arbitrary",)),
        name="route",
    )(logits, bias2)


def _issue_rows(lo, hi, issue_one):
    n_full = (hi - lo) // ISSUE_UNROLL

    def chunk(c, carry):
        for u in range(ISSUE_UNROLL):
            issue_one(lo + c * ISSUE_UNROLL + u)
        return carry

    def tail(r, carry):
        issue_one(r)
        return carry

    lax.fori_loop(0, n_full, chunk, 0)
    lax.fori_loop(lo + n_full * ISSUE_UNROLL, hi, tail, 0)


def _hbm_slab(ref, row):
    return ref.at[pl.ds(pl.multiple_of(row * TOKEN_ROWS, TOKEN_ROWS), TOKEN_ROWS), :]


def _moe_kernel(be_ref, new_ref, nv_ref, nused_ref, tgt_hbm, h_hbm, wg_ref, wu_ref, wd_ref, ys_hbm,
                idx_s, xbuf, ybuf, wg_s, wu_s, wd_s, sem_i, sem_g, sem_s, *, n_tok):
    blk = pl.program_id(0)
    tm = MOE_TM
    per = IDX_CHUNK // tm
    per_log2 = per.bit_length() - 1
    n_used = nused_ref[0]
    slab = TOKEN_ROWS
    buf_rows = tm * SLAB_PITCH

    def staged(buf, base, r):
        return buf.at[pl.ds(pl.multiple_of(base + r * SLAB_PITCH, SUBLANES), slab), :]

    def idx_copy(b):
        c = b >> per_log2
        return pltpu.make_async_copy(
            tgt_hbm.at[pl.ds(pl.multiple_of(c * IDX_CHUNK, IDX_CHUNK), IDX_CHUNK)],
            idx_s.at[pl.ds(pl.multiple_of((c & 1) * IDX_CHUNK, IDX_CHUNK), IDX_CHUNK)], sem_i)

    def idx_base(b):
        return ((b >> per_log2) & 1) * IDX_CHUNK + (b & (per - 1)) * tm

    def gather_block(b, slot):
        ibase = idx_base(b)
        xbase = slot * buf_rows

        def one(r):
            tok = idx_s[ibase + r] & (n_tok - 1)
            pltpu.make_async_copy(_hbm_slab(h_hbm, tok), staged(xbuf, xbase, r), sem_g.at[slot]).start()
        _issue_rows(0, nv_ref[b], one)

    def scatter_block(b, slot):
        ibase = idx_base(b)
        ybase = slot * buf_rows

        def one(r):
            pltpu.make_async_copy(staged(ybuf, ybase, r), _hbm_slab(ys_hbm, idx_s[ibase + r]),
                                  sem_s.at[slot]).start()
        _issue_rows(0, nv_ref[b], one)

    def wait_rows(hbm, buf, sem, b, slot, to_hbm):
        rows = nv_ref[b] * slab
        v = buf.at[pl.ds(pl.multiple_of(slot * buf_rows, SUBLANES), rows), :]
        hv = hbm.at[pl.ds(0, rows), :]
        (pltpu.make_async_copy(v, hv, sem.at[slot]) if to_hbm else pltpu.make_async_copy(hv, v, sem.at[slot])).wait()

    @pl.when(blk < n_used)
    def _():
        slot = blk & 1

        @pl.when(blk == 0)
        def _():
            xbuf[...] = jnp.zeros(xbuf.shape, F32)
            first = idx_copy(0)
            first.start()
            first.wait()
            gather_block(0, 0)

        @pl.when(blk + 1 < n_used)
        def _():
            @pl.when(((blk + 1) & (per - 1)) == 0)
            def _():
                nxt = idx_copy(blk + 1)
                nxt.start()
                nxt.wait()
            gather_block(blk + 1, 1 - slot)

        @pl.when(new_ref[blk] == 1)
        def _():
            wg_s[...] = wg_ref[0].astype(BF16)
            wu_s[...] = wu_ref[0].astype(BF16)
            wd_s[...] = wd_ref[0].astype(BF16)

        wait_rows(h_hbm, xbuf, sem_g, blk, slot, to_hbm=False)

        @pl.when(blk >= 2)
        def _():
            wait_rows(ys_hbm, ybuf, sem_s, blk - 2, slot, to_hbm=True)

        base = slot * buf_rows
        x = jnp.concatenate([xbuf[pl.ds(base + s, tm, stride=SLAB_PITCH), :] for s in range(slab)],
                            axis=-1).astype(BF16)
        gate = jnp.dot(x, wg_s[...], preferred_element_type=F32)
        up = jnp.dot(x, wu_s[...], preferred_element_type=F32)
        act = (_silu(gate) * up).astype(BF16)
        y = jnp.dot(act, wd_s[...], preferred_element_type=F32)
        for s in range(slab):
            ybuf[pl.ds(base + s, tm, stride=SLAB_PITCH), :] = y[:, s * LANES:(s + 1) * LANES]
        scatter_block(blk, slot)

        @pl.when(blk == n_used - 1)
        def _():
            @pl.when(blk >= 1)
            def _():
                wait_rows(ys_hbm, ybuf, sem_s, blk - 1, 1 - slot, to_hbm=True)
            wait_rows(ys_hbm, ybuf, sem_s, blk, slot, to_hbm=True)


def _moe(block_e, block_new, block_nv, n_used, row_tgt, h2d, w_gate, w_up, w_down):
    n = h2d.shape[0] // TOKEN_ROWS
    assert n & (n - 1) == 0
    d = TOKEN_ROWS * LANES
    de = w_gate.shape[2]
    tm = MOE_TM
    n_blk = block_e.shape[0]
    wmap = lambda i, be, nw, nv, nu: (be[i], 0, 0)
    grid_spec = pltpu.PrefetchScalarGridSpec(
        num_scalar_prefetch=4,
        grid=(n_blk,),
        in_specs=[pl.BlockSpec(memory_space=pl.ANY),
                  pl.BlockSpec(memory_space=pl.ANY),
                  pl.BlockSpec((1, d, de), wmap),
                  pl.BlockSpec((1, d, de), wmap),
                  pl.BlockSpec((1, de, d), wmap)],
        out_specs=pl.BlockSpec(memory_space=pl.ANY),
        scratch_shapes=[pltpu.SMEM((2 * IDX_CHUNK,), jnp.int32),
                        pltpu.VMEM((2 * tm * SLAB_PITCH, LANES), F32),
                        pltpu.VMEM((2 * tm * SLAB_PITCH, LANES), F32),
                        pltpu.VMEM((d, de), BF16),
                        pltpu.VMEM((d, de), BF16),
                        pltpu.VMEM((de, d), BF16),
                        pltpu.SemaphoreType.DMA,
                        pltpu.SemaphoreType.DMA((2,)),
                        pltpu.SemaphoreType.DMA((2,))])
    return pl.pallas_call(
        functools.partial(_moe_kernel, n_tok=n),
        grid_spec=grid_spec,
        out_shape=jax.ShapeDtypeStruct((TOP_K * n * TOKEN_ROWS, LANES), F32),
        compiler_params=_cparams(("arbitrary",)),
        name="moe",
    )(block_e, block_new, block_nv, n_used, row_tgt, h2d, w_gate, w_up, w_down)


def _final_kernel(gw_hbm, ys_ref, h_ref, x1_ref, mod_ref, wg_ref, wu_ref, wd_ref, g_ref, b_ref, o_ref,
                  gw_s, acc_ref, sem_i):
    i = pl.program_id(0)
    tm = x1_ref.shape[0]
    per_step = tm * TOP_K
    cp = pltpu.make_async_copy(gw_hbm.at[pl.ds(pl.multiple_of(i * per_step, per_step), per_step)], gw_s, sem_i)
    cp.start()
    cp.wait()

    def combine(t, carry):
        rows = pl.ds(pl.multiple_of(t * TOKEN_ROWS, TOKEN_ROWS), TOKEN_ROWS)
        acc = gw_s[t * TOP_K] * ys_ref[0, rows, :]
        for k in range(1, TOP_K):
            acc = acc + gw_s[t * TOP_K + k] * ys_ref[k, rows, :]
        acc_ref[pl.ds(pl.multiple_of(t * SLAB_PITCH, SUBLANES), TOKEN_ROWS), :] = acc
        return carry

    lax.fori_loop(0, tm, combine, 0, unroll=4)
    moe = jnp.concatenate([acc_ref[pl.ds(s, tm, stride=SLAB_PITCH), :] for s in range(TOKEN_ROWS)], axis=-1)
    h = jnp.concatenate([h_ref[pl.ds(s, tm, stride=TOKEN_ROWS), :] for s in range(TOKEN_ROWS)],
                        axis=-1).astype(BF16)
    gate = jnp.dot(h, wg_ref[...], preferred_element_type=F32)
    up = jnp.dot(h, wu_ref[...], preferred_element_type=F32)
    shared = jnp.dot((_silu(gate) * up).astype(BF16), wd_ref[...], preferred_element_type=F32)
    m = mod_ref[0]
    y = ALPHA * x1_ref[...] + (1.0 + m[5:6]) * (moe + shared)
    o_ref[...] = _layer_norm(y, g_ref[...], b_ref[...])


def _final(gate_w, ys3, h2d, x1, mod3, wsg, wsu, wsd, ln_g, ln_b, seq):
    n, d = x1.shape
    de = wsg.shape[1]
    tm = FINAL_TM
    per_step = tm * TOP_K
    assert per_step % IDX_CHUNK == 0
    row = lambda i: (i, 0)
    const = lambda i: (0, 0)
    return pl.pallas_call(
        _final_kernel,
        grid=(n // tm,),
        in_specs=[pl.BlockSpec(memory_space=pl.ANY),
                  pl.BlockSpec((TOP_K, tm * TOKEN_ROWS, LANES), lambda i: (0, i, 0)),
                  pl.BlockSpec((tm * TOKEN_ROWS, LANES), row),
                  pl.BlockSpec((tm, d), row),
                  pl.BlockSpec((1, 6, d), lambda i: (i * tm // seq, 0, 0)),
                  pl.BlockSpec((d, de), const),
                  pl.BlockSpec((d, de), const),
                  pl.BlockSpec((de, d), const),
                  pl.BlockSpec((1, d), const),
                  pl.BlockSpec((1, d), const)],
        out_specs=pl.BlockSpec((tm, d), row),
        out_shape=jax.ShapeDtypeStruct((n, d), F32),
        scratch_shapes=[pltpu.SMEM((per_step,), F32),
                        pltpu.VMEM((tm * SLAB_PITCH, LANES), F32),
                        pltpu.SemaphoreType.DMA],
        compiler_params=_cparams(("arbitrary",)),
        name="final",
    )(gate_w, ys3, h2d, x1, mod3, wsg, wsu, wsd, ln_g, ln_b)


def _dispatch_tables(eidx, tile_counts, n):
    tm = MOE_TM
    a = n * TOP_K
    i32 = jnp.int32
    experts = jnp.arange(N_EXPERTS, dtype=i32)
    counts = jnp.sum(tile_counts, axis=(0, 1)).astype(i32)
    padded = (counts + tm - 1) // tm * tm
    pad_end = jnp.cumsum(padded)
    starts = pad_end - padded
    n_blk = a // tm + N_EXPERTS
    blk_start = jnp.arange(n_blk, dtype=i32) * tm
    n_used = pad_end[-1] // tm
    in_use = jnp.arange(n_blk) < n_used
    raw_e = jnp.minimum(jnp.sum((pad_end[None, :] <= blk_start[:, None]).astype(i32), axis=1), N_EXPERTS - 1)
    last_e = jnp.sum(jnp.where(jnp.arange(n_blk) == n_used - 1, raw_e, 0))
    block_e = jnp.where(in_use, raw_e, last_e)
    onehot = block_e[:, None] == experts[None, :]
    cnt_b = jnp.sum(jnp.where(onehot, counts[None, :], 0), axis=1)
    start_b = jnp.sum(jnp.where(onehot, starts[None, :], 0), axis=1)
    block_nv = jnp.where(in_use, jnp.clip(cnt_b - (blk_start - start_b), 0, tm), 0).astype(i32)
    block_new = jnp.concatenate([jnp.ones((1,), i32), (block_e[1:] != block_e[:-1]).astype(i32)])
    dummy_keys = jnp.where(jnp.arange(tm, dtype=i32)[None, :] < (padded - counts)[:, None],
                           experts[:, None], N_EXPERTS).reshape(-1)
    keys = jnp.concatenate([eidx.reshape(-1), dummy_keys])
    tgt = (jnp.arange(TOP_K, dtype=i32)[None, :] * n + jnp.arange(n, dtype=i32)[:, None]).reshape(-1)
    payload = jnp.concatenate([tgt, jnp.zeros((N_EXPERTS * tm,), i32)])
    _, row_tgt = lax.sort((keys, payload), num_keys=1, is_stable=True)
    return block_e.astype(i32), block_new, block_nv, n_used.astype(i32).reshape(1), row_tgt


def kernel(x, c, w_mod, b_mod, w_in, conv_w, attn_sinks, w_out, ln1_g, ln1_b, w_router, router_bias,
           w_gate, w_up, w_down, ws_gate, ws_up, ws_down, ln2_g, ln2_b):
    b, s, d = x.shape
    n = b * s
    attn_w = N_Q_HEADS * HEAD_DIM
    kv_w = N_KV_HEADS * HEAD_DIM
    conv_wd = d - attn_w
    in_w = attn_w + 2 * kv_w + 3 * conv_wd
    x2 = x.reshape(n, d)
    c8 = jnp.zeros((SUBLANES, d), F32).at[:b].set(c)
    for l in range(DEPTH):
        mod = _mod(c8, w_mod[l], b_mod[l].reshape(1, -1))[:b]
        mod3 = mod.reshape(b, 6, d)
        w_in3 = w_in[l].astype(BF16).reshape(d, in_w // INPROJ_TN, INPROJ_TN).transpose(1, 0, 2)
        proj = _inproj(x2, mod3, w_in3, s)
        mix = _mixer(proj, attn_sinks[l].reshape(1, -1), conv_w[l], b, s, attn_w, kv_w, conv_wd)
        x1, h2d, logits = _outproj(mix, x2, mod3, w_out[l].astype(BF16), ln1_g[l].reshape(1, -1),
                                   ln1_b[l].reshape(1, -1), w_router[l], s)
        eidx, gate_w, tile_counts = _route(logits, router_bias[l].reshape(1, -1))
        block_e, block_new, block_nv, n_used, row_tgt = _dispatch_tables(eidx, tile_counts, n)
        ys = _moe(block_e, block_new, block_nv, n_used, row_tgt, h2d, w_gate[l], w_up[l], w_down[l])
        x2 = _final(gate_w.reshape(-1), ys.reshape(TOP_K, n * TOKEN_ROWS, LANES), h2d, x1, mod3,
                    ws_gate[l].astype(BF16), ws_up[l].astype(BF16), ws_down[l].astype(BF16),
                    ln2_g[l].reshape(1, -1), ln2_b[l].reshape(1, -1), s)
    return x2.reshape(b, s, d)
```

```python
import functools

import jax
import jax.numpy as jnp
from jax import lax
from jax.experimental import pallas as pl
from jax.experimental.pallas import tpu as pltpu

HEAD_DIM = 64
N_Q_HEADS = 16
N_KV_HEADS = 4
GQA = N_Q_HEADS // N_KV_HEADS
CONV_K = 3
WINDOW = 128
Q_BLOCK = 128
N_EXPERTS = 64
TOP_K = 8
N_GROUPS = 8
GROUP_SIZE = N_EXPERTS // N_GROUPS
TOPK_GROUPS = 4
ROUTED_SCALE = 2.5
DEPTH = 1
ALPHA = (2.0 * DEPTH) ** 0.25
LN_EPS = 1e-5

LANES = 128
SUBLANES = 8
TOKEN_ROWS = 16
SLAB_PITCH = 24
VMEM_LIMIT = 56 * 1024 * 1024

MOD_TN = 1024
INPROJ_TM = 512
INPROJ_TN = 1536
OUTPROJ_TM = 256
ROUTE_TM = 512
MOE_TM = 256
FINAL_TM = 128
IDX_CHUNK = 1024
ISSUE_UNROLL = 8

F32 = jnp.float32
BF16 = jnp.bfloat16


def _cparams(sem):
    return pltpu.CompilerParams(dimension_semantics=sem, vmem_limit_bytes=VMEM_LIMIT)


def _silu(v):
    return v * jax.nn.sigmoid(v)


def _layer_norm(y, g, b):
    mu = jnp.mean(y, axis=-1, keepdims=True)
    yc = y - mu
    var = jnp.mean(yc * yc, axis=-1, keepdims=True)
    return yc * lax.rsqrt(var + LN_EPS) * g + b


def _mod_kernel(c_ref, w_ref, b_ref, o_ref):
    cs = _silu(c_ref[...]).astype(BF16)
    o_ref[...] = jnp.dot(cs, w_ref[...].astype(BF16), preferred_element_type=F32) + b_ref[...]


def _mod(c8, w_mod, b_mod):
    d, n = w_mod.shape
    return pl.pallas_call(
        _mod_kernel,
        grid=(n // MOD_TN,),
        in_specs=[pl.BlockSpec((SUBLANES, d), lambda j: (0, 0)),
                  pl.BlockSpec((d, MOD_TN), lambda j: (0, j)),
                  pl.BlockSpec((1, MOD_TN), lambda j: (0, j))],
        out_specs=pl.BlockSpec((SUBLANES, MOD_TN), lambda j: (0, j)),
        out_shape=jax.ShapeDtypeStruct((SUBLANES, n), F32),
        compiler_params=_cparams(("arbitrary",)),
        name="mod",
    )(c8, w_mod, b_mod)


def _inproj_kernel(x_ref, mod_ref, w_ref, o_ref, h_ref):
    j = pl.program_id(1)

    @pl.when(j == 0)
    def _():
        m = mod_ref[0]
        h_ref[...] = (x_ref[...] * (1.0 + m[1:2]) + m[0:1]).astype(BF16)

    o_ref[...] = jnp.dot(h_ref[...], w_ref[0], preferred_element_type=F32).astype(BF16)


def _inproj(x2, mod3, w_in3, seq):
    n, d = x2.shape
    nj, _, tn = w_in3.shape
    tm = INPROJ_TM
    return pl.pallas_call(
        _inproj_kernel,
        grid=(n // tm, nj),
        in_specs=[pl.BlockSpec((tm, d), lambda i, j: (i, 0)),
                  pl.BlockSpec((1, 6, d), lambda i, j: (i * tm // seq, 0, 0)),
                  pl.BlockSpec((1, d, tn), lambda i, j: (j, 0, 0))],
        out_specs=pl.BlockSpec((tm, tn), lambda i, j: (i, j)),
        out_shape=jax.ShapeDtypeStruct((n, nj * tn), BF16),
        scratch_shapes=[pltpu.VMEM((tm, d), BF16)],
        compiler_params=_cparams(("arbitrary", "arbitrary")),
        name="inproj",
    )(x2, mod3, w_in3)


LOG2E = 1.4426950408889634
CONV_CHUNK = 256


def _mixer_kernel(cur_ref, pk_ref, pv_ref, prow_ref, sink_ref, cw_ref, o_ref, bias_ref, s_ref, p_ref,
                  *, attn_w, kv_w, conv_w):
    first_step = (pl.program_id(0) == 0) & (pl.program_id(1) == 0)
    has_prev = pl.program_id(1) > 0
    qb = Q_BLOCK

    @pl.when(first_step)
    def _():
        qi = lax.broadcasted_iota(jnp.int32, (qb, 2 * qb), 0)
        kj = lax.broadcasted_iota(jnp.int32, (qb, 2 * qb), 1)
        dist = qi + qb - kj
        window = (dist >= 0) & (dist < WINDOW)
        distf = dist.astype(F32)
        for h in range(N_Q_HEADS):
            slope = 2.0 ** (-8.0 * (h + 1) / N_Q_HEADS)
            b1 = jnp.where(window, distf * (-slope * LOG2E), -jnp.inf)
            bias_ref[1, h] = b1
            bias_ref[0, h] = jnp.where(kj >= qb, b1, -jnp.inf)

    table = jnp.where(has_prev, 1, 0)
    sinks2 = sink_ref[...] * LOG2E
    for g in range(N_KV_HEADS):
        kcol = attn_w + g * HEAD_DIM
        kg = jnp.concatenate([pk_ref[:, g * HEAD_DIM:(g + 1) * HEAD_DIM], cur_ref[:, kcol:kcol + HEAD_DIM]], axis=0)
        for j in range(GQA):
            h = g * GQA + j
            q = cur_ref[:, h * HEAD_DIM:(h + 1) * HEAD_DIM]
            s = lax.dot_general(q, kg, (((1,), (1,)), ((), ())), preferred_element_type=F32)
            s_ref[h] = s * (HEAD_DIM ** -0.5 * LOG2E) + bias_ref[table, h]
    denoms = []
    for h in range(N_Q_HEADS):
        s2 = s_ref[h]
        sink2 = sinks2[:, h:h + 1]
        m2 = jnp.maximum(jnp.max(s2, axis=-1, keepdims=True), sink2)
        p = jnp.exp2(s2 - m2)
        denoms.append(jnp.sum(p, axis=-1, keepdims=True) + jnp.exp2(sink2 - m2))
        p_ref[h] = p.astype(BF16)
    for g in range(N_KV_HEADS):
        vcol = attn_w + kv_w + g * HEAD_DIM
        vg = jnp.concatenate([pv_ref[:, g * HEAD_DIM:(g + 1) * HEAD_DIM], cur_ref[:, vcol:vcol + HEAD_DIM]], axis=0)
        for j in range(GQA):
            h = g * GQA + j
            o = jnp.dot(p_ref[h], vg, preferred_element_type=F32) / denoms[h]
            o_ref[:, h * HEAD_DIM:(h + 1) * HEAD_DIM] = o.astype(BF16)

    c0 = attn_w + 2 * kv_w
    prev_on = jnp.where(has_prev, 1.0, 0.0)
    ri = lax.broadcasted_iota(jnp.int32, (qb, CONV_CHUNK), 0)
    for c in range(conv_w // CONV_CHUNK):
        lo = c * CONV_CHUNK
        cols = lambda base: slice(c0 + base + lo, c0 + base + lo + CONV_CHUNK)
        cb = cur_ref[:, cols(0)].astype(F32)
        u = cur_ref[:, cols(conv_w)].astype(F32) * cur_ref[:, cols(2 * conv_w)].astype(F32)
        up = prow_ref[:, cols(conv_w)].astype(F32) * prow_ref[:, cols(2 * conv_w)].astype(F32) * prev_on
        pm1 = up[15:16]
        pm2 = up[14:15]
        u1 = jnp.where(ri == 0, pm1, pltpu.roll(u, 1, 0))
        u2 = jnp.where(ri == 0, pm2, jnp.where(ri == 1, pm1, pltpu.roll(u, 2, 0)))
        cw = cw_ref[:, lo:lo + CONV_CHUNK]
        conv = cb * (cw[0:1] * u2 + cw[1:2] * u1 + cw[2:3] * u)
        o_ref[:, attn_w + lo:attn_w + lo + CONV_CHUNK] = conv.astype(BF16)


def _mixer(proj, sinks2, conv_w, batch, seq, attn_w, kv_w, conv_wd):
    n, in_w = proj.shape
    nb = seq // Q_BLOCK
    kv_blk0 = attn_w // kv_w
    sub16 = Q_BLOCK // 16

    def cur_map(b, i):
        return (b * nb + i, 0)

    def prev_map(col):
        return lambda b, i: (b * nb + jnp.maximum(i - 1, 0), col)

    def prow_map(b, i):
        return (jnp.maximum((b * nb + i) * sub16 - 1, 0), 0)

    kern = functools.partial(_mixer_kernel, attn_w=attn_w, kv_w=kv_w, conv_w=conv_wd)
    return pl.pallas_call(
        kern,
        grid=(batch, nb),
        in_specs=[pl.BlockSpec((Q_BLOCK, in_w), cur_map),
                  pl.BlockSpec((Q_BLOCK, kv_w), prev_map(kv_blk0)),
                  pl.BlockSpec((Q_BLOCK, kv_w), prev_map(kv_blk0 + 1)),
                  pl.BlockSpec((16, in_w), prow_map),
                  pl.BlockSpec((1, N_Q_HEADS), lambda b, i: (0, 0)),
                  pl.BlockSpec((CONV_K, conv_wd), lambda b, i: (0, 0))],
        out_specs=pl.BlockSpec((Q_BLOCK, attn_w + conv_wd), cur_map),
        out_shape=jax.ShapeDtypeStruct((n, attn_w + conv_wd), BF16),
        scratch_shapes=[pltpu.VMEM((2, N_Q_HEADS, Q_BLOCK, 2 * Q_BLOCK), F32),
                        pltpu.VMEM((N_Q_HEADS, Q_BLOCK, 2 * Q_BLOCK), F32),
                        pltpu.VMEM((N_Q_HEADS, Q_BLOCK, 2 * Q_BLOCK), BF16)],
        compiler_params=_cparams(("arbitrary", "arbitrary")),
        name="mixer",
    )(proj, proj, proj, proj, sinks2, conv_w)


def _split_bf16(v):
    hi = v.astype(BF16)
    lo = (v - hi.astype(F32)).astype(BF16)
    return hi, lo


def _outproj_kernel(mix_ref, x_ref, mod_ref, w_ref, g_ref, b_ref, wr_ref, x1_ref, h2_ref, lg_ref):
    m = mod_ref[0]
    mix = jnp.dot(mix_ref[...], w_ref[...], preferred_element_type=F32)
    x1 = _layer_norm(ALPHA * x_ref[...] + (1.0 + m[2:3]) * mix, g_ref[...], b_ref[...])
    x1_ref[...] = x1
    h2 = x1 * (1.0 + m[4:5]) + m[3:4]
    tm = h2.shape[0]
    for s in range(TOKEN_ROWS):
        h2_ref[pl.ds(s, tm, stride=TOKEN_ROWS), :] = h2[:, s * LANES:(s + 1) * LANES]
    h_hi, h_lo = _split_bf16(h2)
    w_hi, w_lo = _split_bf16(wr_ref[...])
    lg_ref[...] = (jnp.dot(h_hi, w_hi, preferred_element_type=F32)
                   + (jnp.dot(h_hi, w_lo, preferred_element_type=F32)
                      + jnp.dot(h_lo, w_hi, preferred_element_type=F32)))


def _outproj(mix, x2, mod3, w_out_bf, ln_g, ln_b, w_router, seq):
    n, d = x2.shape
    tm = OUTPROJ_TM
    ne = w_router.shape[1]
    row = lambda i: (i, 0)
    const = lambda i: (0, 0)
    return pl.pallas_call(
        _outproj_kernel,
        grid=(n // tm,),
        in_specs=[pl.BlockSpec((tm, d), row),
                  pl.BlockSpec((tm, d), row),
                  pl.BlockSpec((1, 6, d), lambda i: (i * tm // seq, 0, 0)),
                  pl.BlockSpec((d, d), const),
                  pl.BlockSpec((1, d), const),
                  pl.BlockSpec((1, d), const),
                  pl.BlockSpec((d, ne), const)],
        out_specs=[pl.BlockSpec((tm, d), row),
                   pl.BlockSpec((tm * TOKEN_ROWS, LANES), row),
                   pl.BlockSpec((tm, ne), row)],
        out_shape=[jax.ShapeDtypeStruct((n, d), F32),
                   jax.ShapeDtypeStruct((n * TOKEN_ROWS, LANES), F32),
                   jax.ShapeDtypeStruct((n, ne), F32)],
        compiler_params=_cparams(("arbitrary",)),
        name="outproj",
    )(mix, x2, mod3, w_out_bf, ln_g, ln_b, w_router)


def _first_argmax(v, lane_f):
    m = jnp.max(v, axis=-1, keepdims=True)
    idx = jnp.min(jnp.where(v == m, lane_f, float(N_EXPERTS)), axis=-1, keepdims=True)
    return m, idx


def _route_kernel(lg_ref, bias_ref, eidx_ref, w_ref, cnt_ref):
    scores = jax.nn.sigmoid(lg_ref[...])
    sel = scores + bias_ref[...]
    tm = sel.shape[0]
    lane = lax.broadcasted_iota(jnp.int32, (tm, N_EXPERTS), 1)
    lane_f = lane.astype(F32)
    grp = lane // GROUP_SIZE
    neg = -jnp.inf
    gs = []
    for g in range(N_GROUPS):
        vg = jnp.where(grp == g, sel, neg)
        m1, i1 = _first_argmax(vg, lane_f)
        m2 = jnp.max(jnp.where(lane_f == i1, neg, vg), axis=-1, keepdims=True)
        gs.append(m1 + m2)
    keep = jnp.zeros((tm, N_EXPERTS), F32)
    for g in range(N_GROUPS):
        rank = jnp.zeros((tm, 1), F32)
        for o in range(N_GROUPS):
            if o == g:
                continue
            ahead = (gs[o] >= gs[g]) if o < g else (gs[o] > gs[g])
            rank = rank + jnp.where(ahead, 1.0, 0.0)
        keep = jnp.where(grp == g, jnp.where(rank < TOPK_GROUPS, 1.0, 0.0), keep)
    cand = jnp.where(keep > 0.5, sel, neg)
    idxs, ws = [], []
    chosen = jnp.zeros((tm, N_EXPERTS), F32)
    for _ in range(TOP_K):
        _, ik = _first_argmax(cand, lane_f)
        hit = lane_f == ik
        ws.append(jnp.sum(jnp.where(hit, scores, 0.0), axis=-1, keepdims=True))
        idxs.append(ik)
        cand = jnp.where(hit, neg, cand)
        chosen = jnp.where(hit, 1.0, chosen)
    cnt_ref[0] = jnp.sum(chosen, axis=0, keepdims=True)
    wsum = ws[0]
    for k in range(1, TOP_K):
        wsum = wsum + ws[k]
    col = lax.broadcasted_iota(jnp.int32, (tm, TOP_K), 1)
    eidx = jnp.zeros((tm, TOP_K), F32)
    wout = jnp.zeros((tm, TOP_K), F32)
    for k in range(TOP_K):
        eidx = jnp.where(col == k, idxs[k], eidx)
        wout = jnp.where(col == k, ws[k] / wsum * ROUTED_SCALE, wout)
    eidx_ref[...] = eidx.astype(jnp.int32)
    w_ref[...] = wout


def _route(logits, bias2):
    n, ne = logits.shape
    tm = ROUTE_TM
    row = lambda i: (i, 0)
    return pl.pallas_call(
        _route_kernel,
        grid=(n // tm,),
        in_specs=[pl.BlockSpec((tm, ne), row), pl.BlockSpec((1, ne), lambda i: (0, 0))],
        out_specs=[pl.BlockSpec((tm, TOP_K), row), pl.BlockSpec((tm, TOP_K), row),
                   pl.BlockSpec((1, 1, ne), lambda i: (i, 0, 0))],
        out_shape=[jax.ShapeDtypeStruct((n, TOP_K), jnp.int32), jax.ShapeDtypeStruct((n, TOP_K), F32),
                   jax.ShapeDtypeStruct((n // tm, 1, ne), F32)],
        compiler_params=_cparams(("arbitrary",)),
        name="route",
    )(logits, bias2)


def _issue_rows(lo, hi, issue_one):
    n_full = (hi - lo) // ISSUE_UNROLL

    def chunk(c, carry):
        for u in range(ISSUE_UNROLL):
            issue_one(lo + c * ISSUE_UNROLL + u)
        return carry

    def tail(r, carry):
        issue_one(r)
        return carry

    lax.fori_loop(0, n_full, chunk, 0)
    lax.fori_loop(lo + n_full * ISSUE_UNROLL, hi, tail, 0)


def _hbm_slab(ref, row):
    return ref.at[pl.ds(pl.multiple_of(row * TOKEN_ROWS, TOKEN_ROWS), TOKEN_ROWS), :]


def _moe_kernel(be_ref, new_ref, nv_ref, nused_ref, tgt_hbm, h_hbm, wg_ref, wu_ref, wd_ref, ys_hbm,
                idx_s, xbuf, ybuf, wg_s, wu_s, wd_s, sem_i, sem_g, sem_s, *, n_tok):
    blk = pl.program_id(0)
    tm = MOE_TM
    per = IDX_CHUNK // tm
    per_log2 = per.bit_length() - 1
    n_used = nused_ref[0]
    slab = TOKEN_ROWS
    buf_rows = tm * SLAB_PITCH

    def staged(buf, base, r):
        return buf.at[pl.ds(pl.multiple_of(base + r * SLAB_PITCH, SUBLANES), slab), :]

    def idx_copy(b):
        c = b >> per_log2
        return pltpu.make_async_copy(
            tgt_hbm.at[pl.ds(pl.multiple_of(c * IDX_CHUNK, IDX_CHUNK), IDX_CHUNK)],
            idx_s.at[pl.ds(pl.multiple_of((c & 1) * IDX_CHUNK, IDX_CHUNK), IDX_CHUNK)], sem_i)

    def idx_base(b):
        return ((b >> per_log2) & 1) * IDX_CHUNK + (b & (per - 1)) * tm

    def gather_block(b, slot):
        ibase = idx_base(b)
        xbase = slot * buf_rows

        def one(r):
            tok = idx_s[ibase + r] & (n_tok - 1)
            pltpu.make_async_copy(_hbm_slab(h_hbm, tok), staged(xbuf, xbase, r), sem_g.at[slot]).start()
        _issue_rows(0, nv_ref[b], one)

    def scatter_block(b, slot):
        ibase = idx_base(b)
        ybase = slot * buf_rows

        def one(r):
            pltpu.make_async_copy(staged(ybuf, ybase, r), _hbm_slab(ys_hbm, idx_s[ibase + r]),
                                  sem_s.at[slot]).start()
        _issue_rows(0, nv_ref[b], one)

    def wait_rows(hbm, buf, sem, b, slot, to_hbm):
        rows = nv_ref[b] * slab
        v = buf.at[pl.ds(pl.multiple_of(slot * buf_rows, SUBLANES), rows), :]
        hv = hbm.at[pl.ds(0, rows), :]
        (pltpu.make_async_copy(v, hv, sem.at[slot]) if to_hbm else pltpu.make_async_copy(hv, v, sem.at[slot])).wait()

    @pl.when(blk < n_used)
    def _():
        slot = blk & 1

        @pl.when(blk == 0)
        def _():
            xbuf[...] = jnp.zeros(xbuf.shape, F32)
            first = idx_copy(0)
            first.start()
            first.wait()
            gather_block(0, 0)

        @pl.when(blk + 1 < n_used)
        def _():
            @pl.when(((blk + 1) & (per - 1)) == 0)
            def _():
                nxt = idx_copy(blk + 1)
                nxt.start()
                nxt.wait()
            gather_block(blk + 1, 1 - slot)

        @pl.when(new_ref[blk] == 1)
        def _():
            wg_s[...] = wg_ref[0].astype(BF16)
            wu_s[...] = wu_ref[0].astype(BF16)
            wd_s[...] = wd_ref[0].astype(BF16)

        wait_rows(h_hbm, xbuf, sem_g, blk, slot, to_hbm=False)

        @pl.when(blk >= 2)
        def _():
            wait_rows(ys_hbm, ybuf, sem_s, blk - 2, slot, to_hbm=True)

        base = slot * buf_rows
        x = jnp.concatenate([xbuf[pl.ds(base + s, tm, stride=SLAB_PITCH), :] for s in range(slab)],
                            axis=-1).astype(BF16)
        gate = jnp.dot(x, wg_s[...], preferred_element_type=F32)
        up = jnp.dot(x, wu_s[...], preferred_element_type=F32)
        act = (_silu(gate) * up).astype(BF16)
        y = jnp.dot(act, wd_s[...], preferred_element_type=F32)
        for s in range(slab):
            ybuf[pl.ds(base + s, tm, stride=SLAB_PITCH), :] = y[:, s * LANES:(s + 1) * LANES]
        scatter_block(blk, slot)

        @pl.when(blk == n_used - 1)
        def _():
            @pl.when(blk >= 1)
            def _():
                wait_rows(ys_hbm, ybuf, sem_s, blk - 1, 1 - slot, to_hbm=True)
            wait_rows(ys_hbm, ybuf, sem_s, blk, slot, to_hbm=True)


def _moe(block_e, block_new, block_nv, n_used, row_tgt, h2d, w_gate, w_up, w_down):
    n = h2d.shape[0] // TOKEN_ROWS
    assert n & (n - 1) == 0
    d = TOKEN_ROWS * LANES
    de = w_gate.shape[2]
    tm = MOE_TM
    n_blk = block_e.shape[0]
    wmap = lambda i, be, nw, nv, nu: (be[i], 0, 0)
    grid_spec = pltpu.PrefetchScalarGridSpec(
        num_scalar_prefetch=4,
        grid=(n_blk,),
        in_specs=[pl.BlockSpec(memory_space=pl.ANY),
                  pl.BlockSpec(memory_space=pl.ANY),
                  pl.BlockSpec((1, d, de), wmap),
                  pl.BlockSpec((1, d, de), wmap),
                  pl.BlockSpec((1, de, d), wmap)],
        out_specs=pl.BlockSpec(memory_space=pl.ANY),
        scratch_shapes=[pltpu.SMEM((2 * IDX_CHUNK,), jnp.int32),
                        pltpu.VMEM((2 * tm * SLAB_PITCH, LANES), F32),
                        pltpu.VMEM((2 * tm * SLAB_PITCH, LANES), F32),
                        pltpu.VMEM((d, de), BF16),
                        pltpu.VMEM((d, de), BF16),
                        pltpu.VMEM((de, d), BF16),
                        pltpu.SemaphoreType.DMA,
                        pltpu.SemaphoreType.DMA((2,)),
                        pltpu.SemaphoreType.DMA((2,))])
    return pl.pallas_call(
        functools.partial(_moe_kernel, n_tok=n),
        grid_spec=grid_spec,
        out_shape=jax.ShapeDtypeStruct((TOP_K * n * TOKEN_ROWS, LANES), F32),
        compiler_params=_cparams(("arbitrary",)),
        name="moe",
    )(block_e, block_new, block_nv, n_used, row_tgt, h2d, w_gate, w_up, w_down)


def _final_kernel(gw_hbm, ys_ref, h_ref, x1_ref, mod_ref, wg_ref, wu_ref, wd_ref, g_ref, b_ref, o_ref,
                  gw_s, acc_ref, sem_i):
    i = pl.program_id(0)
    tm = x1_ref.shape[0]
    per_step = tm * TOP_K
    cp = pltpu.make_async_copy(gw_hbm.at[pl.ds(pl.multiple_of(i * per_step, per_step), per_step)], gw_s, sem_i)
    cp.start()
    cp.wait()

    def combine(t, carry):
        rows = pl.ds(pl.multiple_of(t * TOKEN_ROWS, TOKEN_ROWS), TOKEN_ROWS)
        acc = gw_s[t * TOP_K] * ys_ref[0, rows, :]
        for k in range(1, TOP_K):
            acc = acc + gw_s[t * TOP_K + k] * ys_ref[k, rows, :]
        acc_ref[pl.ds(pl.multiple_of(t * SLAB_PITCH, SUBLANES), TOKEN_ROWS), :] = acc
        return carry

    lax.fori_loop(0, tm, combine, 0, unroll=4)
    moe = jnp.concatenate([acc_ref[pl.ds(s, tm, stride=SLAB_PITCH), :] for s in range(TOKEN_ROWS)], axis=-1)
    h = jnp.concatenate([h_ref[pl.ds(s, tm, stride=TOKEN_ROWS), :] for s in range(TOKEN_ROWS)],
                        axis=-1).astype(BF16)
    gate = jnp.dot(h, wg_ref[...], preferred_element_type=F32)
    up = jnp.dot(h, wu_ref[...], preferred_element_type=F32)
    shared = jnp.dot((_silu(gate) * up).astype(BF16), wd_ref[...], preferred_element_type=F32)
    m = mod_ref[0]
    y = ALPHA * x1_ref[...] + (1.0 + m[5:6]) * (moe + shared)
    o_ref[...] = _layer_norm(y, g_ref[...], b_ref[...])


def _final(gate_w, ys3, h2d, x1, mod3, wsg, wsu, wsd, ln_g, ln_b, seq):
    n, d = x1.shape
    de = wsg.shape[1]
    tm = FINAL_TM
    per_step = tm * TOP_K
    assert per_step % IDX_CHUNK == 0
    row = lambda i: (i, 0)
    const = lambda i: (0, 0)
    return pl.pallas_call(
        _final_kernel,
        grid=(n // tm,),
        in_specs=[pl.BlockSpec(memory_space=pl.ANY),
                  pl.BlockSpec((TOP_K, tm * TOKEN_ROWS, LANES), lambda i: (0, i, 0)),
                  pl.BlockSpec((tm * TOKEN_ROWS, LANES), row),
                  pl.BlockSpec((tm, d), row),
                  pl.BlockSpec((1, 6, d), lambda i: (i * tm // seq, 0, 0)),
                  pl.BlockSpec((d, de), const),
                  pl.BlockSpec((d, de), const),
                  pl.BlockSpec((de, d), const),
                  pl.BlockSpec((1, d), const),
                  pl.BlockSpec((1, d), const)],
        out_specs=pl.BlockSpec((tm, d), row),
        out_shape=jax.ShapeDtypeStruct((n, d), F32),
        scratch_shapes=[pltpu.SMEM((per_step,), F32),
                        pltpu.VMEM((tm * SLAB_PITCH, LANES), F32),
                        pltpu.SemaphoreType.DMA],
        compiler_params=_cparams(("arbitrary",)),
        name="final",
    )(gate_w, ys3, h2d, x1, mod3, wsg, wsu, wsd, ln_g, ln_b)


def _dispatch_tables(eidx, tile_counts, n):
    tm = MOE_TM
    a = n * TOP_K
    i32 = jnp.int32
    experts = jnp.arange(N_EXPERTS, dtype=i32)
    counts = jnp.sum(tile_counts, axis=(0, 1)).astype(i32)
    padded = (counts + tm - 1) // tm * tm
    pad_end = jnp.cumsum(padded)
    starts = pad_end - padded
    n_blk = a // tm + N_EXPERTS
    blk_start = jnp.arange(n_blk, dtype=i32) * tm
    n_used = pad_end[-1] // tm
    in_use = jnp.arange(n_blk) < n_used
    raw_e = jnp.minimum(jnp.sum((pad_end[None, :] <= blk_start[:, None]).astype(i32), axis=1), N_EXPERTS - 1)
    last_e = jnp.sum(jnp.where(jnp.arange(n_blk) == n_used - 1, raw_e, 0))
    block_e = jnp.where(in_use, raw_e, last_e)
    onehot = block_e[:, None] == experts[None, :]
    cnt_b = jnp.sum(jnp.where(onehot, counts[None, :], 0), axis=1)
    start_b = jnp.sum(jnp.where(onehot, starts[None, :], 0), axis=1)
    block_nv = jnp.where(in_use, jnp.clip(cnt_b - (blk_start - start_b), 0, tm), 0).astype(i32)
    block_new = jnp.concatenate([jnp.ones((1,), i32), (block_e[1:] != block_e[:-1]).astype(i32)])
    dummy_keys = jnp.where(jnp.arange(tm, dtype=i32)[None, :] < (padded - counts)[:, None],
                           experts[:, None], N_EXPERTS).reshape(-1)
    keys = jnp.concatenate([eidx.reshape(-1), dummy_keys])
    tgt = (jnp.arange(TOP_K, dtype=i32)[None, :] * n + jnp.arange(n, dtype=i32)[:, None]).reshape(-1)
    payload = jnp.concatenate([tgt, jnp.zeros((N_EXPERTS * tm,), i32)])
    _, row_tgt = lax.sort((keys, payload), num_keys=1, is_stable=True)
    return block_e.astype(i32), block_new, block_nv, n_used.astype(i32).reshape(1), row_tgt


def kernel(x, c, w_mod, b_mod, w_in, conv_w, attn_sinks, w_out, ln1_g, ln1_b, w_router, router_bias,
           w_gate, w_up, w_down, ws_gate, ws_up, ws_down, ln2_g, ln2_b):
    b, s, d = x.shape
    n = b * s
    attn_w = N_Q_HEADS * HEAD_DIM
    kv_w = N_KV_HEADS * HEAD_DIM
    conv_wd = d - attn_w
    in_w = attn_w + 2 * kv_w + 3 * conv_wd
    x2 = x.reshape(n, d)
    c8 = jnp.zeros((SUBLANES, d), F32).at[:b].set(c)
    for l in range(DEPTH):
        mod = _mod(c8, w_mod[l], b_mod[l].reshape(1, -1))[:b]
        mod3 = mod.reshape(b, 6, d)
        w_in3 = w_in[l].astype(BF16).reshape(d, in_w // INPROJ_TN, INPROJ_TN).transpose(1, 0, 2)
        proj = _inproj(x2, mod3, w_in3, s)
        mix = _mixer(proj, attn_sinks[l].reshape(1, -1), conv_w[l], b, s, attn_w, kv_w, conv_wd)
        x1, h2d, logits = _outproj(mix, x2, mod3, w_out[l].astype(BF16), ln1_g[l].reshape(1, -1),
                                   ln1_b[l].reshape(1, -1), w_router[l], s)
        eidx, gate_w, tile_counts = _route(logits, router_bias[l].reshape(1, -1))
        block_e, block_new, block_nv, n_used, row_tgt = _dispatch_tables(eidx, tile_counts, n)
        ys = _moe(block_e, block_new, block_nv, n_used, row_tgt, h2d, w_gate[l], w_up[l], w_down[l])
        x2 = _final(gate_w.reshape(-1), ys.reshape(TOP_K, n * TOKEN_ROWS, LANES), h2d, x1, mod3,
                    ws_gate[l].astype(BF16), ws_up[l].astype(BF16), ws_down[l].astype(BF16),
                    ln2_g[l].reshape(1, -1), ln2_b[l].reshape(1, -1), s)
    return x2.reshape(b, s, d)
```

```python
import functools

import jax
import jax.numpy as jnp
from jax import lax
from jax.experimental import pallas as pl
from jax.experimental.pallas import tpu as pltpu

HEAD_DIM = 64
N_Q_HEADS = 16
N_KV_HEADS = 4
GQA = N_Q_HEADS // N_KV_HEADS
CONV_K = 3
WINDOW = 128
Q_BLOCK = 128
N_EXPERTS = 64
TOP_K = 8
N_GROUPS = 8
GROUP_SIZE = N_EXPERTS // N_GROUPS
TOPK_GROUPS = 4
ROUTED_SCALE = 2.5
DEPTH = 1
ALPHA = (2.0 * DEPTH) ** 0.25
LN_EPS = 1e-5

LANES = 128
SUBLANES = 8
TOKEN_ROWS = 16
SLAB_PITCH = 24
VMEM_LIMIT = 56 * 1024 * 1024

MOD_TN = 1024
INPROJ_TM = 512
INPROJ_TN = 1536
OUTPROJ_TM = 256
ROUTE_TM = 512
MOE_TM = 256
FINAL_TM = 128
IDX_CHUNK = 1024
ISSUE_UNROLL = 8

F32 = jnp.float32
BF16 = jnp.bfloat16


def _cparams(sem):
    return pltpu.CompilerParams(dimension_semantics=sem, vmem_limit_bytes=VMEM_LIMIT)


def _silu(v):
    return v * jax.nn.sigmoid(v)


def _layer_norm(y, g, b):
    mu = jnp.mean(y, axis=-1, keepdims=True)
    yc = y - mu
    var = jnp.mean(yc * yc, axis=-1, keepdims=True)
    return yc * lax.rsqrt(var + LN_EPS) * g + b


def _mod_kernel(c_ref, w_ref, b_ref, o_ref):
    cs = _silu(c_ref[...]).astype(BF16)
    o_ref[...] = jnp.dot(cs, w_ref[...].astype(BF16), preferred_element_type=F32) + b_ref[...]


def _mod(c8, w_mod, b_mod):
    d, n = w_mod.shape
    return pl.pallas_call(
        _mod_kernel,
        grid=(n // MOD_TN,),
        in_specs=[pl.BlockSpec((SUBLANES, d), lambda j: (0, 0)),
                  pl.BlockSpec((d, MOD_TN), lambda j: (0, j)),
                  pl.BlockSpec((1, MOD_TN), lambda j: (0, j))],
        out_specs=pl.BlockSpec((SUBLANES, MOD_TN), lambda j: (0, j)),
        out_shape=jax.ShapeDtypeStruct((SUBLANES, n), F32),
        compiler_params=_cparams(("arbitrary",)),
        name="mod",
    )(c8, w_mod, b_mod)


def _inproj_kernel(x_ref, mod_ref, w_ref, o_ref, h_ref):
    j = pl.program_id(1)

    @pl.when(j == 0)
    def _():
        m = mod_ref[0]
        h_ref[...] = (x_ref[...] * (1.0 + m[1:2]) + m[0:1]).astype(BF16)

    o_ref[...] = jnp.dot(h_ref[...], w_ref[0], preferred_element_type=F32).astype(BF16)


def _inproj(x2, mod3, w_in3, seq):
    n, d = x2.shape
    nj, _, tn = w_in3.shape
    tm = INPROJ_TM
    return pl.pallas_call(
        _inproj_kernel,
        grid=(n // tm, nj),
        in_specs=[pl.BlockSpec((tm, d), lambda i, j: (i, 0)),
                  pl.BlockSpec((1, 6, d), lambda i, j: (i * tm // seq, 0, 0)),
                  pl.BlockSpec((1, d, tn), lambda i, j: (j, 0, 0))],
        out_specs=pl.BlockSpec((tm, tn), lambda i, j: (i, j)),
        out_shape=jax.ShapeDtypeStruct((n, nj * tn), BF16),
        scratch_shapes=[pltpu.VMEM((tm, d), BF16)],
        compiler_params=_cparams(("arbitrary", "arbitrary")),
        name="inproj",
    )(x2, mod3, w_in3)


LOG2E = 1.4426950408889634
CONV_CHUNK = 256


def _mixer_kernel(cur_ref, pk_ref, pv_ref, prow_ref, sink_ref, cw_ref, o_ref, bias_ref, s_ref, p_ref,
                  *, attn_w, kv_w, conv_w):
    first_step = (pl.program_id(0) == 0) & (pl.program_id(1) == 0)
    has_prev = pl.program_id(1) > 0
    qb = Q_BLOCK

    @pl.when(first_step)
    def _():
        qi = lax.broadcasted_iota(jnp.int32, (qb, 2 * qb), 0)
        kj = lax.broadcasted_iota(jnp.int32, (qb, 2 * qb), 1)
        dist = qi + qb - kj
        window = (dist >= 0) & (dist < WINDOW)
        distf = dist.astype(F32)
        for h in range(N_Q_HEADS):
            slope = 2.0 ** (-8.0 * (h + 1) / N_Q_HEADS)
            b1 = jnp.where(window, distf * (-slope * LOG2E), -jnp.inf)
            bias_ref[1, h] = b1
            bias_ref[0, h] = jnp.where(kj >= qb, b1, -jnp.inf)

    table = jnp.where(has_prev, 1, 0)
    sinks2 = sink_ref[...] * LOG2E
    for g in range(N_KV_HEADS):
        kcol = attn_w + g * HEAD_DIM
        kg = jnp.concatenate([pk_ref[:, g * HEAD_DIM:(g + 1) * HEAD_DIM], cur_ref[:, kcol:kcol + HEAD_DIM]], axis=0)
        q4 = jnp.concatenate([cur_ref[:, (g * GQA + j) * HEAD_DIM:(g * GQA + j + 1) * HEAD_DIM]
                              for j in range(GQA)], axis=0)
        s4 = lax.dot_general(q4, kg, (((1,), (1,)), ((), ())), preferred_element_type=F32)
        for j in range(GQA):
            h = g * GQA + j
            s_ref[h * qb:(h + 1) * qb, :] = (s4[j * qb:(j + 1) * qb] * (HEAD_DIM ** -0.5 * LOG2E)
                                             + bias_ref[table, h])
    denoms = []
    for h in range(N_Q_HEADS):
        s2 = s_ref[h * qb:(h + 1) * qb, :]
        sink2 = sinks2[:, h:h + 1]
        m2 = jnp.maximum(jnp.max(s2, axis=-1, keepdims=True), sink2)
        p = jnp.exp2(s2 - m2)
        denoms.append(jnp.sum(p, axis=-1, keepdims=True) + jnp.exp2(sink2 - m2))
        p_ref[h * qb:(h + 1) * qb, :] = p.astype(BF16)
    for g in range(N_KV_HEADS):
        vcol = attn_w + kv_w + g * HEAD_DIM
        vg = jnp.concatenate([pv_ref[:, g * HEAD_DIM:(g + 1) * HEAD_DIM], cur_ref[:, vcol:vcol + HEAD_DIM]], axis=0)
        o4 = jnp.dot(p_ref[g * GQA * qb:(g + 1) * GQA * qb, :], vg, preferred_element_type=F32)
        for j in range(GQA):
            h = g * GQA + j
            o_ref[:, h * HEAD_DIM:(h + 1) * HEAD_DIM] = (o4[j * qb:(j + 1) * qb] / denoms[h]).astype(BF16)

    c0 = attn_w + 2 * kv_w
    prev_on = jnp.where(has_prev, 1.0, 0.0)
    ri = lax.broadcasted_iota(jnp.int32, (qb, CONV_CHUNK), 0)
    for c in range(conv_w // CONV_CHUNK):
        lo = c * CONV_CHUNK
        cols = lambda base: slice(c0 + base + lo, c0 + base + lo + CONV_CHUNK)
        cb = cur_ref[:, cols(0)].astype(F32)
        u = cur_ref[:, cols(conv_w)].astype(F32) * cur_ref[:, cols(2 * conv_w)].astype(F32)
        up = prow_ref[:, cols(conv_w)].astype(F32) * prow_ref[:, cols(2 * conv_w)].astype(F32) * prev_on
        pm1 = up[15:16]
        pm2 = up[14:15]
        u1 = jnp.where(ri == 0, pm1, pltpu.roll(u, 1, 0))
        u2 = jnp.where(ri == 0, pm2, jnp.where(ri == 1, pm1, pltpu.roll(u, 2, 0)))
        cw = cw_ref[:, lo:lo + CONV_CHUNK]
        conv = cb * (cw[0:1] * u2 + cw[1:2] * u1 + cw[2:3] * u)
        o_ref[:, attn_w + lo:attn_w + lo + CONV_CHUNK] = conv.astype(BF16)


def _mixer(proj, sinks2, conv_w, batch, seq, attn_w, kv_w, conv_wd):
    n, in_w = proj.shape
    nb = seq // Q_BLOCK
    kv_blk0 = attn_w // kv_w
    sub16 = Q_BLOCK // 16

    def cur_map(b, i):
        return (b * nb + i, 0)

    def prev_map(col):
        return lambda b, i: (b * nb + jnp.maximum(i - 1, 0), col)

    def prow_map(b, i):
        return (jnp.maximum((b * nb + i) * sub16 - 1, 0), 0)

    kern = functools.partial(_mixer_kernel, attn_w=attn_w, kv_w=kv_w, conv_w=conv_wd)
    return pl.pallas_call(
        kern,
        grid=(batch, nb),
        in_specs=[pl.BlockSpec((Q_BLOCK, in_w), cur_map),
                  pl.BlockSpec((Q_BLOCK, kv_w), prev_map(kv_blk0)),
                  pl.BlockSpec((Q_BLOCK, kv_w), prev_map(kv_blk0 + 1)),
                  pl.BlockSpec((16, in_w), prow_map),
                  pl.BlockSpec((1, N_Q_HEADS), lambda b, i: (0, 0)),
                  pl.BlockSpec((CONV_K, conv_wd), lambda b, i: (0, 0))],
        out_specs=pl.BlockSpec((Q_BLOCK, attn_w + conv_wd), cur_map),
        out_shape=jax.ShapeDtypeStruct((n, attn_w + conv_wd), BF16),
        scratch_shapes=[pltpu.VMEM((2, N_Q_HEADS, Q_BLOCK, 2 * Q_BLOCK), F32),
                        pltpu.VMEM((N_Q_HEADS * Q_BLOCK, 2 * Q_BLOCK), F32),
                        pltpu.VMEM((N_Q_HEADS * Q_BLOCK, 2 * Q_BLOCK), BF16)],
        compiler_params=_cparams(("arbitrary", "arbitrary")),
        name="mixer",
    )(proj, proj, proj, proj, sinks2, conv_w)


def _split_bf16(v):
    hi = v.astype(BF16)
    lo = (v - hi.astype(F32)).astype(BF16)
    return hi, lo


def _outproj_kernel(mix_ref, x_ref, mod_ref, w_ref, g_ref, b_ref, wr_ref, x1_ref, h2_ref, lg_ref):
    m = mod_ref[0]
    mix = jnp.dot(mix_ref[...], w_ref[...], preferred_element_type=F32)
    x1 = _layer_norm(ALPHA * x_ref[...] + (1.0 + m[2:3]) * mix, g_ref[...], b_ref[...])
    x1_ref[...] = x1
    h2 = x1 * (1.0 + m[4:5]) + m[3:4]
    tm = h2.shape[0]
    for s in range(TOKEN_ROWS):
        h2_ref[pl.ds(s, tm, stride=TOKEN_ROWS), :] = h2[:, s * LANES:(s + 1) * LANES]
    h_hi, h_lo = _split_bf16(h2)
    w_hi, w_lo = _split_bf16(wr_ref[...])
    lg_ref[...] = (jnp.dot(h_hi, w_hi, preferred_element_type=F32)
                   + (jnp.dot(h_hi, w_lo, preferred_element_type=F32)
                      + jnp.dot(h_lo, w_hi, preferred_element_type=F32)))


def _outproj(mix, x2, mod3, w_out_bf, ln_g, ln_b, w_router, seq):
    n, d = x2.shape
    tm = OUTPROJ_TM
    ne = w_router.shape[1]
    row = lambda i: (i, 0)
    const = lambda i: (0, 0)
    return pl.pallas_call(
        _outproj_kernel,
        grid=(n // tm,),
        in_specs=[pl.BlockSpec((tm, d), row),
                  pl.BlockSpec((tm, d), row),
                  pl.BlockSpec((1, 6, d), lambda i: (i * tm // seq, 0, 0)),
                  pl.BlockSpec((d, d), const),
                  pl.BlockSpec((1, d), const),
                  pl.BlockSpec((1, d), const),
                  pl.BlockSpec((d, ne), const)],
        out_specs=[pl.BlockSpec((tm, d), row),
                   pl.BlockSpec((tm * TOKEN_ROWS, LANES), row),
                   pl.BlockSpec((tm, ne), row)],
        out_shape=[jax.ShapeDtypeStruct((n, d), F32),
                   jax.ShapeDtypeStruct((n * TOKEN_ROWS, LANES), F32),
                   jax.ShapeDtypeStruct((n, ne), F32)],
        compiler_params=_cparams(("arbitrary",)),
        name="outproj",
    )(mix, x2, mod3, w_out_bf, ln_g, ln_b, w_router)


def _first_argmax(v, lane_f):
    m = jnp.max(v, axis=-1, keepdims=True)
    idx = jnp.min(jnp.where(v == m, lane_f, float(N_EXPERTS)), axis=-1, keepdims=True)
    return m, idx


def _route_kernel(lg_ref, bias_ref, eidx_ref, w_ref, cnt_ref):
    scores = jax.nn.sigmoid(lg_ref[...])
    sel = scores + bias_ref[...]
    tm = sel.shape[0]
    lane = lax.broadcasted_iota(jnp.int32, (tm, N_EXPERTS), 1)
    lane_f = lane.astype(F32)
    grp = lane // GROUP_SIZE
    neg = -jnp.inf
    gs = []
    for g in range(N_GROUPS):
        vg = jnp.where(grp == g, sel, neg)
        m1, i1 = _first_argmax(vg, lane_f)
        m2 = jnp.max(jnp.where(lane_f == i1, neg, vg), axis=-1, keepdims=True)
        gs.append(m1 + m2)
    keep = jnp.zeros((tm, N_EXPERTS), F32)
    for g in range(N_GROUPS):
        rank = jnp.zeros((tm, 1), F32)
        for o in range(N_GROUPS):
            if o == g:
                continue
            ahead = (gs[o] >= gs[g]) if o < g else (gs[o] > gs[g])
            rank = rank + jnp.where(ahead, 1.0, 0.0)
        keep = jnp.where(grp == g, jnp.where(rank < TOPK_GROUPS, 1.0, 0.0), keep)
    cand = jnp.where(keep > 0.5, sel, neg)
    idxs, ws = [], []
    chosen = jnp.zeros((tm, N_EXPERTS), F32)
    for _ in range(TOP_K):
        _, ik = _first_argmax(cand, lane_f)
        hit = lane_f == ik
        ws.append(jnp.sum(jnp.where(hit, scores, 0.0), axis=-1, keepdims=True))
        idxs.append(ik)
        cand = jnp.where(hit, neg, cand)
        chosen = jnp.where(hit, 1.0, chosen)
    cnt_ref[0] = jnp.sum(chosen, axis=0, keepdims=True)
    wsum = ws[0]
    for k in range(1, TOP_K):
        wsum = wsum + ws[k]
    col = lax.broadcasted_iota(jnp.int32, (tm, TOP_K), 1)
    eidx = jnp.zeros((tm, TOP_K), F32)
    wout = jnp.zeros((tm, TOP_K), F32)
    for k in range(TOP_K):
        eidx = jnp.where(col == k, idxs[k], eidx)
        wout = jnp.where(col == k, ws[k] / wsum * ROUTED_SCALE, wout)
    eidx_ref[...] = eidx.astype(jnp.int32)
    w_ref[...] = wout


def _route(logits, bias2):
    n, ne = logits.shape
    tm = ROUTE_TM
    row = lambda i: (i, 0)
    return pl.pallas_call(
        _route_kernel,
        grid=(n // tm,),
        in_specs=[pl.BlockSpec((tm, ne), row), pl.BlockSpec((1, ne), lambda i: (0, 0))],
        out_specs=[pl.BlockSpec((tm, TOP_K), row), pl.BlockSpec((tm, TOP_K), row),
                   pl.BlockSpec((1, 1, ne), lambda i: (i, 0, 0))],
        out_shape=[jax.ShapeDtypeStruct((n, TOP_K), jnp.int32), jax.ShapeDtypeStruct((n, TOP_K), F32),
                   jax.ShapeDtypeStruct((n // tm, 1, ne), F32)],
        compiler_params=_cparams(("arbitrary",)),
        name="route",
    )(logits, bias2)


def _issue_rows(lo, hi, issue_one):
    n_full = (hi - lo) // ISSUE_UNROLL

    def chunk(c, carry):
        for u in range(ISSUE_UNROLL):
            issue_one(lo + c * ISSUE_UNROLL + u)
        return carry

    def tail(r, carry):
        issue_one(r)
        return carry

    lax.fori_loop(0, n_full, chunk, 0)
    lax.fori_loop(lo + n_full * ISSUE_UNROLL, hi, tail, 0)


def _hbm_slab(ref, row):
    return ref.at[pl.ds(pl.multiple_of(row * TOKEN_ROWS, TOKEN_ROWS), TOKEN_ROWS), :]


def _moe_kernel(be_ref, new_ref, nv_ref, nused_ref, tgt_hbm, h_hbm, wg_ref, wu_ref, wd_ref, ys_hbm,
                idx_s, xbuf, ybuf, wg_s, wu_s, wd_s, sem_i, sem_g, sem_s, *, n_tok):
    blk = pl.program_id(0)
    tm = MOE_TM
    per = IDX_CHUNK // tm
    per_log2 = per.bit_length() - 1
    n_used = nused_ref[0]
    slab = TOKEN_ROWS
    buf_rows = tm * SLAB_PITCH

    def staged(buf, base, r):
        return buf.at[pl.ds(pl.multiple_of(base + r * SLAB_PITCH, SUBLANES), slab), :]

    def idx_copy(b):
        c = b >> per_log2
        return pltpu.make_async_copy(
            tgt_hbm.at[pl.ds(pl.multiple_of(c * IDX_CHUNK, IDX_CHUNK), IDX_CHUNK)],
            idx_s.at[pl.ds(pl.multiple_of((c & 1) * IDX_CHUNK, IDX_CHUNK), IDX_CHUNK)], sem_i)

    def idx_base(b):
        return ((b >> per_log2) & 1) * IDX_CHUNK + (b & (per - 1)) * tm

    def gather_block(b, slot):
        ibase = idx_base(b)
        xbase = slot * buf_rows

        def one(r):
            tok = idx_s[ibase + r] & (n_tok - 1)
            pltpu.make_async_copy(_hbm_slab(h_hbm, tok), staged(xbuf, xbase, r), sem_g.at[slot]).start()
        _issue_rows(0, nv_ref[b], one)

    def scatter_block(b, slot):
        ibase = idx_base(b)
        ybase = slot * buf_rows

        def one(r):
            pltpu.make_async_copy(staged(ybuf, ybase, r), _hbm_slab(ys_hbm, idx_s[ibase + r]),
                                  sem_s.at[slot]).start()
        _issue_rows(0, nv_ref[b], one)

    def wait_rows(hbm, buf, sem, b, slot, to_hbm):
        rows = nv_ref[b] * slab
        v = buf.at[pl.ds(pl.multiple_of(slot * buf_rows, SUBLANES), rows), :]
        hv = hbm.at[pl.ds(0, rows), :]
        (pltpu.make_async_copy(v, hv, sem.at[slot]) if to_hbm else pltpu.make_async_copy(hv, v, sem.at[slot])).wait()

    @pl.when(blk < n_used)
    def _():
        slot = blk & 1

        @pl.when(blk == 0)
        def _():
            xbuf[...] = jnp.zeros(xbuf.shape, F32)
            first = idx_copy(0)
            first.start()
            first.wait()
            gather_block(0, 0)

        @pl.when(blk + 1 < n_used)
        def _():
            @pl.when(((blk + 1) & (per - 1)) == 0)
            def _():
                nxt = idx_copy(blk + 1)
                nxt.start()
                nxt.wait()
            gather_block(blk + 1, 1 - slot)

        @pl.when(new_ref[blk] == 1)
        def _():
            wg_s[...] = wg_ref[0].astype(BF16)
            wu_s[...] = wu_ref[0].astype(BF16)
            wd_s[...] = wd_ref[0].astype(BF16)

        wait_rows(h_hbm, xbuf, sem_g, blk, slot, to_hbm=False)

        @pl.when(blk >= 2)
        def _():
            wait_rows(ys_hbm, ybuf, sem_s, blk - 2, slot, to_hbm=True)

        base = slot * buf_rows
        x = jnp.concatenate([xbuf[pl.ds(base + s, tm, stride=SLAB_PITCH), :] for s in range(slab)],
                            axis=-1).astype(BF16)
        gate = jnp.dot(x, wg_s[...], preferred_element_type=F32)
        up = jnp.dot(x, wu_s[...], preferred_element_type=F32)
        act = (_silu(gate) * up).astype(BF16)
        y = jnp.dot(act, wd_s[...], preferred_element_type=F32)
        for s in range(slab):
            ybuf[pl.ds(base + s, tm, stride=SLAB_PITCH), :] = y[:, s * LANES:(s + 1) * LANES]
        scatter_block(blk, slot)

        @pl.when(blk == n_used - 1)
        def _():
            @pl.when(blk >= 1)
            def _():
                wait_rows(ys_hbm, ybuf, sem_s, blk - 1, 1 - slot, to_hbm=True)
            wait_rows(ys_hbm, ybuf, sem_s, blk, slot, to_hbm=True)


def _moe(block_e, block_new, block_nv, n_used, row_tgt, h2d, w_gate, w_up, w_down):
    n = h2d.shape[0] // TOKEN_ROWS
    assert n & (n - 1) == 0
    d = TOKEN_ROWS * LANES
    de = w_gate.shape[2]
    tm = MOE_TM
    n_blk = block_e.shape[0]
    wmap = lambda i, be, nw, nv, nu: (be[i], 0, 0)
    grid_spec = pltpu.PrefetchScalarGridSpec(
        num_scalar_prefetch=4,
        grid=(n_blk,),
        in_specs=[pl.BlockSpec(memory_space=pl.ANY),
                  pl.BlockSpec(memory_space=pl.ANY),
                  pl.BlockSpec((1, d, de), wmap),
                  pl.BlockSpec((1, d, de), wmap),
                  pl.BlockSpec((1, de, d), wmap)],
        out_specs=pl.BlockSpec(memory_space=pl.ANY),
        scratch_shapes=[pltpu.SMEM((2 * IDX_CHUNK,), jnp.int32),
                        pltpu.VMEM((2 * tm * SLAB_PITCH, LANES), F32),
                        pltpu.VMEM((2 * tm * SLAB_PITCH, LANES), F32),
                        pltpu.VMEM((d, de), BF16),
                        pltpu.VMEM((d, de), BF16),
                        pltpu.VMEM((de, d), BF16),
                        pltpu.SemaphoreType.DMA,
                        pltpu.SemaphoreType.DMA((2,)),
                        pltpu.SemaphoreType.DMA((2,))])
    return pl.pallas_call(
        functools.partial(_moe_kernel, n_tok=n),
        grid_spec=grid_spec,
        out_shape=jax.ShapeDtypeStruct((TOP_K * n * TOKEN_ROWS, LANES), F32),
        compiler_params=_cparams(("arbitrary",)),
        name="moe",
    )(block_e, block_new, block_nv, n_used, row_tgt, h2d, w_gate, w_up, w_down)


def _final_kernel(gw_hbm, ys_ref, h_ref, x1_ref, mod_ref, wg_ref, wu_ref, wd_ref, g_ref, b_ref, o_ref,
                  gw_s, acc_ref, sem_i):
    i = pl.program_id(0)
    tm = x1_ref.shape[0]
    per_step = tm * TOP_K
    cp = pltpu.make_async_copy(gw_hbm.at[pl.ds(pl.multiple_of(i * per_step, per_step), per_step)], gw_s, sem_i)
    cp.start()
    cp.wait()

    def combine(t, carry):
        rows = pl.ds(pl.multiple_of(t * TOKEN_ROWS, TOKEN_ROWS), TOKEN_ROWS)
        acc = gw_s[t * TOP_K] * ys_ref[0, rows, :]
        for k in range(1, TOP_K):
            acc = acc + gw_s[t * TOP_K + k] * ys_ref[k, rows, :]
        acc_ref[pl.ds(pl.multiple_of(t * SLAB_PITCH, SUBLANES), TOKEN_ROWS), :] = acc
        return carry

    lax.fori_loop(0, tm, combine, 0, unroll=4)
    moe = jnp.concatenate([acc_ref[pl.ds(s, tm, stride=SLAB_PITCH), :] for s in range(TOKEN_ROWS)], axis=-1)
    h = jnp.concatenate([h_ref[pl.ds(s, tm, stride=TOKEN_ROWS), :] for s in range(TOKEN_ROWS)],
                        axis=-1).astype(BF16)
    gate = jnp.dot(h, wg_ref[...], preferred_element_type=F32)
    up = jnp.dot(h, wu_ref[...], preferred_element_type=F32)
    shared = jnp.dot((_silu(gate) * up).astype(BF16), wd_ref[...], preferred_element_type=F32)
    m = mod_ref[0]
    y = ALPHA * x1_ref[...] + (1.0 + m[5:6]) * (moe + shared)
    o_ref[...] = _layer_norm(y, g_ref[...], b_ref[...])


def _final(gate_w, ys3, h2d, x1, mod3, wsg, wsu, wsd, ln_g, ln_b, seq):
    n, d = x1.shape
    de = wsg.shape[1]
    tm = FINAL_TM
    per_step = tm * TOP_K
    assert per_step % IDX_CHUNK == 0
    row = lambda i: (i, 0)
    const = lambda i: (0, 0)
    return pl.pallas_call(
        _final_kernel,
        grid=(n // tm,),
        in_specs=[pl.BlockSpec(memory_space=pl.ANY),
                  pl.BlockSpec((TOP_K, tm * TOKEN_ROWS, LANES), lambda i: (0, i, 0)),
                  pl.BlockSpec((tm * TOKEN_ROWS, LANES), row),
                  pl.BlockSpec((tm, d), row),
                  pl.BlockSpec((1, 6, d), lambda i: (i * tm // seq, 0, 0)),
                  pl.BlockSpec((d, de), const),
                  pl.BlockSpec((d, de), const),
                  pl.BlockSpec((de, d), const),
                  pl.BlockSpec((1, d), const),
                  pl.BlockSpec((1, d), const)],
        out_specs=pl.BlockSpec((tm, d), row),
        out_shape=jax.ShapeDtypeStruct((n, d), F32),
        scratch_shapes=[pltpu.SMEM((per_step,), F32),
                        pltpu.VMEM((tm * SLAB_PITCH, LANES), F32),
                        pltpu.SemaphoreType.DMA],
        compiler_params=_cparams(("arbitrary",)),
        name="final",
    )(gate_w, ys3, h2d, x1, mod3, wsg, wsu, wsd, ln_g, ln_b)


def _dispatch_tables(eidx, tile_counts, n):
    tm = MOE_TM
    a = n * TOP_K
    i32 = jnp.int32
    experts = jnp.arange(N_EXPERTS, dtype=i32)
    counts = jnp.sum(tile_counts, axis=(0, 1)).astype(i32)
    padded = (counts + tm - 1) // tm * tm
    pad_end = jnp.cumsum(padded)
    starts = pad_end - padded
    n_blk = a // tm + N_EXPERTS
    blk_start = jnp.arange(n_blk, dtype=i32) * tm
    n_used = pad_end[-1] // tm
    in_use = jnp.arange(n_blk) < n_used
    raw_e = jnp.minimum(jnp.sum((pad_end[None, :] <= blk_start[:, None]).astype(i32), axis=1), N_EXPERTS - 1)
    last_e = jnp.sum(jnp.where(jnp.arange(n_blk) == n_used - 1, raw_e, 0))
    block_e = jnp.where(in_use, raw_e, last_e)
    onehot = block_e[:, None] == experts[None, :]
    cnt_b = jnp.sum(jnp.where(onehot, counts[None, :], 0), axis=1)
    start_b = jnp.sum(jnp.where(onehot, starts[None, :], 0), axis=1)
    block_nv = jnp.where(in_use, jnp.clip(cnt_b - (blk_start - start_b), 0, tm), 0).astype(i32)
    block_new = jnp.concatenate([jnp.ones((1,), i32), (block_e[1:] != block_e[:-1]).astype(i32)])
    dummy_keys = jnp.where(jnp.arange(tm, dtype=i32)[None, :] < (padded - counts)[:, None],
                           experts[:, None], N_EXPERTS).reshape(-1)
    keys = jnp.concatenate([eidx.reshape(-1), dummy_keys])
    tgt = (jnp.arange(TOP_K, dtype=i32)[None, :] * n + jnp.arange(n, dtype=i32)[:, None]).reshape(-1)
    payload = jnp.concatenate([tgt, jnp.zeros((N_EXPERTS * tm,), i32)])
    _, row_tgt = lax.sort((keys, payload), num_keys=1, is_stable=True)
    return block_e.astype(i32), block_new, block_nv, n_used.astype(i32).reshape(1), row_tgt


def kernel(x, c, w_mod, b_mod, w_in, conv_w, attn_sinks, w_out, ln1_g, ln1_b, w_router, router_bias,
           w_gate, w_up, w_down, ws_gate, ws_up, ws_down, ln2_g, ln2_b):
    b, s, d = x.shape
    n = b * s
    attn_w = N_Q_HEADS * HEAD_DIM
    kv_w = N_KV_HEADS * HEAD_DIM
    conv_wd = d - attn_w
    in_w = attn_w + 2 * kv_w + 3 * conv_wd
    x2 = x.reshape(n, d)
    c8 = jnp.zeros((SUBLANES, d), F32).at[:b].set(c)
    for l in range(DEPTH):
        mod = _mod(c8, w_mod[l], b_mod[l].reshape(1, -1))[:b]
        mod3 = mod.reshape(b, 6, d)
        w_in3 = w_in[l].astype(BF16).reshape(d, in_w // INPROJ_TN, INPROJ_TN).transpose(1, 0, 2)
        proj = _inproj(x2, mod3, w_in3, s)
        mix = _mixer(proj, attn_sinks[l].reshape(1, -1), conv_w[l], b, s, attn_w, kv_w, conv_wd)
        x1, h2d, logits = _outproj(mix, x2, mod3, w_out[l].astype(BF16), ln1_g[l].reshape(1, -1),
                                   ln1_b[l].reshape(1, -1), w_router[l], s)
        eidx, gate_w, tile_counts = _route(logits, router_bias[l].reshape(1, -1))
        block_e, block_new, block_nv, n_used, row_tgt = _dispatch_tables(eidx, tile_counts, n)
        ys = _moe(block_e, block_new, block_nv, n_used, row_tgt, h2d, w_gate[l], w_up[l], w_down[l])
        x2 = _final(gate_w.reshape(-1), ys.reshape(TOP_K, n * TOKEN_ROWS, LANES), h2d, x1, mod3,
                    ws_gate[l].astype(BF16), ws_up[l].astype(BF16), ws_down[l].astype(BF16),
                    ln2_g[l].reshape(1, -1), ln2_b[l].reshape(1, -1), s)
    return x2.reshape(b, s, d)
```

```python
import functools

import jax
import jax.numpy as jnp
from jax import lax
from jax.experimental import pallas as pl
from jax.experimental.pallas import tpu as pltpu

HEAD_DIM = 64
N_Q_HEADS = 16
N_KV_HEADS = 4
GQA = N_Q_HEADS // N_KV_HEADS
CONV_K = 3
WINDOW = 128
Q_BLOCK = 128
N_EXPERTS = 64
TOP_K = 8
N_GROUPS = 8
GROUP_SIZE = N_EXPERTS // N_GROUPS
TOPK_GROUPS = 4
ROUTED_SCALE = 2.5
DEPTH = 1
ALPHA = (2.0 * DEPTH) ** 0.25
LN_EPS = 1e-5

LANES = 128
SUBLANES = 8
TOKEN_ROWS = 8
SLAB_PITCH = 8
VMEM_LIMIT = 56 * 1024 * 1024

MOD_TN = 1024
INPROJ_TM = 512
INPROJ_TN = 1536
OUTPROJ_TM = 256
ROUTE_TM = 512
MOE_TM = 256
FINAL_TM = 128
IDX_CHUNK = 1024
ISSUE_UNROLL = 8

F32 = jnp.float32
BF16 = jnp.bfloat16


def _cparams(sem):
    return pltpu.CompilerParams(dimension_semantics=sem, vmem_limit_bytes=VMEM_LIMIT)


def _silu(v):
    return v * jax.nn.sigmoid(v)


U32 = jnp.uint32
HI_MASK = 0xFFFF0000


def _pack_pairs(lo, hi):
    lo_bits = lax.bitcast_convert_type(lo.astype(BF16).astype(F32), U32) >> 16
    hi_bits = lax.bitcast_convert_type(hi.astype(BF16).astype(F32), U32) & U32(HI_MASK)
    return lo_bits | hi_bits


def _unpack_pairs(w):
    return (lax.bitcast_convert_type(w << 16, F32), lax.bitcast_convert_type(w & U32(HI_MASK), F32))


def _slab_rows_to_matrix(ref, base, tm, pitch):
    return jnp.concatenate([ref[pl.ds(base + s, tm, stride=pitch), :] for s in range(TOKEN_ROWS)], axis=-1)


def _layer_norm(y, g, b):
    mu = jnp.mean(y, axis=-1, keepdims=True)
    yc = y - mu
    var = jnp.mean(yc * yc, axis=-1, keepdims=True)
    return yc * lax.rsqrt(var + LN_EPS) * g + b


def _mod_kernel(c_ref, w_ref, b_ref, o_ref):
    cs = _silu(c_ref[...]).astype(BF16)
    o_ref[...] = jnp.dot(cs, w_ref[...].astype(BF16), preferred_element_type=F32) + b_ref[...]


def _mod(c8, w_mod, b_mod):
    d, n = w_mod.shape
    return pl.pallas_call(
        _mod_kernel,
        grid=(n // MOD_TN,),
        in_specs=[pl.BlockSpec((SUBLANES, d), lambda j: (0, 0)),
                  pl.BlockSpec((d, MOD_TN), lambda j: (0, j)),
                  pl.BlockSpec((1, MOD_TN), lambda j: (0, j))],
        out_specs=pl.BlockSpec((SUBLANES, MOD_TN), lambda j: (0, j)),
        out_shape=jax.ShapeDtypeStruct((SUBLANES, n), F32),
        compiler_params=_cparams(("arbitrary",)),
        name="mod",
    )(c8, w_mod, b_mod)


def _inproj_kernel(x_ref, mod_ref, w_ref, o_ref, h_ref):
    j = pl.program_id(1)

    @pl.when(j == 0)
    def _():
        m = mod_ref[0]
        h_ref[...] = (x_ref[...] * (1.0 + m[1:2]) + m[0:1]).astype(BF16)

    o_ref[...] = jnp.dot(h_ref[...], w_ref[0], preferred_element_type=F32).astype(BF16)


def _inproj(x2, mod3, w_in3, seq):
    n, d = x2.shape
    nj, _, tn = w_in3.shape
    tm = INPROJ_TM
    return pl.pallas_call(
        _inproj_kernel,
        grid=(n // tm, nj),
        in_specs=[pl.BlockSpec((tm, d), lambda i, j: (i, 0)),
                  pl.BlockSpec((1, 6, d), lambda i, j: (i * tm // seq, 0, 0)),
                  pl.BlockSpec((1, d, tn), lambda i, j: (j, 0, 0))],
        out_specs=pl.BlockSpec((tm, tn), lambda i, j: (i, j)),
        out_shape=jax.ShapeDtypeStruct((n, nj * tn), BF16),
        scratch_shapes=[pltpu.VMEM((tm, d), BF16)],
        compiler_params=_cparams(("arbitrary", "arbitrary")),
        name="inproj",
    )(x2, mod3, w_in3)


def _mixer_kernel(cur_ref, pk_ref, pv_ref, prow_ref, sink_ref, cw_ref, o_ref, *, attn_w, kv_w, conv_w):
    nblk = pl.program_id(1)
    has_prev = nblk > 0
    qb = Q_BLOCK
    cur = cur_ref[...]
    k_cur = cur[:, attn_w:attn_w + kv_w]
    v_cur = cur[:, attn_w + kv_w:attn_w + 2 * kv_w]
    k_all = jnp.concatenate([pk_ref[...], k_cur], axis=0)
    v_all = jnp.concatenate([pv_ref[...], v_cur], axis=0)

    rows = GQA * qb
    qi = lax.broadcasted_iota(jnp.int32, (rows, 2 * qb), 0) % qb
    kj = lax.broadcasted_iota(jnp.int32, (rows, 2 * qb), 1)
    dist = qi + qb - kj
    kmin = jnp.where(has_prev, 0, qb)
    valid = (dist >= 0) & (dist < WINDOW) & (kj >= kmin)
    distf = dist.astype(F32)
    head_in_group = lax.broadcasted_iota(jnp.int32, (rows, 1), 0) // qb
    sinks = sink_ref[...]

    outs = []
    for g in range(N_KV_HEADS):
        q4 = jnp.concatenate(
            [cur[:, (g * GQA + j) * HEAD_DIM:(g * GQA + j + 1) * HEAD_DIM] for j in range(GQA)], axis=0)
        kg = k_all[:, g * HEAD_DIM:(g + 1) * HEAD_DIM]
        vg = v_all[:, g * HEAD_DIM:(g + 1) * HEAD_DIM]
        s = lax.dot_general(q4, kg, (((1,), (1,)), ((), ())), preferred_element_type=F32)
        s = s * (HEAD_DIM ** -0.5)
        slope = jnp.zeros((rows, 1), F32)
        sink = jnp.zeros((rows, 1), F32)
        for j in range(GQA):
            h = g * GQA + j
            sel = head_in_group == j
            slope = jnp.where(sel, 2.0 ** (-8.0 * (h + 1) / N_Q_HEADS), slope)
            sink = jnp.where(sel, sinks[:, h:h + 1], sink)
        s = jnp.where(valid, s - slope * distf, -jnp.inf)
        m = jnp.maximum(jnp.max(s, axis=-1, keepdims=True), sink)
        p = jnp.exp(s - m)
        denom = jnp.sum(p, axis=-1, keepdims=True) + jnp.exp(sink - m)
        o4 = jnp.dot(p.astype(BF16), vg, preferred_element_type=F32) / denom
        outs.extend(o4[j * qb:(j + 1) * qb] for j in range(GQA))
    attn = jnp.concatenate(outs, axis=-1)

    c0 = attn_w + 2 * kv_w
    cb = cur[:, c0:c0 + conv_w].astype(F32)
    u = cur[:, c0 + conv_w:c0 + 2 * conv_w].astype(F32) * cur[:, c0 + 2 * conv_w:c0 + 3 * conv_w].astype(F32)
    prow = prow_ref[...]
    up = prow[:, c0 + conv_w:c0 + 2 * conv_w].astype(F32) * prow[:, c0 + 2 * conv_w:c0 + 3 * conv_w].astype(F32)
    up = up * jnp.where(has_prev, 1.0, 0.0)
    pm1 = up[15:16]
    pm2 = up[14:15]
    ri = lax.broadcasted_iota(jnp.int32, u.shape, 0)
    u1 = jnp.where(ri == 0, pm1, pltpu.roll(u, 1, 0))
    u2 = jnp.where(ri == 0, pm2, jnp.where(ri == 1, pm1, pltpu.roll(u, 2, 0)))
    cw = cw_ref[...]
    conv = cb * (cw[0:1] * u2 + cw[1:2] * u1 + cw[2:3] * u)
    o_ref[...] = jnp.concatenate([attn, conv], axis=-1).astype(BF16)


def _mixer(proj, sinks2, conv_w, batch, seq, attn_w, kv_w, conv_wd):
    n, in_w = proj.shape
    nb = seq // Q_BLOCK
    kv_blk0 = attn_w // kv_w
    sub16 = Q_BLOCK // 16

    def cur_map(b, i):
        return (b * nb + i, 0)

    def prev_map(col):
        return lambda b, i: (b * nb + jnp.maximum(i - 1, 0), col)

    def prow_map(b, i):
        return (jnp.maximum((b * nb + i) * sub16 - 1, 0), 0)

    kern = functools.partial(_mixer_kernel, attn_w=attn_w, kv_w=kv_w, conv_w=conv_wd)
    return pl.pallas_call(
        kern,
        grid=(batch, nb),
        in_specs=[pl.BlockSpec((Q_BLOCK, in_w), cur_map),
                  pl.BlockSpec((Q_BLOCK, kv_w), prev_map(kv_blk0)),
                  pl.BlockSpec((Q_BLOCK, kv_w), prev_map(kv_blk0 + 1)),
                  pl.BlockSpec((16, in_w), prow_map),
                  pl.BlockSpec((1, N_Q_HEADS), lambda b, i: (0, 0)),
                  pl.BlockSpec((CONV_K, conv_wd), lambda b, i: (0, 0))],
        out_specs=pl.BlockSpec((Q_BLOCK, attn_w + conv_wd), cur_map),
        out_shape=jax.ShapeDtypeStruct((n, attn_w + conv_wd), BF16),
        compiler_params=_cparams(("arbitrary", "arbitrary")),
        name="mixer",
    )(proj, proj, proj, proj, sinks2, conv_w)


def _split_bf16(v):
    hi = v.astype(BF16)
    lo = (v - hi.astype(F32)).astype(BF16)
    return hi, lo


def _outproj_kernel(mix_ref, x_ref, mod_ref, w_ref, g_ref, b_ref, wr_ref, x1_ref, h2_ref, lg_ref):
    m = mod_ref[0]
    mix = jnp.dot(mix_ref[...], w_ref[...], preferred_element_type=F32)
    x1 = _layer_norm(ALPHA * x_ref[...] + (1.0 + m[2:3]) * mix, g_ref[...], b_ref[...])
    x1_ref[...] = x1
    h2 = x1 * (1.0 + m[4:5]) + m[3:4]
    tm, d = h2.shape
    words = _pack_pairs(h2[:, :d // 2], h2[:, d // 2:])
    for s in range(TOKEN_ROWS):
        h2_ref[pl.ds(s, tm, stride=TOKEN_ROWS), :] = words[:, s * LANES:(s + 1) * LANES]
    h_hi, h_lo = _split_bf16(h2)
    w_hi, w_lo = _split_bf16(wr_ref[...])
    lg_ref[...] = (jnp.dot(h_hi, w_hi, preferred_element_type=F32)
                   + (jnp.dot(h_hi, w_lo, preferred_element_type=F32)
                      + jnp.dot(h_lo, w_hi, preferred_element_type=F32)))


def _outproj(mix, x2, mod3, w_out_bf, ln_g, ln_b, w_router, seq):
    n, d = x2.shape
    tm = OUTPROJ_TM
    ne = w_router.shape[1]
    row = lambda i: (i, 0)
    const = lambda i: (0, 0)
    return pl.pallas_call(
        _outproj_kernel,
        grid=(n // tm,),
        in_specs=[pl.BlockSpec((tm, d), row),
                  pl.BlockSpec((tm, d), row),
                  pl.BlockSpec((1, 6, d), lambda i: (i * tm // seq, 0, 0)),
                  pl.BlockSpec((d, d), const),
                  pl.BlockSpec((1, d), const),
                  pl.BlockSpec((1, d), const),
                  pl.BlockSpec((d, ne), const)],
        out_specs=[pl.BlockSpec((tm, d), row),
                   pl.BlockSpec((tm * TOKEN_ROWS, LANES), row),
                   pl.BlockSpec((tm, ne), row)],
        out_shape=[jax.ShapeDtypeStruct((n, d), F32),
                   jax.ShapeDtypeStruct((n * TOKEN_ROWS, LANES), U32),
                   jax.ShapeDtypeStruct((n, ne), F32)],
        compiler_params=_cparams(("arbitrary",)),
        name="outproj",
    )(mix, x2, mod3, w_out_bf, ln_g, ln_b, w_router)


def _first_argmax(v, lane_f):
    m = jnp.max(v, axis=-1, keepdims=True)
    idx = jnp.min(jnp.where(v == m, lane_f, float(N_EXPERTS)), axis=-1, keepdims=True)
    return m, idx


def _route_kernel(lg_ref, bias_ref, eidx_ref, w_ref, cnt_ref):
    scores = jax.nn.sigmoid(lg_ref[...])
    sel = scores + bias_ref[...]
    tm = sel.shape[0]
    lane = lax.broadcasted_iota(jnp.int32, (tm, N_EXPERTS), 1)
    lane_f = lane.astype(F32)
    grp = lane // GROUP_SIZE
    neg = -jnp.inf
    gs = []
    for g in range(N_GROUPS):
        vg = jnp.where(grp == g, sel, neg)
        m1, i1 = _first_argmax(vg, lane_f)
        m2 = jnp.max(jnp.where(lane_f == i1, neg, vg), axis=-1, keepdims=True)
        gs.append(m1 + m2)
    keep = jnp.zeros((tm, N_EXPERTS), F32)
    for g in range(N_GROUPS):
        rank = jnp.zeros((tm, 1), F32)
        for o in range(N_GROUPS):
            if o == g:
                continue
            ahead = (gs[o] >= gs[g]) if o < g else (gs[o] > gs[g])
            rank = rank + jnp.where(ahead, 1.0, 0.0)
        keep = jnp.where(grp == g, jnp.where(rank < TOPK_GROUPS, 1.0, 0.0), keep)
    cand = jnp.where(keep > 0.5, sel, neg)
    idxs, ws = [], []
    chosen = jnp.zeros((tm, N_EXPERTS), F32)
    for _ in range(TOP_K):
        _, ik = _first_argmax(cand, lane_f)
        hit = lane_f == ik
        ws.append(jnp.sum(jnp.where(hit, scores, 0.0), axis=-1, keepdims=True))
        idxs.append(ik)
        cand = jnp.where(hit, neg, cand)
        chosen = jnp.where(hit, 1.0, chosen)
    cnt_ref[0] = jnp.sum(chosen, axis=0, keepdims=True)
    wsum = ws[0]
    for k in range(1, TOP_K):
        wsum = wsum + ws[k]
    col = lax.broadcasted_iota(jnp.int32, (tm, TOP_K), 1)
    eidx = jnp.zeros((tm, TOP_K), F32)
    wout = jnp.zeros((tm, TOP_K), F32)
    for k in range(TOP_K):
        eidx = jnp.where(col == k, idxs[k], eidx)
        wout = jnp.where(col == k, ws[k] / wsum * ROUTED_SCALE, wout)
    eidx_ref[...] = eidx.astype(jnp.int32)
    w_ref[...] = wout


def _route(logits, bias2):
    n, ne = logits.shape
    tm = ROUTE_TM
    row = lambda i: (i, 0)
    return pl.pallas_call(
        _route_kernel,
        grid=(n // tm,),
        in_specs=[pl.BlockSpec((tm, ne), row), pl.BlockSpec((1, ne), lambda i: (0, 0))],
        out_specs=[pl.BlockSpec((tm, TOP_K), row), pl.BlockSpec((tm, TOP_K), row),
                   pl.BlockSpec((1, 1, ne), lambda i: (i, 0, 0))],
        out_shape=[jax.ShapeDtypeStruct((n, TOP_K), jnp.int32), jax.ShapeDtypeStruct((n, TOP_K), F32),
                   jax.ShapeDtypeStruct((n // tm, 1, ne), F32)],
        compiler_params=_cparams(("arbitrary",)),
        name="route",
    )(logits, bias2)


def _issue_rows(lo, hi, issue_one):
    n_full = (hi - lo) // ISSUE_UNROLL

    def chunk(c, carry):
        for u in range(ISSUE_UNROLL):
            issue_one(lo + c * ISSUE_UNROLL + u)
        return carry

    def tail(r, carry):
        issue_one(r)
        return carry

    lax.fori_loop(0, n_full, chunk, 0)
    lax.fori_loop(lo + n_full * ISSUE_UNROLL, hi, tail, 0)


def _hbm_slab(ref, row):
    return ref.at[pl.ds(pl.multiple_of(row * TOKEN_ROWS, TOKEN_ROWS), TOKEN_ROWS), :]


def _moe_kernel(be_ref, new_ref, nv_ref, nused_ref, tgt_hbm, h_hbm, wg_ref, wu_ref, wd_ref, ys_hbm,
                idx_s, xbuf, ybuf, wg_s, wu_s, wd_s, sem_i, sem_g, sem_s, *, n_tok):
    blk = pl.program_id(0)
    tm = MOE_TM
    per = IDX_CHUNK // tm
    per_log2 = per.bit_length() - 1
    n_used = nused_ref[0]
    slab = TOKEN_ROWS
    buf_rows = tm * SLAB_PITCH

    def staged(buf, base, r):
        return buf.at[pl.ds(pl.multiple_of(base + r * SLAB_PITCH, SUBLANES), slab), :]

    def idx_copy(b):
        c = b >> per_log2
        return pltpu.make_async_copy(
            tgt_hbm.at[pl.ds(pl.multiple_of(c * IDX_CHUNK, IDX_CHUNK), IDX_CHUNK)],
            idx_s.at[pl.ds(pl.multiple_of((c & 1) * IDX_CHUNK, IDX_CHUNK), IDX_CHUNK)], sem_i)

    def idx_base(b):
        return ((b >> per_log2) & 1) * IDX_CHUNK + (b & (per - 1)) * tm

    def gather_block(b, slot):
        ibase = idx_base(b)
        xbase = slot * buf_rows

        def one(r):
            tok = idx_s[ibase + r] & (n_tok - 1)
            pltpu.make_async_copy(_hbm_slab(h_hbm, tok), staged(xbuf, xbase, r), sem_g.at[slot]).start()
        _issue_rows(0, nv_ref[b], one)

    def scatter_block(b, slot):
        ibase = idx_base(b)
        ybase = slot * buf_rows

        def one(r):
            pltpu.make_async_copy(staged(ybuf, ybase, r), _hbm_slab(ys_hbm, idx_s[ibase + r]),
                                  sem_s.at[slot]).start()
        _issue_rows(0, nv_ref[b], one)

    def wait_rows(hbm, buf, sem, b, slot, to_hbm):
        rows = nv_ref[b] * slab
        v = buf.at[pl.ds(pl.multiple_of(slot * buf_rows, SUBLANES), rows), :]
        hv = hbm.at[pl.ds(0, rows), :]
        (pltpu.make_async_copy(v, hv, sem.at[slot]) if to_hbm else pltpu.make_async_copy(hv, v, sem.at[slot])).wait()

    @pl.when(blk < n_used)
    def _():
        slot = blk & 1

        @pl.when(blk == 0)
        def _():
            xbuf[...] = jnp.zeros(xbuf.shape, U32)
            first = idx_copy(0)
            first.start()
            first.wait()
            gather_block(0, 0)

        @pl.when(blk + 1 < n_used)
        def _():
            @pl.when(((blk + 1) & (per - 1)) == 0)
            def _():
                nxt = idx_copy(blk + 1)
                nxt.start()
                nxt.wait()
            gather_block(blk + 1, 1 - slot)

        @pl.when(new_ref[blk] == 1)
        def _():
            wg_s[...] = wg_ref[0].astype(BF16)
            wu_s[...] = wu_ref[0].astype(BF16)
            wd_s[...] = wd_ref[0].astype(BF16)

        wait_rows(h_hbm, xbuf, sem_g, blk, slot, to_hbm=False)

        @pl.when(blk >= 2)
        def _():
            wait_rows(ys_hbm, ybuf, sem_s, blk - 2, slot, to_hbm=True)

        base = slot * buf_rows
        x_lo, x_hi = _unpack_pairs(_slab_rows_to_matrix(xbuf, base, tm, SLAB_PITCH))
        x = jnp.concatenate([x_lo.astype(BF16), x_hi.astype(BF16)], axis=-1)
        gate = jnp.dot(x, wg_s[...], preferred_element_type=F32)
        up = jnp.dot(x, wu_s[...], preferred_element_type=F32)
        act = (_silu(gate) * up).astype(BF16)
        y = jnp.dot(act, wd_s[...], preferred_element_type=F32)
        half = slab * LANES
        words = _pack_pairs(y[:, :half], y[:, half:])
        for s in range(slab):
            ybuf[pl.ds(base + s, tm, stride=SLAB_PITCH), :] = words[:, s * LANES:(s + 1) * LANES]
        scatter_block(blk, slot)

        @pl.when(blk == n_used - 1)
        def _():
            @pl.when(blk >= 1)
            def _():
                wait_rows(ys_hbm, ybuf, sem_s, blk - 1, 1 - slot, to_hbm=True)
            wait_rows(ys_hbm, ybuf, sem_s, blk, slot, to_hbm=True)


def _moe(block_e, block_new, block_nv, n_used, row_tgt, h2d, w_gate, w_up, w_down):
    n = h2d.shape[0] // TOKEN_ROWS
    assert n & (n - 1) == 0
    d = 2 * TOKEN_ROWS * LANES
    de = w_gate.shape[2]
    tm = MOE_TM
    n_blk = block_e.shape[0]
    wmap = lambda i, be, nw, nv, nu: (be[i], 0, 0)
    grid_spec = pltpu.PrefetchScalarGridSpec(
        num_scalar_prefetch=4,
        grid=(n_blk,),
        in_specs=[pl.BlockSpec(memory_space=pl.ANY),
                  pl.BlockSpec(memory_space=pl.ANY),
                  pl.BlockSpec((1, d, de), wmap),
                  pl.BlockSpec((1, d, de), wmap),
                  pl.BlockSpec((1, de, d), wmap)],
        out_specs=pl.BlockSpec(memory_space=pl.ANY),
        scratch_shapes=[pltpu.SMEM((2 * IDX_CHUNK,), jnp.int32),
                        pltpu.VMEM((2 * tm * SLAB_PITCH, LANES), U32),
                        pltpu.VMEM((2 * tm * SLAB_PITCH, LANES), U32),
                        pltpu.VMEM((d, de), BF16),
                        pltpu.VMEM((d, de), BF16),
                        pltpu.VMEM((de, d), BF16),
                        pltpu.SemaphoreType.DMA,
                        pltpu.SemaphoreType.DMA((2,)),
                        pltpu.SemaphoreType.DMA((2,))])
    return pl.pallas_call(
        functools.partial(_moe_kernel, n_tok=n),
        grid_spec=grid_spec,
        out_shape=jax.ShapeDtypeStruct((TOP_K * n * TOKEN_ROWS, LANES), U32),
        compiler_params=_cparams(("arbitrary",)),
        name="moe",
    )(block_e, block_new, block_nv, n_used, row_tgt, h2d, w_gate, w_up, w_down)


def _final_kernel(gw_hbm, ys_ref, h_ref, x1_ref, mod_ref, wg_ref, wu_ref, wd_ref, g_ref, b_ref, o_ref,
                  gw_s, acc_ref, sem_i):
    i = pl.program_id(0)
    tm = x1_ref.shape[0]
    per_step = tm * TOP_K
    cp = pltpu.make_async_copy(gw_hbm.at[pl.ds(pl.multiple_of(i * per_step, per_step), per_step)], gw_s, sem_i)
    cp.start()
    cp.wait()

    hi_base = tm * SLAB_PITCH

    def combine(t, carry):
        rows = pl.ds(pl.multiple_of(t * TOKEN_ROWS, TOKEN_ROWS), TOKEN_ROWS)
        acc_lo = acc_hi = None
        for k in range(TOP_K):
            lo, hi = _unpack_pairs(ys_ref[k, rows, :])
            g = gw_s[t * TOP_K + k]
            acc_lo = g * lo if acc_lo is None else acc_lo + g * lo
            acc_hi = g * hi if acc_hi is None else acc_hi + g * hi
        acc_ref[pl.ds(pl.multiple_of(t * SLAB_PITCH, SUBLANES), TOKEN_ROWS), :] = acc_lo
        acc_ref[pl.ds(pl.multiple_of(hi_base + t * SLAB_PITCH, SUBLANES), TOKEN_ROWS), :] = acc_hi
        return carry

    lax.fori_loop(0, tm, combine, 0, unroll=4)
    moe = jnp.concatenate([_slab_rows_to_matrix(acc_ref, 0, tm, SLAB_PITCH),
                           _slab_rows_to_matrix(acc_ref, hi_base, tm, SLAB_PITCH)], axis=-1)
    h_lo, h_hi = _unpack_pairs(_slab_rows_to_matrix(h_ref, 0, tm, TOKEN_ROWS))
    h = jnp.concatenate([h_lo.astype(BF16), h_hi.astype(BF16)], axis=-1)
    gate = jnp.dot(h, wg_ref[...], preferred_element_type=F32)
    up = jnp.dot(h, wu_ref[...], preferred_element_type=F32)
    shared = jnp.dot((_silu(gate) * up).astype(BF16), wd_ref[...], preferred_element_type=F32)
    m = mod_ref[0]
    y = ALPHA * x1_ref[...] + (1.0 + m[5:6]) * (moe + shared)
    o_ref[...] = _layer_norm(y, g_ref[...], b_ref[...])


def _final(gate_w, ys3, h2d, x1, mod3, wsg, wsu, wsd, ln_g, ln_b, seq):
    n, d = x1.shape
    de = wsg.shape[1]
    tm = FINAL_TM
    per_step = tm * TOP_K
    assert per_step % IDX_CHUNK == 0
    row = lambda i: (i, 0)
    const = lambda i: (0, 0)
    return pl.pallas_call(
        _final_kernel,
        grid=(n // tm,),
        in_specs=[pl.BlockSpec(memory_space=pl.ANY),
                  pl.BlockSpec((TOP_K, tm * TOKEN_ROWS, LANES), lambda i: (0, i, 0)),
                  pl.BlockSpec((tm * TOKEN_ROWS, LANES), row),
                  pl.BlockSpec((tm, d), row),
                  pl.BlockSpec((1, 6, d), lambda i: (i * tm // seq, 0, 0)),
                  pl.BlockSpec((d, de), const),
                  pl.BlockSpec((d, de), const),
                  pl.BlockSpec((de, d), const),
                  pl.BlockSpec((1, d), const),
                  pl.BlockSpec((1, d), const)],
        out_specs=pl.BlockSpec((tm, d), row),
        out_shape=jax.ShapeDtypeStruct((n, d), F32),
        scratch_shapes=[pltpu.SMEM((per_step,), F32),
                        pltpu.VMEM((2 * tm * SLAB_PITCH, LANES), F32),
                        pltpu.SemaphoreType.DMA],
        compiler_params=_cparams(("arbitrary",)),
        name="final",
    )(gate_w, ys3, h2d, x1, mod3, wsg, wsu, wsd, ln_g, ln_b)


def _dispatch_tables(eidx, tile_counts, n):
    tm = MOE_TM
    a = n * TOP_K
    i32 = jnp.int32
    experts = jnp.arange(N_EXPERTS, dtype=i32)
    counts = jnp.sum(tile_counts, axis=(0, 1)).astype(i32)
    padded = (counts + tm - 1) // tm * tm
    pad_end = jnp.cumsum(padded)
    starts = pad_end - padded
    n_blk = a // tm + N_EXPERTS
    blk_start = jnp.arange(n_blk, dtype=i32) * tm
    n_used = pad_end[-1] // tm
    in_use = jnp.arange(n_blk) < n_used
    raw_e = jnp.minimum(jnp.sum((pad_end[None, :] <= blk_start[:, None]).astype(i32), axis=1), N_EXPERTS - 1)
    last_e = jnp.sum(jnp.where(jnp.arange(n_blk) == n_used - 1, raw_e, 0))
    block_e = jnp.where(in_use, raw_e, last_e)
    onehot = block_e[:, None] == experts[None, :]
    cnt_b = jnp.sum(jnp.where(onehot, counts[None, :], 0), axis=1)
    start_b = jnp.sum(jnp.where(onehot, starts[None, :], 0), axis=1)
    block_nv = jnp.where(in_use, jnp.clip(cnt_b - (blk_start - start_b), 0, tm), 0).astype(i32)
    block_new = jnp.concatenate([jnp.ones((1,), i32), (block_e[1:] != block_e[:-1]).astype(i32)])
    dummy_keys = jnp.where(jnp.arange(tm, dtype=i32)[None, :] < (padded - counts)[:, None],
                           experts[:, None], N_EXPERTS).reshape(-1)
    keys = jnp.concatenate([eidx.reshape(-1), dummy_keys])
    tgt = (jnp.arange(TOP_K, dtype=i32)[None, :] * n + jnp.arange(n, dtype=i32)[:, None]).reshape(-1)
    payload = jnp.concatenate([tgt, jnp.zeros((N_EXPERTS * tm,), i32)])
    _, row_tgt = lax.sort((keys, payload), num_keys=1, is_stable=True)
    return block_e.astype(i32), block_new, block_nv, n_used.astype(i32).reshape(1), row_tgt


def kernel(x, c, w_mod, b_mod, w_in, conv_w, attn_sinks, w_out, ln1_g, ln1_b, w_router, router_bias,
           w_gate, w_up, w_down, ws_gate, ws_up, ws_down, ln2_g, ln2_b):
    b, s, d = x.shape
    n = b * s
    attn_w = N_Q_HEADS * HEAD_DIM
    kv_w = N_KV_HEADS * HEAD_DIM
    conv_wd = d - attn_w
    in_w = attn_w + 2 * kv_w + 3 * conv_wd
    x2 = x.reshape(n, d)
    c8 = jnp.zeros((SUBLANES, d), F32).at[:b].set(c)
    for l in range(DEPTH):
        mod = _mod(c8, w_mod[l], b_mod[l].reshape(1, -1))[:b]
        mod3 = mod.reshape(b, 6, d)
        w_in3 = w_in[l].astype(BF16).reshape(d, in_w // INPROJ_TN, INPROJ_TN).transpose(1, 0, 2)
        proj = _inproj(x2, mod3, w_in3, s)
        mix = _mixer(proj, attn_sinks[l].reshape(1, -1), conv_w[l], b, s, attn_w, kv_w, conv_wd)
        x1, h2d, logits = _outproj(mix, x2, mod3, w_out[l].astype(BF16), ln1_g[l].reshape(1, -1),
                                   ln1_b[l].reshape(1, -1), w_router[l], s)
        eidx, gate_w, tile_counts = _route(logits, router_bias[l].reshape(1, -1))
        block_e, block_new, block_nv, n_used, row_tgt = _dispatch_tables(eidx, tile_counts, n)
        ys = _moe(block_e, block_new, block_nv, n_used, row_tgt, h2d, w_gate[l], w_up[l], w_down[l])
        x2 = _final(gate_w.reshape(-1), ys.reshape(TOP_K, n * TOKEN_ROWS, LANES), h2d, x1, mod3,
                    ws_gate[l].astype(BF16), ws_up[l].astype(BF16), ws_down[l].astype(BF16),
                    ln2_g[l].reshape(1, -1), ln2_b[l].reshape(1, -1), s)
    return x2.reshape(b, s, d)
```

```python
import functools

import jax
import jax.numpy as jnp
from jax import lax
from jax.experimental import pallas as pl
from jax.experimental.pallas import tpu as pltpu

HEAD_DIM = 64
N_Q_HEADS = 16
N_KV_HEADS = 4
GQA = N_Q_HEADS // N_KV_HEADS
CONV_K = 3
WINDOW = 128
Q_BLOCK = 128
N_EXPERTS = 64
TOP_K = 8
N_GROUPS = 8
GROUP_SIZE = N_EXPERTS // N_GROUPS
TOPK_GROUPS = 4
ROUTED_SCALE = 2.5
DEPTH = 1
ALPHA = (2.0 * DEPTH) ** 0.25
LN_EPS = 1e-5

LANES = 128
SUBLANES = 8
TOKEN_ROWS = 8
SLAB_PITCH = 8
VMEM_LIMIT = 56 * 1024 * 1024

MOD_TN = 1024
INPROJ_TM = 512
INPROJ_TN = 1536
OUTPROJ_TM = 256
ROUTE_TM = 512
MOE_TM = 256
FINAL_TM = 128
IDX_CHUNK = 1024
ISSUE_UNROLL = 8

F32 = jnp.float32
BF16 = jnp.bfloat16


def _cparams(sem):
    return pltpu.CompilerParams(dimension_semantics=sem, vmem_limit_bytes=VMEM_LIMIT)


def _silu(v):
    return v * jax.nn.sigmoid(v)


U32 = jnp.uint32
HI_MASK = 0xFFFF0000


def _pack_pairs(lo, hi):
    lo_bits = lax.bitcast_convert_type(lo.astype(BF16).astype(F32), U32) >> 16
    hi_bits = lax.bitcast_convert_type(hi.astype(BF16).astype(F32), U32) & U32(HI_MASK)
    return lo_bits | hi_bits


def _unpack_pairs(w):
    return (lax.bitcast_convert_type(w << 16, F32), lax.bitcast_convert_type(w & U32(HI_MASK), F32))


def _slab_rows_to_matrix(ref, base, tm, pitch):
    return jnp.concatenate([ref[pl.ds(base + s, tm, stride=pitch), :] for s in range(TOKEN_ROWS)], axis=-1)


def _layer_norm(y, g, b):
    mu = jnp.mean(y, axis=-1, keepdims=True)
    yc = y - mu
    var = jnp.mean(yc * yc, axis=-1, keepdims=True)
    return yc * lax.rsqrt(var + LN_EPS) * g + b


def _mod_kernel(c_ref, w_ref, b_ref, o_ref):
    cs = _silu(c_ref[...]).astype(BF16)
    o_ref[...] = jnp.dot(cs, w_ref[...].astype(BF16), preferred_element_type=F32) + b_ref[...]


def _mod(c8, w_mod, b_mod):
    d, n = w_mod.shape
    return pl.pallas_call(
        _mod_kernel,
        grid=(n // MOD_TN,),
        in_specs=[pl.BlockSpec((SUBLANES, d), lambda j: (0, 0)),
                  pl.BlockSpec((d, MOD_TN), lambda j: (0, j)),
                  pl.BlockSpec((1, MOD_TN), lambda j: (0, j))],
        out_specs=pl.BlockSpec((SUBLANES, MOD_TN), lambda j: (0, j)),
        out_shape=jax.ShapeDtypeStruct((SUBLANES, n), F32),
        compiler_params=_cparams(("arbitrary",)),
        name="mod",
    )(c8, w_mod, b_mod)


def _inproj_kernel(x_ref, mod_ref, w_ref, o_ref, h_ref):
    j = pl.program_id(1)

    @pl.when(j == 0)
    def _():
        m = mod_ref[0]
        h_ref[...] = (x_ref[...] * (1.0 + m[1:2]) + m[0:1]).astype(BF16)

    o_ref[...] = jnp.dot(h_ref[...], w_ref[0], preferred_element_type=F32).astype(BF16)


def _inproj(x2, mod3, w_in3, seq):
    n, d = x2.shape
    nj, _, tn = w_in3.shape
    tm = INPROJ_TM
    return pl.pallas_call(
        _inproj_kernel,
        grid=(n // tm, nj),
        in_specs=[pl.BlockSpec((tm, d), lambda i, j: (i, 0)),
                  pl.BlockSpec((1, 6, d), lambda i, j: (i * tm // seq, 0, 0)),
                  pl.BlockSpec((1, d, tn), lambda i, j: (j, 0, 0))],
        out_specs=pl.BlockSpec((tm, tn), lambda i, j: (i, j)),
        out_shape=jax.ShapeDtypeStruct((n, nj * tn), BF16),
        scratch_shapes=[pltpu.VMEM((tm, d), BF16)],
        compiler_params=_cparams(("arbitrary", "arbitrary")),
        name="inproj",
    )(x2, mod3, w_in3)


def _mixer_kernel(cur_ref, pk_ref, pv_ref, prow_ref, sink_ref, cw_ref, o_ref, *, attn_w, kv_w, conv_w):
    nblk = pl.program_id(1)
    has_prev = nblk > 0
    qb = Q_BLOCK
    cur = cur_ref[...]
    k_cur = cur[:, attn_w:attn_w + kv_w]
    v_cur = cur[:, attn_w + kv_w:attn_w + 2 * kv_w]
    k_all = jnp.concatenate([pk_ref[...], k_cur], axis=0)
    v_all = jnp.concatenate([pv_ref[...], v_cur], axis=0)

    rows = GQA * qb
    qi = lax.broadcasted_iota(jnp.int32, (rows, 2 * qb), 0) % qb
    kj = lax.broadcasted_iota(jnp.int32, (rows, 2 * qb), 1)
    dist = qi + qb - kj
    kmin = jnp.where(has_prev, 0, qb)
    valid = (dist >= 0) & (dist < WINDOW) & (kj >= kmin)
    distf = dist.astype(F32)
    head_in_group = lax.broadcasted_iota(jnp.int32, (rows, 1), 0) // qb
    sinks = sink_ref[...]

    outs = []
    for g in range(N_KV_HEADS):
        q4 = jnp.concatenate(
            [cur[:, (g * GQA + j) * HEAD_DIM:(g * GQA + j + 1) * HEAD_DIM] for j in range(GQA)], axis=0)
        kg = k_all[:, g * HEAD_DIM:(g + 1) * HEAD_DIM]
        vg = v_all[:, g * HEAD_DIM:(g + 1) * HEAD_DIM]
        s = lax.dot_general(q4, kg, (((1,), (1,)), ((), ())), preferred_element_type=F32)
        s = s * (HEAD_DIM ** -0.5)
        slope = jnp.zeros((rows, 1), F32)
        sink = jnp.zeros((rows, 1), F32)
        for j in range(GQA):
            h = g * GQA + j
            sel = head_in_group == j
            slope = jnp.where(sel, 2.0 ** (-8.0 * (h + 1) / N_Q_HEADS), slope)
            sink = jnp.where(sel, sinks[:, h:h + 1], sink)
        s = jnp.where(valid, s - slope * distf, -jnp.inf)
        m = jnp.maximum(jnp.max(s, axis=-1, keepdims=True), sink)
        p = jnp.exp(s - m)
        denom = jnp.sum(p, axis=-1, keepdims=True) + jnp.exp(sink - m)
        o4 = jnp.dot(p.astype(BF16), vg, preferred_element_type=F32) / denom
        outs.extend(o4[j * qb:(j + 1) * qb] for j in range(GQA))
    attn = jnp.concatenate(outs, axis=-1)

    c0 = attn_w + 2 * kv_w
    cb = cur[:, c0:c0 + conv_w].astype(F32)
    u = cur[:, c0 + conv_w:c0 + 2 * conv_w].astype(F32) * cur[:, c0 + 2 * conv_w:c0 + 3 * conv_w].astype(F32)
    prow = prow_ref[...]
    up = prow[:, c0 + conv_w:c0 + 2 * conv_w].astype(F32) * prow[:, c0 + 2 * conv_w:c0 + 3 * conv_w].astype(F32)
    up = up * jnp.where(has_prev, 1.0, 0.0)
    pm1 = up[15:16]
    pm2 = up[14:15]
    ri = lax.broadcasted_iota(jnp.int32, u.shape, 0)
    u1 = jnp.where(ri == 0, pm1, pltpu.roll(u, 1, 0))
    u2 = jnp.where(ri == 0, pm2, jnp.where(ri == 1, pm1, pltpu.roll(u, 2, 0)))
    cw = cw_ref[...]
    conv = cb * (cw[0:1] * u2 + cw[1:2] * u1 + cw[2:3] * u)
    o_ref[...] = jnp.concatenate([attn, conv], axis=-1).astype(BF16)


def _mixer(proj, sinks2, conv_w, batch, seq, attn_w, kv_w, conv_wd):
    n, in_w = proj.shape
    nb = seq // Q_BLOCK
    kv_blk0 = attn_w // kv_w
    sub16 = Q_BLOCK // 16

    def cur_map(b, i):
        return (b * nb + i, 0)

    def prev_map(col):
        return lambda b, i: (b * nb + jnp.maximum(i - 1, 0), col)

    def prow_map(b, i):
        return (jnp.maximum((b * nb + i) * sub16 - 1, 0), 0)

    kern = functools.partial(_mixer_kernel, attn_w=attn_w, kv_w=kv_w, conv_w=conv_wd)
    return pl.pallas_call(
        kern,
        grid=(batch, nb),
        in_specs=[pl.BlockSpec((Q_BLOCK, in_w), cur_map),
                  pl.BlockSpec((Q_BLOCK, kv_w), prev_map(kv_blk0)),
                  pl.BlockSpec((Q_BLOCK, kv_w), prev_map(kv_blk0 + 1)),
                  pl.BlockSpec((16, in_w), prow_map),
                  pl.BlockSpec((1, N_Q_HEADS), lambda b, i: (0, 0)),
                  pl.BlockSpec((CONV_K, conv_wd), lambda b, i: (0, 0))],
        out_specs=pl.BlockSpec((Q_BLOCK, attn_w + conv_wd), cur_map),
        out_shape=jax.ShapeDtypeStruct((n, attn_w + conv_wd), BF16),
        compiler_params=_cparams(("arbitrary", "arbitrary")),
        name="mixer",
    )(proj, proj, proj, proj, sinks2, conv_w)


def _split_bf16(v):
    hi = v.astype(BF16)
    lo = (v - hi.astype(F32)).astype(BF16)
    return hi, lo


def _outproj_kernel(mix_ref, x_ref, mod_ref, w_ref, g_ref, b_ref, wr_ref, x1_ref, h2_ref, lg_ref):
    m = mod_ref[0]
    mix = jnp.dot(mix_ref[...], w_ref[...], preferred_element_type=F32)
    x1 = _layer_norm(ALPHA * x_ref[...] + (1.0 + m[2:3]) * mix, g_ref[...], b_ref[...])
    x1_ref[...] = x1
    h2 = x1 * (1.0 + m[4:5]) + m[3:4]
    tm, d = h2.shape
    words = _pack_pairs(h2[:, :d // 2], h2[:, d // 2:])
    for s in range(TOKEN_ROWS):
        h2_ref[pl.ds(s, tm, stride=TOKEN_ROWS), :] = words[:, s * LANES:(s + 1) * LANES]
    h_hi, h_lo = _split_bf16(h2)
    w_hi, w_lo = _split_bf16(wr_ref[...])
    lg_ref[...] = (jnp.dot(h_hi, w_hi, preferred_element_type=F32)
                   + (jnp.dot(h_hi, w_lo, preferred_element_type=F32)
                      + jnp.dot(h_lo, w_hi, preferred_element_type=F32)))


def _outproj(mix, x2, mod3, w_out_bf, ln_g, ln_b, w_router, seq):
    n, d = x2.shape
    tm = OUTPROJ_TM
    ne = w_router.shape[1]
    row = lambda i: (i, 0)
    const = lambda i: (0, 0)
    return pl.pallas_call(
        _outproj_kernel,
        grid=(n // tm,),
        in_specs=[pl.BlockSpec((tm, d), row),
                  pl.BlockSpec((tm, d), row),
                  pl.BlockSpec((1, 6, d), lambda i: (i * tm // seq, 0, 0)),
                  pl.BlockSpec((d, d), const),
                  pl.BlockSpec((1, d), const),
                  pl.BlockSpec((1, d), const),
                  pl.BlockSpec((d, ne), const)],
        out_specs=[pl.BlockSpec((tm, d), row),
                   pl.BlockSpec((tm * TOKEN_ROWS, LANES), row),
                   pl.BlockSpec((tm, ne), row)],
        out_shape=[jax.ShapeDtypeStruct((n, d), F32),
                   jax.ShapeDtypeStruct((n * TOKEN_ROWS, LANES), U32),
                   jax.ShapeDtypeStruct((n, ne), F32)],
        compiler_params=_cparams(("arbitrary",)),
        name="outproj",
    )(mix, x2, mod3, w_out_bf, ln_g, ln_b, w_router)


def _first_argmax(v, lane_f):
    m = jnp.max(v, axis=-1, keepdims=True)
    idx = jnp.min(jnp.where(v == m, lane_f, float(N_EXPERTS)), axis=-1, keepdims=True)
    return m, idx


def _route_kernel(lg_ref, bias_ref, eidx_ref, w_ref, cnt_ref):
    scores = jax.nn.sigmoid(lg_ref[...])
    sel = scores + bias_ref[...]
    tm = sel.shape[0]
    lane = lax.broadcasted_iota(jnp.int32, (tm, N_EXPERTS), 1)
    lane_f = lane.astype(F32)
    grp = lane // GROUP_SIZE
    neg = -jnp.inf
    gs = []
    for g in range(N_GROUPS):
        vg = jnp.where(grp == g, sel, neg)
        m1, i1 = _first_argmax(vg, lane_f)
        m2 = jnp.max(jnp.where(lane_f == i1, neg, vg), axis=-1, keepdims=True)
        gs.append(m1 + m2)
    keep = jnp.zeros((tm, N_EXPERTS), F32)
    for g in range(N_GROUPS):
        rank = jnp.zeros((tm, 1), F32)
        for o in range(N_GROUPS):
            if o == g:
                continue
            ahead = (gs[o] >= gs[g]) if o < g else (gs[o] > gs[g])
            rank = rank + jnp.where(ahead, 1.0, 0.0)
        keep = jnp.where(grp == g, jnp.where(rank < TOPK_GROUPS, 1.0, 0.0), keep)
    cand = jnp.where(keep > 0.5, sel, neg)
    idxs, ws = [], []
    chosen = jnp.zeros((tm, N_EXPERTS), F32)
    for _ in range(TOP_K):
        _, ik = _first_argmax(cand, lane_f)
        hit = lane_f == ik
        ws.append(jnp.sum(jnp.where(hit, scores, 0.0), axis=-1, keepdims=True))
        idxs.append(ik)
        cand = jnp.where(hit, neg, cand)
        chosen = jnp.where(hit, 1.0, chosen)
    cnt_ref[0] = jnp.sum(chosen, axis=0, keepdims=True)
    wsum = ws[0]
    for k in range(1, TOP_K):
        wsum = wsum + ws[k]
    col = lax.broadcasted_iota(jnp.int32, (tm, TOP_K), 1)
    eidx = jnp.zeros((tm, TOP_K), F32)
    wout = jnp.zeros((tm, TOP_K), F32)
    for k in range(TOP_K):
        eidx = jnp.where(col == k, idxs[k], eidx)
        wout = jnp.where(col == k, ws[k] / wsum * ROUTED_SCALE, wout)
    eidx_ref[...] = eidx.astype(jnp.int32)
    w_ref[...] = wout


def _route(logits, bias2):
    n, ne = logits.shape
    tm = ROUTE_TM
    row = lambda i: (i, 0)
    return pl.pallas_call(
        _route_kernel,
        grid=(n // tm,),
        in_specs=[pl.BlockSpec((tm, ne), row), pl.BlockSpec((1, ne), lambda i: (0, 0))],
        out_specs=[pl.BlockSpec((tm, TOP_K), row), pl.BlockSpec((tm, TOP_K), row),
                   pl.BlockSpec((1, 1, ne), lambda i: (i, 0, 0))],
        out_shape=[jax.ShapeDtypeStruct((n, TOP_K), jnp.int32), jax.ShapeDtypeStruct((n, TOP_K), F32),
                   jax.ShapeDtypeStruct((n // tm, 1, ne), F32)],
        compiler_params=_cparams(("arbitrary",)),
        name="route",
    )(logits, bias2)


def _issue_rows(lo, hi, issue_one):
    n_full = (hi - lo) // ISSUE_UNROLL

    def chunk(c, carry):
        for u in range(ISSUE_UNROLL):
            issue_one(lo + c * ISSUE_UNROLL + u)
        return carry

    def tail(r, carry):
        issue_one(r)
        return carry

    lax.fori_loop(0, n_full, chunk, 0)
    lax.fori_loop(lo + n_full * ISSUE_UNROLL, hi, tail, 0)


def _hbm_slab(ref, row):
    return ref.at[pl.ds(pl.multiple_of(row * TOKEN_ROWS, TOKEN_ROWS), TOKEN_ROWS), :]


def _moe_kernel(be_ref, new_ref, nused_ref, tgt_hbm, h_hbm, wg_ref, wu_ref, wd_ref, ys_hbm,
                idx_s, xbuf, ybuf, wg_s, wu_s, wd_s, sem_i, sem_g, sem_s, *, n_tok):
    s = pl.program_id(0)
    tm = MOE_TM
    per = IDX_CHUNK // tm
    per_log2 = per.bit_length() - 1
    n_used = nused_ref[0]
    slab = TOKEN_ROWS
    buf_rows = tm * SLAB_PITCH
    dump_row0 = TOP_K * n_tok
    slot = s & 1

    def staged(buf, base, r):
        return buf.at[pl.ds(pl.multiple_of(base + r * SLAB_PITCH, SUBLANES), slab), :]

    def idx_copy(c):
        return pltpu.make_async_copy(
            tgt_hbm.at[pl.ds(pl.multiple_of(c * IDX_CHUNK, IDX_CHUNK), IDX_CHUNK)],
            idx_s.at[pl.ds(pl.multiple_of((c & 1) * IDX_CHUNK, IDX_CHUNK), IDX_CHUNK)], sem_i)

    def idx_base(b):
        return ((b >> per_log2) & 1) * IDX_CHUNK + (b & (per - 1)) * tm

    def for_rows(inline, body):
        if inline:
            for r in range(tm):
                body(r, r % 2)
        else:
            def pair(c, carry):
                body(2 * c, 0)
                body(2 * c + 1, 1)
                return carry
            lax.fori_loop(0, tm // 2, pair, 0)

    def issue_gather(b, to_slot, inline):
        ibase = idx_base(b)
        xbase = to_slot * buf_rows

        def one(r, prio):
            tok = idx_s[ibase + r] & (n_tok - 1)
            pltpu.make_async_copy(_hbm_slab(h_hbm, tok), staged(xbuf, xbase, r),
                                  sem_g.at[to_slot]).start(priority=prio)
        for_rows(inline, one)

    def issue_scatter(b, from_slot, to_dump, inline):
        ibase = idx_base(b)
        ybase = from_slot * buf_rows

        def one(r, prio):
            tgt = jnp.where(to_dump, dump_row0 + tm + r, idx_s[ibase + r])
            pltpu.make_async_copy(staged(ybuf, ybase, r), _hbm_slab(ys_hbm, tgt),
                                  sem_s.at[from_slot]).start(priority=prio)
        for_rows(inline, one)

    def wait_gather(at_slot):
        v = xbuf.at[pl.ds(pl.multiple_of(at_slot * buf_rows, SUBLANES), buf_rows), :]
        pltpu.make_async_copy(h_hbm.at[pl.ds(0, buf_rows), :], v, sem_g.at[at_slot]).wait()

    def wait_scatter(at_slot):
        v = ybuf.at[pl.ds(pl.multiple_of(at_slot * buf_rows, SUBLANES), buf_rows), :]
        pltpu.make_async_copy(v, ys_hbm.at[pl.ds(0, buf_rows), :], sem_s.at[at_slot]).wait()

    @pl.when(s == 0)
    def _():
        ybuf[...] = jnp.zeros(ybuf.shape, U32)
        first = idx_copy(0)
        first.start()
        first.wait()
        init = pltpu.make_async_copy(ybuf, ys_hbm.at[pl.ds(pl.multiple_of(dump_row0 * slab, slab), 2 * buf_rows), :],
                                     sem_i)
        init.start()
        init.wait()
        issue_gather(0, 0, inline=False)

    @pl.when(s < n_used)
    def _():
        @pl.when(((s + 1) & (per - 1)) == 0)
        def _():
            nxt = idx_copy((s + 1) >> per_log2)
            nxt.start()
            nxt.wait()

        @pl.when(new_ref[s] == 1)
        def _():
            wg_s[...] = wg_ref[0].astype(BF16)
            wu_s[...] = wu_ref[0].astype(BF16)
            wd_s[...] = wd_ref[0].astype(BF16)

        wait_gather(slot)

        @pl.when(s >= 1)
        def _():
            wait_scatter(slot)

        issue_gather(s + 1, 1 - slot, inline=True)
        issue_scatter(jnp.maximum(s - 1, 0), 1 - slot, s == 0, inline=True)
        base = slot * buf_rows
        x_lo, x_hi = _unpack_pairs(_slab_rows_to_matrix(xbuf, base, tm, SLAB_PITCH))
        x = jnp.concatenate([x_lo.astype(BF16), x_hi.astype(BF16)], axis=-1)
        gate = jnp.dot(x, wg_s[...], preferred_element_type=F32)
        up = jnp.dot(x, wu_s[...], preferred_element_type=F32)
        act = (_silu(gate) * up).astype(BF16)
        y = jnp.dot(act, wd_s[...], preferred_element_type=F32)
        half = slab * LANES
        words = _pack_pairs(y[:, :half], y[:, half:])
        for j in range(slab):
            ybuf[pl.ds(base + j, tm, stride=SLAB_PITCH), :] = words[:, j * LANES:(j + 1) * LANES]

    @pl.when(s == n_used)
    def _():
        wait_gather(slot)
        wait_scatter(slot)
        issue_scatter(s - 1, 1 - slot, False, inline=False)
        wait_scatter(1 - slot)


def _moe(block_e, block_new, n_used, row_tgt, h2d, w_gate, w_up, w_down):
    n = h2d.shape[0] // TOKEN_ROWS
    assert n & (n - 1) == 0
    d = 2 * TOKEN_ROWS * LANES
    de = w_gate.shape[2]
    tm = MOE_TM
    n_blk = block_e.shape[0]
    wmap = lambda i, be, nw, nu: (be[i], 0, 0)
    grid_spec = pltpu.PrefetchScalarGridSpec(
        num_scalar_prefetch=3,
        grid=(n_blk,),
        in_specs=[pl.BlockSpec(memory_space=pl.ANY),
                  pl.BlockSpec(memory_space=pl.ANY),
                  pl.BlockSpec((1, d, de), wmap),
                  pl.BlockSpec((1, d, de), wmap),
                  pl.BlockSpec((1, de, d), wmap)],
        out_specs=pl.BlockSpec(memory_space=pl.ANY),
        scratch_shapes=[pltpu.SMEM((2 * IDX_CHUNK,), jnp.int32),
                        pltpu.VMEM((2 * tm * SLAB_PITCH, LANES), U32),
                        pltpu.VMEM((2 * tm * SLAB_PITCH, LANES), U32),
                        pltpu.VMEM((d, de), BF16),
                        pltpu.VMEM((d, de), BF16),
                        pltpu.VMEM((de, d), BF16),
                        pltpu.SemaphoreType.DMA,
                        pltpu.SemaphoreType.DMA((2,)),
                        pltpu.SemaphoreType.DMA((2,))])
    return pl.pallas_call(
        functools.partial(_moe_kernel, n_tok=n),
        grid_spec=grid_spec,
        out_shape=jax.ShapeDtypeStruct(((TOP_K * n + 2 * tm) * TOKEN_ROWS, LANES), U32),
        compiler_params=_cparams(("arbitrary",)),
        name="moe",
    )(block_e, block_new, n_used, row_tgt, h2d, w_gate, w_up, w_down)


def _final_kernel(gw_hbm, *refs):
    ys_refs = refs[:TOP_K]
    h_ref, x1_ref, mod_ref, wg_ref, wu_ref, wd_ref, g_ref, b_ref, o_ref, gw_s, acc_ref, sem_i = refs[TOP_K:]
    i = pl.program_id(0)
    tm = x1_ref.shape[0]
    per_step = tm * TOP_K
    cp = pltpu.make_async_copy(gw_hbm.at[pl.ds(pl.multiple_of(i * per_step, per_step), per_step)], gw_s, sem_i)
    cp.start()
    cp.wait()

    hi_base = tm * SLAB_PITCH

    def combine(t, carry):
        rows = pl.ds(pl.multiple_of(t * TOKEN_ROWS, TOKEN_ROWS), TOKEN_ROWS)
        acc_lo = acc_hi = None
        for k in range(TOP_K):
            lo, hi = _unpack_pairs(ys_refs[k][rows, :])
            g = gw_s[t * TOP_K + k]
            acc_lo = g * lo if acc_lo is None else acc_lo + g * lo
            acc_hi = g * hi if acc_hi is None else acc_hi + g * hi
        acc_ref[pl.ds(pl.multiple_of(t * SLAB_PITCH, SUBLANES), TOKEN_ROWS), :] = acc_lo
        acc_ref[pl.ds(pl.multiple_of(hi_base + t * SLAB_PITCH, SUBLANES), TOKEN_ROWS), :] = acc_hi
        return carry

    lax.fori_loop(0, tm, combine, 0, unroll=4)
    moe = jnp.concatenate([_slab_rows_to_matrix(acc_ref, 0, tm, SLAB_PITCH),
                           _slab_rows_to_matrix(acc_ref, hi_base, tm, SLAB_PITCH)], axis=-1)
    h_lo, h_hi = _unpack_pairs(_slab_rows_to_matrix(h_ref, 0, tm, TOKEN_ROWS))
    h = jnp.concatenate([h_lo.astype(BF16), h_hi.astype(BF16)], axis=-1)
    gate = jnp.dot(h, wg_ref[...], preferred_element_type=F32)
    up = jnp.dot(h, wu_ref[...], preferred_element_type=F32)
    shared = jnp.dot((_silu(gate) * up).astype(BF16), wd_ref[...], preferred_element_type=F32)
    m = mod_ref[0]
    y = ALPHA * x1_ref[...] + (1.0 + m[5:6]) * (moe + shared)
    o_ref[...] = _layer_norm(y, g_ref[...], b_ref[...])


def _final(gate_w, ys, h2d, x1, mod3, wsg, wsu, wsd, ln_g, ln_b, seq):
    n, d = x1.shape
    de = wsg.shape[1]
    tm = FINAL_TM
    per_step = tm * TOP_K
    assert per_step % IDX_CHUNK == 0
    row = lambda i: (i, 0)
    const = lambda i: (0, 0)
    slot_rows = lambda k: (lambda i: (k * (n // tm) + i, 0))
    return pl.pallas_call(
        _final_kernel,
        grid=(n // tm,),
        in_specs=[pl.BlockSpec(memory_space=pl.ANY)]
                 + [pl.BlockSpec((tm * TOKEN_ROWS, LANES), slot_rows(k)) for k in range(TOP_K)]
                 + [pl.BlockSpec((tm * TOKEN_ROWS, LANES), row),
                  pl.BlockSpec((tm, d), row),
                  pl.BlockSpec((1, 6, d), lambda i: (i * tm // seq, 0, 0)),
                  pl.BlockSpec((d, de), const),
                  pl.BlockSpec((d, de), const),
                  pl.BlockSpec((de, d), const),
                  pl.BlockSpec((1, d), const),
                  pl.BlockSpec((1, d), const)],
        out_specs=pl.BlockSpec((tm, d), row),
        out_shape=jax.ShapeDtypeStruct((n, d), F32),
        scratch_shapes=[pltpu.SMEM((per_step,), F32),
                        pltpu.VMEM((2 * tm * SLAB_PITCH, LANES), F32),
                        pltpu.SemaphoreType.DMA],
        compiler_params=_cparams(("arbitrary",)),
        name="final",
    )(gate_w, *([ys] * TOP_K), h2d, x1, mod3, wsg, wsu, wsd, ln_g, ln_b)


def _dispatch_tables(eidx, tile_counts, n):
    tm = MOE_TM
    a = n * TOP_K
    i32 = jnp.int32
    experts = jnp.arange(N_EXPERTS, dtype=i32)
    counts = jnp.sum(tile_counts, axis=(0, 1)).astype(i32)
    padded = (counts + tm - 1) // tm * tm
    pad_end = jnp.cumsum(padded)
    starts = pad_end - padded
    n_blk = a // tm + N_EXPERTS + 1
    blk_start = jnp.arange(n_blk, dtype=i32) * tm
    n_used = pad_end[-1] // tm
    in_use = jnp.arange(n_blk) < n_used
    raw_e = jnp.minimum(jnp.sum((pad_end[None, :] <= blk_start[:, None]).astype(i32), axis=1), N_EXPERTS - 1)
    last_e = jnp.sum(jnp.where(jnp.arange(n_blk) == n_used - 1, raw_e, 0))
    block_e = jnp.where(in_use, raw_e, last_e)
    onehot = block_e[:, None] == experts[None, :]
    cnt_b = jnp.sum(jnp.where(onehot, counts[None, :], 0), axis=1)
    start_b = jnp.sum(jnp.where(onehot, starts[None, :], 0), axis=1)
    block_nv = jnp.where(in_use, jnp.clip(cnt_b - (blk_start - start_b), 0, tm), 0).astype(i32)
    block_new = jnp.concatenate([jnp.ones((1,), i32), (block_e[1:] != block_e[:-1]).astype(i32)])
    dummy_keys = jnp.where(jnp.arange(tm, dtype=i32)[None, :] < (padded - counts)[:, None],
                           experts[:, None], N_EXPERTS).reshape(-1)
    keys = jnp.concatenate([eidx.reshape(-1), dummy_keys, jnp.full((tm,), N_EXPERTS, i32)])
    tgt = (jnp.arange(TOP_K, dtype=i32)[None, :] * n + jnp.arange(n, dtype=i32)[:, None]).reshape(-1)
    payload = jnp.concatenate([tgt, jnp.zeros((n_blk * tm - a,), i32)])
    _, row_tgt = lax.sort((keys, payload), num_keys=1, is_stable=True)
    r = jnp.arange(tm, dtype=i32)[None, :]
    dump = a + (jnp.arange(n_blk, dtype=i32)[:, None] & 1) * tm + r
    row_tgt = jnp.where(r >= block_nv[:, None], dump, row_tgt.reshape(n_blk, tm)).reshape(-1)
    row_tgt = jnp.pad(row_tgt, (0, -(n_blk * tm) % IDX_CHUNK))
    return block_e.astype(i32), block_new, n_used.astype(i32).reshape(1), row_tgt


def kernel(x, c, w_mod, b_mod, w_in, conv_w, attn_sinks, w_out, ln1_g, ln1_b, w_router, router_bias,
           w_gate, w_up, w_down, ws_gate, ws_up, ws_down, ln2_g, ln2_b):
    b, s, d = x.shape
    n = b * s
    attn_w = N_Q_HEADS * HEAD_DIM
    kv_w = N_KV_HEADS * HEAD_DIM
    conv_wd = d - attn_w
    in_w = attn_w + 2 * kv_w + 3 * conv_wd
    x2 = x.reshape(n, d)
    c8 = jnp.zeros((SUBLANES, d), F32).at[:b].set(c)
    for l in range(DEPTH):
        mod = _mod(c8, w_mod[l], b_mod[l].reshape(1, -1))[:b]
        mod3 = mod.reshape(b, 6, d)
        w_in3 = w_in[l].astype(BF16).reshape(d, in_w // INPROJ_TN, INPROJ_TN).transpose(1, 0, 2)
        proj = _inproj(x2, mod3, w_in3, s)
        mix = _mixer(proj, attn_sinks[l].reshape(1, -1), conv_w[l], b, s, attn_w, kv_w, conv_wd)
        x1, h2d, logits = _outproj(mix, x2, mod3, w_out[l].astype(BF16), ln1_g[l].reshape(1, -1),
                                   ln1_b[l].reshape(1, -1), w_router[l], s)
        eidx, gate_w, tile_counts = _route(logits, router_bias[l].reshape(1, -1))
        block_e, block_new, n_used, row_tgt = _dispatch_tables(eidx, tile_counts, n)
        ys = _moe(block_e, block_new, n_used, row_tgt, h2d, w_gate[l], w_up[l], w_down[l])
        x2 = _final(gate_w.reshape(-1), ys, h2d, x1, mod3,
                    ws_gate[l].astype(BF16), ws_up[l].astype(BF16), ws_down[l].astype(BF16),
                    ln2_g[l].reshape(1, -1), ln2_b[l].reshape(1, -1), s)
    return x2.reshape(b, s, d)
```

```python
import functools

import jax
import jax.numpy as jnp
from jax import lax
from jax.experimental import pallas as pl
from jax.experimental.pallas import tpu as pltpu

HEAD_DIM = 64
N_Q_HEADS = 16
N_KV_HEADS = 4
GQA = N_Q_HEADS // N_KV_HEADS
CONV_K = 3
WINDOW = 128
Q_BLOCK = 128
N_EXPERTS = 64
TOP_K = 8
N_GROUPS = 8
GROUP_SIZE = N_EXPERTS // N_GROUPS
TOPK_GROUPS = 4
ROUTED_SCALE = 2.5
DEPTH = 1
ALPHA = (2.0 * DEPTH) ** 0.25
LN_EPS = 1e-5

LANES = 128
SUBLANES = 8
TOKEN_ROWS = 8
SLAB_PITCH = 8
VMEM_LIMIT = 56 * 1024 * 1024

MOD_TN = 1024
INPROJ_TM = 512
INPROJ_TN = 1536
OUTPROJ_TM = 256
ROUTE_TM = 512
MOE_TM = 256
FINAL_TM = 128
IDX_CHUNK = 1024
ISSUE_UNROLL = 8

F32 = jnp.float32
BF16 = jnp.bfloat16


def _cparams(sem):
    return pltpu.CompilerParams(dimension_semantics=sem, vmem_limit_bytes=VMEM_LIMIT)


def _silu(v):
    return v * jax.nn.sigmoid(v)


U32 = jnp.uint32
HI_MASK = 0xFFFF0000


def _pack_pairs(lo, hi):
    lo_bits = lax.bitcast_convert_type(lo.astype(BF16).astype(F32), U32) >> 16
    hi_bits = lax.bitcast_convert_type(hi.astype(BF16).astype(F32), U32) & U32(HI_MASK)
    return lo_bits | hi_bits


def _unpack_pairs(w):
    return (lax.bitcast_convert_type(w << 16, F32), lax.bitcast_convert_type(w & U32(HI_MASK), F32))


def _slab_rows_to_matrix(ref, base, tm, pitch):
    return jnp.concatenate([ref[pl.ds(base + s, tm, stride=pitch), :] for s in range(TOKEN_ROWS)], axis=-1)


def _layer_norm(y, g, b):
    mu = jnp.mean(y, axis=-1, keepdims=True)
    yc = y - mu
    var = jnp.mean(yc * yc, axis=-1, keepdims=True)
    return yc * lax.rsqrt(var + LN_EPS) * g + b


def _mod_kernel(c_ref, w_ref, b_ref, o_ref):
    cs = _silu(c_ref[...]).astype(BF16)
    o_ref[...] = jnp.dot(cs, w_ref[...].astype(BF16), preferred_element_type=F32) + b_ref[...]


def _mod(c8, w_mod, b_mod):
    d, n = w_mod.shape
    return pl.pallas_call(
        _mod_kernel,
        grid=(n // MOD_TN,),
        in_specs=[pl.BlockSpec((SUBLANES, d), lambda j: (0, 0)),
                  pl.BlockSpec((d, MOD_TN), lambda j: (0, j)),
                  pl.BlockSpec((1, MOD_TN), lambda j: (0, j))],
        out_specs=pl.BlockSpec((SUBLANES, MOD_TN), lambda j: (0, j)),
        out_shape=jax.ShapeDtypeStruct((SUBLANES, n), F32),
        compiler_params=_cparams(("arbitrary",)),
        name="mod",
    )(c8, w_mod, b_mod)


def _inproj_kernel(x_ref, mod_ref, w_ref, o_ref, h_ref):
    j = pl.program_id(1)

    @pl.when(j == 0)
    def _():
        m = mod_ref[0]
        h_ref[...] = (x_ref[...] * (1.0 + m[1:2]) + m[0:1]).astype(BF16)

    o_ref[...] = jnp.dot(h_ref[...], w_ref[0], preferred_element_type=F32).astype(BF16)


def _inproj(x2, mod3, w_in3, seq):
    n, d = x2.shape
    nj, _, tn = w_in3.shape
    tm = INPROJ_TM
    return pl.pallas_call(
        _inproj_kernel,
        grid=(n // tm, nj),
        in_specs=[pl.BlockSpec((tm, d), lambda i, j: (i, 0)),
                  pl.BlockSpec((1, 6, d), lambda i, j: (i * tm // seq, 0, 0)),
                  pl.BlockSpec((1, d, tn), lambda i, j: (j, 0, 0))],
        out_specs=pl.BlockSpec((tm, tn), lambda i, j: (i, j)),
        out_shape=jax.ShapeDtypeStruct((n, nj * tn), BF16),
        scratch_shapes=[pltpu.VMEM((tm, d), BF16)],
        compiler_params=_cparams(("arbitrary", "arbitrary")),
        name="inproj",
    )(x2, mod3, w_in3)


def _mixer_kernel(cur_ref, pk_ref, pv_ref, prow_ref, sink_ref, cw_ref, o_ref, *, attn_w, kv_w, conv_w):
    nblk = pl.program_id(1)
    has_prev = nblk > 0
    qb = Q_BLOCK
    cur = cur_ref[...]
    k_cur = cur[:, attn_w:attn_w + kv_w]
    v_cur = cur[:, attn_w + kv_w:attn_w + 2 * kv_w]
    k_all = jnp.concatenate([pk_ref[...], k_cur], axis=0)
    v_all = jnp.concatenate([pv_ref[...], v_cur], axis=0)

    rows = GQA * qb
    qi = lax.broadcasted_iota(jnp.int32, (rows, 2 * qb), 0) % qb
    kj = lax.broadcasted_iota(jnp.int32, (rows, 2 * qb), 1)
    dist = qi + qb - kj
    kmin = jnp.where(has_prev, 0, qb)
    valid = (dist >= 0) & (dist < WINDOW) & (kj >= kmin)
    distf = dist.astype(F32)
    head_in_group = lax.broadcasted_iota(jnp.int32, (rows, 1), 0) // qb
    sinks = sink_ref[...]

    outs = []
    for g in range(N_KV_HEADS):
        q4 = jnp.concatenate(
            [cur[:, (g * GQA + j) * HEAD_DIM:(g * GQA + j + 1) * HEAD_DIM] for j in range(GQA)], axis=0)
        kg = k_all[:, g * HEAD_DIM:(g + 1) * HEAD_DIM]
        vg = v_all[:, g * HEAD_DIM:(g + 1) * HEAD_DIM]
        s = lax.dot_general(q4, kg, (((1,), (1,)), ((), ())), preferred_element_type=F32)
        s = s * (HEAD_DIM ** -0.5)
        slope = jnp.zeros((rows, 1), F32)
        sink = jnp.zeros((rows, 1), F32)
        for j in range(GQA):
            h = g * GQA + j
            sel = head_in_group == j
            slope = jnp.where(sel, 2.0 ** (-8.0 * (h + 1) / N_Q_HEADS), slope)
            sink = jnp.where(sel, sinks[:, h:h + 1], sink)
        s = jnp.where(valid, s - slope * distf, -jnp.inf)
        m = jnp.maximum(jnp.max(s, axis=-1, keepdims=True), sink)
        p = jnp.exp(s - m)
        denom = jnp.sum(p, axis=-1, keepdims=True) + jnp.exp(sink - m)
        o4 = jnp.dot(p.astype(BF16), vg, preferred_element_type=F32) / denom
        outs.extend(o4[j * qb:(j + 1) * qb] for j in range(GQA))
    attn = jnp.concatenate(outs, axis=-1)

    c0 = attn_w + 2 * kv_w
    cb = cur[:, c0:c0 + conv_w].astype(F32)
    u = cur[:, c0 + conv_w:c0 + 2 * conv_w].astype(F32) * cur[:, c0 + 2 * conv_w:c0 + 3 * conv_w].astype(F32)
    prow = prow_ref[...]
    up = prow[:, c0 + conv_w:c0 + 2 * conv_w].astype(F32) * prow[:, c0 + 2 * conv_w:c0 + 3 * conv_w].astype(F32)
    up = up * jnp.where(has_prev, 1.0, 0.0)
    pm1 = up[15:16]
    pm2 = up[14:15]
    ri = lax.broadcasted_iota(jnp.int32, u.shape, 0)
    u1 = jnp.where(ri == 0, pm1, pltpu.roll(u, 1, 0))
    u2 = jnp.where(ri == 0, pm2, jnp.where(ri == 1, pm1, pltpu.roll(u, 2, 0)))
    cw = cw_ref[...]
    conv = cb * (cw[0:1] * u2 + cw[1:2] * u1 + cw[2:3] * u)
    o_ref[...] = jnp.concatenate([attn, conv], axis=-1).astype(BF16)


def _mixer(proj, sinks2, conv_w, batch, seq, attn_w, kv_w, conv_wd):
    n, in_w = proj.shape
    nb = seq // Q_BLOCK
    kv_blk0 = attn_w // kv_w
    sub16 = Q_BLOCK // 16

    def cur_map(b, i):
        return (b * nb + i, 0)

    def prev_map(col):
        return lambda b, i: (b * nb + jnp.maximum(i - 1, 0), col)

    def prow_map(b, i):
        return (jnp.maximum((b * nb + i) * sub16 - 1, 0), 0)

    kern = functools.partial(_mixer_kernel, attn_w=attn_w, kv_w=kv_w, conv_w=conv_wd)
    return pl.pallas_call(
        kern,
        grid=(batch, nb),
        in_specs=[pl.BlockSpec((Q_BLOCK, in_w), cur_map),
                  pl.BlockSpec((Q_BLOCK, kv_w), prev_map(kv_blk0)),
                  pl.BlockSpec((Q_BLOCK, kv_w), prev_map(kv_blk0 + 1)),
                  pl.BlockSpec((16, in_w), prow_map),
                  pl.BlockSpec((1, N_Q_HEADS), lambda b, i: (0, 0)),
                  pl.BlockSpec((CONV_K, conv_wd), lambda b, i: (0, 0))],
        out_specs=pl.BlockSpec((Q_BLOCK, attn_w + conv_wd), cur_map),
        out_shape=jax.ShapeDtypeStruct((n, attn_w + conv_wd), BF16),
        compiler_params=_cparams(("arbitrary", "arbitrary")),
        name="mixer",
    )(proj, proj, proj, proj, sinks2, conv_w)


def _split_bf16(v):
    hi = v.astype(BF16)
    lo = (v - hi.astype(F32)).astype(BF16)
    return hi, lo


def _outproj_kernel(mix_ref, x_ref, mod_ref, w_ref, g_ref, b_ref, wr_ref, x1_ref, h2_ref, lg_ref):
    m = mod_ref[0]
    mix = jnp.dot(mix_ref[...], w_ref[...], preferred_element_type=F32)
    x1 = _layer_norm(ALPHA * x_ref[...] + (1.0 + m[2:3]) * mix, g_ref[...], b_ref[...])
    x1_ref[...] = x1
    h2 = x1 * (1.0 + m[4:5]) + m[3:4]
    tm, d = h2.shape
    words = _pack_pairs(h2[:, :d // 2], h2[:, d // 2:])
    for s in range(TOKEN_ROWS):
        h2_ref[pl.ds(s, tm, stride=TOKEN_ROWS), :] = words[:, s * LANES:(s + 1) * LANES]
    h_hi, h_lo = _split_bf16(h2)
    w_hi, w_lo = _split_bf16(wr_ref[...])
    lg_ref[...] = (jnp.dot(h_hi, w_hi, preferred_element_type=F32)
                   + (jnp.dot(h_hi, w_lo, preferred_element_type=F32)
                      + jnp.dot(h_lo, w_hi, preferred_element_type=F32)))


def _outproj(mix, x2, mod3, w_out_bf, ln_g, ln_b, w_router, seq):
    n, d = x2.shape
    tm = OUTPROJ_TM
    ne = w_router.shape[1]
    row = lambda i: (i, 0)
    const = lambda i: (0, 0)
    return pl.pallas_call(
        _outproj_kernel,
        grid=(n // tm,),
        in_specs=[pl.BlockSpec((tm, d), row),
                  pl.BlockSpec((tm, d), row),
                  pl.BlockSpec((1, 6, d), lambda i: (i * tm // seq, 0, 0)),
                  pl.BlockSpec((d, d), const),
                  pl.BlockSpec((1, d), const),
                  pl.BlockSpec((1, d), const),
                  pl.BlockSpec((d, ne), const)],
        out_specs=[pl.BlockSpec((tm, d), row),
                   pl.BlockSpec((tm * TOKEN_ROWS, LANES), row),
                   pl.BlockSpec((tm, ne), row)],
        out_shape=[jax.ShapeDtypeStruct((n, d), F32),
                   jax.ShapeDtypeStruct((n * TOKEN_ROWS, LANES), U32),
                   jax.ShapeDtypeStruct((n, ne), F32)],
        compiler_params=_cparams(("arbitrary",)),
        name="outproj",
    )(mix, x2, mod3, w_out_bf, ln_g, ln_b, w_router)


def _first_argmax(v, lane_f):
    m = jnp.max(v, axis=-1, keepdims=True)
    idx = jnp.min(jnp.where(v == m, lane_f, float(N_EXPERTS)), axis=-1, keepdims=True)
    return m, idx


def _route_kernel(lg_ref, bias_ref, eidx_ref, w_ref, cnt_ref):
    scores = jax.nn.sigmoid(lg_ref[...])
    sel = scores + bias_ref[...]
    tm = sel.shape[0]
    lane = lax.broadcasted_iota(jnp.int32, (tm, N_EXPERTS), 1)
    lane_f = lane.astype(F32)
    grp = lane // GROUP_SIZE
    neg = -jnp.inf
    gs = []
    for g in range(N_GROUPS):
        vg = jnp.where(grp == g, sel, neg)
        m1, i1 = _first_argmax(vg, lane_f)
        m2 = jnp.max(jnp.where(lane_f == i1, neg, vg), axis=-1, keepdims=True)
        gs.append(m1 + m2)
    keep = jnp.zeros((tm, N_EXPERTS), F32)
    for g in range(N_GROUPS):
        rank = jnp.zeros((tm, 1), F32)
        for o in range(N_GROUPS):
            if o == g:
                continue
            ahead = (gs[o] >= gs[g]) if o < g else (gs[o] > gs[g])
            rank = rank + jnp.where(ahead, 1.0, 0.0)
        keep = jnp.where(grp == g, jnp.where(rank < TOPK_GROUPS, 1.0, 0.0), keep)
    cand = jnp.where(keep > 0.5, sel, neg)
    idxs, ws = [], []
    chosen = jnp.zeros((tm, N_EXPERTS), F32)
    for _ in range(TOP_K):
        _, ik = _first_argmax(cand, lane_f)
        hit = lane_f == ik
        ws.append(jnp.sum(jnp.where(hit, scores, 0.0), axis=-1, keepdims=True))
        idxs.append(ik)
        cand = jnp.where(hit, neg, cand)
        chosen = jnp.where(hit, 1.0, chosen)
    cnt_ref[0] = jnp.sum(chosen, axis=0, keepdims=True)
    wsum = ws[0]
    for k in range(1, TOP_K):
        wsum = wsum + ws[k]
    col = lax.broadcasted_iota(jnp.int32, (tm, TOP_K), 1)
    eidx = jnp.zeros((tm, TOP_K), F32)
    wout = jnp.zeros((tm, TOP_K), F32)
    for k in range(TOP_K):
        eidx = jnp.where(col == k, idxs[k], eidx)
        wout = jnp.where(col == k, ws[k] / wsum * ROUTED_SCALE, wout)
    eidx_ref[...] = eidx.astype(jnp.int32)
    w_ref[...] = wout


def _route(logits, bias2):
    n, ne = logits.shape
    tm = ROUTE_TM
    row = lambda i: (i, 0)
    return pl.pallas_call(
        _route_kernel,
        grid=(n // tm,),
        in_specs=[pl.BlockSpec((tm, ne), row), pl.BlockSpec((1, ne), lambda i: (0, 0))],
        out_specs=[pl.BlockSpec((tm, TOP_K), row), pl.BlockSpec((tm, TOP_K), row),
                   pl.BlockSpec((1, 1, ne), lambda i: (i, 0, 0))],
        out_shape=[jax.ShapeDtypeStruct((n, TOP_K), jnp.int32), jax.ShapeDtypeStruct((n, TOP_K), F32),
                   jax.ShapeDtypeStruct((n // tm, 1, ne), F32)],
        compiler_params=_cparams(("arbitrary",)),
        name="route",
    )(logits, bias2)


def _issue_rows(lo, hi, issue_one):
    n_full = (hi - lo) // ISSUE_UNROLL

    def chunk(c, carry):
        for u in range(ISSUE_UNROLL):
            issue_one(lo + c * ISSUE_UNROLL + u)
        return carry

    def tail(r, carry):
        issue_one(r)
        return carry

    lax.fori_loop(0, n_full, chunk, 0)
    lax.fori_loop(lo + n_full * ISSUE_UNROLL, hi, tail, 0)


def _hbm_slab(ref, row):
    return ref.at[pl.ds(pl.multiple_of(row * TOKEN_ROWS, TOKEN_ROWS), TOKEN_ROWS), :]


def _moe_kernel(be_ref, new_ref, nexte_ref, epar_ref, nused_ref, tgt_hbm, h_hbm, wg_hbm, wu_hbm, wd_hbm, ys_hbm,
                idx_s, xbuf, ybuf, wg_f, wu_f, wd_f, wg_s, wu_s, wd_s, sem_i, sem_g, sem_s, sem_w, *, n_tok):
    s = pl.program_id(0)
    tm = MOE_TM
    per = IDX_CHUNK // tm
    per_log2 = per.bit_length() - 1
    n_used = nused_ref[0]
    slab = TOKEN_ROWS
    buf_rows = tm * SLAB_PITCH
    dump_row0 = TOP_K * n_tok
    slot = s & 1

    def staged(buf, base, r):
        return buf.at[pl.ds(pl.multiple_of(base + r * SLAB_PITCH, SUBLANES), slab), :]

    def idx_copy(c):
        return pltpu.make_async_copy(
            tgt_hbm.at[pl.ds(pl.multiple_of(c * IDX_CHUNK, IDX_CHUNK), IDX_CHUNK)],
            idx_s.at[pl.ds(pl.multiple_of((c & 1) * IDX_CHUNK, IDX_CHUNK), IDX_CHUNK)], sem_i)

    def idx_base(b):
        return ((b >> per_log2) & 1) * IDX_CHUNK + (b & (per - 1)) * tm

    def weight_copies(e, p):
        return (pltpu.make_async_copy(wg_hbm.at[e], wg_f.at[p], sem_w.at[p]),
                pltpu.make_async_copy(wu_hbm.at[e], wu_f.at[p], sem_w.at[p]),
                pltpu.make_async_copy(wd_hbm.at[e], wd_f.at[p], sem_w.at[p]))

    def for_rows(inline, body):
        if inline:
            for r in range(tm):
                body(r, r % 2)
        else:
            def pair(c, carry):
                body(2 * c, 0)
                body(2 * c + 1, 1)
                return carry
            lax.fori_loop(0, tm // 2, pair, 0)

    def issue_gather(b, to_slot, inline):
        ibase = idx_base(b)
        xbase = to_slot * buf_rows

        def one(r, prio):
            tok = idx_s[ibase + r] & (n_tok - 1)
            pltpu.make_async_copy(_hbm_slab(h_hbm, tok), staged(xbuf, xbase, r),
                                  sem_g.at[to_slot]).start(priority=prio)
        for_rows(inline, one)

    def issue_scatter(b, from_slot, to_dump, inline):
        ibase = idx_base(b)
        ybase = from_slot * buf_rows

        def one(r, prio):
            tgt = jnp.where(to_dump, dump_row0 + tm + r, idx_s[ibase + r])
            pltpu.make_async_copy(staged(ybuf, ybase, r), _hbm_slab(ys_hbm, tgt),
                                  sem_s.at[from_slot]).start(priority=prio)
        for_rows(inline, one)

    def wait_gather(at_slot):
        v = xbuf.at[pl.ds(pl.multiple_of(at_slot * buf_rows, SUBLANES), buf_rows), :]
        pltpu.make_async_copy(h_hbm.at[pl.ds(0, buf_rows), :], v, sem_g.at[at_slot]).wait()

    def wait_scatter(at_slot):
        v = ybuf.at[pl.ds(pl.multiple_of(at_slot * buf_rows, SUBLANES), buf_rows), :]
        pltpu.make_async_copy(v, ys_hbm.at[pl.ds(0, buf_rows), :], sem_s.at[at_slot]).wait()

    @pl.when(s == 0)
    def _():
        ybuf[...] = jnp.zeros(ybuf.shape, U32)
        first = idx_copy(0)
        first.start()
        first.wait()
        init = pltpu.make_async_copy(ybuf, ys_hbm.at[pl.ds(pl.multiple_of(dump_row0 * slab, slab), 2 * buf_rows), :],
                                     sem_i)
        init.start()
        init.wait()
        issue_gather(0, 0, inline=False)
        for cp in weight_copies(be_ref[0], 0):
            cp.start()

    @pl.when(s < n_used)
    def _():
        @pl.when((s & (per - 1)) == 1)
        def _():
            idx_copy((s >> per_log2) + 1).start()

        @pl.when(((s + 1) & (per - 1)) == 0)
        def _():
            idx_copy((s + 1) >> per_log2).wait()

        @pl.when(new_ref[s] == 1)
        def _():
            p = epar_ref[s]

            @pl.when(nexte_ref[s] >= 0)
            def _():
                for cp in weight_copies(nexte_ref[s], 1 - p):
                    cp.start()

            for cp in weight_copies(be_ref[s], p):
                cp.wait()
            wg_s[...] = wg_f[p].astype(BF16)
            wu_s[...] = wu_f[p].astype(BF16)
            wd_s[...] = wd_f[p].astype(BF16)

        wait_gather(slot)

        @pl.when(s >= 1)
        def _():
            wait_scatter(slot)

        issue_gather(s + 1, 1 - slot, inline=True)
        issue_scatter(jnp.maximum(s - 1, 0), 1 - slot, s == 0, inline=True)
        base = slot * buf_rows
        x_lo, x_hi = _unpack_pairs(_slab_rows_to_matrix(xbuf, base, tm, SLAB_PITCH))
        x = jnp.concatenate([x_lo.astype(BF16), x_hi.astype(BF16)], axis=-1)
        gate = jnp.dot(x, wg_s[...], preferred_element_type=F32)
        up = jnp.dot(x, wu_s[...], preferred_element_type=F32)
        act = (_silu(gate) * up).astype(BF16)
        y = jnp.dot(act, wd_s[...], preferred_element_type=F32)
        half = slab * LANES
        words = _pack_pairs(y[:, :half], y[:, half:])
        for j in range(slab):
            ybuf[pl.ds(base + j, tm, stride=SLAB_PITCH), :] = words[:, j * LANES:(j + 1) * LANES]

    @pl.when(s == n_used)
    def _():
        last = (s - 1) & (per - 1)

        @pl.when((last == 1) | (last == 2))
        def _():
            idx_copy(((s - 1) >> per_log2) + 1).wait()

        wait_gather(slot)
        wait_scatter(slot)
        issue_scatter(s - 1, 1 - slot, False, inline=False)
        wait_scatter(1 - slot)


def _moe(block_e, block_new, block_nexte, block_epar, n_used, row_tgt, h2d, w_gate, w_up, w_down):
    n = h2d.shape[0] // TOKEN_ROWS
    assert n & (n - 1) == 0
    d = 2 * TOKEN_ROWS * LANES
    de = w_gate.shape[2]
    tm = MOE_TM
    n_blk = block_e.shape[0]
    grid_spec = pltpu.PrefetchScalarGridSpec(
        num_scalar_prefetch=5,
        grid=(n_blk,),
        in_specs=[pl.BlockSpec(memory_space=pl.ANY)] * 5,
        out_specs=pl.BlockSpec(memory_space=pl.ANY),
        scratch_shapes=[pltpu.SMEM((2 * IDX_CHUNK,), jnp.int32),
                        pltpu.VMEM((2 * tm * SLAB_PITCH, LANES), U32),
                        pltpu.VMEM((2 * tm * SLAB_PITCH, LANES), U32),
                        pltpu.VMEM((2, d, de), F32),
                        pltpu.VMEM((2, d, de), F32),
                        pltpu.VMEM((2, de, d), F32),
                        pltpu.VMEM((d, de), BF16),
                        pltpu.VMEM((d, de), BF16),
                        pltpu.VMEM((de, d), BF16),
                        pltpu.SemaphoreType.DMA,
                        pltpu.SemaphoreType.DMA((2,)),
                        pltpu.SemaphoreType.DMA((2,)),
                        pltpu.SemaphoreType.DMA((2,))])
    return pl.pallas_call(
        functools.partial(_moe_kernel, n_tok=n),
        grid_spec=grid_spec,
        out_shape=jax.ShapeDtypeStruct(((TOP_K * n + 2 * tm) * TOKEN_ROWS, LANES), U32),
        compiler_params=_cparams(("arbitrary",)),
        name="moe",
    )(block_e, block_new, block_nexte, block_epar, n_used, row_tgt, h2d, w_gate, w_up, w_down)


def _final_kernel(gw_hbm, *refs):
    ys_refs = refs[:TOP_K]
    h_ref, x1_ref, mod_ref, wg_ref, wu_ref, wd_ref, g_ref, b_ref, o_ref, gw_s, acc_ref, sem_i = refs[TOP_K:]
    i = pl.program_id(0)
    tm = x1_ref.shape[0]
    per_step = tm * TOP_K
    cp = pltpu.make_async_copy(gw_hbm.at[pl.ds(pl.multiple_of(i * per_step, per_step), per_step)], gw_s, sem_i)
    cp.start()
    cp.wait()

    hi_base = tm * SLAB_PITCH

    def combine(t, carry):
        rows = pl.ds(pl.multiple_of(t * TOKEN_ROWS, TOKEN_ROWS), TOKEN_ROWS)
        acc_lo = acc_hi = None
        for k in range(TOP_K):
            lo, hi = _unpack_pairs(ys_refs[k][rows, :])
            g = gw_s[t * TOP_K + k]
            acc_lo = g * lo if acc_lo is None else acc_lo + g * lo
            acc_hi = g * hi if acc_hi is None else acc_hi + g * hi
        acc_ref[pl.ds(pl.multiple_of(t * SLAB_PITCH, SUBLANES), TOKEN_ROWS), :] = acc_lo
        acc_ref[pl.ds(pl.multiple_of(hi_base + t * SLAB_PITCH, SUBLANES), TOKEN_ROWS), :] = acc_hi
        return carry

    lax.fori_loop(0, tm, combine, 0, unroll=4)
    moe = jnp.concatenate([_slab_rows_to_matrix(acc_ref, 0, tm, SLAB_PITCH),
                           _slab_rows_to_matrix(acc_ref, hi_base, tm, SLAB_PITCH)], axis=-1)
    h_lo, h_hi = _unpack_pairs(_slab_rows_to_matrix(h_ref, 0, tm, TOKEN_ROWS))
    h = jnp.concatenate([h_lo.astype(BF16), h_hi.astype(BF16)], axis=-1)
    gate = jnp.dot(h, wg_ref[...], preferred_element_type=F32)
    up = jnp.dot(h, wu_ref[...], preferred_element_type=F32)
    shared = jnp.dot((_silu(gate) * up).astype(BF16), wd_ref[...], preferred_element_type=F32)
    m = mod_ref[0]
    y = ALPHA * x1_ref[...] + (1.0 + m[5:6]) * (moe + shared)
    o_ref[...] = _layer_norm(y, g_ref[...], b_ref[...])


def _final(gate_w, ys, h2d, x1, mod3, wsg, wsu, wsd, ln_g, ln_b, seq):
    n, d = x1.shape
    de = wsg.shape[1]
    tm = FINAL_TM
    per_step = tm * TOP_K
    assert per_step % IDX_CHUNK == 0
    row = lambda i: (i, 0)
    const = lambda i: (0, 0)
    slot_rows = lambda k: (lambda i: (k * (n // tm) + i, 0))
    return pl.pallas_call(
        _final_kernel,
        grid=(n // tm,),
        in_specs=[pl.BlockSpec(memory_space=pl.ANY)]
                 + [pl.BlockSpec((tm * TOKEN_ROWS, LANES), slot_rows(k)) for k in range(TOP_K)]
                 + [pl.BlockSpec((tm * TOKEN_ROWS, LANES), row),
                  pl.BlockSpec((tm, d), row),
                  pl.BlockSpec((1, 6, d), lambda i: (i * tm // seq, 0, 0)),
                  pl.BlockSpec((d, de), const),
                  pl.BlockSpec((d, de), const),
                  pl.BlockSpec((de, d), const),
                  pl.BlockSpec((1, d), const),
                  pl.BlockSpec((1, d), const)],
        out_specs=pl.BlockSpec((tm, d), row),
        out_shape=jax.ShapeDtypeStruct((n, d), F32),
        scratch_shapes=[pltpu.SMEM((per_step,), F32),
                        pltpu.VMEM((2 * tm * SLAB_PITCH, LANES), F32),
                        pltpu.SemaphoreType.DMA],
        compiler_params=_cparams(("arbitrary",)),
        name="final",
    )(gate_w, *([ys] * TOP_K), h2d, x1, mod3, wsg, wsu, wsd, ln_g, ln_b)


def _dispatch_tables(eidx, tile_counts, n):
    tm = MOE_TM
    a = n * TOP_K
    i32 = jnp.int32
    experts = jnp.arange(N_EXPERTS, dtype=i32)
    counts = jnp.sum(tile_counts, axis=(0, 1)).astype(i32)
    padded = (counts + tm - 1) // tm * tm
    pad_end = jnp.cumsum(padded)
    starts = pad_end - padded
    n_blk = a // tm + N_EXPERTS + 1
    blk_start = jnp.arange(n_blk, dtype=i32) * tm
    n_used = pad_end[-1] // tm
    in_use = jnp.arange(n_blk) < n_used
    raw_e = jnp.minimum(jnp.sum((pad_end[None, :] <= blk_start[:, None]).astype(i32), axis=1), N_EXPERTS - 1)
    last_e = jnp.sum(jnp.where(jnp.arange(n_blk) == n_used - 1, raw_e, 0))
    block_e = jnp.where(in_use, raw_e, last_e)
    onehot = block_e[:, None] == experts[None, :]
    cnt_b = jnp.sum(jnp.where(onehot, counts[None, :], 0), axis=1)
    start_b = jnp.sum(jnp.where(onehot, starts[None, :], 0), axis=1)
    block_nv = jnp.where(in_use, jnp.clip(cnt_b - (blk_start - start_b), 0, tm), 0).astype(i32)
    block_new = jnp.concatenate([jnp.ones((1,), i32), (block_e[1:] != block_e[:-1]).astype(i32)])
    has_rows = counts > 0
    later = (experts[None, :] > experts[:, None]) & has_rows[None, :]
    next_e = jnp.min(jnp.where(later, experts[None, :], N_EXPERTS), axis=1)
    next_e = jnp.where(next_e == N_EXPERTS, -1, next_e)
    parity_e = (jnp.cumsum(has_rows.astype(i32)) - 1) & 1
    block_nexte = jnp.sum(jnp.where(onehot, next_e[None, :], 0), axis=1).astype(i32)
    block_epar = jnp.sum(jnp.where(onehot, parity_e[None, :], 0), axis=1).astype(i32)
    dummy_keys = jnp.where(jnp.arange(tm, dtype=i32)[None, :] < (padded - counts)[:, None],
                           experts[:, None], N_EXPERTS).reshape(-1)
    keys = jnp.concatenate([eidx.reshape(-1), dummy_keys, jnp.full((tm,), N_EXPERTS, i32)])
    tgt = (jnp.arange(TOP_K, dtype=i32)[None, :] * n + jnp.arange(n, dtype=i32)[:, None]).reshape(-1)
    payload = jnp.concatenate([tgt, jnp.zeros((n_blk * tm - a,), i32)])
    _, row_tgt = lax.sort((keys, payload), num_keys=1, is_stable=True)
    r = jnp.arange(tm, dtype=i32)[None, :]
    dump = a + (jnp.arange(n_blk, dtype=i32)[:, None] & 1) * tm + r
    row_tgt = jnp.where(r >= block_nv[:, None], dump, row_tgt.reshape(n_blk, tm)).reshape(-1)
    row_tgt = jnp.pad(row_tgt, (0, -(n_blk * tm) % IDX_CHUNK))
    return block_e.astype(i32), block_new, block_nexte, block_epar, n_used.astype(i32).reshape(1), row_tgt


def kernel(x, c, w_mod, b_mod, w_in, conv_w, attn_sinks, w_out, ln1_g, ln1_b, w_router, router_bias,
           w_gate, w_up, w_down, ws_gate, ws_up, ws_down, ln2_g, ln2_b):
    b, s, d = x.shape
    n = b * s
    attn_w = N_Q_HEADS * HEAD_DIM
    kv_w = N_KV_HEADS * HEAD_DIM
    conv_wd = d - attn_w
    in_w = attn_w + 2 * kv_w + 3 * conv_wd
    x2 = x.reshape(n, d)
    c8 = jnp.zeros((SUBLANES, d), F32).at[:b].set(c)
    for l in range(DEPTH):
        mod = _mod(c8, w_mod[l], b_mod[l].reshape(1, -1))[:b]
        mod3 = mod.reshape(b, 6, d)
        w_in3 = w_in[l].astype(BF16).reshape(d, in_w // INPROJ_TN, INPROJ_TN).transpose(1, 0, 2)
        proj = _inproj(x2, mod3, w_in3, s)
        mix = _mixer(proj, attn_sinks[l].reshape(1, -1), conv_w[l], b, s, attn_w, kv_w, conv_wd)
        x1, h2d, logits = _outproj(mix, x2, mod3, w_out[l].astype(BF16), ln1_g[l].reshape(1, -1),
                                   ln1_b[l].reshape(1, -1), w_router[l], s)
        eidx, gate_w, tile_counts = _route(logits, router_bias[l].reshape(1, -1))
        block_e, block_new, block_nexte, block_epar, n_used, row_tgt = _dispatch_tables(eidx, tile_counts, n)
        ys = _moe(block_e, block_new, block_nexte, block_epar, n_used, row_tgt, h2d,
                  w_gate[l], w_up[l], w_down[l])
        x2 = _final(gate_w.reshape(-1), ys, h2d, x1, mod3,
                    ws_gate[l].astype(BF16), ws_up[l].astype(BF16), ws_down[l].astype(BF16),
                    ln2_g[l].reshape(1, -1), ln2_b[l].reshape(1, -1), s)
    return x2.reshape(b, s, d)
```

```python
import functools

import jax
import jax.numpy as jnp
from jax import lax
from jax.experimental import pallas as pl
from jax.experimental.pallas import tpu as pltpu

HEAD_DIM = 64
N_Q_HEADS = 16
N_KV_HEADS = 4
GQA = N_Q_HEADS // N_KV_HEADS
CONV_K = 3
WINDOW = 128
Q_BLOCK = 128
N_EXPERTS = 64
TOP_K = 8
N_GROUPS = 8
GROUP_SIZE = N_EXPERTS // N_GROUPS
TOPK_GROUPS = 4
ROUTED_SCALE = 2.5
DEPTH = 1
ALPHA = (2.0 * DEPTH) ** 0.25
LN_EPS = 1e-5

LANES = 128
SUBLANES = 8
TOKEN_ROWS = 8
SLAB_PITCH = 8
VMEM_LIMIT = 56 * 1024 * 1024

MOD_TN = 1024
INPROJ_TM = 512
INPROJ_TN = 1536
OUTPROJ_TM = 256
ROUTE_TM = 512
MOE_TM = 256
FINAL_TM = 256
IDX_CHUNK = 1024
ISSUE_UNROLL = 8

F32 = jnp.float32
BF16 = jnp.bfloat16


def _cparams(sem):
    return pltpu.CompilerParams(dimension_semantics=sem, vmem_limit_bytes=VMEM_LIMIT)


def _silu(v):
    return v * jax.nn.sigmoid(v)


U32 = jnp.uint32
HI_MASK = 0xFFFF0000


def _pack_pairs(lo, hi):
    lo_bits = lax.bitcast_convert_type(lo.astype(BF16).astype(F32), U32) >> 16
    hi_bits = lax.bitcast_convert_type(hi.astype(BF16).astype(F32), U32) & U32(HI_MASK)
    return lo_bits | hi_bits


def _unpack_pairs(w):
    return (lax.bitcast_convert_type(w << 16, F32), lax.bitcast_convert_type(w & U32(HI_MASK), F32))


def _slab_rows_to_matrix(ref, base, tm, pitch):
    return jnp.concatenate([ref[pl.ds(base + s, tm, stride=pitch), :] for s in range(TOKEN_ROWS)], axis=-1)


def _layer_norm(y, g, b):
    mu = jnp.mean(y, axis=-1, keepdims=True)
    yc = y - mu
    var = jnp.mean(yc * yc, axis=-1, keepdims=True)
    return yc * lax.rsqrt(var + LN_EPS) * g + b


def _mod_kernel(c_ref, w_ref, b_ref, o_ref):
    cs = _silu(c_ref[...]).astype(BF16)
    o_ref[...] = jnp.dot(cs, w_ref[...].astype(BF16), preferred_element_type=F32) + b_ref[...]


def _mod(c8, w_mod, b_mod):
    d, n = w_mod.shape
    return pl.pallas_call(
        _mod_kernel,
        grid=(n // MOD_TN,),
        in_specs=[pl.BlockSpec((SUBLANES, d), lambda j: (0, 0)),
                  pl.BlockSpec((d, MOD_TN), lambda j: (0, j)),
                  pl.BlockSpec((1, MOD_TN), lambda j: (0, j))],
        out_specs=pl.BlockSpec((SUBLANES, MOD_TN), lambda j: (0, j)),
        out_shape=jax.ShapeDtypeStruct((SUBLANES, n), F32),
        compiler_params=_cparams(("arbitrary",)),
        name="mod",
    )(c8, w_mod, b_mod)


def _inproj_kernel(x_ref, mod_ref, w_ref, o_ref, h_ref):
    j = pl.program_id(1)

    @pl.when(j == 0)
    def _():
        m = mod_ref[0]
        h_ref[...] = (x_ref[...] * (1.0 + m[1:2]) + m[0:1]).astype(BF16)

    o_ref[...] = jnp.dot(h_ref[...], w_ref[...], preferred_element_type=F32).astype(BF16)


def _inproj(x2, mod3, w_in_bf, seq):
    n, d = x2.shape
    in_w = w_in_bf.shape[1]
    tm, tn = INPROJ_TM, INPROJ_TN
    return pl.pallas_call(
        _inproj_kernel,
        grid=(n // tm, in_w // tn),
        in_specs=[pl.BlockSpec((tm, d), lambda i, j: (i, 0)),
                  pl.BlockSpec((1, 6, d), lambda i, j: (i * tm // seq, 0, 0)),
                  pl.BlockSpec((d, tn), lambda i, j: (0, j))],
        out_specs=pl.BlockSpec((tm, tn), lambda i, j: (i, j)),
        out_shape=jax.ShapeDtypeStruct((n, in_w), BF16),
        scratch_shapes=[pltpu.VMEM((tm, d), BF16)],
        compiler_params=_cparams(("arbitrary", "arbitrary")),
        name="inproj",
    )(x2, mod3, w_in_bf)


def _mixer_kernel(cur_ref, pk_ref, pv_ref, prow_ref, sink_ref, cw_ref, o_ref, *, attn_w, kv_w, conv_w):
    nblk = pl.program_id(1)
    has_prev = nblk > 0
    qb = Q_BLOCK
    cur = cur_ref[...]
    k_cur = cur[:, attn_w:attn_w + kv_w]
    v_cur = cur[:, attn_w + kv_w:attn_w + 2 * kv_w]
    k_all = jnp.concatenate([pk_ref[...], k_cur], axis=0)
    v_all = jnp.concatenate([pv_ref[...], v_cur], axis=0)

    rows = GQA * qb
    qi = lax.broadcasted_iota(jnp.int32, (rows, 2 * qb), 0) % qb
    kj = lax.broadcasted_iota(jnp.int32, (rows, 2 * qb), 1)
    dist = qi + qb - kj
    kmin = jnp.where(has_prev, 0, qb)
    valid = (dist >= 0) & (dist < WINDOW) & (kj >= kmin)
    distf = dist.astype(F32)
    head_in_group = lax.broadcasted_iota(jnp.int32, (rows, 1), 0) // qb
    sinks = sink_ref[...]

    outs = []
    for g in range(N_KV_HEADS):
        q4 = jnp.concatenate(
            [cur[:, (g * GQA + j) * HEAD_DIM:(g * GQA + j + 1) * HEAD_DIM] for j in range(GQA)], axis=0)
        kg = k_all[:, g * HEAD_DIM:(g + 1) * HEAD_DIM]
        vg = v_all[:, g * HEAD_DIM:(g + 1) * HEAD_DIM]
        s = lax.dot_general(q4, kg, (((1,), (1,)), ((), ())), preferred_element_type=F32)
        s = s * (HEAD_DIM ** -0.5)
        slope = jnp.zeros((rows, 1), F32)
        sink = jnp.zeros((rows, 1), F32)
        for j in range(GQA):
            h = g * GQA + j
            sel = head_in_group == j
            slope = jnp.where(sel, 2.0 ** (-8.0 * (h + 1) / N_Q_HEADS), slope)
            sink = jnp.where(sel, sinks[:, h:h + 1], sink)
        s = jnp.where(valid, s - slope * distf, -jnp.inf)
        m = jnp.maximum(jnp.max(s, axis=-1, keepdims=True), sink)
        p = jnp.exp(s - m)
        denom = jnp.sum(p, axis=-1, keepdims=True) + jnp.exp(sink - m)
        o4 = jnp.dot(p.astype(BF16), vg, preferred_element_type=F32) / denom
        outs.extend(o4[j * qb:(j + 1) * qb] for j in range(GQA))
    attn = jnp.concatenate(outs, axis=-1)

    c0 = attn_w + 2 * kv_w
    cb = cur[:, c0:c0 + conv_w].astype(F32)
    u = cur[:, c0 + conv_w:c0 + 2 * conv_w].astype(F32) * cur[:, c0 + 2 * conv_w:c0 + 3 * conv_w].astype(F32)
    prow = prow_ref[...]
    up = prow[:, c0 + conv_w:c0 + 2 * conv_w].astype(F32) * prow[:, c0 + 2 * conv_w:c0 + 3 * conv_w].astype(F32)
    up = up * jnp.where(has_prev, 1.0, 0.0)
    pm1 = up[15:16]
    pm2 = up[14:15]
    ri = lax.broadcasted_iota(jnp.int32, u.shape, 0)
    u1 = jnp.where(ri == 0, pm1, pltpu.roll(u, 1, 0))
    u2 = jnp.where(ri == 0, pm2, jnp.where(ri == 1, pm1, pltpu.roll(u, 2, 0)))
    cw = cw_ref[...]
    conv = cb * (cw[0:1] * u2 + cw[1:2] * u1 + cw[2:3] * u)
    o_ref[...] = jnp.concatenate([attn, conv], axis=-1).astype(BF16)


def _mixer(proj, sinks2, conv_w, batch, seq, attn_w, kv_w, conv_wd):
    n, in_w = proj.shape
    nb = seq // Q_BLOCK
    kv_blk0 = attn_w // kv_w
    sub16 = Q_BLOCK // 16

    def cur_map(b, i):
        return (b * nb + i, 0)

    def prev_map(col):
        return lambda b, i: (b * nb + jnp.maximum(i - 1, 0), col)

    def prow_map(b, i):
        return (jnp.maximum((b * nb + i) * sub16 - 1, 0), 0)

    kern = functools.partial(_mixer_kernel, attn_w=attn_w, kv_w=kv_w, conv_w=conv_wd)
    return pl.pallas_call(
        kern,
        grid=(batch, nb),
        in_specs=[pl.BlockSpec((Q_BLOCK, in_w), cur_map),
                  pl.BlockSpec((Q_BLOCK, kv_w), prev_map(kv_blk0)),
                  pl.BlockSpec((Q_BLOCK, kv_w), prev_map(kv_blk0 + 1)),
                  pl.BlockSpec((16, in_w), prow_map),
                  pl.BlockSpec((1, N_Q_HEADS), lambda b, i: (0, 0)),
                  pl.BlockSpec((CONV_K, conv_wd), lambda b, i: (0, 0))],
        out_specs=pl.BlockSpec((Q_BLOCK, attn_w + conv_wd), cur_map),
        out_shape=jax.ShapeDtypeStruct((n, attn_w + conv_wd), BF16),
        compiler_params=_cparams(("arbitrary", "arbitrary")),
        name="mixer",
    )(proj, proj, proj, proj, sinks2, conv_w)


def _split_bf16(v):
    hi = v.astype(BF16)
    lo = (v - hi.astype(F32)).astype(BF16)
    return hi, lo


def _outproj_kernel(mix_ref, x_ref, mod_ref, w_ref, g_ref, b_ref, wr_ref, x1_ref, h2_ref, lg_ref):
    m = mod_ref[0]
    mix = jnp.dot(mix_ref[...], w_ref[...], preferred_element_type=F32)
    x1 = _layer_norm(ALPHA * x_ref[...] + (1.0 + m[2:3]) * mix, g_ref[...], b_ref[...])
    x1_ref[...] = x1
    h2 = x1 * (1.0 + m[4:5]) + m[3:4]
    tm, d = h2.shape
    words = _pack_pairs(h2[:, :d // 2], h2[:, d // 2:])
    for s in range(TOKEN_ROWS):
        h2_ref[pl.ds(s, tm, stride=TOKEN_ROWS), :] = words[:, s * LANES:(s + 1) * LANES]
    h_hi, h_lo = _split_bf16(h2)
    w_hi, w_lo = _split_bf16(wr_ref[...])
    lg_ref[...] = (jnp.dot(h_hi, w_hi, preferred_element_type=F32)
                   + (jnp.dot(h_hi, w_lo, preferred_element_type=F32)
                      + jnp.dot(h_lo, w_hi, preferred_element_type=F32)))


def _outproj(mix, x2, mod3, w_out_bf, ln_g, ln_b, w_router, seq):
    n, d = x2.shape
    tm = OUTPROJ_TM
    ne = w_router.shape[1]
    row = lambda i: (i, 0)
    const = lambda i: (0, 0)
    return pl.pallas_call(
        _outproj_kernel,
        grid=(n // tm,),
        in_specs=[pl.BlockSpec((tm, d), row),
                  pl.BlockSpec((tm, d), row),
                  pl.BlockSpec((1, 6, d), lambda i: (i * tm // seq, 0, 0)),
                  pl.BlockSpec((d, d), const),
                  pl.BlockSpec((1, d), const),
                  pl.BlockSpec((1, d), const),
                  pl.BlockSpec((d, ne), const)],
        out_specs=[pl.BlockSpec((tm, d), row),
                   pl.BlockSpec((tm * TOKEN_ROWS, LANES), row),
                   pl.BlockSpec((tm, ne), row)],
        out_shape=[jax.ShapeDtypeStruct((n, d), F32),
                   jax.ShapeDtypeStruct((n * TOKEN_ROWS, LANES), U32),
                   jax.ShapeDtypeStruct((n, ne), F32)],
        compiler_params=_cparams(("arbitrary",)),
        name="outproj",
    )(mix, x2, mod3, w_out_bf, ln_g, ln_b, w_router)


def _first_argmax(v, lane_f):
    m = jnp.max(v, axis=-1, keepdims=True)
    idx = jnp.min(jnp.where(v == m, lane_f, float(N_EXPERTS)), axis=-1, keepdims=True)
    return m, idx


def _route_kernel(lg_ref, bias_ref, eidx_ref, w_ref, rank_ref, cnt_ref, carry_ref):
    scores = jax.nn.sigmoid(lg_ref[...])
    sel = scores + bias_ref[...]
    tm = sel.shape[0]
    lane = lax.broadcasted_iota(jnp.int32, (tm, N_EXPERTS), 1)
    lane_f = lane.astype(F32)
    grp = lane // GROUP_SIZE
    neg = -jnp.inf
    gs = []
    for g in range(N_GROUPS):
        vg = jnp.where(grp == g, sel, neg)
        m1, i1 = _first_argmax(vg, lane_f)
        m2 = jnp.max(jnp.where(lane_f == i1, neg, vg), axis=-1, keepdims=True)
        gs.append(m1 + m2)
    keep = jnp.zeros((tm, N_EXPERTS), F32)
    for g in range(N_GROUPS):
        rank = jnp.zeros((tm, 1), F32)
        for o in range(N_GROUPS):
            if o == g:
                continue
            ahead = (gs[o] >= gs[g]) if o < g else (gs[o] > gs[g])
            rank = rank + jnp.where(ahead, 1.0, 0.0)
        keep = jnp.where(grp == g, jnp.where(rank < TOPK_GROUPS, 1.0, 0.0), keep)
    cand = jnp.where(keep > 0.5, sel, neg)
    idxs, ws = [], []
    chosen = jnp.zeros((tm, N_EXPERTS), F32)
    for _ in range(TOP_K):
        _, ik = _first_argmax(cand, lane_f)
        hit = lane_f == ik
        ws.append(jnp.sum(jnp.where(hit, scores, 0.0), axis=-1, keepdims=True))
        idxs.append(ik)
        cand = jnp.where(hit, neg, cand)
        chosen = jnp.where(hit, 1.0, chosen)

    @pl.when(pl.program_id(0) == 0)
    def _():
        carry_ref[...] = jnp.zeros(carry_ref.shape, F32)

    ti = lax.broadcasted_iota(jnp.int32, (tm, tm), 0)
    tj = lax.broadcasted_iota(jnp.int32, (tm, tm), 1)
    earlier = jnp.where(tj < ti, 1.0, 0.0).astype(BF16)
    before = jnp.dot(earlier, chosen.astype(BF16), preferred_element_type=F32) + carry_ref[...]
    tile_cnt = jnp.sum(chosen, axis=0, keepdims=True)
    carry_ref[...] = carry_ref[...] + tile_cnt
    cnt_ref[0] = tile_cnt

    wsum = ws[0]
    for k in range(1, TOP_K):
        wsum = wsum + ws[k]
    col = lax.broadcasted_iota(jnp.int32, (tm, TOP_K), 1)
    eidx = jnp.zeros((tm, TOP_K), F32)
    wout = jnp.zeros((tm, TOP_K), F32)
    rank = jnp.zeros((tm, TOP_K), F32)
    for k in range(TOP_K):
        eidx = jnp.where(col == k, idxs[k], eidx)
        wout = jnp.where(col == k, ws[k] / wsum * ROUTED_SCALE, wout)
        rk = jnp.sum(jnp.where(lane_f == idxs[k], before, 0.0), axis=-1, keepdims=True)
        rank = jnp.where(col == k, rk, rank)
    eidx_ref[...] = eidx.astype(jnp.int32)
    w_ref[...] = wout
    rank_ref[...] = rank.astype(jnp.int32)


def _route(logits, bias2):
    n, ne = logits.shape
    tm = ROUTE_TM
    row = lambda i: (i, 0)
    return pl.pallas_call(
        _route_kernel,
        grid=(n // tm,),
        in_specs=[pl.BlockSpec((tm, ne), row), pl.BlockSpec((1, ne), lambda i: (0, 0))],
        out_specs=[pl.BlockSpec((tm, TOP_K), row), pl.BlockSpec((tm, TOP_K), row), pl.BlockSpec((tm, TOP_K), row),
                   pl.BlockSpec((1, 1, ne), lambda i: (i, 0, 0))],
        out_shape=[jax.ShapeDtypeStruct((n, TOP_K), jnp.int32), jax.ShapeDtypeStruct((n, TOP_K), F32),
                   jax.ShapeDtypeStruct((n, TOP_K), jnp.int32), jax.ShapeDtypeStruct((n // tm, 1, ne), F32)],
        scratch_shapes=[pltpu.VMEM((1, ne), F32)],
        compiler_params=_cparams(("arbitrary",)),
        name="route",
    )(logits, bias2)


def _issue_rows(lo, hi, issue_one):
    n_full = (hi - lo) // ISSUE_UNROLL

    def chunk(c, carry):
        for u in range(ISSUE_UNROLL):
            issue_one(lo + c * ISSUE_UNROLL + u)
        return carry

    def tail(r, carry):
        issue_one(r)
        return carry

    lax.fori_loop(0, n_full, chunk, 0)
    lax.fori_loop(lo + n_full * ISSUE_UNROLL, hi, tail, 0)


def _hbm_slab(ref, row):
    return ref.at[pl.ds(pl.multiple_of(row * TOKEN_ROWS, TOKEN_ROWS), TOKEN_ROWS), :]


def _moe_kernel(be_ref, new_ref, nexte_ref, epar_ref, nused_ref, tgt_hbm, h_hbm, wg_hbm, wu_hbm, wd_hbm, ys_hbm,
                idx_s, xbuf, ybuf, wg_f, wu_f, wd_f, wg_s, wu_s, wd_s, sem_i, sem_g, sem_s, sem_w, *, n_tok):
    s = pl.program_id(0)
    tm = MOE_TM
    per = IDX_CHUNK // tm
    per_log2 = per.bit_length() - 1
    n_used = nused_ref[0]
    slab = TOKEN_ROWS
    buf_rows = tm * SLAB_PITCH
    dump_row0 = TOP_K * n_tok
    slot = s & 1

    def staged(buf, base, r):
        return buf.at[pl.ds(pl.multiple_of(base + r * SLAB_PITCH, SUBLANES), slab), :]

    def idx_copy(c):
        return pltpu.make_async_copy(
            tgt_hbm.at[pl.ds(pl.multiple_of(c * IDX_CHUNK, IDX_CHUNK), IDX_CHUNK)],
            idx_s.at[pl.ds(pl.multiple_of((c & 1) * IDX_CHUNK, IDX_CHUNK), IDX_CHUNK)], sem_i)

    def idx_base(b):
        return ((b >> per_log2) & 1) * IDX_CHUNK + (b & (per - 1)) * tm

    def weight_copies(e, p):
        return (pltpu.make_async_copy(wg_hbm.at[e], wg_f.at[p], sem_w.at[p]),
                pltpu.make_async_copy(wu_hbm.at[e], wu_f.at[p], sem_w.at[p]),
                pltpu.make_async_copy(wd_hbm.at[e], wd_f.at[p], sem_w.at[p]))

    def for_rows(inline, body):
        if inline:
            for r in range(tm):
                body(r, r % 2)
        else:
            def pair(c, carry):
                body(2 * c, 0)
                body(2 * c + 1, 1)
                return carry
            lax.fori_loop(0, tm // 2, pair, 0)

    def issue_gather(b, to_slot, inline):
        ibase = idx_base(b)
        xbase = to_slot * buf_rows

        def one(r, prio):
            tok = idx_s[ibase + r] & (n_tok - 1)
            pltpu.make_async_copy(_hbm_slab(h_hbm, tok), staged(xbuf, xbase, r),
                                  sem_g.at[to_slot]).start(priority=prio)
        for_rows(inline, one)

    def issue_scatter(b, from_slot, to_dump, inline):
        ibase = idx_base(b)
        ybase = from_slot * buf_rows

        def one(r, prio):
            tgt = jnp.where(to_dump, dump_row0 + tm + r, idx_s[ibase + r])
            pltpu.make_async_copy(staged(ybuf, ybase, r), _hbm_slab(ys_hbm, tgt),
                                  sem_s.at[from_slot]).start(priority=prio)
        for_rows(inline, one)

    def wait_gather(at_slot):
        v = xbuf.at[pl.ds(pl.multiple_of(at_slot * buf_rows, SUBLANES), buf_rows), :]
        pltpu.make_async_copy(h_hbm.at[pl.ds(0, buf_rows), :], v, sem_g.at[at_slot]).wait()

    def wait_scatter(at_slot):
        v = ybuf.at[pl.ds(pl.multiple_of(at_slot * buf_rows, SUBLANES), buf_rows), :]
        pltpu.make_async_copy(v, ys_hbm.at[pl.ds(0, buf_rows), :], sem_s.at[at_slot]).wait()

    @pl.when(s == 0)
    def _():
        ybuf[...] = jnp.zeros(ybuf.shape, U32)
        first = idx_copy(0)
        first.start()
        first.wait()
        init = pltpu.make_async_copy(ybuf, ys_hbm.at[pl.ds(pl.multiple_of(dump_row0 * slab, slab), 2 * buf_rows), :],
                                     sem_i)
        init.start()
        init.wait()
        issue_gather(0, 0, inline=False)
        for cp in weight_copies(be_ref[0], 0):
            cp.start()

    @pl.when(s < n_used)
    def _():
        @pl.when((s & (per - 1)) == 1)
        def _():
            idx_copy((s >> per_log2) + 1).start()

        @pl.when(((s + 1) & (per - 1)) == 0)
        def _():
            idx_copy((s + 1) >> per_log2).wait()

        @pl.when(new_ref[s] == 1)
        def _():
            p = epar_ref[s]

            @pl.when(nexte_ref[s] >= 0)
            def _():
                for cp in weight_copies(nexte_ref[s], 1 - p):
                    cp.start()

            for cp in weight_copies(be_ref[s], p):
                cp.wait()
            wg_s[...] = wg_f[p].astype(BF16)
            wu_s[...] = wu_f[p].astype(BF16)
            wd_s[...] = wd_f[p].astype(BF16)

        wait_gather(slot)

        @pl.when(s >= 1)
        def _():
            wait_scatter(slot)

        issue_gather(s + 1, 1 - slot, inline=True)
        issue_scatter(jnp.maximum(s - 1, 0), 1 - slot, s == 0, inline=True)
        base = slot * buf_rows
        x_lo, x_hi = _unpack_pairs(_slab_rows_to_matrix(xbuf, base, tm, SLAB_PITCH))
        x = jnp.concatenate([x_lo.astype(BF16), x_hi.astype(BF16)], axis=-1)
        gate = jnp.dot(x, wg_s[...], preferred_element_type=F32)
        up = jnp.dot(x, wu_s[...], preferred_element_type=F32)
        act = (_silu(gate) * up).astype(BF16)
        y = jnp.dot(act, wd_s[...], preferred_element_type=F32)
        half = slab * LANES
        words = _pack_pairs(y[:, :half], y[:, half:])
        for j in range(slab):
            ybuf[pl.ds(base + j, tm, stride=SLAB_PITCH), :] = words[:, j * LANES:(j + 1) * LANES]

    @pl.when(s == n_used)
    def _():
        last = (s - 1) & (per - 1)

        @pl.when((last == 1) | (last == 2))
        def _():
            idx_copy(((s - 1) >> per_log2) + 1).wait()

        wait_gather(slot)
        wait_scatter(slot)
        issue_scatter(s - 1, 1 - slot, False, inline=False)
        wait_scatter(1 - slot)


def _moe(block_e, block_new, block_nexte, block_epar, n_used, row_tgt, h2d, w_gate, w_up, w_down):
    n = h2d.shape[0] // TOKEN_ROWS
    assert n & (n - 1) == 0
    d = 2 * TOKEN_ROWS * LANES
    de = w_gate.shape[2]
    tm = MOE_TM
    n_blk = block_e.shape[0]
    grid_spec = pltpu.PrefetchScalarGridSpec(
        num_scalar_prefetch=5,
        grid=(n_blk,),
        in_specs=[pl.BlockSpec(memory_space=pl.ANY)] * 5,
        out_specs=pl.BlockSpec(memory_space=pl.ANY),
        scratch_shapes=[pltpu.SMEM((2 * IDX_CHUNK,), jnp.int32),
                        pltpu.VMEM((2 * tm * SLAB_PITCH, LANES), U32),
                        pltpu.VMEM((2 * tm * SLAB_PITCH, LANES), U32),
                        pltpu.VMEM((2, d, de), F32),
                        pltpu.VMEM((2, d, de), F32),
                        pltpu.VMEM((2, de, d), F32),
                        pltpu.VMEM((d, de), BF16),
                        pltpu.VMEM((d, de), BF16),
                        pltpu.VMEM((de, d), BF16),
                        pltpu.SemaphoreType.DMA,
                        pltpu.SemaphoreType.DMA((2,)),
                        pltpu.SemaphoreType.DMA((2,)),
                        pltpu.SemaphoreType.DMA((2,))])
    return pl.pallas_call(
        functools.partial(_moe_kernel, n_tok=n),
        grid_spec=grid_spec,
        out_shape=jax.ShapeDtypeStruct(((TOP_K * n + 2 * tm) * TOKEN_ROWS, LANES), U32),
        compiler_params=_cparams(("arbitrary",)),
        name="moe",
    )(block_e, block_new, block_nexte, block_epar, n_used, row_tgt, h2d, w_gate, w_up, w_down)


def _final_kernel(gw_hbm, *refs):
    ys_refs = refs[:TOP_K]
    h_ref, x1_ref, mod_ref, wg_ref, wu_ref, wd_ref, g_ref, b_ref, o_ref, gw_s, acc_ref, sem_i = refs[TOP_K:]
    i = pl.program_id(0)
    tm = x1_ref.shape[0]
    per_step = tm * TOP_K
    cp = pltpu.make_async_copy(gw_hbm.at[pl.ds(pl.multiple_of(i * per_step, per_step), per_step)], gw_s, sem_i)
    cp.start()
    cp.wait()

    hi_base = tm * SLAB_PITCH

    def combine(t, carry):
        rows = pl.ds(pl.multiple_of(t * TOKEN_ROWS, TOKEN_ROWS), TOKEN_ROWS)
        acc_lo = acc_hi = None
        for k in range(TOP_K):
            lo, hi = _unpack_pairs(ys_refs[k][rows, :])
            g = gw_s[t * TOP_K + k]
            acc_lo = g * lo if acc_lo is None else acc_lo + g * lo
            acc_hi = g * hi if acc_hi is None else acc_hi + g * hi
        acc_ref[pl.ds(pl.multiple_of(t * SLAB_PITCH, SUBLANES), TOKEN_ROWS), :] = acc_lo
        acc_ref[pl.ds(pl.multiple_of(hi_base + t * SLAB_PITCH, SUBLANES), TOKEN_ROWS), :] = acc_hi
        return carry

    lax.fori_loop(0, tm, combine, 0, unroll=4)
    moe = jnp.concatenate([_slab_rows_to_matrix(acc_ref, 0, tm, SLAB_PITCH),
                           _slab_rows_to_matrix(acc_ref, hi_base, tm, SLAB_PITCH)], axis=-1)
    h_lo, h_hi = _unpack_pairs(_slab_rows_to_matrix(h_ref, 0, tm, TOKEN_ROWS))
    h = jnp.concatenate([h_lo.astype(BF16), h_hi.astype(BF16)], axis=-1)
    gate = jnp.dot(h, wg_ref[...], preferred_element_type=F32)
    up = jnp.dot(h, wu_ref[...], preferred_element_type=F32)
    shared = jnp.dot((_silu(gate) * up).astype(BF16), wd_ref[...], preferred_element_type=F32)
    m = mod_ref[0]
    y = ALPHA * x1_ref[...] + (1.0 + m[5:6]) * (moe + shared)
    o_ref[...] = _layer_norm(y, g_ref[...], b_ref[...])


def _final(gate_w, ys, h2d, x1, mod3, wsg, wsu, wsd, ln_g, ln_b, seq):
    n, d = x1.shape
    de = wsg.shape[1]
    tm = FINAL_TM
    per_step = tm * TOP_K
    assert per_step % IDX_CHUNK == 0
    row = lambda i: (i, 0)
    const = lambda i: (0, 0)
    slot_rows = lambda k: (lambda i: (k * (n // tm) + i, 0))
    return pl.pallas_call(
        _final_kernel,
        grid=(n // tm,),
        in_specs=[pl.BlockSpec(memory_space=pl.ANY)]
                 + [pl.BlockSpec((tm * TOKEN_ROWS, LANES), slot_rows(k)) for k in range(TOP_K)]
                 + [pl.BlockSpec((tm * TOKEN_ROWS, LANES), row),
                  pl.BlockSpec((tm, d), row),
                  pl.BlockSpec((1, 6, d), lambda i: (i * tm // seq, 0, 0)),
                  pl.BlockSpec((d, de), const),
                  pl.BlockSpec((d, de), const),
                  pl.BlockSpec((de, d), const),
                  pl.BlockSpec((1, d), const),
                  pl.BlockSpec((1, d), const)],
        out_specs=pl.BlockSpec((tm, d), row),
        out_shape=jax.ShapeDtypeStruct((n, d), F32),
        scratch_shapes=[pltpu.SMEM((per_step,), F32),
                        pltpu.VMEM((2 * tm * SLAB_PITCH, LANES), F32),
                        pltpu.SemaphoreType.DMA],
        compiler_params=_cparams(("arbitrary",)),
        name="final",
    )(gate_w, *([ys] * TOP_K), h2d, x1, mod3, wsg, wsu, wsd, ln_g, ln_b)


def _dispatch_tables(eidx, rank, tile_counts, n):
    tm = MOE_TM
    a = n * TOP_K
    i32 = jnp.int32
    experts = jnp.arange(N_EXPERTS, dtype=i32)
    counts = jnp.sum(tile_counts, axis=(0, 1)).astype(i32)
    padded = (counts + tm - 1) // tm * tm
    pad_end = jnp.cumsum(padded)
    starts = pad_end - padded
    n_blk = a // tm + N_EXPERTS + 1
    blk_start = jnp.arange(n_blk, dtype=i32) * tm
    n_used = pad_end[-1] // tm
    in_use = jnp.arange(n_blk) < n_used
    raw_e = jnp.minimum(jnp.sum((pad_end[None, :] <= blk_start[:, None]).astype(i32), axis=1), N_EXPERTS - 1)
    last_e = jnp.sum(jnp.where(jnp.arange(n_blk) == n_used - 1, raw_e, 0))
    block_e = jnp.where(in_use, raw_e, last_e)
    onehot = block_e[:, None] == experts[None, :]
    block_new = jnp.concatenate([jnp.ones((1,), i32), (block_e[1:] != block_e[:-1]).astype(i32)])
    has_rows = counts > 0
    later = (experts[None, :] > experts[:, None]) & has_rows[None, :]
    next_e = jnp.min(jnp.where(later, experts[None, :], N_EXPERTS), axis=1)
    next_e = jnp.where(next_e == N_EXPERTS, -1, next_e)
    parity_e = (jnp.cumsum(has_rows.astype(i32)) - 1) & 1
    block_nexte = jnp.sum(jnp.where(onehot, next_e[None, :], 0), axis=1).astype(i32)
    block_epar = jnp.sum(jnp.where(onehot, parity_e[None, :], 0), axis=1).astype(i32)
    pos = jnp.sum(jnp.where(eidx[:, :, None] == experts[None, None, :], starts[None, None, :], 0), axis=-1) + rank
    tgt = jnp.arange(TOP_K, dtype=i32)[None, :] * n + jnp.arange(n, dtype=i32)[:, None]
    r = jnp.arange(tm, dtype=i32)[None, :]
    dump = (a + (jnp.arange(n_blk, dtype=i32)[:, None] & 1) * tm + r).reshape(-1)
    row_tgt = dump.at[pos.reshape(-1)].set(tgt.reshape(-1), unique_indices=True)
    row_tgt = jnp.pad(row_tgt, (0, -(n_blk * tm) % IDX_CHUNK))
    return block_e.astype(i32), block_new, block_nexte, block_epar, n_used.astype(i32).reshape(1), row_tgt


def kernel(x, c, w_mod, b_mod, w_in, conv_w, attn_sinks, w_out, ln1_g, ln1_b, w_router, router_bias,
           w_gate, w_up, w_down, ws_gate, ws_up, ws_down, ln2_g, ln2_b):
    b, s, d = x.shape
    n = b * s
    attn_w = N_Q_HEADS * HEAD_DIM
    kv_w = N_KV_HEADS * HEAD_DIM
    conv_wd = d - attn_w
    in_w = attn_w + 2 * kv_w + 3 * conv_wd
    x2 = x.reshape(n, d)
    c8 = jnp.zeros((SUBLANES, d), F32).at[:b].set(c)
    for l in range(DEPTH):
        mod = _mod(c8, w_mod[l], b_mod[l].reshape(1, -1))[:b]
        mod3 = mod.reshape(b, 6, d)
        proj = _inproj(x2, mod3, w_in[l].astype(BF16), s)
        mix = _mixer(proj, attn_sinks[l].reshape(1, -1), conv_w[l], b, s, attn_w, kv_w, conv_wd)
        x1, h2d, logits = _outproj(mix, x2, mod3, w_out[l].astype(BF16), ln1_g[l].reshape(1, -1),
                                   ln1_b[l].reshape(1, -1), w_router[l], s)
        eidx, gate_w, rank, tile_counts = _route(logits, router_bias[l].reshape(1, -1))
        block_e, block_new, block_nexte, block_epar, n_used, row_tgt = _dispatch_tables(eidx, rank, tile_counts, n)
        ys = _moe(block_e, block_new, block_nexte, block_epar, n_used, row_tgt, h2d,
                  w_gate[l], w_up[l], w_down[l])
        x2 = _final(gate_w.reshape(-1), ys, h2d, x1, mod3,
                    ws_gate[l].astype(BF16), ws_up[l].astype(BF16), ws_down[l].astype(BF16),
                    ln2_g[l].reshape(1, -1), ln2_b[l].reshape(1, -1), s)
    return x2.reshape(b, s, d)
```

```python
import functools

import jax
import jax.numpy as jnp
from jax import lax
from jax.experimental import pallas as pl
from jax.experimental.pallas import tpu as pltpu

HEAD_DIM = 64
N_Q_HEADS = 16
N_KV_HEADS = 4
GQA = N_Q_HEADS // N_KV_HEADS
CONV_K = 3
WINDOW = 128
Q_BLOCK = 128
N_EXPERTS = 64
TOP_K = 8
N_GROUPS = 8
GROUP_SIZE = N_EXPERTS // N_GROUPS
TOPK_GROUPS = 4
ROUTED_SCALE = 2.5
DEPTH = 1
ALPHA = (2.0 * DEPTH) ** 0.25
LN_EPS = 1e-5

LANES = 128
SUBLANES = 8
TOKEN_ROWS = 8
SLAB_PITCH = 8
VMEM_LIMIT = 56 * 1024 * 1024

MOD_TN = 1024
INPROJ_TM = 512
INPROJ_TN = 1536
OUTPROJ_TM = 512
ROUTE_TM = 512
MOE_TM = 256
FINAL_TM = 256
IDX_CHUNK = 1024
ISSUE_UNROLL = 8

F32 = jnp.float32
BF16 = jnp.bfloat16


def _cparams(sem):
    return pltpu.CompilerParams(dimension_semantics=sem, vmem_limit_bytes=VMEM_LIMIT)


def _silu(v):
    return v * jax.nn.sigmoid(v)


U32 = jnp.uint32
HI_MASK = 0xFFFF0000


def _pack_pairs(lo, hi):
    lo_bits = lax.bitcast_convert_type(lo.astype(BF16).astype(F32), U32) >> 16
    hi_bits = lax.bitcast_convert_type(hi.astype(BF16).astype(F32), U32) & U32(HI_MASK)
    return lo_bits | hi_bits


def _unpack_pairs(w):
    return (lax.bitcast_convert_type(w << 16, F32), lax.bitcast_convert_type(w & U32(HI_MASK), F32))


def _slab_rows_to_matrix(ref, base, tm, pitch):
    return jnp.concatenate([ref[pl.ds(base + s, tm, stride=pitch), :] for s in range(TOKEN_ROWS)], axis=-1)


def _layer_norm(y, g, b):
    mu = jnp.mean(y, axis=-1, keepdims=True)
    yc = y - mu
    var = jnp.mean(yc * yc, axis=-1, keepdims=True)
    return yc * lax.rsqrt(var + LN_EPS) * g + b


def _mod_kernel(c_ref, w_ref, b_ref, o_ref):
    cs = _silu(c_ref[...]).astype(BF16)
    o_ref[...] = jnp.dot(cs, w_ref[...].astype(BF16), preferred_element_type=F32) + b_ref[...]


def _mod(c8, w_mod, b_mod):
    d, n = w_mod.shape
    return pl.pallas_call(
        _mod_kernel,
        grid=(n // MOD_TN,),
        in_specs=[pl.BlockSpec((SUBLANES, d), lambda j: (0, 0)),
                  pl.BlockSpec((d, MOD_TN), lambda j: (0, j)),
                  pl.BlockSpec((1, MOD_TN), lambda j: (0, j))],
        out_specs=pl.BlockSpec((SUBLANES, MOD_TN), lambda j: (0, j)),
        out_shape=jax.ShapeDtypeStruct((SUBLANES, n), F32),
        compiler_params=_cparams(("arbitrary",)),
        name="mod",
    )(c8, w_mod, b_mod)


def _inproj_kernel(x_ref, mod_ref, w_ref, o_ref, h_ref):
    j = pl.program_id(1)

    @pl.when(j == 0)
    def _():
        m = mod_ref[0]
        h_ref[...] = (x_ref[...] * (1.0 + m[1:2]) + m[0:1]).astype(BF16)

    o_ref[...] = jnp.dot(h_ref[...], w_ref[...], preferred_element_type=F32).astype(BF16)


def _inproj(x2, mod3, w_in_bf, seq):
    n, d = x2.shape
    in_w = w_in_bf.shape[1]
    tm, tn = INPROJ_TM, INPROJ_TN
    return pl.pallas_call(
        _inproj_kernel,
        grid=(n // tm, in_w // tn),
        in_specs=[pl.BlockSpec((tm, d), lambda i, j: (i, 0)),
                  pl.BlockSpec((1, 6, d), lambda i, j: (i * tm // seq, 0, 0)),
                  pl.BlockSpec((d, tn), lambda i, j: (0, j))],
        out_specs=pl.BlockSpec((tm, tn), lambda i, j: (i, j)),
        out_shape=jax.ShapeDtypeStruct((n, in_w), BF16),
        scratch_shapes=[pltpu.VMEM((tm, d), BF16)],
        compiler_params=_cparams(("arbitrary", "arbitrary")),
        name="inproj",
    )(x2, mod3, w_in_bf)


def _mixer_kernel(cur_ref, pk_ref, pv_ref, prow_ref, sink_ref, cw_ref, o_ref, *, attn_w, kv_w, conv_w):
    nblk = pl.program_id(1)
    has_prev = nblk > 0
    qb = Q_BLOCK
    cur = cur_ref[...]
    k_cur = cur[:, attn_w:attn_w + kv_w]
    v_cur = cur[:, attn_w + kv_w:attn_w + 2 * kv_w]
    k_all = jnp.concatenate([pk_ref[...], k_cur], axis=0)
    v_all = jnp.concatenate([pv_ref[...], v_cur], axis=0)

    rows = GQA * qb
    qi = lax.broadcasted_iota(jnp.int32, (rows, 2 * qb), 0) % qb
    kj = lax.broadcasted_iota(jnp.int32, (rows, 2 * qb), 1)
    dist = qi + qb - kj
    kmin = jnp.where(has_prev, 0, qb)
    valid = (dist >= 0) & (dist < WINDOW) & (kj >= kmin)
    distf = dist.astype(F32)
    head_in_group = lax.broadcasted_iota(jnp.int32, (rows, 1), 0) // qb
    sinks = sink_ref[...]

    outs = []
    for g in range(N_KV_HEADS):
        q4 = jnp.concatenate(
            [cur[:, (g * GQA + j) * HEAD_DIM:(g * GQA + j + 1) * HEAD_DIM] for j in range(GQA)], axis=0)
        kg = k_all[:, g * HEAD_DIM:(g + 1) * HEAD_DIM]
        vg = v_all[:, g * HEAD_DIM:(g + 1) * HEAD_DIM]
        s = lax.dot_general(q4, kg, (((1,), (1,)), ((), ())), preferred_element_type=F32)
        s = s * (HEAD_DIM ** -0.5)
        slope = jnp.zeros((rows, 1), F32)
        sink = jnp.zeros((rows, 1), F32)
        for j in range(GQA):
            h = g * GQA + j
            sel = head_in_group == j
            slope = jnp.where(sel, 2.0 ** (-8.0 * (h + 1) / N_Q_HEADS), slope)
            sink = jnp.where(sel, sinks[:, h:h + 1], sink)
        s = jnp.where(valid, s - slope * distf, -jnp.inf)
        m = jnp.maximum(jnp.max(s, axis=-1, keepdims=True), sink)
        p = jnp.exp(s - m)
        denom = jnp.sum(p, axis=-1, keepdims=True) + jnp.exp(sink - m)
        o4 = jnp.dot(p.astype(BF16), vg, preferred_element_type=F32) / denom
        outs.extend(o4[j * qb:(j + 1) * qb] for j in range(GQA))
    attn = jnp.concatenate(outs, axis=-1)

    c0 = attn_w + 2 * kv_w
    cb = cur[:, c0:c0 + conv_w].astype(F32)
    u = cur[:, c0 + conv_w:c0 + 2 * conv_w].astype(F32) * cur[:, c0 + 2 * conv_w:c0 + 3 * conv_w].astype(F32)
    prow = prow_ref[...]
    up = prow[:, c0 + conv_w:c0 + 2 * conv_w].astype(F32) * prow[:, c0 + 2 * conv_w:c0 + 3 * conv_w].astype(F32)
    up = up * jnp.where(has_prev, 1.0, 0.0)
    pm1 = up[15:16]
    pm2 = up[14:15]
    ri = lax.broadcasted_iota(jnp.int32, u.shape, 0)
    u1 = jnp.where(ri == 0, pm1, pltpu.roll(u, 1, 0))
    u2 = jnp.where(ri == 0, pm2, jnp.where(ri == 1, pm1, pltpu.roll(u, 2, 0)))
    cw = cw_ref[...]
    conv = cb * (cw[0:1] * u2 + cw[1:2] * u1 + cw[2:3] * u)
    o_ref[...] = jnp.concatenate([attn, conv], axis=-1).astype(BF16)


def _mixer(proj, sinks2, conv_w, batch, seq, attn_w, kv_w, conv_wd):
    n, in_w = proj.shape
    nb = seq // Q_BLOCK
    kv_blk0 = attn_w // kv_w
    sub16 = Q_BLOCK // 16

    def cur_map(b, i):
        return (b * nb + i, 0)

    def prev_map(col):
        return lambda b, i: (b * nb + jnp.maximum(i - 1, 0), col)

    def prow_map(b, i):
        return (jnp.maximum((b * nb + i) * sub16 - 1, 0), 0)

    kern = functools.partial(_mixer_kernel, attn_w=attn_w, kv_w=kv_w, conv_w=conv_wd)
    return pl.pallas_call(
        kern,
        grid=(batch, nb),
        in_specs=[pl.BlockSpec((Q_BLOCK, in_w), cur_map),
                  pl.BlockSpec((Q_BLOCK, kv_w), prev_map(kv_blk0)),
                  pl.BlockSpec((Q_BLOCK, kv_w), prev_map(kv_blk0 + 1)),
                  pl.BlockSpec((16, in_w), prow_map),
                  pl.BlockSpec((1, N_Q_HEADS), lambda b, i: (0, 0)),
                  pl.BlockSpec((CONV_K, conv_wd), lambda b, i: (0, 0))],
        out_specs=pl.BlockSpec((Q_BLOCK, attn_w + conv_wd), cur_map),
        out_shape=jax.ShapeDtypeStruct((n, attn_w + conv_wd), BF16),
        compiler_params=_cparams(("arbitrary", "arbitrary")),
        name="mixer",
    )(proj, proj, proj, proj, sinks2, conv_w)


def _split_bf16(v):
    hi = v.astype(BF16)
    lo = (v - hi.astype(F32)).astype(BF16)
    return hi, lo


def _outproj_kernel(mix_ref, x_ref, mod_ref, w_ref, g_ref, b_ref, wr_ref, x1_ref, h2_ref, lg_ref):
    m = mod_ref[0]
    mix = jnp.dot(mix_ref[...], w_ref[...], preferred_element_type=F32)
    x1 = _layer_norm(ALPHA * x_ref[...] + (1.0 + m[2:3]) * mix, g_ref[...], b_ref[...])
    x1_ref[...] = x1
    h2 = x1 * (1.0 + m[4:5]) + m[3:4]
    tm, d = h2.shape
    words = _pack_pairs(h2[:, :d // 2], h2[:, d // 2:])
    for s in range(TOKEN_ROWS):
        h2_ref[pl.ds(s, tm, stride=TOKEN_ROWS), :] = words[:, s * LANES:(s + 1) * LANES]
    h_hi, h_lo = _split_bf16(h2)
    w_hi, w_lo = _split_bf16(wr_ref[...])
    lg_ref[...] = (jnp.dot(h_hi, w_hi, preferred_element_type=F32)
                   + (jnp.dot(h_hi, w_lo, preferred_element_type=F32)
                      + jnp.dot(h_lo, w_hi, preferred_element_type=F32)))


def _outproj(mix, x2, mod3, w_out_bf, ln_g, ln_b, w_router, seq):
    n, d = x2.shape
    tm = OUTPROJ_TM
    ne = w_router.shape[1]
    row = lambda i: (i, 0)
    const = lambda i: (0, 0)
    return pl.pallas_call(
        _outproj_kernel,
        grid=(n // tm,),
        in_specs=[pl.BlockSpec((tm, d), row),
                  pl.BlockSpec((tm, d), row),
                  pl.BlockSpec((1, 6, d), lambda i: (i * tm // seq, 0, 0)),
                  pl.BlockSpec((d, d), const),
                  pl.BlockSpec((1, d), const),
                  pl.BlockSpec((1, d), const),
                  pl.BlockSpec((d, ne), const)],
        out_specs=[pl.BlockSpec((tm, d), row),
                   pl.BlockSpec((tm * TOKEN_ROWS, LANES), row),
                   pl.BlockSpec((tm, ne), row)],
        out_shape=[jax.ShapeDtypeStruct((n, d), F32),
                   jax.ShapeDtypeStruct((n * TOKEN_ROWS, LANES), U32),
                   jax.ShapeDtypeStruct((n, ne), F32)],
        compiler_params=_cparams(("arbitrary",)),
        name="outproj",
    )(mix, x2, mod3, w_out_bf, ln_g, ln_b, w_router)


def _first_argmax(v, lane_f):
    m = jnp.max(v, axis=-1, keepdims=True)
    idx = jnp.min(jnp.where(v == m, lane_f, float(N_EXPERTS)), axis=-1, keepdims=True)
    return m, idx


def _route_kernel(lg_ref, bias_ref, eidx_ref, w_ref, cnt_ref):
    scores = jax.nn.sigmoid(lg_ref[...])
    sel = scores + bias_ref[...]
    tm = sel.shape[0]
    lane = lax.broadcasted_iota(jnp.int32, (tm, N_EXPERTS), 1)
    lane_f = lane.astype(F32)
    grp = lane // GROUP_SIZE
    neg = -jnp.inf
    gs = []
    for g in range(N_GROUPS):
        vg = jnp.where(grp == g, sel, neg)
        m1, i1 = _first_argmax(vg, lane_f)
        m2 = jnp.max(jnp.where(lane_f == i1, neg, vg), axis=-1, keepdims=True)
        gs.append(m1 + m2)
    keep = jnp.zeros((tm, N_EXPERTS), F32)
    for g in range(N_GROUPS):
        rank = jnp.zeros((tm, 1), F32)
        for o in range(N_GROUPS):
            if o == g:
                continue
            ahead = (gs[o] >= gs[g]) if o < g else (gs[o] > gs[g])
            rank = rank + jnp.where(ahead, 1.0, 0.0)
        keep = jnp.where(grp == g, jnp.where(rank < TOPK_GROUPS, 1.0, 0.0), keep)
    cand = jnp.where(keep > 0.5, sel, neg)
    idxs, ws = [], []
    chosen = jnp.zeros((tm, N_EXPERTS), F32)
    for _ in range(TOP_K):
        _, ik = _first_argmax(cand, lane_f)
        hit = lane_f == ik
        ws.append(jnp.sum(jnp.where(hit, scores, 0.0), axis=-1, keepdims=True))
        idxs.append(ik)
        cand = jnp.where(hit, neg, cand)
        chosen = jnp.where(hit, 1.0, chosen)
    cnt_ref[0] = jnp.sum(chosen, axis=0, keepdims=True)
    wsum = ws[0]
    for k in range(1, TOP_K):
        wsum = wsum + ws[k]
    col = lax.broadcasted_iota(jnp.int32, (tm, TOP_K), 1)
    eidx = jnp.zeros((tm, TOP_K), F32)
    wout = jnp.zeros((tm, TOP_K), F32)
    for k in range(TOP_K):
        eidx = jnp.where(col == k, idxs[k], eidx)
        wout = jnp.where(col == k, ws[k] / wsum * ROUTED_SCALE, wout)
    eidx_ref[...] = eidx.astype(jnp.int32)
    w_ref[...] = wout


def _route(logits, bias2):
    n, ne = logits.shape
    tm = ROUTE_TM
    row = lambda i: (i, 0)
    return pl.pallas_call(
        _route_kernel,
        grid=(n // tm,),
        in_specs=[pl.BlockSpec((tm, ne), row), pl.BlockSpec((1, ne), lambda i: (0, 0))],
        out_specs=[pl.BlockSpec((tm, TOP_K), row), pl.BlockSpec((tm, TOP_K), row),
                   pl.BlockSpec((1, 1, ne), lambda i: (i, 0, 0))],
        out_shape=[jax.ShapeDtypeStruct((n, TOP_K), jnp.int32), jax.ShapeDtypeStruct((n, TOP_K), F32),
                   jax.ShapeDtypeStruct((n // tm, 1, ne), F32)],
        compiler_params=_cparams(("arbitrary",)),
        name="route",
    )(logits, bias2)


def _issue_rows(lo, hi, issue_one):
    n_full = (hi - lo) // ISSUE_UNROLL

    def chunk(c, carry):
        for u in range(ISSUE_UNROLL):
            issue_one(lo + c * ISSUE_UNROLL + u)
        return carry

    def tail(r, carry):
        issue_one(r)
        return carry

    lax.fori_loop(0, n_full, chunk, 0)
    lax.fori_loop(lo + n_full * ISSUE_UNROLL, hi, tail, 0)


def _hbm_slab(ref, row):
    return ref.at[pl.ds(pl.multiple_of(row * TOKEN_ROWS, TOKEN_ROWS), TOKEN_ROWS), :]


def _moe_kernel(be_ref, new_ref, nexte_ref, epar_ref, nused_ref, tgt_hbm, h_hbm, wg_hbm, wu_hbm, wd_hbm, ys_hbm,
                idx_s, xbuf, ybuf, wg_f, wu_f, wd_f, wg_s, wu_s, wd_s, sem_i, sem_g, sem_s, sem_w, *, n_tok):
    s = pl.program_id(0)
    tm = MOE_TM
    per = IDX_CHUNK // tm
    per_log2 = per.bit_length() - 1
    n_used = nused_ref[0]
    slab = TOKEN_ROWS
    buf_rows = tm * SLAB_PITCH
    dump_row0 = TOP_K * n_tok
    slot = s & 1

    def staged(buf, base, r):
        return buf.at[pl.ds(pl.multiple_of(base + r * SLAB_PITCH, SUBLANES), slab), :]

    def idx_copy(c):
        return pltpu.make_async_copy(
            tgt_hbm.at[pl.ds(pl.multiple_of(c * IDX_CHUNK, IDX_CHUNK), IDX_CHUNK)],
            idx_s.at[pl.ds(pl.multiple_of((c & 1) * IDX_CHUNK, IDX_CHUNK), IDX_CHUNK)], sem_i)

    def idx_base(b):
        return ((b >> per_log2) & 1) * IDX_CHUNK + (b & (per - 1)) * tm

    def weight_copies(e, p):
        return (pltpu.make_async_copy(wg_hbm.at[e], wg_f.at[p], sem_w.at[p]),
                pltpu.make_async_copy(wu_hbm.at[e], wu_f.at[p], sem_w.at[p]),
                pltpu.make_async_copy(wd_hbm.at[e], wd_f.at[p], sem_w.at[p]))

    def for_rows(inline, body):
        if inline:
            for r in range(tm):
                body(r, r % 2)
        else:
            def pair(c, carry):
                body(2 * c, 0)
                body(2 * c + 1, 1)
                return carry
            lax.fori_loop(0, tm // 2, pair, 0)

    def issue_gather(b, to_slot, inline):
        ibase = idx_base(b)
        xbase = to_slot * buf_rows

        def one(r, prio):
            tok = idx_s[ibase + r] & (n_tok - 1)
            pltpu.make_async_copy(_hbm_slab(h_hbm, tok), staged(xbuf, xbase, r),
                                  sem_g.at[to_slot]).start(priority=prio)
        for_rows(inline, one)

    def issue_scatter(b, from_slot, to_dump, inline):
        ibase = idx_base(b)
        ybase = from_slot * buf_rows

        def one(r, prio):
            tgt = jnp.where(to_dump, dump_row0 + tm + r, idx_s[ibase + r])
            pltpu.make_async_copy(staged(ybuf, ybase, r), _hbm_slab(ys_hbm, tgt),
                                  sem_s.at[from_slot]).start(priority=prio)
        for_rows(inline, one)

    def wait_gather(at_slot):
        v = xbuf.at[pl.ds(pl.multiple_of(at_slot * buf_rows, SUBLANES), buf_rows), :]
        pltpu.make_async_copy(h_hbm.at[pl.ds(0, buf_rows), :], v, sem_g.at[at_slot]).wait()

    def wait_scatter(at_slot):
        v = ybuf.at[pl.ds(pl.multiple_of(at_slot * buf_rows, SUBLANES), buf_rows), :]
        pltpu.make_async_copy(v, ys_hbm.at[pl.ds(0, buf_rows), :], sem_s.at[at_slot]).wait()

    @pl.when(s == 0)
    def _():
        ybuf[...] = jnp.zeros(ybuf.shape, U32)
        first = idx_copy(0)
        first.start()
        first.wait()
        init = pltpu.make_async_copy(ybuf, ys_hbm.at[pl.ds(pl.multiple_of(dump_row0 * slab, slab), 2 * buf_rows), :],
                                     sem_i)
        init.start()
        init.wait()
        issue_gather(0, 0, inline=False)
        for cp in weight_copies(be_ref[0], 0):
            cp.start()

    @pl.when(s < n_used)
    def _():
        @pl.when((s & (per - 1)) == 1)
        def _():
            idx_copy((s >> per_log2) + 1).start()

        @pl.when(((s + 1) & (per - 1)) == 0)
        def _():
            idx_copy((s + 1) >> per_log2).wait()

        @pl.when(new_ref[s] == 1)
        def _():
            p = epar_ref[s]

            @pl.when(nexte_ref[s] >= 0)
            def _():
                for cp in weight_copies(nexte_ref[s], 1 - p):
                    cp.start()

            for cp in weight_copies(be_ref[s], p):
                cp.wait()
            wg_s[...] = wg_f[p].astype(BF16)
            wu_s[...] = wu_f[p].astype(BF16)
            wd_s[...] = wd_f[p].astype(BF16)

        wait_gather(slot)

        @pl.when(s >= 1)
        def _():
            wait_scatter(slot)

        issue_gather(s + 1, 1 - slot, inline=True)
        issue_scatter(jnp.maximum(s - 1, 0), 1 - slot, s == 0, inline=True)
        base = slot * buf_rows
        x_lo, x_hi = _unpack_pairs(_slab_rows_to_matrix(xbuf, base, tm, SLAB_PITCH))
        x = jnp.concatenate([x_lo.astype(BF16), x_hi.astype(BF16)], axis=-1)
        gate = jnp.dot(x, wg_s[...], preferred_element_type=F32)
        up = jnp.dot(x, wu_s[...], preferred_element_type=F32)
        act = (_silu(gate) * up).astype(BF16)
        y = jnp.dot(act, wd_s[...], preferred_element_type=F32)
        half = slab * LANES
        words = _pack_pairs(y[:, :half], y[:, half:])
        for j in range(slab):
            ybuf[pl.ds(base + j, tm, stride=SLAB_PITCH), :] = words[:, j * LANES:(j + 1) * LANES]

    @pl.when(s == n_used)
    def _():
        last = (s - 1) & (per - 1)

        @pl.when((last == 1) | (last == 2))
        def _():
            idx_copy(((s - 1) >> per_log2) + 1).wait()

        wait_gather(slot)
        wait_scatter(slot)
        issue_scatter(s - 1, 1 - slot, False, inline=False)
        wait_scatter(1 - slot)


def _moe(block_e, block_new, block_nexte, block_epar, n_used, row_tgt, h2d, w_gate, w_up, w_down):
    n = h2d.shape[0] // TOKEN_ROWS
    assert n & (n - 1) == 0
    d = 2 * TOKEN_ROWS * LANES
    de = w_gate.shape[2]
    tm = MOE_TM
    n_blk = block_e.shape[0]
    grid_spec = pltpu.PrefetchScalarGridSpec(
        num_scalar_prefetch=5,
        grid=(n_blk,),
        in_specs=[pl.BlockSpec(memory_space=pl.ANY)] * 5,
        out_specs=pl.BlockSpec(memory_space=pl.ANY),
        scratch_shapes=[pltpu.SMEM((2 * IDX_CHUNK,), jnp.int32),
                        pltpu.VMEM((2 * tm * SLAB_PITCH, LANES), U32),
                        pltpu.VMEM((2 * tm * SLAB_PITCH, LANES), U32),
                        pltpu.VMEM((2, d, de), F32),
                        pltpu.VMEM((2, d, de), F32),
                        pltpu.VMEM((2, de, d), F32),
                        pltpu.VMEM((d, de), BF16),
                        pltpu.VMEM((d, de), BF16),
                        pltpu.VMEM((de, d), BF16),
                        pltpu.SemaphoreType.DMA,
                        pltpu.SemaphoreType.DMA((2,)),
                        pltpu.SemaphoreType.DMA((2,)),
                        pltpu.SemaphoreType.DMA((2,))])
    return pl.pallas_call(
        functools.partial(_moe_kernel, n_tok=n),
        grid_spec=grid_spec,
        out_shape=jax.ShapeDtypeStruct(((TOP_K * n + 2 * tm) * TOKEN_ROWS, LANES), U32),
        compiler_params=_cparams(("arbitrary",)),
        name="moe",
    )(block_e, block_new, block_nexte, block_epar, n_used, row_tgt, h2d, w_gate, w_up, w_down)


def _final_kernel(gw_hbm, *refs):
    ys_refs = refs[:TOP_K]
    h_ref, x1_ref, mod_ref, wg_ref, wu_ref, wd_ref, g_ref, b_ref, o_ref, gw_s, acc_ref, sem_i = refs[TOP_K:]
    i = pl.program_id(0)
    tm = x1_ref.shape[0]
    per_step = tm * TOP_K
    cp = pltpu.make_async_copy(gw_hbm.at[pl.ds(pl.multiple_of(i * per_step, per_step), per_step)], gw_s, sem_i)
    cp.start()
    cp.wait()

    hi_base = tm * SLAB_PITCH

    def combine(t, carry):
        rows = pl.ds(pl.multiple_of(t * TOKEN_ROWS, TOKEN_ROWS), TOKEN_ROWS)
        acc_lo = acc_hi = None
        for k in range(TOP_K):
            lo, hi = _unpack_pairs(ys_refs[k][rows, :])
            g = gw_s[t * TOP_K + k]
            acc_lo = g * lo if acc_lo is None else acc_lo + g * lo
            acc_hi = g * hi if acc_hi is None else acc_hi + g * hi
        acc_ref[pl.ds(pl.multiple_of(t * SLAB_PITCH, SUBLANES), TOKEN_ROWS), :] = acc_lo
        acc_ref[pl.ds(pl.multiple_of(hi_base + t * SLAB_PITCH, SUBLANES), TOKEN_ROWS), :] = acc_hi
        return carry

    lax.fori_loop(0, tm, combine, 0, unroll=4)
    moe = jnp.concatenate([_slab_rows_to_matrix(acc_ref, 0, tm, SLAB_PITCH),
                           _slab_rows_to_matrix(acc_ref, hi_base, tm, SLAB_PITCH)], axis=-1)
    h_lo, h_hi = _unpack_pairs(_slab_rows_to_matrix(h_ref, 0, tm, TOKEN_ROWS))
    h = jnp.concatenate([h_lo.astype(BF16), h_hi.astype(BF16)], axis=-1)
    gate = jnp.dot(h, wg_ref[...], preferred_element_type=F32)
    up = jnp.dot(h, wu_ref[...], preferred_element_type=F32)
    shared = jnp.dot((_silu(gate) * up).astype(BF16), wd_ref[...], preferred_element_type=F32)
    m = mod_ref[0]
    y = ALPHA * x1_ref[...] + (1.0 + m[5:6]) * (moe + shared)
    o_ref[...] = _layer_norm(y, g_ref[...], b_ref[...])


def _final(gate_w, ys, h2d, x1, mod3, wsg, wsu, wsd, ln_g, ln_b, seq):
    n, d = x1.shape
    de = wsg.shape[1]
    tm = FINAL_TM
    per_step = tm * TOP_K
    assert per_step % IDX_CHUNK == 0
    row = lambda i: (i, 0)
    const = lambda i: (0, 0)
    slot_rows = lambda k: (lambda i: (k * (n // tm) + i, 0))
    return pl.pallas_call(
        _final_kernel,
        grid=(n // tm,),
        in_specs=[pl.BlockSpec(memory_space=pl.ANY)]
                 + [pl.BlockSpec((tm * TOKEN_ROWS, LANES), slot_rows(k)) for k in range(TOP_K)]
                 + [pl.BlockSpec((tm * TOKEN_ROWS, LANES), row),
                  pl.BlockSpec((tm, d), row),
                  pl.BlockSpec((1, 6, d), lambda i: (i * tm // seq, 0, 0)),
                  pl.BlockSpec((d, de), const),
                  pl.BlockSpec((d, de), const),
                  pl.BlockSpec((de, d), const),
                  pl.BlockSpec((1, d), const),
                  pl.BlockSpec((1, d), const)],
        out_specs=pl.BlockSpec((tm, d), row),
        out_shape=jax.ShapeDtypeStruct((n, d), F32),
        scratch_shapes=[pltpu.SMEM((per_step,), F32),
                        pltpu.VMEM((2 * tm * SLAB_PITCH, LANES), F32),
                        pltpu.SemaphoreType.DMA],
        compiler_params=_cparams(("arbitrary",)),
        name="final",
    )(gate_w, *([ys] * TOP_K), h2d, x1, mod3, wsg, wsu, wsd, ln_g, ln_b)


def _dispatch_tables(eidx, tile_counts, n):
    tm = MOE_TM
    a = n * TOP_K
    i32 = jnp.int32
    experts = jnp.arange(N_EXPERTS, dtype=i32)
    counts = jnp.sum(tile_counts, axis=(0, 1)).astype(i32)
    padded = (counts + tm - 1) // tm * tm
    pad_end = jnp.cumsum(padded)
    starts = pad_end - padded
    n_blk = a // tm + N_EXPERTS + 1
    blk_start = jnp.arange(n_blk, dtype=i32) * tm
    n_used = pad_end[-1] // tm
    in_use = jnp.arange(n_blk) < n_used
    raw_e = jnp.minimum(jnp.sum((pad_end[None, :] <= blk_start[:, None]).astype(i32), axis=1), N_EXPERTS - 1)
    last_e = jnp.sum(jnp.where(jnp.arange(n_blk) == n_used - 1, raw_e, 0))
    block_e = jnp.where(in_use, raw_e, last_e)
    onehot = block_e[:, None] == experts[None, :]
    block_new = jnp.concatenate([jnp.ones((1,), i32), (block_e[1:] != block_e[:-1]).astype(i32)])
    has_rows = counts > 0
    later = (experts[None, :] > experts[:, None]) & has_rows[None, :]
    next_e = jnp.min(jnp.where(later, experts[None, :], N_EXPERTS), axis=1)
    next_e = jnp.where(next_e == N_EXPERTS, -1, next_e)
    parity_e = (jnp.cumsum(has_rows.astype(i32)) - 1) & 1
    block_nexte = jnp.sum(jnp.where(onehot, next_e[None, :], 0), axis=1).astype(i32)
    block_epar = jnp.sum(jnp.where(onehot, parity_e[None, :], 0), axis=1).astype(i32)
    cnt_b = jnp.sum(jnp.where(onehot, counts[None, :], 0), axis=1)
    start_b = jnp.sum(jnp.where(onehot, starts[None, :], 0), axis=1)
    block_nv = jnp.where(in_use, jnp.clip(cnt_b - (blk_start - start_b), 0, tm), 0)
    dummy_keys = jnp.where(jnp.arange(tm, dtype=i32)[None, :] < (padded - counts)[:, None],
                           experts[:, None], N_EXPERTS).reshape(-1)
    keys = jnp.concatenate([eidx.reshape(-1), dummy_keys, jnp.full((tm,), N_EXPERTS, i32)])
    tgt = (jnp.arange(TOP_K, dtype=i32)[None, :] * n + jnp.arange(n, dtype=i32)[:, None]).reshape(-1)
    payload = jnp.concatenate([tgt, jnp.zeros((n_blk * tm - a,), i32)])
    _, row_tgt = lax.sort((keys, payload), num_keys=1, is_stable=True)
    r = jnp.arange(tm, dtype=i32)[None, :]
    dump = a + (jnp.arange(n_blk, dtype=i32)[:, None] & 1) * tm + r
    row_tgt = jnp.where(r >= block_nv[:, None], dump, row_tgt.reshape(n_blk, tm)).reshape(-1)
    row_tgt = jnp.pad(row_tgt, (0, -(n_blk * tm) % IDX_CHUNK))
    return block_e.astype(i32), block_new, block_nexte, block_epar, n_used.astype(i32).reshape(1), row_tgt


def kernel(x, c, w_mod, b_mod, w_in, conv_w, attn_sinks, w_out, ln1_g, ln1_b, w_router, router_bias,
           w_gate, w_up, w_down, ws_gate, ws_up, ws_down, ln2_g, ln2_b):
    b, s, d = x.shape
    n = b * s
    attn_w = N_Q_HEADS * HEAD_DIM
    kv_w = N_KV_HEADS * HEAD_DIM
    conv_wd = d - attn_w
    in_w = attn_w + 2 * kv_w + 3 * conv_wd
    x2 = x.reshape(n, d)
    c8 = jnp.zeros((SUBLANES, d), F32).at[:b].set(c)
    for l in range(DEPTH):
        mod = _mod(c8, w_mod[l], b_mod[l].reshape(1, -1))[:b]
        mod3 = mod.reshape(b, 6, d)
        proj = _inproj(x2, mod3, w_in[l].astype(BF16), s)
        mix = _mixer(proj, attn_sinks[l].reshape(1, -1), conv_w[l], b, s, attn_w, kv_w, conv_wd)
        x1, h2d, logits = _outproj(mix, x2, mod3, w_out[l].astype(BF16), ln1_g[l].reshape(1, -1),
                                   ln1_b[l].reshape(1, -1), w_router[l], s)
        eidx, gate_w, tile_counts = _route(logits, router_bias[l].reshape(1, -1))
        block_e, block_new, block_nexte, block_epar, n_used, row_tgt = _dispatch_tables(eidx, tile_counts, n)
        ys = _moe(block_e, block_new, block_nexte, block_epar, n_used, row_tgt, h2d,
                  w_gate[l], w_up[l], w_down[l])
        x2 = _final(gate_w.reshape(-1), ys, h2d, x1, mod3,
                    ws_gate[l].astype(BF16), ws_up[l].astype(BF16), ws_down[l].astype(BF16),
                    ln2_g[l].reshape(1, -1), ln2_b[l].reshape(1, -1), s)
    return x2.reshape(b, s, d)
```

```python
import functools

import jax
import jax.numpy as jnp
from jax import lax
from jax.experimental import pallas as pl
from jax.experimental.pallas import tpu as pltpu

HEAD_DIM = 64
N_Q_HEADS = 16
N_KV_HEADS = 4
GQA = N_Q_HEADS // N_KV_HEADS
CONV_K = 3
WINDOW = 128
Q_BLOCK = 128
N_EXPERTS = 64
TOP_K = 8
N_GROUPS = 8
GROUP_SIZE = N_EXPERTS // N_GROUPS
TOPK_GROUPS = 4
ROUTED_SCALE = 2.5
DEPTH = 1
ALPHA = (2.0 * DEPTH) ** 0.25
LN_EPS = 1e-5

LANES = 128
SUBLANES = 8
TOKEN_ROWS = 8
SLAB_PITCH = 8
VMEM_LIMIT = 56 * 1024 * 1024

MOD_TN = 1024
INPROJ_TM = 512
INPROJ_TN = 1536
OUTPROJ_TM = 256
ROUTE_TM = 512
MOE_TM = 256
FINAL_TM = 256
IDX_CHUNK = 1024
ISSUE_UNROLL = 8

F32 = jnp.float32
BF16 = jnp.bfloat16


def _cparams(sem):
    return pltpu.CompilerParams(dimension_semantics=sem, vmem_limit_bytes=VMEM_LIMIT)


def _silu(v):
    return v * jax.nn.sigmoid(v)


U32 = jnp.uint32
HI_MASK = 0xFFFF0000


def _pack_pairs(lo, hi):
    lo_bits = lax.bitcast_convert_type(lo.astype(BF16).astype(F32), U32) >> 16
    hi_bits = lax.bitcast_convert_type(hi.astype(BF16).astype(F32), U32) & U32(HI_MASK)
    return lo_bits | hi_bits


def _unpack_pairs(w):
    return (lax.bitcast_convert_type(w << 16, F32), lax.bitcast_convert_type(w & U32(HI_MASK), F32))


def _slab_rows_to_matrix(ref, base, tm, pitch):
    return jnp.concatenate([ref[pl.ds(base + s, tm, stride=pitch), :] for s in range(TOKEN_ROWS)], axis=-1)


def _layer_norm(y, g, b):
    mu = jnp.mean(y, axis=-1, keepdims=True)
    yc = y - mu
    var = jnp.mean(yc * yc, axis=-1, keepdims=True)
    return yc * lax.rsqrt(var + LN_EPS) * g + b


def _mod_kernel(c_ref, w_ref, b_ref, o_ref):
    cs = _silu(c_ref[...]).astype(BF16)
    o_ref[...] = jnp.dot(cs, w_ref[...].astype(BF16), preferred_element_type=F32) + b_ref[...]


def _mod(c8, w_mod, b_mod):
    d, n = w_mod.shape
    return pl.pallas_call(
        _mod_kernel,
        grid=(n // MOD_TN,),
        in_specs=[pl.BlockSpec((SUBLANES, d), lambda j: (0, 0)),
                  pl.BlockSpec((d, MOD_TN), lambda j: (0, j)),
                  pl.BlockSpec((1, MOD_TN), lambda j: (0, j))],
        out_specs=pl.BlockSpec((SUBLANES, MOD_TN), lambda j: (0, j)),
        out_shape=jax.ShapeDtypeStruct((SUBLANES, n), F32),
        compiler_params=_cparams(("arbitrary",)),
        name="mod",
    )(c8, w_mod, b_mod)


def _inproj_kernel(x_ref, mod_ref, w_ref, o_ref, h_ref):
    j = pl.program_id(1)

    @pl.when(j == 0)
    def _():
        m = mod_ref[0]
        h_ref[...] = (x_ref[...] * (1.0 + m[1:2]) + m[0:1]).astype(BF16)

    o_ref[...] = jnp.dot(h_ref[...], w_ref[...], preferred_element_type=F32).astype(BF16)


def _inproj(x2, mod3, w_in_bf, seq):
    n, d = x2.shape
    in_w = w_in_bf.shape[1]
    tm, tn = INPROJ_TM, INPROJ_TN
    return pl.pallas_call(
        _inproj_kernel,
        grid=(n // tm, in_w // tn),
        in_specs=[pl.BlockSpec((tm, d), lambda i, j: (i, 0)),
                  pl.BlockSpec((1, 6, d), lambda i, j: (i * tm // seq, 0, 0)),
                  pl.BlockSpec((d, tn), lambda i, j: (0, j))],
        out_specs=pl.BlockSpec((tm, tn), lambda i, j: (i, j)),
        out_shape=jax.ShapeDtypeStruct((n, in_w), BF16),
        scratch_shapes=[pltpu.VMEM((tm, d), BF16)],
        compiler_params=_cparams(("arbitrary", "arbitrary")),
        name="inproj",
    )(x2, mod3, w_in_bf)


def _mixer_kernel(cur_ref, pk_ref, pv_ref, prow_ref, sink_ref, cw_ref, o_ref, *, attn_w, kv_w, conv_w):
    nblk = pl.program_id(1)
    has_prev = nblk > 0
    qb = Q_BLOCK
    cur = cur_ref[...]
    k_cur = cur[:, attn_w:attn_w + kv_w]
    v_cur = cur[:, attn_w + kv_w:attn_w + 2 * kv_w]
    k_all = jnp.concatenate([pk_ref[...], k_cur], axis=0)
    v_all = jnp.concatenate([pv_ref[...], v_cur], axis=0)

    rows = GQA * qb
    qi = lax.broadcasted_iota(jnp.int32, (rows, 2 * qb), 0) % qb
    kj = lax.broadcasted_iota(jnp.int32, (rows, 2 * qb), 1)
    dist = qi + qb - kj
    kmin = jnp.where(has_prev, 0, qb)
    valid = (dist >= 0) & (dist < WINDOW) & (kj >= kmin)
    distf = dist.astype(F32)
    head_in_group = lax.broadcasted_iota(jnp.int32, (rows, 1), 0) // qb
    sinks = sink_ref[...]

    outs = []
    for g in range(N_KV_HEADS):
        q4 = jnp.concatenate(
            [cur[:, (g * GQA + j) * HEAD_DIM:(g * GQA + j + 1) * HEAD_DIM] for j in range(GQA)], axis=0)
        kg = k_all[:, g * HEAD_DIM:(g + 1) * HEAD_DIM]
        vg = v_all[:, g * HEAD_DIM:(g + 1) * HEAD_DIM]
        s = lax.dot_general(q4, kg, (((1,), (1,)), ((), ())), preferred_element_type=F32)
        s = s * (HEAD_DIM ** -0.5)
        slope = jnp.zeros((rows, 1), F32)
        sink = jnp.zeros((rows, 1), F32)
        for j in range(GQA):
            h = g * GQA + j
            sel = head_in_group == j
            slope = jnp.where(sel, 2.0 ** (-8.0 * (h + 1) / N_Q_HEADS), slope)
            sink = jnp.where(sel, sinks[:, h:h + 1], sink)
        s = jnp.where(valid, s - slope * distf, -jnp.inf)
        m = jnp.maximum(jnp.max(s, axis=-1, keepdims=True), sink)
        p = jnp.exp(s - m)
        denom = jnp.sum(p, axis=-1, keepdims=True) + jnp.exp(sink - m)
        o4 = jnp.dot(p.astype(BF16), vg, preferred_element_type=F32) / denom
        outs.extend(o4[j * qb:(j + 1) * qb] for j in range(GQA))
    attn = jnp.concatenate(outs, axis=-1)

    c0 = attn_w + 2 * kv_w
    cb = cur[:, c0:c0 + conv_w].astype(F32)
    u = cur[:, c0 + conv_w:c0 + 2 * conv_w].astype(F32) * cur[:, c0 + 2 * conv_w:c0 + 3 * conv_w].astype(F32)
    prow = prow_ref[...]
    up = prow[:, c0 + conv_w:c0 + 2 * conv_w].astype(F32) * prow[:, c0 + 2 * conv_w:c0 + 3 * conv_w].astype(F32)
    up = up * jnp.where(has_prev, 1.0, 0.0)
    pm1 = up[15:16]
    pm2 = up[14:15]
    ri = lax.broadcasted_iota(jnp.int32, u.shape, 0)
    u1 = jnp.where(ri == 0, pm1, pltpu.roll(u, 1, 0))
    u2 = jnp.where(ri == 0, pm2, jnp.where(ri == 1, pm1, pltpu.roll(u, 2, 0)))
    cw = cw_ref[...]
    conv = cb * (cw[0:1] * u2 + cw[1:2] * u1 + cw[2:3] * u)
    o_ref[...] = jnp.concatenate([attn, conv], axis=-1).astype(BF16)


def _mixer(proj, sinks2, conv_w, batch, seq, attn_w, kv_w, conv_wd):
    n, in_w = proj.shape
    nb = seq // Q_BLOCK
    kv_blk0 = attn_w // kv_w
    sub16 = Q_BLOCK // 16

    def cur_map(b, i):
        return (b * nb + i, 0)

    def prev_map(col):
        return lambda b, i: (b * nb + jnp.maximum(i - 1, 0), col)

    def prow_map(b, i):
        return (jnp.maximum((b * nb + i) * sub16 - 1, 0), 0)

    kern = functools.partial(_mixer_kernel, attn_w=attn_w, kv_w=kv_w, conv_w=conv_wd)
    return pl.pallas_call(
        kern,
        grid=(batch, nb),
        in_specs=[pl.BlockSpec((Q_BLOCK, in_w), cur_map),
                  pl.BlockSpec((Q_BLOCK, kv_w), prev_map(kv_blk0)),
                  pl.BlockSpec((Q_BLOCK, kv_w), prev_map(kv_blk0 + 1)),
                  pl.BlockSpec((16, in_w), prow_map),
                  pl.BlockSpec((1, N_Q_HEADS), lambda b, i: (0, 0)),
                  pl.BlockSpec((CONV_K, conv_wd), lambda b, i: (0, 0))],
        out_specs=pl.BlockSpec((Q_BLOCK, attn_w + conv_wd), cur_map),
        out_shape=jax.ShapeDtypeStruct((n, attn_w + conv_wd), BF16),
        compiler_params=_cparams(("arbitrary", "arbitrary")),
        name="mixer",
    )(proj, proj, proj, proj, sinks2, conv_w)


def _split_bf16(v):
    hi = v.astype(BF16)
    lo = (v - hi.astype(F32)).astype(BF16)
    return hi, lo


def _outproj_kernel(mix_ref, x_ref, mod_ref, w_ref, g_ref, b_ref, wr_ref, x1_ref, h2_ref, lg_ref):
    m = mod_ref[0]
    mix = jnp.dot(mix_ref[...], w_ref[...], preferred_element_type=F32)
    x1 = _layer_norm(ALPHA * x_ref[...] + (1.0 + m[2:3]) * mix, g_ref[...], b_ref[...])
    x1_ref[...] = x1
    h2 = x1 * (1.0 + m[4:5]) + m[3:4]
    tm, d = h2.shape
    words = _pack_pairs(h2[:, :d // 2], h2[:, d // 2:])
    for s in range(TOKEN_ROWS):
        h2_ref[pl.ds(s, tm, stride=TOKEN_ROWS), :] = words[:, s * LANES:(s + 1) * LANES]
    h_hi, h_lo = _split_bf16(h2)
    w_hi, w_lo = _split_bf16(wr_ref[...])
    lg_ref[...] = (jnp.dot(h_hi, w_hi, preferred_element_type=F32)
                   + (jnp.dot(h_hi, w_lo, preferred_element_type=F32)
                      + jnp.dot(h_lo, w_hi, preferred_element_type=F32)))


def _outproj(mix, x2, mod3, w_out_bf, ln_g, ln_b, w_router, seq):
    n, d = x2.shape
    tm = OUTPROJ_TM
    ne = w_router.shape[1]
    row = lambda i: (i, 0)
    const = lambda i: (0, 0)
    return pl.pallas_call(
        _outproj_kernel,
        grid=(n // tm,),
        in_specs=[pl.BlockSpec((tm, d), row),
                  pl.BlockSpec((tm, d), row),
                  pl.BlockSpec((1, 6, d), lambda i: (i * tm // seq, 0, 0)),
                  pl.BlockSpec((d, d), const),
                  pl.BlockSpec((1, d), const),
                  pl.BlockSpec((1, d), const),
                  pl.BlockSpec((d, ne), const)],
        out_specs=[pl.BlockSpec((tm, d), row),
                   pl.BlockSpec((tm * TOKEN_ROWS, LANES), row),
                   pl.BlockSpec((tm, ne), row)],
        out_shape=[jax.ShapeDtypeStruct((n, d), F32),
                   jax.ShapeDtypeStruct((n * TOKEN_ROWS, LANES), U32),
                   jax.ShapeDtypeStruct((n, ne), F32)],
        compiler_params=_cparams(("arbitrary",)),
        name="outproj",
    )(mix, x2, mod3, w_out_bf, ln_g, ln_b, w_router)


def _first_argmax(v, lane_f):
    m = jnp.max(v, axis=-1, keepdims=True)
    idx = jnp.min(jnp.where(v == m, lane_f, float(N_EXPERTS)), axis=-1, keepdims=True)
    return m, idx


def _route_kernel(lg_ref, bias_ref, eidx_ref, w_ref, cnt_ref):
    scores = jax.nn.sigmoid(lg_ref[...])
    sel = scores + bias_ref[...]
    tm = sel.shape[0]
    lane = lax.broadcasted_iota(jnp.int32, (tm, N_EXPERTS), 1)
    lane_f = lane.astype(F32)
    grp = lane // GROUP_SIZE
    neg = -jnp.inf
    gs = []
    for g in range(N_GROUPS):
        vg = jnp.where(grp == g, sel, neg)
        m1, i1 = _first_argmax(vg, lane_f)
        m2 = jnp.max(jnp.where(lane_f == i1, neg, vg), axis=-1, keepdims=True)
        gs.append(m1 + m2)
    keep = jnp.zeros((tm, N_EXPERTS), F32)
    for g in range(N_GROUPS):
        rank = jnp.zeros((tm, 1), F32)
        for o in range(N_GROUPS):
            if o == g:
                continue
            ahead = (gs[o] >= gs[g]) if o < g else (gs[o] > gs[g])
            rank = rank + jnp.where(ahead, 1.0, 0.0)
        keep = jnp.where(grp == g, jnp.where(rank < TOPK_GROUPS, 1.0, 0.0), keep)
    cand = jnp.where(keep > 0.5, sel, neg)
    idxs, ws = [], []
    chosen = jnp.zeros((tm, N_EXPERTS), F32)
    for _ in range(TOP_K):
        _, ik = _first_argmax(cand, lane_f)
        hit = lane_f == ik
        ws.append(jnp.sum(jnp.where(hit, scores, 0.0), axis=-1, keepdims=True))
        idxs.append(ik)
        cand = jnp.where(hit, neg, cand)
        chosen = jnp.where(hit, 1.0, chosen)
    cnt_ref[0] = jnp.sum(chosen, axis=0, keepdims=True)
    wsum = ws[0]
    for k in range(1, TOP_K):
        wsum = wsum + ws[k]
    col = lax.broadcasted_iota(jnp.int32, (tm, TOP_K), 1)
    eidx = jnp.zeros((tm, TOP_K), F32)
    wout = jnp.zeros((tm, TOP_K), F32)
    for k in range(TOP_K):
        eidx = jnp.where(col == k, idxs[k], eidx)
        wout = jnp.where(col == k, ws[k] / wsum * ROUTED_SCALE, wout)
    eidx_ref[...] = eidx.astype(jnp.int32)
    w_ref[...] = wout


def _route(logits, bias2):
    n, ne = logits.shape
    tm = ROUTE_TM
    row = lambda i: (i, 0)
    return pl.pallas_call(
        _route_kernel,
        grid=(n // tm,),
        in_specs=[pl.BlockSpec((tm, ne), row), pl.BlockSpec((1, ne), lambda i: (0, 0))],
        out_specs=[pl.BlockSpec((tm, TOP_K), row), pl.BlockSpec((tm, TOP_K), row),
                   pl.BlockSpec((1, 1, ne), lambda i: (i, 0, 0))],
        out_shape=[jax.ShapeDtypeStruct((n, TOP_K), jnp.int32), jax.ShapeDtypeStruct((n, TOP_K), F32),
                   jax.ShapeDtypeStruct((n // tm, 1, ne), F32)],
        compiler_params=_cparams(("arbitrary",)),
        name="route",
    )(logits, bias2)


def _issue_rows(lo, hi, issue_one):
    n_full = (hi - lo) // ISSUE_UNROLL

    def chunk(c, carry):
        for u in range(ISSUE_UNROLL):
            issue_one(lo + c * ISSUE_UNROLL + u)
        return carry

    def tail(r, carry):
        issue_one(r)
        return carry

    lax.fori_loop(0, n_full, chunk, 0)
    lax.fori_loop(lo + n_full * ISSUE_UNROLL, hi, tail, 0)


def _hbm_slab(ref, row):
    return ref.at[pl.ds(pl.multiple_of(row * TOKEN_ROWS, TOKEN_ROWS), TOKEN_ROWS), :]


def _moe_kernel(be_ref, new_ref, nexte_ref, epar_ref, nused_ref, tgt_hbm, h_hbm, wg_hbm, wu_hbm, wd_hbm, ys_hbm,
                idx_s, xbuf, ybuf, wg_f, wu_f, wd_f, wg_s, wu_s, wd_s, sem_i, sem_g, sem_s, sem_w, *, n_tok):
    s = pl.program_id(0)
    tm = MOE_TM
    per = IDX_CHUNK // tm
    per_log2 = per.bit_length() - 1
    n_used = nused_ref[0]
    slab = TOKEN_ROWS
    buf_rows = tm * SLAB_PITCH
    dump_row0 = TOP_K * n_tok
    slot = s & 1

    def staged(buf, base, r):
        return buf.at[pl.ds(pl.multiple_of(base + r * SLAB_PITCH, SUBLANES), slab), :]

    def idx_copy(c):
        return pltpu.make_async_copy(
            tgt_hbm.at[pl.ds(pl.multiple_of(c * IDX_CHUNK, IDX_CHUNK), IDX_CHUNK)],
            idx_s.at[pl.ds(pl.multiple_of((c & 1) * IDX_CHUNK, IDX_CHUNK), IDX_CHUNK)], sem_i)

    def idx_base(b):
        return ((b >> per_log2) & 1) * IDX_CHUNK + (b & (per - 1)) * tm

    def weight_copies(e, p):
        return (pltpu.make_async_copy(wg_hbm.at[e], wg_f.at[p], sem_w.at[p]),
                pltpu.make_async_copy(wu_hbm.at[e], wu_f.at[p], sem_w.at[p]),
                pltpu.make_async_copy(wd_hbm.at[e], wd_f.at[p], sem_w.at[p]))

    def for_rows(inline, body):
        if inline:
            for r in range(tm):
                body(r, r % 2)
        else:
            def pair(c, carry):
                body(2 * c, 0)
                body(2 * c + 1, 1)
                return carry
            lax.fori_loop(0, tm // 2, pair, 0)

    def issue_gather(b, to_slot, inline):
        ibase = idx_base(b)
        xbase = to_slot * buf_rows

        def one(r, prio):
            tok = idx_s[ibase + r] & (n_tok - 1)
            pltpu.make_async_copy(_hbm_slab(h_hbm, tok), staged(xbuf, xbase, r),
                                  sem_g.at[to_slot]).start(priority=prio)
        for_rows(inline, one)

    def issue_scatter(b, from_slot, to_dump, inline):
        ibase = idx_base(b)
        ybase = from_slot * buf_rows

        def one(r, prio):
            tgt = jnp.where(to_dump, dump_row0 + tm + r, idx_s[ibase + r])
            pltpu.make_async_copy(staged(ybuf, ybase, r), _hbm_slab(ys_hbm, tgt),
                                  sem_s.at[from_slot]).start(priority=prio)
        for_rows(inline, one)

    def wait_gather(at_slot):
        v = xbuf.at[pl.ds(pl.multiple_of(at_slot * buf_rows, SUBLANES), buf_rows), :]
        pltpu.make_async_copy(h_hbm.at[pl.ds(0, buf_rows), :], v, sem_g.at[at_slot]).wait()

    def wait_scatter(at_slot):
        v = ybuf.at[pl.ds(pl.multiple_of(at_slot * buf_rows, SUBLANES), buf_rows), :]
        pltpu.make_async_copy(v, ys_hbm.at[pl.ds(0, buf_rows), :], sem_s.at[at_slot]).wait()

    @pl.when(s == 0)
    def _():
        ybuf[...] = jnp.zeros(ybuf.shape, U32)
        first = idx_copy(0)
        first.start()
        first.wait()
        init = pltpu.make_async_copy(ybuf, ys_hbm.at[pl.ds(pl.multiple_of(dump_row0 * slab, slab), 2 * buf_rows), :],
                                     sem_i)
        init.start()
        init.wait()
        issue_gather(0, 0, inline=False)
        for cp in weight_copies(be_ref[0], 0):
            cp.start()

    @pl.when(s < n_used)
    def _():
        @pl.when((s & (per - 1)) == 1)
        def _():
            idx_copy((s >> per_log2) + 1).start()

        @pl.when(((s + 1) & (per - 1)) == 0)
        def _():
            idx_copy((s + 1) >> per_log2).wait()

        @pl.when(new_ref[s] == 1)
        def _():
            p = epar_ref[s]

            @pl.when(nexte_ref[s] >= 0)
            def _():
                for cp in weight_copies(nexte_ref[s], 1 - p):
                    cp.start()

            for cp in weight_copies(be_ref[s], p):
                cp.wait()
            wg_s[...] = wg_f[p].astype(BF16)
            wu_s[...] = wu_f[p].astype(BF16)
            wd_s[...] = wd_f[p].astype(BF16)

        wait_gather(slot)

        @pl.when(s >= 1)
        def _():
            wait_scatter(slot)

        issue_gather(s + 1, 1 - slot, inline=True)
        issue_scatter(jnp.maximum(s - 1, 0), 1 - slot, s == 0, inline=True)
        base = slot * buf_rows
        x_lo, x_hi = _unpack_pairs(_slab_rows_to_matrix(xbuf, base, tm, SLAB_PITCH))
        x = jnp.concatenate([x_lo.astype(BF16), x_hi.astype(BF16)], axis=-1)
        gate = jnp.dot(x, wg_s[...], preferred_element_type=F32)
        up = jnp.dot(x, wu_s[...], preferred_element_type=F32)
        act = (_silu(gate) * up).astype(BF16)
        y = jnp.dot(act, wd_s[...], preferred_element_type=F32)
        half = slab * LANES
        words = _pack_pairs(y[:, :half], y[:, half:])
        for j in range(slab):
            ybuf[pl.ds(base + j, tm, stride=SLAB_PITCH), :] = words[:, j * LANES:(j + 1) * LANES]

    @pl.when(s == n_used)
    def _():
        last = (s - 1) & (per - 1)

        @pl.when((last == 1) | (last == 2))
        def _():
            idx_copy(((s - 1) >> per_log2) + 1).wait()

        wait_gather(slot)
        wait_scatter(slot)
        issue_scatter(s - 1, 1 - slot, False, inline=False)
        wait_scatter(1 - slot)


def _moe(block_e, block_new, block_nexte, block_epar, n_used, row_tgt, h2d, w_gate, w_up, w_down):
    n = h2d.shape[0] // TOKEN_ROWS
    assert n & (n - 1) == 0
    d = 2 * TOKEN_ROWS * LANES
    de = w_gate.shape[2]
    tm = MOE_TM
    n_blk = block_e.shape[0]
    grid_spec = pltpu.PrefetchScalarGridSpec(
        num_scalar_prefetch=5,
        grid=(n_blk,),
        in_specs=[pl.BlockSpec(memory_space=pl.ANY)] * 5,
        out_specs=pl.BlockSpec(memory_space=pl.ANY),
        scratch_shapes=[pltpu.SMEM((2 * IDX_CHUNK,), jnp.int32),
                        pltpu.VMEM((2 * tm * SLAB_PITCH, LANES), U32),
                        pltpu.VMEM((2 * tm * SLAB_PITCH, LANES), U32),
                        pltpu.VMEM((2, d, de), F32),
                        pltpu.VMEM((2, d, de), F32),
                        pltpu.VMEM((2, de, d), F32),
                        pltpu.VMEM((d, de), BF16),
                        pltpu.VMEM((d, de), BF16),
                        pltpu.VMEM((de, d), BF16),
                        pltpu.SemaphoreType.DMA,
                        pltpu.SemaphoreType.DMA((2,)),
                        pltpu.SemaphoreType.DMA((2,)),
                        pltpu.SemaphoreType.DMA((2,))])
    return pl.pallas_call(
        functools.partial(_moe_kernel, n_tok=n),
        grid_spec=grid_spec,
        out_shape=jax.ShapeDtypeStruct(((TOP_K * n + 2 * tm) * TOKEN_ROWS, LANES), U32),
        compiler_params=_cparams(("arbitrary",)),
        name="moe",
    )(block_e, block_new, block_nexte, block_epar, n_used, row_tgt, h2d, w_gate, w_up, w_down)


def _final_kernel(gw_hbm, *refs):
    ys_refs = refs[:TOP_K]
    h_ref, x1_ref, mod_ref, wg_ref, wu_ref, wd_ref, g_ref, b_ref, o_ref, gw_s, acc_ref, sem_i = refs[TOP_K:]
    i = pl.program_id(0)
    tm = x1_ref.shape[0]
    per_step = tm * TOP_K
    cp = pltpu.make_async_copy(gw_hbm.at[pl.ds(pl.multiple_of(i * per_step, per_step), per_step)], gw_s, sem_i)
    cp.start()
    cp.wait()

    hi_base = tm * SLAB_PITCH

    def combine(t, carry):
        rows = pl.ds(pl.multiple_of(t * TOKEN_ROWS, TOKEN_ROWS), TOKEN_ROWS)
        acc_lo = acc_hi = None
        for k in range(TOP_K):
            lo, hi = _unpack_pairs(ys_refs[k][rows, :])
            g = gw_s[t * TOP_K + k]
            acc_lo = g * lo if acc_lo is None else acc_lo + g * lo
            acc_hi = g * hi if acc_hi is None else acc_hi + g * hi
        acc_ref[pl.ds(pl.multiple_of(t * SLAB_PITCH, SUBLANES), TOKEN_ROWS), :] = acc_lo
        acc_ref[pl.ds(pl.multiple_of(hi_base + t * SLAB_PITCH, SUBLANES), TOKEN_ROWS), :] = acc_hi
        return carry

    lax.fori_loop(0, tm, combine, 0, unroll=4)
    moe = jnp.concatenate([_slab_rows_to_matrix(acc_ref, 0, tm, SLAB_PITCH),
                           _slab_rows_to_matrix(acc_ref, hi_base, tm, SLAB_PITCH)], axis=-1)
    h_lo, h_hi = _unpack_pairs(_slab_rows_to_matrix(h_ref, 0, tm, TOKEN_ROWS))
    h = jnp.concatenate([h_lo.astype(BF16), h_hi.astype(BF16)], axis=-1)
    gate = jnp.dot(h, wg_ref[...], preferred_element_type=F32)
    up = jnp.dot(h, wu_ref[...], preferred_element_type=F32)
    shared = jnp.dot((_silu(gate) * up).astype(BF16), wd_ref[...], preferred_element_type=F32)
    m = mod_ref[0]
    y = ALPHA * x1_ref[...] + (1.0 + m[5:6]) * (moe + shared)
    o_ref[...] = _layer_norm(y, g_ref[...], b_ref[...])


def _final(gate_w, ys, h2d, x1, mod3, wsg, wsu, wsd, ln_g, ln_b, seq):
    n, d = x1.shape
    de = wsg.shape[1]
    tm = FINAL_TM
    per_step = tm * TOP_K
    assert per_step % IDX_CHUNK == 0
    row = lambda i: (i, 0)
    const = lambda i: (0, 0)
    slot_rows = lambda k: (lambda i: (k * (n // tm) + i, 0))
    return pl.pallas_call(
        _final_kernel,
        grid=(n // tm,),
        in_specs=[pl.BlockSpec(memory_space=pl.ANY)]
                 + [pl.BlockSpec((tm * TOKEN_ROWS, LANES), slot_rows(k)) for k in range(TOP_K)]
                 + [pl.BlockSpec((tm * TOKEN_ROWS, LANES), row),
                  pl.BlockSpec((tm, d), row),
                  pl.BlockSpec((1, 6, d), lambda i: (i * tm // seq, 0, 0)),
                  pl.BlockSpec((d, de), const),
                  pl.BlockSpec((d, de), const),
                  pl.BlockSpec((de, d), const),
                  pl.BlockSpec((1, d), const),
                  pl.BlockSpec((1, d), const)],
        out_specs=pl.BlockSpec((tm, d), row),
        out_shape=jax.ShapeDtypeStruct((n, d), F32),
        scratch_shapes=[pltpu.SMEM((per_step,), F32),
                        pltpu.VMEM((2 * tm * SLAB_PITCH, LANES), F32),
                        pltpu.SemaphoreType.DMA],
        compiler_params=_cparams(("arbitrary",)),
        name="final",
    )(gate_w, *([ys] * TOP_K), h2d, x1, mod3, wsg, wsu, wsd, ln_g, ln_b)


def _dispatch_tables(eidx, tile_counts, n):
    tm = MOE_TM
    a = n * TOP_K
    i32 = jnp.int32
    experts = jnp.arange(N_EXPERTS, dtype=i32)
    counts = jnp.sum(tile_counts, axis=(0, 1)).astype(i32)
    padded = (counts + tm - 1) // tm * tm
    pad_end = jnp.cumsum(padded)
    starts = pad_end - padded
    n_blk = a // tm + N_EXPERTS + 1
    blk_start = jnp.arange(n_blk, dtype=i32) * tm
    n_used = pad_end[-1] // tm
    in_use = jnp.arange(n_blk) < n_used
    raw_e = jnp.minimum(jnp.sum((pad_end[None, :] <= blk_start[:, None]).astype(i32), axis=1), N_EXPERTS - 1)
    last_e = jnp.sum(jnp.where(jnp.arange(n_blk) == n_used - 1, raw_e, 0))
    block_e = jnp.where(in_use, raw_e, last_e)
    onehot = block_e[:, None] == experts[None, :]
    block_new = jnp.concatenate([jnp.ones((1,), i32), (block_e[1:] != block_e[:-1]).astype(i32)])
    has_rows = counts > 0
    later = (experts[None, :] > experts[:, None]) & has_rows[None, :]
    next_e = jnp.min(jnp.where(later, experts[None, :], N_EXPERTS), axis=1)
    next_e = jnp.where(next_e == N_EXPERTS, -1, next_e)
    parity_e = (jnp.cumsum(has_rows.astype(i32)) - 1) & 1
    block_nexte = jnp.sum(jnp.where(onehot, next_e[None, :], 0), axis=1).astype(i32)
    block_epar = jnp.sum(jnp.where(onehot, parity_e[None, :], 0), axis=1).astype(i32)
    cnt_b = jnp.sum(jnp.where(onehot, counts[None, :], 0), axis=1)
    start_b = jnp.sum(jnp.where(onehot, starts[None, :], 0), axis=1)
    block_nv = jnp.where(in_use, jnp.clip(cnt_b - (blk_start - start_b), 0, tm), 0)
    dummy_keys = jnp.where(jnp.arange(tm, dtype=i32)[None, :] < (padded - counts)[:, None],
                           experts[:, None], N_EXPERTS).reshape(-1)
    keys = jnp.concatenate([eidx.reshape(-1), dummy_keys, jnp.full((tm,), N_EXPERTS, i32)])
    tgt = (jnp.arange(TOP_K, dtype=i32)[None, :] * n + jnp.arange(n, dtype=i32)[:, None]).reshape(-1)
    bits = (a - 1).bit_length() + 1
    low = (1 << bits) - 1
    payload = jnp.concatenate([tgt, jnp.full((n_blk * tm - a,), low, i32)])
    row_tgt = lax.sort(keys * (1 << bits) + payload) & low
    r = jnp.arange(tm, dtype=i32)[None, :]
    dump = a + (jnp.arange(n_blk, dtype=i32)[:, None] & 1) * tm + r
    row_tgt = jnp.where(r >= block_nv[:, None], dump, row_tgt.reshape(n_blk, tm)).reshape(-1)
    row_tgt = jnp.pad(row_tgt, (0, -(n_blk * tm) % IDX_CHUNK))
    return block_e.astype(i32), block_new, block_nexte, block_epar, n_used.astype(i32).reshape(1), row_tgt


def kernel(x, c, w_mod, b_mod, w_in, conv_w, attn_sinks, w_out, ln1_g, ln1_b, w_router, router_bias,
           w_gate, w_up, w_down, ws_gate, ws_up, ws_down, ln2_g, ln2_b):
    b, s, d = x.shape
    n = b * s
    attn_w = N_Q_HEADS * HEAD_DIM
    kv_w = N_KV_HEADS * HEAD_DIM
    conv_wd = d - attn_w
    in_w = attn_w + 2 * kv_w + 3 * conv_wd
    x2 = x.reshape(n, d)
    c8 = jnp.zeros((SUBLANES, d), F32).at[:b].set(c)
    for l in range(DEPTH):
        mod = _mod(c8, w_mod[l], b_mod[l].reshape(1, -1))[:b]
        mod3 = mod.reshape(b, 6, d)
        proj = _inproj(x2, mod3, w_in[l].astype(BF16), s)
        mix = _mixer(proj, attn_sinks[l].reshape(1, -1), conv_w[l], b, s, attn_w, kv_w, conv_wd)
        x1, h2d, logits = _outproj(mix, x2, mod3, w_out[l].astype(BF16), ln1_g[l].reshape(1, -1),
                                   ln1_b[l].reshape(1, -1), w_router[l], s)
        eidx, gate_w, tile_counts = _route(logits, router_bias[l].reshape(1, -1))
        block_e, block_new, block_nexte, block_epar, n_used, row_tgt = _dispatch_tables(eidx, tile_counts, n)
        ys = _moe(block_e, block_new, block_nexte, block_epar, n_used, row_tgt, h2d,
                  w_gate[l], w_up[l], w_down[l])
        x2 = _final(gate_w.reshape(-1), ys, h2d, x1, mod3,
                    ws_gate[l].astype(BF16), ws_up[l].astype(BF16), ws_down[l].astype(BF16),
                    ln2_g[l].reshape(1, -1), ln2_b[l].reshape(1, -1), s)
    return x2.reshape(b, s, d)
```

```python
import functools

import jax
import jax.numpy as jnp
from jax import lax
from jax.experimental import pallas as pl
from jax.experimental.pallas import tpu as pltpu

HEAD_DIM = 64
N_Q_HEADS = 16
N_KV_HEADS = 4
GQA = N_Q_HEADS // N_KV_HEADS
CONV_K = 3
WINDOW = 128
Q_BLOCK = 128
N_EXPERTS = 64
TOP_K = 8
N_GROUPS = 8
GROUP_SIZE = N_EXPERTS // N_GROUPS
TOPK_GROUPS = 4
ROUTED_SCALE = 2.5
DEPTH = 1
ALPHA = (2.0 * DEPTH) ** 0.25
LN_EPS = 1e-5

LANES = 128
SUBLANES = 8
TOKEN_ROWS = 8
SLAB_PITCH = 8
VMEM_LIMIT = 56 * 1024 * 1024

MOD_TN = 1024
INPROJ_TM = 512
INPROJ_TN = 1536
OUTPROJ_TM = 256
ROUTE_TM = 512
MOE_TM = 256
FINAL_TM = 256
IDX_CHUNK = 1024
ISSUE_UNROLL = 8

F32 = jnp.float32
BF16 = jnp.bfloat16


def _cparams(sem):
    return pltpu.CompilerParams(dimension_semantics=sem, vmem_limit_bytes=VMEM_LIMIT)


def _silu(v):
    return v * jax.nn.sigmoid(v)


U32 = jnp.uint32
HI_MASK = 0xFFFF0000


def _pack_pairs(lo, hi):
    lo_bits = lax.bitcast_convert_type(lo.astype(BF16).astype(F32), U32) >> 16
    hi_bits = lax.bitcast_convert_type(hi.astype(BF16).astype(F32), U32) & U32(HI_MASK)
    return lo_bits | hi_bits


def _unpack_pairs(w):
    return (lax.bitcast_convert_type(w << 16, F32), lax.bitcast_convert_type(w & U32(HI_MASK), F32))


def _slab_rows_to_matrix(ref, base, tm, pitch):
    return jnp.concatenate([ref[pl.ds(base + s, tm, stride=pitch), :] for s in range(TOKEN_ROWS)], axis=-1)


def _layer_norm(y, g, b):
    mu = jnp.mean(y, axis=-1, keepdims=True)
    yc = y - mu
    var = jnp.mean(yc * yc, axis=-1, keepdims=True)
    return yc * lax.rsqrt(var + LN_EPS) * g + b


def _mod_kernel(c_ref, w_ref, b_ref, o_ref):
    cs = _silu(c_ref[...]).astype(BF16)
    o_ref[...] = jnp.dot(cs, w_ref[...].astype(BF16), preferred_element_type=F32) + b_ref[...]


def _mod(c8, w_mod, b_mod):
    d, n = w_mod.shape
    return pl.pallas_call(
        _mod_kernel,
        grid=(n // MOD_TN,),
        in_specs=[pl.BlockSpec((SUBLANES, d), lambda j: (0, 0)),
                  pl.BlockSpec((d, MOD_TN), lambda j: (0, j)),
                  pl.BlockSpec((1, MOD_TN), lambda j: (0, j))],
        out_specs=pl.BlockSpec((SUBLANES, MOD_TN), lambda j: (0, j)),
        out_shape=jax.ShapeDtypeStruct((SUBLANES, n), F32),
        compiler_params=_cparams(("arbitrary",)),
        name="mod",
    )(c8, w_mod, b_mod)


def _inproj_kernel(x_ref, mod_ref, w_ref, o_ref, h_ref):
    j = pl.program_id(1)

    @pl.when(j == 0)
    def _():
        m = mod_ref[0]
        h_ref[...] = (x_ref[...] * (1.0 + m[1:2]) + m[0:1]).astype(BF16)

    o_ref[...] = jnp.dot(h_ref[...], w_ref[...], preferred_element_type=F32).astype(BF16)


def _inproj(x2, mod3, w_in_bf, seq):
    n, d = x2.shape
    in_w = w_in_bf.shape[1]
    tm, tn = INPROJ_TM, INPROJ_TN
    return pl.pallas_call(
        _inproj_kernel,
        grid=(n // tm, in_w // tn),
        in_specs=[pl.BlockSpec((tm, d), lambda i, j: (i, 0)),
                  pl.BlockSpec((1, 6, d), lambda i, j: (i * tm // seq, 0, 0)),
                  pl.BlockSpec((d, tn), lambda i, j: (0, j))],
        out_specs=pl.BlockSpec((tm, tn), lambda i, j: (i, j)),
        out_shape=jax.ShapeDtypeStruct((n, in_w), BF16),
        scratch_shapes=[pltpu.VMEM((tm, d), BF16)],
        compiler_params=_cparams(("arbitrary", "arbitrary")),
        name="inproj",
    )(x2, mod3, w_in_bf)


def _mixer_kernel(cur_ref, pk_ref, pv_ref, prow_ref, sink_ref, cw_ref, o_ref, *, attn_w, kv_w, conv_w):
    nblk = pl.program_id(1)
    has_prev = nblk > 0
    qb = Q_BLOCK
    cur = cur_ref[...]
    k_cur = cur[:, attn_w:attn_w + kv_w]
    v_cur = cur[:, attn_w + kv_w:attn_w + 2 * kv_w]
    k_all = jnp.concatenate([pk_ref[...], k_cur], axis=0)
    v_all = jnp.concatenate([pv_ref[...], v_cur], axis=0)

    rows = GQA * qb
    qi = lax.broadcasted_iota(jnp.int32, (rows, 2 * qb), 0) % qb
    kj = lax.broadcasted_iota(jnp.int32, (rows, 2 * qb), 1)
    dist = qi + qb - kj
    kmin = jnp.where(has_prev, 0, qb)
    valid = (dist >= 0) & (dist < WINDOW) & (kj >= kmin)
    distf = dist.astype(F32)
    head_in_group = lax.broadcasted_iota(jnp.int32, (rows, 1), 0) // qb
    sinks = sink_ref[...]

    outs = []
    for g in range(N_KV_HEADS):
        q4 = jnp.concatenate(
            [cur[:, (g * GQA + j) * HEAD_DIM:(g * GQA + j + 1) * HEAD_DIM] for j in range(GQA)], axis=0)
        kg = k_all[:, g * HEAD_DIM:(g + 1) * HEAD_DIM]
        vg = v_all[:, g * HEAD_DIM:(g + 1) * HEAD_DIM]
        s = lax.dot_general(q4, kg, (((1,), (1,)), ((), ())), preferred_element_type=F32)
        s = s * (HEAD_DIM ** -0.5)
        slope = jnp.zeros((rows, 1), F32)
        sink = jnp.zeros((rows, 1), F32)
        for j in range(GQA):
            h = g * GQA + j
            sel = head_in_group == j
            slope = jnp.where(sel, 2.0 ** (-8.0 * (h + 1) / N_Q_HEADS), slope)
            sink = jnp.where(sel, sinks[:, h:h + 1], sink)
        s = jnp.where(valid, s - slope * distf, -jnp.inf)
        m = jnp.maximum(jnp.max(s, axis=-1, keepdims=True), sink)
        p = jnp.exp(s - m)
        denom = jnp.sum(p, axis=-1, keepdims=True) + jnp.exp(sink - m)
        o4 = jnp.dot(p.astype(BF16), vg, preferred_element_type=F32) / denom
        outs.extend(o4[j * qb:(j + 1) * qb] for j in range(GQA))
    attn = jnp.concatenate(outs, axis=-1)

    c0 = attn_w + 2 * kv_w
    cb = cur[:, c0:c0 + conv_w].astype(F32)
    u = cur[:, c0 + conv_w:c0 + 2 * conv_w].astype(F32) * cur[:, c0 + 2 * conv_w:c0 + 3 * conv_w].astype(F32)
    prow = prow_ref[...]
    up = prow[:, c0 + conv_w:c0 + 2 * conv_w].astype(F32) * prow[:, c0 + 2 * conv_w:c0 + 3 * conv_w].astype(F32)
    up = up * jnp.where(has_prev, 1.0, 0.0)
    pm1 = up[15:16]
    pm2 = up[14:15]
    ri = lax.broadcasted_iota(jnp.int32, u.shape, 0)
    u1 = jnp.where(ri == 0, pm1, pltpu.roll(u, 1, 0))
    u2 = jnp.where(ri == 0, pm2, jnp.where(ri == 1, pm1, pltpu.roll(u, 2, 0)))
    cw = cw_ref[...]
    conv = cb * (cw[0:1] * u2 + cw[1:2] * u1 + cw[2:3] * u)
    o_ref[...] = jnp.concatenate([attn, conv], axis=-1).astype(BF16)


def _mixer(proj, sinks2, conv_w, batch, seq, attn_w, kv_w, conv_wd):
    n, in_w = proj.shape
    nb = seq // Q_BLOCK
    kv_blk0 = attn_w // kv_w
    sub16 = Q_BLOCK // 16

    def cur_map(b, i):
        return (b * nb + i, 0)

    def prev_map(col):
        return lambda b, i: (b * nb + jnp.maximum(i - 1, 0), col)

    def prow_map(b, i):
        return (jnp.maximum((b * nb + i) * sub16 - 1, 0), 0)

    kern = functools.partial(_mixer_kernel, attn_w=attn_w, kv_w=kv_w, conv_w=conv_wd)
    return pl.pallas_call(
        kern,
        grid=(batch, nb),
        in_specs=[pl.BlockSpec((Q_BLOCK, in_w), cur_map),
                  pl.BlockSpec((Q_BLOCK, kv_w), prev_map(kv_blk0)),
                  pl.BlockSpec((Q_BLOCK, kv_w), prev_map(kv_blk0 + 1)),
                  pl.BlockSpec((16, in_w), prow_map),
                  pl.BlockSpec((1, N_Q_HEADS), lambda b, i: (0, 0)),
                  pl.BlockSpec((CONV_K, conv_wd), lambda b, i: (0, 0))],
        out_specs=pl.BlockSpec((Q_BLOCK, attn_w + conv_wd), cur_map),
        out_shape=jax.ShapeDtypeStruct((n, attn_w + conv_wd), BF16),
        compiler_params=_cparams(("arbitrary", "arbitrary")),
        name="mixer",
    )(proj, proj, proj, proj, sinks2, conv_w)


def _split_bf16(v):
    hi = v.astype(BF16)
    lo = (v - hi.astype(F32)).astype(BF16)
    return hi, lo


def _outproj_kernel(mix_ref, x_ref, mod_ref, w_ref, g_ref, b_ref, wr_ref, x1_ref, h2_ref, lg_ref):
    m = mod_ref[0]
    mix = jnp.dot(mix_ref[...], w_ref[...], preferred_element_type=F32)
    x1 = _layer_norm(ALPHA * x_ref[...] + (1.0 + m[2:3]) * mix, g_ref[...], b_ref[...])
    x1_ref[...] = x1
    h2 = x1 * (1.0 + m[4:5]) + m[3:4]
    tm, d = h2.shape
    words = _pack_pairs(h2[:, :d // 2], h2[:, d // 2:])
    for s in range(TOKEN_ROWS):
        h2_ref[pl.ds(s, tm, stride=TOKEN_ROWS), :] = words[:, s * LANES:(s + 1) * LANES]
    h_hi, h_lo = _split_bf16(h2)
    w_hi, w_lo = _split_bf16(wr_ref[...])
    lg_ref[...] = (jnp.dot(h_hi, w_hi, preferred_element_type=F32)
                   + (jnp.dot(h_hi, w_lo, preferred_element_type=F32)
                      + jnp.dot(h_lo, w_hi, preferred_element_type=F32)))


def _outproj(mix, x2, mod3, w_out_bf, ln_g, ln_b, w_router, seq):
    n, d = x2.shape
    tm = OUTPROJ_TM
    ne = w_router.shape[1]
    row = lambda i: (i, 0)
    const = lambda i: (0, 0)
    return pl.pallas_call(
        _outproj_kernel,
        grid=(n // tm,),
        in_specs=[pl.BlockSpec((tm, d), row),
                  pl.BlockSpec((tm, d), row),
                  pl.BlockSpec((1, 6, d), lambda i: (i * tm // seq, 0, 0)),
                  pl.BlockSpec((d, d), const),
                  pl.BlockSpec((1, d), const),
                  pl.BlockSpec((1, d), const),
                  pl.BlockSpec((d, ne), const)],
        out_specs=[pl.BlockSpec((tm, d), row),
                   pl.BlockSpec((tm * TOKEN_ROWS, LANES), row),
                   pl.BlockSpec((tm, ne), row)],
        out_shape=[jax.ShapeDtypeStruct((n, d), F32),
                   jax.ShapeDtypeStruct((n * TOKEN_ROWS, LANES), U32),
                   jax.ShapeDtypeStruct((n, ne), F32)],
        compiler_params=_cparams(("arbitrary",)),
        name="outproj",
    )(mix, x2, mod3, w_out_bf, ln_g, ln_b, w_router)


def _col_argmax(tiles, row_f):
    m = tiles[0]
    for t in tiles[1:]:
        m = jnp.maximum(m, t)
    m = jnp.max(m, axis=0, keepdims=True)
    idx = None
    for t, r in zip(tiles, row_f):
        c = jnp.where(t == m, r, float(N_EXPERTS))
        idx = c if idx is None else jnp.minimum(idx, c)
    return m, jnp.min(idx, axis=0, keepdims=True)


def _route_kernel(lg_ref, bias_ref, eidx_ref, w_ref, cnt_ref):
    lt = lg_ref[...].T
    tm = lt.shape[1]
    neg = -jnp.inf
    sub = lax.broadcasted_iota(jnp.int32, (GROUP_SIZE, tm), 0).astype(F32)
    row_f = [sub + float(g * GROUP_SIZE) for g in range(N_GROUPS)]
    scores = [jax.nn.sigmoid(lt[g * GROUP_SIZE:(g + 1) * GROUP_SIZE]) for g in range(N_GROUPS)]
    sel = [scores[g] + bias_ref[g * GROUP_SIZE:(g + 1) * GROUP_SIZE, :] for g in range(N_GROUPS)]
    gs = []
    for g in range(N_GROUPS):
        m1, i1 = _col_argmax([sel[g]], [row_f[g]])
        m2 = jnp.max(jnp.where(row_f[g] == i1, neg, sel[g]), axis=0, keepdims=True)
        gs.append(m1 + m2)
    cand = []
    for g in range(N_GROUPS):
        rank = jnp.zeros((1, tm), F32)
        for o in range(N_GROUPS):
            if o == g:
                continue
            ahead = (gs[o] >= gs[g]) if o < g else (gs[o] > gs[g])
            rank = rank + jnp.where(ahead, 1.0, 0.0)
        cand.append(jnp.where(rank < TOPK_GROUPS, sel[g], neg))
    idxs, ws = [], []
    chosen = [jnp.zeros((GROUP_SIZE, tm), F32) for _ in range(N_GROUPS)]
    for _ in range(TOP_K):
        _, ik = _col_argmax(cand, row_f)
        wk = jnp.zeros((1, tm), F32)
        for g in range(N_GROUPS):
            hit = row_f[g] == ik
            wk = wk + jnp.sum(jnp.where(hit, scores[g], 0.0), axis=0, keepdims=True)
            cand[g] = jnp.where(hit, neg, cand[g])
            chosen[g] = jnp.where(hit, 1.0, chosen[g])
        ws.append(wk)
        idxs.append(ik)
    for g in range(N_GROUPS):
        cnt_ref[0, g * GROUP_SIZE:(g + 1) * GROUP_SIZE, :] = jnp.sum(chosen[g], axis=1, keepdims=True)
    wsum = ws[0]
    for k in range(1, TOP_K):
        wsum = wsum + ws[k]
    eidx_ref[...] = jnp.concatenate(idxs, axis=0).astype(jnp.int32)
    w_ref[...] = jnp.concatenate([wk / wsum * ROUTED_SCALE for wk in ws], axis=0)


def _route(logits, bias_col):
    n, ne = logits.shape
    tm = ROUTE_TM
    col = lambda i: (0, i)
    return pl.pallas_call(
        _route_kernel,
        grid=(n // tm,),
        in_specs=[pl.BlockSpec((tm, ne), lambda i: (i, 0)), pl.BlockSpec((ne, 1), lambda i: (0, 0))],
        out_specs=[pl.BlockSpec((TOP_K, tm), col), pl.BlockSpec((TOP_K, tm), col),
                   pl.BlockSpec((1, ne, 1), lambda i: (i, 0, 0))],
        out_shape=[jax.ShapeDtypeStruct((TOP_K, n), jnp.int32), jax.ShapeDtypeStruct((TOP_K, n), F32),
                   jax.ShapeDtypeStruct((n // tm, ne, 1), F32)],
        compiler_params=_cparams(("arbitrary",)),
        name="route",
    )(logits, bias_col)


def _issue_rows(lo, hi, issue_one):
    n_full = (hi - lo) // ISSUE_UNROLL

    def chunk(c, carry):
        for u in range(ISSUE_UNROLL):
            issue_one(lo + c * ISSUE_UNROLL + u)
        return carry

    def tail(r, carry):
        issue_one(r)
        return carry

    lax.fori_loop(0, n_full, chunk, 0)
    lax.fori_loop(lo + n_full * ISSUE_UNROLL, hi, tail, 0)


def _hbm_slab(ref, row):
    return ref.at[pl.ds(pl.multiple_of(row * TOKEN_ROWS, TOKEN_ROWS), TOKEN_ROWS), :]


def _moe_kernel(be_ref, new_ref, nexte_ref, epar_ref, nused_ref, tgt_hbm, h_hbm, wg_hbm, wu_hbm, wd_hbm, ys_hbm,
                idx_s, xbuf, ybuf, wg_f, wu_f, wd_f, wg_s, wu_s, wd_s, sem_i, sem_g, sem_s, sem_w, *, n_tok):
    s = pl.program_id(0)
    tm = MOE_TM
    per = IDX_CHUNK // tm
    per_log2 = per.bit_length() - 1
    n_used = nused_ref[0]
    slab = TOKEN_ROWS
    buf_rows = tm * SLAB_PITCH
    dump_row0 = TOP_K * n_tok
    slot = s & 1

    def staged(buf, base, r):
        return buf.at[pl.ds(pl.multiple_of(base + r * SLAB_PITCH, SUBLANES), slab), :]

    def idx_copy(c):
        return pltpu.make_async_copy(
            tgt_hbm.at[pl.ds(pl.multiple_of(c * IDX_CHUNK, IDX_CHUNK), IDX_CHUNK)],
            idx_s.at[pl.ds(pl.multiple_of((c & 1) * IDX_CHUNK, IDX_CHUNK), IDX_CHUNK)], sem_i)

    def idx_base(b):
        return ((b >> per_log2) & 1) * IDX_CHUNK + (b & (per - 1)) * tm

    def weight_copies(e, p):
        return (pltpu.make_async_copy(wg_hbm.at[e], wg_f.at[p], sem_w.at[p]),
                pltpu.make_async_copy(wu_hbm.at[e], wu_f.at[p], sem_w.at[p]),
                pltpu.make_async_copy(wd_hbm.at[e], wd_f.at[p], sem_w.at[p]))

    def for_rows(inline, body):
        if inline:
            for r in range(tm):
                body(r, r % 2)
        else:
            def pair(c, carry):
                body(2 * c, 0)
                body(2 * c + 1, 1)
                return carry
            lax.fori_loop(0, tm // 2, pair, 0)

    def issue_gather(b, to_slot, inline):
        ibase = idx_base(b)
        xbase = to_slot * buf_rows

        def one(r, prio):
            tok = idx_s[ibase + r] & (n_tok - 1)
            pltpu.make_async_copy(_hbm_slab(h_hbm, tok), staged(xbuf, xbase, r),
                                  sem_g.at[to_slot]).start(priority=prio)
        for_rows(inline, one)

    def issue_scatter(b, from_slot, to_dump, inline):
        ibase = idx_base(b)
        ybase = from_slot * buf_rows

        def one(r, prio):
            tgt = jnp.where(to_dump, dump_row0 + tm + r, idx_s[ibase + r])
            pltpu.make_async_copy(staged(ybuf, ybase, r), _hbm_slab(ys_hbm, tgt),
                                  sem_s.at[from_slot]).start(priority=prio)
        for_rows(inline, one)

    def wait_gather(at_slot):
        v = xbuf.at[pl.ds(pl.multiple_of(at_slot * buf_rows, SUBLANES), buf_rows), :]
        pltpu.make_async_copy(h_hbm.at[pl.ds(0, buf_rows), :], v, sem_g.at[at_slot]).wait()

    def wait_scatter(at_slot):
        v = ybuf.at[pl.ds(pl.multiple_of(at_slot * buf_rows, SUBLANES), buf_rows), :]
        pltpu.make_async_copy(v, ys_hbm.at[pl.ds(0, buf_rows), :], sem_s.at[at_slot]).wait()

    @pl.when(s == 0)
    def _():
        ybuf[...] = jnp.zeros(ybuf.shape, U32)
        first = idx_copy(0)
        first.start()
        first.wait()
        init = pltpu.make_async_copy(ybuf, ys_hbm.at[pl.ds(pl.multiple_of(dump_row0 * slab, slab), 2 * buf_rows), :],
                                     sem_i)
        init.start()
        init.wait()
        issue_gather(0, 0, inline=False)
        for cp in weight_copies(be_ref[0], 0):
            cp.start()

    @pl.when(s < n_used)
    def _():
        @pl.when((s & (per - 1)) == 1)
        def _():
            idx_copy((s >> per_log2) + 1).start()

        @pl.when(((s + 1) & (per - 1)) == 0)
        def _():
            idx_copy((s + 1) >> per_log2).wait()

        @pl.when(new_ref[s] == 1)
        def _():
            p = epar_ref[s]

            @pl.when(nexte_ref[s] >= 0)
            def _():
                for cp in weight_copies(nexte_ref[s], 1 - p):
                    cp.start()

            for cp in weight_copies(be_ref[s], p):
                cp.wait()
            wg_s[...] = wg_f[p].astype(BF16)
            wu_s[...] = wu_f[p].astype(BF16)
            wd_s[...] = wd_f[p].astype(BF16)

        wait_gather(slot)

        @pl.when(s >= 1)
        def _():
            wait_scatter(slot)

        issue_gather(s + 1, 1 - slot, inline=True)
        issue_scatter(jnp.maximum(s - 1, 0), 1 - slot, s == 0, inline=True)
        base = slot * buf_rows
        x_lo, x_hi = _unpack_pairs(_slab_rows_to_matrix(xbuf, base, tm, SLAB_PITCH))
        x = jnp.concatenate([x_lo.astype(BF16), x_hi.astype(BF16)], axis=-1)
        gate = jnp.dot(x, wg_s[...], preferred_element_type=F32)
        up = jnp.dot(x, wu_s[...], preferred_element_type=F32)
        act = (_silu(gate) * up).astype(BF16)
        y = jnp.dot(act, wd_s[...], preferred_element_type=F32)
        half = slab * LANES
        words = _pack_pairs(y[:, :half], y[:, half:])
        for j in range(slab):
            ybuf[pl.ds(base + j, tm, stride=SLAB_PITCH), :] = words[:, j * LANES:(j + 1) * LANES]

    @pl.when(s == n_used)
    def _():
        last = (s - 1) & (per - 1)

        @pl.when((last == 1) | (last == 2))
        def _():
            idx_copy(((s - 1) >> per_log2) + 1).wait()

        wait_gather(slot)
        wait_scatter(slot)
        issue_scatter(s - 1, 1 - slot, False, inline=False)
        wait_scatter(1 - slot)


def _moe(block_e, block_new, block_nexte, block_epar, n_used, row_tgt, h2d, w_gate, w_up, w_down):
    n = h2d.shape[0] // TOKEN_ROWS
    assert n & (n - 1) == 0
    d = 2 * TOKEN_ROWS * LANES
    de = w_gate.shape[2]
    tm = MOE_TM
    n_blk = block_e.shape[0]
    grid_spec = pltpu.PrefetchScalarGridSpec(
        num_scalar_prefetch=5,
        grid=(n_blk,),
        in_specs=[pl.BlockSpec(memory_space=pl.ANY)] * 5,
        out_specs=pl.BlockSpec(memory_space=pl.ANY),
        scratch_shapes=[pltpu.SMEM((2 * IDX_CHUNK,), jnp.int32),
                        pltpu.VMEM((2 * tm * SLAB_PITCH, LANES), U32),
                        pltpu.VMEM((2 * tm * SLAB_PITCH, LANES), U32),
                        pltpu.VMEM((2, d, de), F32),
                        pltpu.VMEM((2, d, de), F32),
                        pltpu.VMEM((2, de, d), F32),
                        pltpu.VMEM((d, de), BF16),
                        pltpu.VMEM((d, de), BF16),
                        pltpu.VMEM((de, d), BF16),
                        pltpu.SemaphoreType.DMA,
                        pltpu.SemaphoreType.DMA((2,)),
                        pltpu.SemaphoreType.DMA((2,)),
                        pltpu.SemaphoreType.DMA((2,))])
    return pl.pallas_call(
        functools.partial(_moe_kernel, n_tok=n),
        grid_spec=grid_spec,
        out_shape=jax.ShapeDtypeStruct(((TOP_K * n + 2 * tm) * TOKEN_ROWS, LANES), U32),
        compiler_params=_cparams(("arbitrary",)),
        name="moe",
    )(block_e, block_new, block_nexte, block_epar, n_used, row_tgt, h2d, w_gate, w_up, w_down)


def _final_kernel(gw_hbm, *refs):
    ys_refs = refs[:TOP_K]
    h_ref, x1_ref, mod_ref, wg_ref, wu_ref, wd_ref, g_ref, b_ref, o_ref, gw_s, acc_ref, sem_i = refs[TOP_K:]
    i = pl.program_id(0)
    tm = x1_ref.shape[0]
    per_step = tm * TOP_K
    cp = pltpu.make_async_copy(gw_hbm.at[pl.ds(pl.multiple_of(i * per_step, per_step), per_step)], gw_s, sem_i)
    cp.start()
    cp.wait()

    hi_base = tm * SLAB_PITCH

    def combine(t, carry):
        rows = pl.ds(pl.multiple_of(t * TOKEN_ROWS, TOKEN_ROWS), TOKEN_ROWS)
        acc_lo = acc_hi = None
        for k in range(TOP_K):
            lo, hi = _unpack_pairs(ys_refs[k][rows, :])
            g = gw_s[t * TOP_K + k]
            acc_lo = g * lo if acc_lo is None else acc_lo + g * lo
            acc_hi = g * hi if acc_hi is None else acc_hi + g * hi
        acc_ref[pl.ds(pl.multiple_of(t * SLAB_PITCH, SUBLANES), TOKEN_ROWS), :] = acc_lo
        acc_ref[pl.ds(pl.multiple_of(hi_base + t * SLAB_PITCH, SUBLANES), TOKEN_ROWS), :] = acc_hi
        return carry

    lax.fori_loop(0, tm, combine, 0, unroll=4)
    moe = jnp.concatenate([_slab_rows_to_matrix(acc_ref, 0, tm, SLAB_PITCH),
                           _slab_rows_to_matrix(acc_ref, hi_base, tm, SLAB_PITCH)], axis=-1)
    h_lo, h_hi = _unpack_pairs(_slab_rows_to_matrix(h_ref, 0, tm, TOKEN_ROWS))
    h = jnp.concatenate([h_lo.astype(BF16), h_hi.astype(BF16)], axis=-1)
    gate = jnp.dot(h, wg_ref[...], preferred_element_type=F32)
    up = jnp.dot(h, wu_ref[...], preferred_element_type=F32)
    shared = jnp.dot((_silu(gate) * up).astype(BF16), wd_ref[...], preferred_element_type=F32)
    m = mod_ref[0]
    y = ALPHA * x1_ref[...] + (1.0 + m[5:6]) * (moe + shared)
    o_ref[...] = _layer_norm(y, g_ref[...], b_ref[...])


def _final(gate_w, ys, h2d, x1, mod3, wsg, wsu, wsd, ln_g, ln_b, seq):
    n, d = x1.shape
    de = wsg.shape[1]
    tm = FINAL_TM
    per_step = tm * TOP_K
    assert per_step % IDX_CHUNK == 0
    row = lambda i: (i, 0)
    const = lambda i: (0, 0)
    slot_rows = lambda k: (lambda i: (k * (n // tm) + i, 0))
    return pl.pallas_call(
        _final_kernel,
        grid=(n // tm,),
        in_specs=[pl.BlockSpec(memory_space=pl.ANY)]
                 + [pl.BlockSpec((tm * TOKEN_ROWS, LANES), slot_rows(k)) for k in range(TOP_K)]
                 + [pl.BlockSpec((tm * TOKEN_ROWS, LANES), row),
                  pl.BlockSpec((tm, d), row),
                  pl.BlockSpec((1, 6, d), lambda i: (i * tm // seq, 0, 0)),
                  pl.BlockSpec((d, de), const),
                  pl.BlockSpec((d, de), const),
                  pl.BlockSpec((de, d), const),
                  pl.BlockSpec((1, d), const),
                  pl.BlockSpec((1, d), const)],
        out_specs=pl.BlockSpec((tm, d), row),
        out_shape=jax.ShapeDtypeStruct((n, d), F32),
        scratch_shapes=[pltpu.SMEM((per_step,), F32),
                        pltpu.VMEM((2 * tm * SLAB_PITCH, LANES), F32),
                        pltpu.SemaphoreType.DMA],
        compiler_params=_cparams(("arbitrary",)),
        name="final",
    )(gate_w, *([ys] * TOP_K), h2d, x1, mod3, wsg, wsu, wsd, ln_g, ln_b)


def _dispatch_tables(eidx, tile_counts, n):
    tm = MOE_TM
    a = n * TOP_K
    i32 = jnp.int32
    experts = jnp.arange(N_EXPERTS, dtype=i32)
    counts = jnp.sum(tile_counts, axis=(0, 2)).astype(i32)
    padded = (counts + tm - 1) // tm * tm
    pad_end = jnp.cumsum(padded)
    starts = pad_end - padded
    n_blk = a // tm + N_EXPERTS + 1
    blk_start = jnp.arange(n_blk, dtype=i32) * tm
    n_used = pad_end[-1] // tm
    in_use = jnp.arange(n_blk) < n_used
    raw_e = jnp.minimum(jnp.sum((pad_end[None, :] <= blk_start[:, None]).astype(i32), axis=1), N_EXPERTS - 1)
    last_e = jnp.sum(jnp.where(jnp.arange(n_blk) == n_used - 1, raw_e, 0))
    block_e = jnp.where(in_use, raw_e, last_e)
    onehot = block_e[:, None] == experts[None, :]
    block_new = jnp.concatenate([jnp.ones((1,), i32), (block_e[1:] != block_e[:-1]).astype(i32)])
    has_rows = counts > 0
    later = (experts[None, :] > experts[:, None]) & has_rows[None, :]
    next_e = jnp.min(jnp.where(later, experts[None, :], N_EXPERTS), axis=1)
    next_e = jnp.where(next_e == N_EXPERTS, -1, next_e)
    parity_e = (jnp.cumsum(has_rows.astype(i32)) - 1) & 1
    block_nexte = jnp.sum(jnp.where(onehot, next_e[None, :], 0), axis=1).astype(i32)
    block_epar = jnp.sum(jnp.where(onehot, parity_e[None, :], 0), axis=1).astype(i32)
    cnt_b = jnp.sum(jnp.where(onehot, counts[None, :], 0), axis=1)
    start_b = jnp.sum(jnp.where(onehot, starts[None, :], 0), axis=1)
    block_nv = jnp.where(in_use, jnp.clip(cnt_b - (blk_start - start_b), 0, tm), 0)
    dummy_keys = jnp.where(jnp.arange(tm, dtype=i32)[None, :] < (padded - counts)[:, None],
                           experts[:, None], N_EXPERTS).reshape(-1)
    keys = jnp.concatenate([eidx.reshape(-1), dummy_keys, jnp.full((tm,), N_EXPERTS, i32)])
    tgt = jnp.arange(a, dtype=i32)
    bits = (a - 1).bit_length() + 1
    low = (1 << bits) - 1
    payload = jnp.concatenate([tgt, jnp.full((n_blk * tm - a,), low, i32)])
    row_tgt = lax.sort(keys * (1 << bits) + payload) & low
    r = jnp.arange(tm, dtype=i32)[None, :]
    dump = a + (jnp.arange(n_blk, dtype=i32)[:, None] & 1) * tm + r
    row_tgt = jnp.where(r >= block_nv[:, None], dump, row_tgt.reshape(n_blk, tm)).reshape(-1)
    row_tgt = jnp.pad(row_tgt, (0, -(n_blk * tm) % IDX_CHUNK))
    return block_e.astype(i32), block_new, block_nexte, block_epar, n_used.astype(i32).reshape(1), row_tgt


def kernel(x, c, w_mod, b_mod, w_in, conv_w, attn_sinks, w_out, ln1_g, ln1_b, w_router, router_bias,
           w_gate, w_up, w_down, ws_gate, ws_up, ws_down, ln2_g, ln2_b):
    b, s, d = x.shape
    n = b * s
    attn_w = N_Q_HEADS * HEAD_DIM
    kv_w = N_KV_HEADS * HEAD_DIM
    conv_wd = d - attn_w
    in_w = attn_w + 2 * kv_w + 3 * conv_wd
    x2 = x.reshape(n, d)
    c8 = jnp.zeros((SUBLANES, d), F32).at[:b].set(c)
    for l in range(DEPTH):
        mod = _mod(c8, w_mod[l], b_mod[l].reshape(1, -1))[:b]
        mod3 = mod.reshape(b, 6, d)
        proj = _inproj(x2, mod3, w_in[l].astype(BF16), s)
        mix = _mixer(proj, attn_sinks[l].reshape(1, -1), conv_w[l], b, s, attn_w, kv_w, conv_wd)
        x1, h2d, logits = _outproj(mix, x2, mod3, w_out[l].astype(BF16), ln1_g[l].reshape(1, -1),
                                   ln1_b[l].reshape(1, -1), w_router[l], s)
        eidx, gate_w, tile_counts = _route(logits, router_bias[l].reshape(-1, 1))
        block_e, block_new, block_nexte, block_epar, n_used, row_tgt = _dispatch_tables(eidx, tile_counts, n)
        ys = _moe(block_e, block_new, block_nexte, block_epar, n_used, row_tgt, h2d,
                  w_gate[l], w_up[l], w_down[l])
        x2 = _final(gate_w.T.reshape(-1), ys, h2d, x1, mod3,
                    ws_gate[l].astype(BF16), ws_up[l].astype(BF16), ws_down[l].astype(BF16),
                    ln2_g[l].reshape(1, -1), ln2_b[l].reshape(1, -1), s)
    return x2.reshape(b, s, d)
```

```python
import functools

import jax
import jax.numpy as jnp
from jax import lax
from jax.experimental import pallas as pl
from jax.experimental.pallas import tpu as pltpu

HEAD_DIM = 64
N_Q_HEADS = 16
N_KV_HEADS = 4
GQA = N_Q_HEADS // N_KV_HEADS
CONV_K = 3
WINDOW = 128
Q_BLOCK = 128
N_EXPERTS = 64
TOP_K = 8
N_GROUPS = 8
GROUP_SIZE = N_EXPERTS // N_GROUPS
TOPK_GROUPS = 4
ROUTED_SCALE = 2.5
DEPTH = 1
ALPHA = (2.0 * DEPTH) ** 0.25
LN_EPS = 1e-5

LANES = 128
SUBLANES = 8
TOKEN_ROWS = 8
SLAB_PITCH = 8
VMEM_LIMIT = 56 * 1024 * 1024

MOD_TN = 1024
INPROJ_TM = 512
INPROJ_TN = 1536
OUTPROJ_TM = 256
ROUTE_TM = 512
MOE_TM = 256
FINAL_TM = 256
IDX_CHUNK = 1024
ISSUE_UNROLL = 8

F32 = jnp.float32
BF16 = jnp.bfloat16


def _cparams(sem):
    return pltpu.CompilerParams(dimension_semantics=sem, vmem_limit_bytes=VMEM_LIMIT)


def _silu(v):
    return v * jax.nn.sigmoid(v)


U32 = jnp.uint32
HI_MASK = 0xFFFF0000


def _pack_pairs(lo, hi):
    lo_bits = lax.bitcast_convert_type(lo.astype(BF16).astype(F32), U32) >> 16
    hi_bits = lax.bitcast_convert_type(hi.astype(BF16).astype(F32), U32) & U32(HI_MASK)
    return lo_bits | hi_bits


def _unpack_pairs(w):
    return (lax.bitcast_convert_type(w << 16, F32), lax.bitcast_convert_type(w & U32(HI_MASK), F32))


def _slab_rows_to_matrix(ref, base, tm, pitch):
    return jnp.concatenate([ref[pl.ds(base + s, tm, stride=pitch), :] for s in range(TOKEN_ROWS)], axis=-1)


def _layer_norm(y, g, b):
    mu = jnp.mean(y, axis=-1, keepdims=True)
    yc = y - mu
    var = jnp.mean(yc * yc, axis=-1, keepdims=True)
    return yc * lax.rsqrt(var + LN_EPS) * g + b


def _mod_kernel(c_ref, w_ref, b_ref, o_ref):
    cs = _silu(c_ref[...]).astype(BF16)
    o_ref[...] = jnp.dot(cs, w_ref[...].astype(BF16), preferred_element_type=F32) + b_ref[...]


def _mod(c8, w_mod, b_mod):
    d, n = w_mod.shape
    return pl.pallas_call(
        _mod_kernel,
        grid=(n // MOD_TN,),
        in_specs=[pl.BlockSpec((SUBLANES, d), lambda j: (0, 0)),
                  pl.BlockSpec((d, MOD_TN), lambda j: (0, j)),
                  pl.BlockSpec((1, MOD_TN), lambda j: (0, j))],
        out_specs=pl.BlockSpec((SUBLANES, MOD_TN), lambda j: (0, j)),
        out_shape=jax.ShapeDtypeStruct((SUBLANES, n), F32),
        compiler_params=_cparams(("arbitrary",)),
        name="mod",
    )(c8, w_mod, b_mod)


def _inproj_kernel(x_ref, mod_ref, w_ref, o_ref, h_ref):
    j = pl.program_id(1)

    @pl.when(j == 0)
    def _():
        m = mod_ref[0]
        h_ref[...] = (x_ref[...] * (1.0 + m[1:2]) + m[0:1]).astype(BF16)

    o_ref[...] = jnp.dot(h_ref[...], w_ref[...], preferred_element_type=F32).astype(BF16)


def _inproj(x2, mod3, w_in_bf, seq):
    n, d = x2.shape
    in_w = w_in_bf.shape[1]
    tm, tn = INPROJ_TM, INPROJ_TN
    return pl.pallas_call(
        _inproj_kernel,
        grid=(n // tm, in_w // tn),
        in_specs=[pl.BlockSpec((tm, d), lambda i, j: (i, 0)),
                  pl.BlockSpec((1, 6, d), lambda i, j: (i * tm // seq, 0, 0)),
                  pl.BlockSpec((d, tn), lambda i, j: (0, j))],
        out_specs=pl.BlockSpec((tm, tn), lambda i, j: (i, j)),
        out_shape=jax.ShapeDtypeStruct((n, in_w), BF16),
        scratch_shapes=[pltpu.VMEM((tm, d), BF16)],
        compiler_params=_cparams(("arbitrary", "arbitrary")),
        name="inproj",
    )(x2, mod3, w_in_bf)


def _mixer_kernel(cur_ref, pk_ref, pv_ref, prow_ref, sink_ref, cw_ref, o_ref, *, attn_w, kv_w, conv_w):
    nblk = pl.program_id(1)
    has_prev = nblk > 0
    qb = Q_BLOCK
    cur = cur_ref[...]
    k_cur = cur[:, attn_w:attn_w + kv_w]
    v_cur = cur[:, attn_w + kv_w:attn_w + 2 * kv_w]
    k_all = jnp.concatenate([pk_ref[...], k_cur], axis=0)
    v_all = jnp.concatenate([pv_ref[...], v_cur], axis=0)

    rows = GQA * qb
    qi = lax.broadcasted_iota(jnp.int32, (rows, 2 * qb), 0) % qb
    kj = lax.broadcasted_iota(jnp.int32, (rows, 2 * qb), 1)
    dist = qi + qb - kj
    kmin = jnp.where(has_prev, 0, qb)
    valid = (dist >= 0) & (dist < WINDOW) & (kj >= kmin)
    distf = dist.astype(F32)
    head_in_group = lax.broadcasted_iota(jnp.int32, (rows, 1), 0) // qb
    sinks = sink_ref[...]

    outs = []
    for g in range(N_KV_HEADS):
        q4 = jnp.concatenate(
            [cur[:, (g * GQA + j) * HEAD_DIM:(g * GQA + j + 1) * HEAD_DIM] for j in range(GQA)], axis=0)
        kg = k_all[:, g * HEAD_DIM:(g + 1) * HEAD_DIM]
        vg = v_all[:, g * HEAD_DIM:(g + 1) * HEAD_DIM]
        s = lax.dot_general(q4, kg, (((1,), (1,)), ((), ())), preferred_element_type=F32)
        s = s * (HEAD_DIM ** -0.5)
        slope = jnp.zeros((rows, 1), F32)
        sink = jnp.zeros((rows, 1), F32)
        for j in range(GQA):
            h = g * GQA + j
            sel = head_in_group == j
            slope = jnp.where(sel, 2.0 ** (-8.0 * (h + 1) / N_Q_HEADS), slope)
            sink = jnp.where(sel, sinks[:, h:h + 1], sink)
        s = jnp.where(valid, s - slope * distf, -jnp.inf)
        m = jnp.maximum(jnp.max(s, axis=-1, keepdims=True), sink)
        p = jnp.exp(s - m)
        denom = jnp.sum(p, axis=-1, keepdims=True) + jnp.exp(sink - m)
        o4 = jnp.dot(p.astype(BF16), vg, preferred_element_type=F32) / denom
        outs.extend(o4[j * qb:(j + 1) * qb] for j in range(GQA))
    attn = jnp.concatenate(outs, axis=-1)

    c0 = attn_w + 2 * kv_w
    cb = cur[:, c0:c0 + conv_w].astype(F32)
    u = cur[:, c0 + conv_w:c0 + 2 * conv_w].astype(F32) * cur[:, c0 + 2 * conv_w:c0 + 3 * conv_w].astype(F32)
    prow = prow_ref[...]
    up = prow[:, c0 + conv_w:c0 + 2 * conv_w].astype(F32) * prow[:, c0 + 2 * conv_w:c0 + 3 * conv_w].astype(F32)
    up = up * jnp.where(has_prev, 1.0, 0.0)
    pm1 = up[15:16]
    pm2 = up[14:15]
    ri = lax.broadcasted_iota(jnp.int32, u.shape, 0)
    u1 = jnp.where(ri == 0, pm1, pltpu.roll(u, 1, 0))
    u2 = jnp.where(ri == 0, pm2, jnp.where(ri == 1, pm1, pltpu.roll(u, 2, 0)))
    cw = cw_ref[...]
    conv = cb * (cw[0:1] * u2 + cw[1:2] * u1 + cw[2:3] * u)
    o_ref[...] = jnp.concatenate([attn, conv], axis=-1).astype(BF16)


def _mixer(proj, sinks2, conv_w, batch, seq, attn_w, kv_w, conv_wd):
    n, in_w = proj.shape
    nb = seq // Q_BLOCK
    kv_blk0 = attn_w // kv_w
    sub16 = Q_BLOCK // 16

    def cur_map(b, i):
        return (b * nb + i, 0)

    def prev_map(col):
        return lambda b, i: (b * nb + jnp.maximum(i - 1, 0), col)

    def prow_map(b, i):
        return (jnp.maximum((b * nb + i) * sub16 - 1, 0), 0)

    kern = functools.partial(_mixer_kernel, attn_w=attn_w, kv_w=kv_w, conv_w=conv_wd)
    return pl.pallas_call(
        kern,
        grid=(batch, nb),
        in_specs=[pl.BlockSpec((Q_BLOCK, in_w), cur_map),
                  pl.BlockSpec((Q_BLOCK, kv_w), prev_map(kv_blk0)),
                  pl.BlockSpec((Q_BLOCK, kv_w), prev_map(kv_blk0 + 1)),
                  pl.BlockSpec((16, in_w), prow_map),
                  pl.BlockSpec((1, N_Q_HEADS), lambda b, i: (0, 0)),
                  pl.BlockSpec((CONV_K, conv_wd), lambda b, i: (0, 0))],
        out_specs=pl.BlockSpec((Q_BLOCK, attn_w + conv_wd), cur_map),
        out_shape=jax.ShapeDtypeStruct((n, attn_w + conv_wd), BF16),
        compiler_params=_cparams(("arbitrary", "arbitrary")),
        name="mixer",
    )(proj, proj, proj, proj, sinks2, conv_w)


def _split_bf16(v):
    hi = v.astype(BF16)
    lo = (v - hi.astype(F32)).astype(BF16)
    return hi, lo


def _outproj_kernel(mix_ref, x_ref, mod_ref, w_ref, g_ref, b_ref, wr_ref, x1_ref, h2_ref, lg_ref):
    m = mod_ref[0]
    mix = jnp.dot(mix_ref[...], w_ref[...], preferred_element_type=F32)
    x1 = _layer_norm(ALPHA * x_ref[...] + (1.0 + m[2:3]) * mix, g_ref[...], b_ref[...])
    x1_ref[...] = x1
    h2 = x1 * (1.0 + m[4:5]) + m[3:4]
    tm, d = h2.shape
    words = _pack_pairs(h2[:, :d // 2], h2[:, d // 2:])
    for s in range(TOKEN_ROWS):
        h2_ref[pl.ds(s, tm, stride=TOKEN_ROWS), :] = words[:, s * LANES:(s + 1) * LANES]
    h_hi, h_lo = _split_bf16(h2)
    w_hi, w_lo = _split_bf16(wr_ref[...])
    lg_ref[...] = (jnp.dot(h_hi, w_hi, preferred_element_type=F32)
                   + (jnp.dot(h_hi, w_lo, preferred_element_type=F32)
                      + jnp.dot(h_lo, w_hi, preferred_element_type=F32)))


def _outproj(mix, x2, mod3, w_out_bf, ln_g, ln_b, w_router, seq):
    n, d = x2.shape
    tm = OUTPROJ_TM
    ne = w_router.shape[1]
    row = lambda i: (i, 0)
    const = lambda i: (0, 0)
    return pl.pallas_call(
        _outproj_kernel,
        grid=(n // tm,),
        in_specs=[pl.BlockSpec((tm, d), row),
                  pl.BlockSpec((tm, d), row),
                  pl.BlockSpec((1, 6, d), lambda i: (i * tm // seq, 0, 0)),
                  pl.BlockSpec((d, d), const),
                  pl.BlockSpec((1, d), const),
                  pl.BlockSpec((1, d), const),
                  pl.BlockSpec((d, ne), const)],
        out_specs=[pl.BlockSpec((tm, d), row),
                   pl.BlockSpec((tm * TOKEN_ROWS, LANES), row),
                   pl.BlockSpec((tm, ne), row)],
        out_shape=[jax.ShapeDtypeStruct((n, d), F32),
                   jax.ShapeDtypeStruct((n * TOKEN_ROWS, LANES), U32),
                   jax.ShapeDtypeStruct((n, ne), F32)],
        compiler_params=_cparams(("arbitrary",)),
        name="outproj",
    )(mix, x2, mod3, w_out_bf, ln_g, ln_b, w_router)


def _col_argmax(tiles, row_f):
    m = tiles[0]
    for t in tiles[1:]:
        m = jnp.maximum(m, t)
    m = jnp.max(m, axis=0, keepdims=True)
    idx = None
    for t, r in zip(tiles, row_f):
        c = jnp.where(t == m, r, float(N_EXPERTS))
        idx = c if idx is None else jnp.minimum(idx, c)
    return m, jnp.min(idx, axis=0, keepdims=True)


def _route_kernel(lg_ref, bias_ref, eidx_ref, w_ref, cnt_ref):
    lt = lg_ref[...].T
    tm = lt.shape[1]
    neg = -jnp.inf
    sub = lax.broadcasted_iota(jnp.int32, (GROUP_SIZE, tm), 0).astype(F32)
    row_f = [sub + float(g * GROUP_SIZE) for g in range(N_GROUPS)]
    scores = [jax.nn.sigmoid(lt[g * GROUP_SIZE:(g + 1) * GROUP_SIZE]) for g in range(N_GROUPS)]
    sel = [scores[g] + bias_ref[g * GROUP_SIZE:(g + 1) * GROUP_SIZE, :] for g in range(N_GROUPS)]
    gs = []
    for g in range(N_GROUPS):
        m1, i1 = _col_argmax([sel[g]], [row_f[g]])
        m2 = jnp.max(jnp.where(row_f[g] == i1, neg, sel[g]), axis=0, keepdims=True)
        gs.append(m1 + m2)
    cand = []
    for g in range(N_GROUPS):
        rank = jnp.zeros((1, tm), F32)
        for o in range(N_GROUPS):
            if o == g:
                continue
            ahead = (gs[o] >= gs[g]) if o < g else (gs[o] > gs[g])
            rank = rank + jnp.where(ahead, 1.0, 0.0)
        cand.append(jnp.where(rank < TOPK_GROUPS, sel[g], neg))
    idxs, ws = [], []
    chosen = [jnp.zeros((GROUP_SIZE, tm), F32) for _ in range(N_GROUPS)]
    for _ in range(TOP_K):
        _, ik = _col_argmax(cand, row_f)
        wk = jnp.zeros((1, tm), F32)
        for g in range(N_GROUPS):
            hit = row_f[g] == ik
            wk = wk + jnp.sum(jnp.where(hit, scores[g], 0.0), axis=0, keepdims=True)
            cand[g] = jnp.where(hit, neg, cand[g])
            chosen[g] = jnp.where(hit, 1.0, chosen[g])
        ws.append(wk)
        idxs.append(ik)
    for g in range(N_GROUPS):
        cnt_ref[0, g * GROUP_SIZE:(g + 1) * GROUP_SIZE, :] = jnp.sum(chosen[g], axis=1, keepdims=True)
    wsum = ws[0]
    for k in range(1, TOP_K):
        wsum = wsum + ws[k]
    eidx_ref[...] = jnp.concatenate(idxs, axis=0).astype(jnp.int32)
    w_ref[...] = jnp.concatenate([wk / wsum * ROUTED_SCALE for wk in ws], axis=0)


def _route(logits, bias_col):
    n, ne = logits.shape
    tm = ROUTE_TM
    col = lambda i: (0, i)
    return pl.pallas_call(
        _route_kernel,
        grid=(n // tm,),
        in_specs=[pl.BlockSpec((tm, ne), lambda i: (i, 0)), pl.BlockSpec((ne, 1), lambda i: (0, 0))],
        out_specs=[pl.BlockSpec((TOP_K, tm), col), pl.BlockSpec((TOP_K, tm), col),
                   pl.BlockSpec((1, ne, 1), lambda i: (i, 0, 0))],
        out_shape=[jax.ShapeDtypeStruct((TOP_K, n), jnp.int32), jax.ShapeDtypeStruct((TOP_K, n), F32),
                   jax.ShapeDtypeStruct((n // tm, ne, 1), F32)],
        compiler_params=_cparams(("arbitrary",)),
        name="route",
    )(logits, bias_col)


def _issue_rows(lo, hi, issue_one):
    n_full = (hi - lo) // ISSUE_UNROLL

    def chunk(c, carry):
        for u in range(ISSUE_UNROLL):
            issue_one(lo + c * ISSUE_UNROLL + u)
        return carry

    def tail(r, carry):
        issue_one(r)
        return carry

    lax.fori_loop(0, n_full, chunk, 0)
    lax.fori_loop(lo + n_full * ISSUE_UNROLL, hi, tail, 0)


def _hbm_slab(ref, row):
    return ref.at[pl.ds(pl.multiple_of(row * TOKEN_ROWS, TOKEN_ROWS), TOKEN_ROWS), :]


def _moe_kernel(be_ref, new_ref, nexte_ref, epar_ref, nused_ref, tgt_hbm, h_hbm, wg_hbm, wu_hbm, wd_hbm, ys_hbm,
                idx_s, xbuf, ybuf, wg_f, wu_f, wd_f, wg_s, wu_s, wd_s, sem_i, sem_g, sem_s, sem_w, *, n_tok):
    s = pl.program_id(0)
    tm = MOE_TM
    per = IDX_CHUNK // tm
    per_log2 = per.bit_length() - 1
    n_used = nused_ref[0]
    slab = TOKEN_ROWS
    buf_rows = tm * SLAB_PITCH
    dump_row0 = TOP_K * n_tok
    slot = s & 1

    def staged(buf, base, r):
        return buf.at[pl.ds(pl.multiple_of(base + r * SLAB_PITCH, SUBLANES), slab), :]

    def idx_copy(c):
        return pltpu.make_async_copy(
            tgt_hbm.at[pl.ds(pl.multiple_of(c * IDX_CHUNK, IDX_CHUNK), IDX_CHUNK)],
            idx_s.at[pl.ds(pl.multiple_of((c & 1) * IDX_CHUNK, IDX_CHUNK), IDX_CHUNK)], sem_i)

    def idx_base(b):
        return ((b >> per_log2) & 1) * IDX_CHUNK + (b & (per - 1)) * tm

    def weight_copies(e, p):
        return (pltpu.make_async_copy(wg_hbm.at[e], wg_f.at[p], sem_w.at[p]),
                pltpu.make_async_copy(wu_hbm.at[e], wu_f.at[p], sem_w.at[p]),
                pltpu.make_async_copy(wd_hbm.at[e], wd_f.at[p], sem_w.at[p]))

    def for_rows(inline, body):
        if inline:
            for r in range(tm):
                body(r, r % 2)
        else:
            def pair(c, carry):
                body(2 * c, 0)
                body(2 * c + 1, 1)
                return carry
            lax.fori_loop(0, tm // 2, pair, 0)

    def issue_gather(b, to_slot, inline):
        ibase = idx_base(b)
        xbase = to_slot * buf_rows

        def one(r, prio):
            tok = idx_s[ibase + r] & (n_tok - 1)
            pltpu.make_async_copy(_hbm_slab(h_hbm, tok), staged(xbuf, xbase, r),
                                  sem_g.at[to_slot]).start(priority=prio)
        for_rows(inline, one)

    def issue_scatter(b, from_slot, to_dump, inline):
        ibase = idx_base(b)
        ybase = from_slot * buf_rows

        def one(r, prio):
            tgt = jnp.where(to_dump, dump_row0 + tm + r, idx_s[ibase + r])
            pltpu.make_async_copy(staged(ybuf, ybase, r), _hbm_slab(ys_hbm, tgt),
                                  sem_s.at[from_slot]).start(priority=prio)
        for_rows(inline, one)

    def wait_gather(at_slot):
        v = xbuf.at[pl.ds(pl.multiple_of(at_slot * buf_rows, SUBLANES), buf_rows), :]
        pltpu.make_async_copy(h_hbm.at[pl.ds(0, buf_rows), :], v, sem_g.at[at_slot]).wait()

    def wait_scatter(at_slot):
        v = ybuf.at[pl.ds(pl.multiple_of(at_slot * buf_rows, SUBLANES), buf_rows), :]
        pltpu.make_async_copy(v, ys_hbm.at[pl.ds(0, buf_rows), :], sem_s.at[at_slot]).wait()

    @pl.when(s == 0)
    def _():
        ybuf[...] = jnp.zeros(ybuf.shape, U32)
        first = idx_copy(0)
        first.start()
        first.wait()
        init = pltpu.make_async_copy(ybuf, ys_hbm.at[pl.ds(pl.multiple_of(dump_row0 * slab, slab), 2 * buf_rows), :],
                                     sem_i)
        init.start()
        init.wait()
        issue_gather(0, 0, inline=False)
        for cp in weight_copies(be_ref[0], 0):
            cp.start()

    @pl.when(s < n_used)
    def _():
        @pl.when((s & (per - 1)) == 1)
        def _():
            idx_copy((s >> per_log2) + 1).start()

        @pl.when(((s + 1) & (per - 1)) == 0)
        def _():
            idx_copy((s + 1) >> per_log2).wait()

        @pl.when(new_ref[s] == 1)
        def _():
            p = epar_ref[s]

            @pl.when(nexte_ref[s] >= 0)
            def _():
                for cp in weight_copies(nexte_ref[s], 1 - p):
                    cp.start()

            for cp in weight_copies(be_ref[s], p):
                cp.wait()
            wg_s[...] = wg_f[p].astype(BF16)
            wu_s[...] = wu_f[p].astype(BF16)
            wd_s[...] = wd_f[p].astype(BF16)

        wait_gather(slot)

        @pl.when(s >= 1)
        def _():
            wait_scatter(slot)

        base = slot * buf_rows
        x_lo, x_hi = _unpack_pairs(_slab_rows_to_matrix(xbuf, base, tm, SLAB_PITCH))
        x = jnp.concatenate([x_lo.astype(BF16), x_hi.astype(BF16)], axis=-1)
        issue_gather(s + 1, 1 - slot, inline=True)
        issue_scatter(jnp.maximum(s - 1, 0), 1 - slot, s == 0, inline=True)
        gate = jnp.dot(x, wg_s[...], preferred_element_type=F32)
        up = jnp.dot(x, wu_s[...], preferred_element_type=F32)
        act = (_silu(gate) * up).astype(BF16)
        y = jnp.dot(act, wd_s[...], preferred_element_type=F32)
        half = slab * LANES
        words = _pack_pairs(y[:, :half], y[:, half:])
        for j in range(slab):
            ybuf[pl.ds(base + j, tm, stride=SLAB_PITCH), :] = words[:, j * LANES:(j + 1) * LANES]

    @pl.when(s == n_used)
    def _():
        last = (s - 1) & (per - 1)

        @pl.when((last == 1) | (last == 2))
        def _():
            idx_copy(((s - 1) >> per_log2) + 1).wait()

        wait_gather(slot)
        wait_scatter(slot)
        issue_scatter(s - 1, 1 - slot, False, inline=False)
        wait_scatter(1 - slot)


def _moe(block_e, block_new, block_nexte, block_epar, n_used, row_tgt, h2d, w_gate, w_up, w_down):
    n = h2d.shape[0] // TOKEN_ROWS
    assert n & (n - 1) == 0
    d = 2 * TOKEN_ROWS * LANES
    de = w_gate.shape[2]
    tm = MOE_TM
    n_blk = block_e.shape[0]
    grid_spec = pltpu.PrefetchScalarGridSpec(
        num_scalar_prefetch=5,
        grid=(n_blk,),
        in_specs=[pl.BlockSpec(memory_space=pl.ANY)] * 5,
        out_specs=pl.BlockSpec(memory_space=pl.ANY),
        scratch_shapes=[pltpu.SMEM((2 * IDX_CHUNK,), jnp.int32),
                        pltpu.VMEM((2 * tm * SLAB_PITCH, LANES), U32),
                        pltpu.VMEM((2 * tm * SLAB_PITCH, LANES), U32),
                        pltpu.VMEM((2, d, de), F32),
                        pltpu.VMEM((2, d, de), F32),
                        pltpu.VMEM((2, de, d), F32),
                        pltpu.VMEM((d, de), BF16),
                        pltpu.VMEM((d, de), BF16),
                        pltpu.VMEM((de, d), BF16),
                        pltpu.SemaphoreType.DMA,
                        pltpu.SemaphoreType.DMA((2,)),
                        pltpu.SemaphoreType.DMA((2,)),
                        pltpu.SemaphoreType.DMA((2,))])
    return pl.pallas_call(
        functools.partial(_moe_kernel, n_tok=n),
        grid_spec=grid_spec,
        out_shape=jax.ShapeDtypeStruct(((TOP_K * n + 2 * tm) * TOKEN_ROWS, LANES), U32),
        compiler_params=_cparams(("arbitrary",)),
        name="moe",
    )(block_e, block_new, block_nexte, block_epar, n_used, row_tgt, h2d, w_gate, w_up, w_down)


def _final_kernel(gw_hbm, *refs):
    ys_refs = refs[:TOP_K]
    h_ref, x1_ref, mod_ref, wg_ref, wu_ref, wd_ref, g_ref, b_ref, o_ref, gw_s, acc_ref, sem_i = refs[TOP_K:]
    i = pl.program_id(0)
    tm = x1_ref.shape[0]
    per_step = tm * TOP_K
    cp = pltpu.make_async_copy(gw_hbm.at[pl.ds(pl.multiple_of(i * per_step, per_step), per_step)], gw_s, sem_i)
    cp.start()
    cp.wait()

    hi_base = tm * SLAB_PITCH

    def combine(t, carry):
        rows = pl.ds(pl.multiple_of(t * TOKEN_ROWS, TOKEN_ROWS), TOKEN_ROWS)
        acc_lo = acc_hi = None
        for k in range(TOP_K):
            lo, hi = _unpack_pairs(ys_refs[k][rows, :])
            g = gw_s[t * TOP_K + k]
            acc_lo = g * lo if acc_lo is None else acc_lo + g * lo
            acc_hi = g * hi if acc_hi is None else acc_hi + g * hi
        acc_ref[pl.ds(pl.multiple_of(t * SLAB_PITCH, SUBLANES), TOKEN_ROWS), :] = acc_lo
        acc_ref[pl.ds(pl.multiple_of(hi_base + t * SLAB_PITCH, SUBLANES), TOKEN_ROWS), :] = acc_hi
        return carry

    lax.fori_loop(0, tm, combine, 0, unroll=4)
    moe = jnp.concatenate([_slab_rows_to_matrix(acc_ref, 0, tm, SLAB_PITCH),
                           _slab_rows_to_matrix(acc_ref, hi_base, tm, SLAB_PITCH)], axis=-1)
    h_lo, h_hi = _unpack_pairs(_slab_rows_to_matrix(h_ref, 0, tm, TOKEN_ROWS))
    h = jnp.concatenate([h_lo.astype(BF16), h_hi.astype(BF16)], axis=-1)
    gate = jnp.dot(h, wg_ref[...], preferred_element_type=F32)
    up = jnp.dot(h, wu_ref[...], preferred_element_type=F32)
    shared = jnp.dot((_silu(gate) * up).astype(BF16), wd_ref[...], preferred_element_type=F32)
    m = mod_ref[0]
    y = ALPHA * x1_ref[...] + (1.0 + m[5:6]) * (moe + shared)
    o_ref[...] = _layer_norm(y, g_ref[...], b_ref[...])


def _final(gate_w, ys, h2d, x1, mod3, wsg, wsu, wsd, ln_g, ln_b, seq):
    n, d = x1.shape
    de = wsg.shape[1]
    tm = FINAL_TM
    per_step = tm * TOP_K
    assert per_step % IDX_CHUNK == 0
    row = lambda i: (i, 0)
    const = lambda i: (0, 0)
    slot_rows = lambda k: (lambda i: (k * (n // tm) + i, 0))
    return pl.pallas_call(
        _final_kernel,
        grid=(n // tm,),
        in_specs=[pl.BlockSpec(memory_space=pl.ANY)]
                 + [pl.BlockSpec((tm * TOKEN_ROWS, LANES), slot_rows(k)) for k in range(TOP_K)]
                 + [pl.BlockSpec((tm * TOKEN_ROWS, LANES), row),
                  pl.BlockSpec((tm, d), row),
                  pl.BlockSpec((1, 6, d), lambda i: (i * tm // seq, 0, 0)),
                  pl.BlockSpec((d, de), const),
                  pl.BlockSpec((d, de), const),
                  pl.BlockSpec((de, d), const),
                  pl.BlockSpec((1, d), const),
                  pl.BlockSpec((1, d), const)],
        out_specs=pl.BlockSpec((tm, d), row),
        out_shape=jax.ShapeDtypeStruct((n, d), F32),
        scratch_shapes=[pltpu.SMEM((per_step,), F32),
                        pltpu.VMEM((2 * tm * SLAB_PITCH, LANES), F32),
                        pltpu.SemaphoreType.DMA],
        compiler_params=_cparams(("arbitrary",)),
        name="final",
    )(gate_w, *([ys] * TOP_K), h2d, x1, mod3, wsg, wsu, wsd, ln_g, ln_b)


def _dispatch_tables(eidx, tile_counts, n):
    tm = MOE_TM
    a = n * TOP_K
    i32 = jnp.int32
    experts = jnp.arange(N_EXPERTS, dtype=i32)
    counts = jnp.sum(tile_counts, axis=(0, 2)).astype(i32)
    padded = (counts + tm - 1) // tm * tm
    pad_end = jnp.cumsum(padded)
    starts = pad_end - padded
    n_blk = a // tm + N_EXPERTS + 1
    blk_start = jnp.arange(n_blk, dtype=i32) * tm
    n_used = pad_end[-1] // tm
    in_use = jnp.arange(n_blk) < n_used
    raw_e = jnp.minimum(jnp.sum((pad_end[None, :] <= blk_start[:, None]).astype(i32), axis=1), N_EXPERTS - 1)
    last_e = jnp.sum(jnp.where(jnp.arange(n_blk) == n_used - 1, raw_e, 0))
    block_e = jnp.where(in_use, raw_e, last_e)
    onehot = block_e[:, None] == experts[None, :]
    block_new = jnp.concatenate([jnp.ones((1,), i32), (block_e[1:] != block_e[:-1]).astype(i32)])
    has_rows = counts > 0
    later = (experts[None, :] > experts[:, None]) & has_rows[None, :]
    next_e = jnp.min(jnp.where(later, experts[None, :], N_EXPERTS), axis=1)
    next_e = jnp.where(next_e == N_EXPERTS, -1, next_e)
    parity_e = (jnp.cumsum(has_rows.astype(i32)) - 1) & 1
    block_nexte = jnp.sum(jnp.where(onehot, next_e[None, :], 0), axis=1).astype(i32)
    block_epar = jnp.sum(jnp.where(onehot, parity_e[None, :], 0), axis=1).astype(i32)
    cnt_b = jnp.sum(jnp.where(onehot, counts[None, :], 0), axis=1)
    start_b = jnp.sum(jnp.where(onehot, starts[None, :], 0), axis=1)
    block_nv = jnp.where(in_use, jnp.clip(cnt_b - (blk_start - start_b), 0, tm), 0)
    dummy_keys = jnp.where(jnp.arange(tm, dtype=i32)[None, :] < (padded - counts)[:, None],
                           experts[:, None], N_EXPERTS).reshape(-1)
    keys = jnp.concatenate([eidx.reshape(-1), dummy_keys, jnp.full((tm,), N_EXPERTS, i32)])
    tgt = jnp.arange(a, dtype=i32)
    bits = (a - 1).bit_length() + 1
    low = (1 << bits) - 1
    payload = jnp.concatenate([tgt, jnp.full((n_blk * tm - a,), low, i32)])
    row_tgt = lax.sort(keys * (1 << bits) + payload) & low
    r = jnp.arange(tm, dtype=i32)[None, :]
    dump = a + (jnp.arange(n_blk, dtype=i32)[:, None] & 1) * tm + r
    row_tgt = jnp.where(r >= block_nv[:, None], dump, row_tgt.reshape(n_blk, tm)).reshape(-1)
    row_tgt = jnp.pad(row_tgt, (0, -(n_blk * tm) % IDX_CHUNK))
    return block_e.astype(i32), block_new, block_nexte, block_epar, n_used.astype(i32).reshape(1), row_tgt


def kernel(x, c, w_mod, b_mod, w_in, conv_w, attn_sinks, w_out, ln1_g, ln1_b, w_router, router_bias,
           w_gate, w_up, w_down, ws_gate, ws_up, ws_down, ln2_g, ln2_b):
    b, s, d = x.shape
    n = b * s
    attn_w = N_Q_HEADS * HEAD_DIM
    kv_w = N_KV_HEADS * HEAD_DIM
    conv_wd = d - attn_w
    in_w = attn_w + 2 * kv_w + 3 * conv_wd
    x2 = x.reshape(n, d)
    c8 = jnp.zeros((SUBLANES, d), F32).at[:b].set(c)
    for l in range(DEPTH):
        mod = _mod(c8, w_mod[l], b_mod[l].reshape(1, -1))[:b]
        mod3 = mod.reshape(b, 6, d)
        proj = _inproj(x2, mod3, w_in[l].astype(BF16), s)
        mix = _mixer(proj, attn_sinks[l].reshape(1, -1), conv_w[l], b, s, attn_w, kv_w, conv_wd)
        x1, h2d, logits = _outproj(mix, x2, mod3, w_out[l].astype(BF16), ln1_g[l].reshape(1, -1),
                                   ln1_b[l].reshape(1, -1), w_router[l], s)
        eidx, gate_w, tile_counts = _route(logits, router_bias[l].reshape(-1, 1))
        block_e, block_new, block_nexte, block_epar, n_used, row_tgt = _dispatch_tables(eidx, tile_counts, n)
        ys = _moe(block_e, block_new, block_nexte, block_epar, n_used, row_tgt, h2d,
                  w_gate[l], w_up[l], w_down[l])
        x2 = _final(gate_w.T.reshape(-1), ys, h2d, x1, mod3,
                    ws_gate[l].astype(BF16), ws_up[l].astype(BF16), ws_down[l].astype(BF16),
                    ln2_g[l].reshape(1, -1), ln2_b[l].reshape(1, -1), s)
    return x2.reshape(b, s, d)
```

```python
import functools

import jax
import jax.numpy as jnp
from jax import lax
from jax.experimental import pallas as pl
from jax.experimental.pallas import tpu as pltpu

HEAD_DIM = 64
N_Q_HEADS = 16
N_KV_HEADS = 4
GQA = N_Q_HEADS // N_KV_HEADS
CONV_K = 3
WINDOW = 128
Q_BLOCK = 128
N_EXPERTS = 64
TOP_K = 8
N_GROUPS = 8
GROUP_SIZE = N_EXPERTS // N_GROUPS
TOPK_GROUPS = 4
ROUTED_SCALE = 2.5
DEPTH = 1
ALPHA = (2.0 * DEPTH) ** 0.25
LN_EPS = 1e-5

LANES = 128
SUBLANES = 8
TOKEN_ROWS = 8
SLAB_PITCH = 8
VMEM_LIMIT = 56 * 1024 * 1024

MOD_TN = 1024
INPROJ_TM = 512
INPROJ_TN = 1536
OUTPROJ_TM = 256
ROUTE_TM = 512
MOE_TM = 256
FINAL_TM = 256
IDX_CHUNK = 1024
ISSUE_UNROLL = 8

F32 = jnp.float32
BF16 = jnp.bfloat16


def _cparams(sem):
    return pltpu.CompilerParams(dimension_semantics=sem, vmem_limit_bytes=VMEM_LIMIT)


def _silu(v):
    return v * jax.nn.sigmoid(v)


U32 = jnp.uint32
HI_MASK = 0xFFFF0000


def _pack_pairs(lo, hi):
    lo_bits = lax.bitcast_convert_type(lo.astype(BF16).astype(F32), U32) >> 16
    hi_bits = lax.bitcast_convert_type(hi.astype(BF16).astype(F32), U32) & U32(HI_MASK)
    return lo_bits | hi_bits


def _unpack_pairs(w):
    return (lax.bitcast_convert_type(w << 16, F32), lax.bitcast_convert_type(w & U32(HI_MASK), F32))


def _slab_rows_to_matrix(ref, base, tm, pitch):
    return jnp.concatenate([ref[pl.ds(base + s, tm, stride=pitch), :] for s in range(TOKEN_ROWS)], axis=-1)


def _layer_norm(y, g, b):
    mu = jnp.mean(y, axis=-1, keepdims=True)
    yc = y - mu
    var = jnp.mean(yc * yc, axis=-1, keepdims=True)
    return yc * lax.rsqrt(var + LN_EPS) * g + b


def _mod_kernel(c_ref, w_ref, b_ref, o_ref):
    cs = _silu(c_ref[...]).astype(BF16)
    o_ref[...] = jnp.dot(cs, w_ref[...].astype(BF16), preferred_element_type=F32) + b_ref[...]


def _mod(c8, w_mod, b_mod):
    d, n = w_mod.shape
    return pl.pallas_call(
        _mod_kernel,
        grid=(n // MOD_TN,),
        in_specs=[pl.BlockSpec((SUBLANES, d), lambda j: (0, 0)),
                  pl.BlockSpec((d, MOD_TN), lambda j: (0, j)),
                  pl.BlockSpec((1, MOD_TN), lambda j: (0, j))],
        out_specs=pl.BlockSpec((SUBLANES, MOD_TN), lambda j: (0, j)),
        out_shape=jax.ShapeDtypeStruct((SUBLANES, n), F32),
        compiler_params=_cparams(("arbitrary",)),
        name="mod",
    )(c8, w_mod, b_mod)


def _inproj_kernel(x_ref, mod_ref, w_ref, o_ref, h_ref):
    j = pl.program_id(1)

    @pl.when(j == 0)
    def _():
        m = mod_ref[0]
        h_ref[...] = (x_ref[...] * (1.0 + m[1:2]) + m[0:1]).astype(BF16)

    o_ref[...] = jnp.dot(h_ref[...], w_ref[...], preferred_element_type=F32).astype(BF16)


def _inproj(x2, mod3, w_in_bf, seq):
    n, d = x2.shape
    in_w = w_in_bf.shape[1]
    tm, tn = INPROJ_TM, INPROJ_TN
    return pl.pallas_call(
        _inproj_kernel,
        grid=(n // tm, in_w // tn),
        in_specs=[pl.BlockSpec((tm, d), lambda i, j: (i, 0)),
                  pl.BlockSpec((1, 6, d), lambda i, j: (i * tm // seq, 0, 0)),
                  pl.BlockSpec((d, tn), lambda i, j: (0, j))],
        out_specs=pl.BlockSpec((tm, tn), lambda i, j: (i, j)),
        out_shape=jax.ShapeDtypeStruct((n, in_w), BF16),
        scratch_shapes=[pltpu.VMEM((tm, d), BF16)],
        compiler_params=_cparams(("arbitrary", "arbitrary")),
        name="inproj",
    )(x2, mod3, w_in_bf)


def _mixer_kernel(cur_ref, pk_ref, pv_ref, prow_ref, sink_ref, cw_ref, o_ref, bias_ref, *, attn_w, kv_w, conv_w):
    nblk = pl.program_id(1)
    has_prev = nblk > 0
    qb = Q_BLOCK
    rows = GQA * qb

    @pl.when((pl.program_id(0) == 0) & (nblk == 0))
    def _():
        qi = lax.broadcasted_iota(jnp.int32, (qb, 2 * qb), 0)
        kj = lax.broadcasted_iota(jnp.int32, (qb, 2 * qb), 1)
        dist = qi + qb - kj
        window = (dist >= 0) & (dist < WINDOW)
        distf = dist.astype(F32)
        for h in range(N_Q_HEADS):
            b1 = jnp.where(window, distf * -(2.0 ** (-8.0 * (h + 1) / N_Q_HEADS)), -jnp.inf)
            bias_ref[1, h * qb:(h + 1) * qb, :] = b1
            bias_ref[0, h * qb:(h + 1) * qb, :] = jnp.where(kj >= qb, b1, -jnp.inf)

    table = jnp.where(has_prev, 1, 0)
    cur = cur_ref[...]
    k_cur = cur[:, attn_w:attn_w + kv_w]
    v_cur = cur[:, attn_w + kv_w:attn_w + 2 * kv_w]
    k_all = jnp.concatenate([pk_ref[...], k_cur], axis=0)
    v_all = jnp.concatenate([pv_ref[...], v_cur], axis=0)

    outs = []
    for g in range(N_KV_HEADS):
        q4 = jnp.concatenate(
            [cur[:, (g * GQA + j) * HEAD_DIM:(g * GQA + j + 1) * HEAD_DIM] for j in range(GQA)], axis=0)
        kg = k_all[:, g * HEAD_DIM:(g + 1) * HEAD_DIM]
        vg = v_all[:, g * HEAD_DIM:(g + 1) * HEAD_DIM]
        s = lax.dot_general(q4, kg, (((1,), (1,)), ((), ())), preferred_element_type=F32)
        s = s * (HEAD_DIM ** -0.5) + bias_ref[table, g * rows:(g + 1) * rows, :]
        sink = sink_ref[g * rows:(g + 1) * rows, :]
        m = jnp.maximum(jnp.max(s, axis=-1, keepdims=True), sink)
        p = jnp.exp(s - m)
        denom = jnp.sum(p, axis=-1, keepdims=True) + jnp.exp(sink - m)
        o4 = jnp.dot(p.astype(BF16), vg, preferred_element_type=F32) / denom
        outs.extend(o4[j * qb:(j + 1) * qb] for j in range(GQA))
    attn = jnp.concatenate(outs, axis=-1)

    c0 = attn_w + 2 * kv_w
    cb = cur[:, c0:c0 + conv_w].astype(F32)
    u = cur[:, c0 + conv_w:c0 + 2 * conv_w].astype(F32) * cur[:, c0 + 2 * conv_w:c0 + 3 * conv_w].astype(F32)
    prow = prow_ref[...]
    up = prow[:, c0 + conv_w:c0 + 2 * conv_w].astype(F32) * prow[:, c0 + 2 * conv_w:c0 + 3 * conv_w].astype(F32)
    up = up * jnp.where(has_prev, 1.0, 0.0)
    pm1 = up[15:16]
    pm2 = up[14:15]
    ri = lax.broadcasted_iota(jnp.int32, u.shape, 0)
    u1 = jnp.where(ri == 0, pm1, pltpu.roll(u, 1, 0))
    u2 = jnp.where(ri == 0, pm2, jnp.where(ri == 1, pm1, pltpu.roll(u, 2, 0)))
    cw = cw_ref[...]
    conv = cb * (cw[0:1] * u2 + cw[1:2] * u1 + cw[2:3] * u)
    o_ref[...] = jnp.concatenate([attn, conv], axis=-1).astype(BF16)


def _mixer(proj, sinks2, conv_w, batch, seq, attn_w, kv_w, conv_wd):
    n, in_w = proj.shape
    nb = seq // Q_BLOCK
    kv_blk0 = attn_w // kv_w
    sub16 = Q_BLOCK // 16

    def cur_map(b, i):
        return (b * nb + i, 0)

    def prev_map(col):
        return lambda b, i: (b * nb + jnp.maximum(i - 1, 0), col)

    def prow_map(b, i):
        return (jnp.maximum((b * nb + i) * sub16 - 1, 0), 0)

    kern = functools.partial(_mixer_kernel, attn_w=attn_w, kv_w=kv_w, conv_w=conv_wd)
    return pl.pallas_call(
        kern,
        grid=(batch, nb),
        in_specs=[pl.BlockSpec((Q_BLOCK, in_w), cur_map),
                  pl.BlockSpec((Q_BLOCK, kv_w), prev_map(kv_blk0)),
                  pl.BlockSpec((Q_BLOCK, kv_w), prev_map(kv_blk0 + 1)),
                  pl.BlockSpec((16, in_w), prow_map),
                  pl.BlockSpec((N_Q_HEADS * Q_BLOCK, 1), lambda b, i: (0, 0)),
                  pl.BlockSpec((CONV_K, conv_wd), lambda b, i: (0, 0))],
        out_specs=pl.BlockSpec((Q_BLOCK, attn_w + conv_wd), cur_map),
        out_shape=jax.ShapeDtypeStruct((n, attn_w + conv_wd), BF16),
        scratch_shapes=[pltpu.VMEM((2, N_Q_HEADS * Q_BLOCK, 2 * Q_BLOCK), F32)],
        compiler_params=_cparams(("arbitrary", "arbitrary")),
        name="mixer",
    )(proj, proj, proj, proj, sinks2, conv_w)


def _split_bf16(v):
    hi = v.astype(BF16)
    lo = (v - hi.astype(F32)).astype(BF16)
    return hi, lo


def _outproj_kernel(mix_ref, x_ref, mod_ref, w_ref, g_ref, b_ref, wr_ref, x1_ref, h2_ref, lg_ref):
    m = mod_ref[0]
    mix = jnp.dot(mix_ref[...], w_ref[...], preferred_element_type=F32)
    x1 = _layer_norm(ALPHA * x_ref[...] + (1.0 + m[2:3]) * mix, g_ref[...], b_ref[...])
    x1_ref[...] = x1
    h2 = x1 * (1.0 + m[4:5]) + m[3:4]
    tm, d = h2.shape
    words = _pack_pairs(h2[:, :d // 2], h2[:, d // 2:])
    for s in range(TOKEN_ROWS):
        h2_ref[pl.ds(s, tm, stride=TOKEN_ROWS), :] = words[:, s * LANES:(s + 1) * LANES]
    h_hi, h_lo = _split_bf16(h2)
    w_hi, w_lo = _split_bf16(wr_ref[...])
    lg_ref[...] = (jnp.dot(h_hi, w_hi, preferred_element_type=F32)
                   + (jnp.dot(h_hi, w_lo, preferred_element_type=F32)
                      + jnp.dot(h_lo, w_hi, preferred_element_type=F32)))


def _outproj(mix, x2, mod3, w_out_bf, ln_g, ln_b, w_router, seq):
    n, d = x2.shape
    tm = OUTPROJ_TM
    ne = w_router.shape[1]
    row = lambda i: (i, 0)
    const = lambda i: (0, 0)
    return pl.pallas_call(
        _outproj_kernel,
        grid=(n // tm,),
        in_specs=[pl.BlockSpec((tm, d), row),
                  pl.BlockSpec((tm, d), row),
                  pl.BlockSpec((1, 6, d), lambda i: (i * tm // seq, 0, 0)),
                  pl.BlockSpec((d, d), const),
                  pl.BlockSpec((1, d), const),
                  pl.BlockSpec((1, d), const),
                  pl.BlockSpec((d, ne), const)],
        out_specs=[pl.BlockSpec((tm, d), row),
                   pl.BlockSpec((tm * TOKEN_ROWS, LANES), row),
                   pl.BlockSpec((tm, ne), row)],
        out_shape=[jax.ShapeDtypeStruct((n, d), F32),
                   jax.ShapeDtypeStruct((n * TOKEN_ROWS, LANES), U32),
                   jax.ShapeDtypeStruct((n, ne), F32)],
        compiler_params=_cparams(("arbitrary",)),
        name="outproj",
    )(mix, x2, mod3, w_out_bf, ln_g, ln_b, w_router)


def _col_argmax(tiles, row_f):
    m = tiles[0]
    for t in tiles[1:]:
        m = jnp.maximum(m, t)
    m = jnp.max(m, axis=0, keepdims=True)
    idx = None
    for t, r in zip(tiles, row_f):
        c = jnp.where(t == m, r, float(N_EXPERTS))
        idx = c if idx is None else jnp.minimum(idx, c)
    return m, jnp.min(idx, axis=0, keepdims=True)


def _route_kernel(lg_ref, bias_ref, eidx_ref, w_ref, cnt_ref):
    lt = lg_ref[...].T
    tm = lt.shape[1]
    neg = -jnp.inf
    sub = lax.broadcasted_iota(jnp.int32, (GROUP_SIZE, tm), 0).astype(F32)
    row_f = [sub + float(g * GROUP_SIZE) for g in range(N_GROUPS)]
    scores = [jax.nn.sigmoid(lt[g * GROUP_SIZE:(g + 1) * GROUP_SIZE]) for g in range(N_GROUPS)]
    sel = [scores[g] + bias_ref[g * GROUP_SIZE:(g + 1) * GROUP_SIZE, :] for g in range(N_GROUPS)]
    gs = []
    for g in range(N_GROUPS):
        m1, i1 = _col_argmax([sel[g]], [row_f[g]])
        m2 = jnp.max(jnp.where(row_f[g] == i1, neg, sel[g]), axis=0, keepdims=True)
        gs.append(m1 + m2)
    cand = []
    for g in range(N_GROUPS):
        rank = jnp.zeros((1, tm), F32)
        for o in range(N_GROUPS):
            if o == g:
                continue
            ahead = (gs[o] >= gs[g]) if o < g else (gs[o] > gs[g])
            rank = rank + jnp.where(ahead, 1.0, 0.0)
        cand.append(jnp.where(rank < TOPK_GROUPS, sel[g], neg))
    idxs, ws = [], []
    chosen = [jnp.zeros((GROUP_SIZE, tm), F32) for _ in range(N_GROUPS)]
    for _ in range(TOP_K):
        _, ik = _col_argmax(cand, row_f)
        wk = jnp.zeros((1, tm), F32)
        for g in range(N_GROUPS):
            hit = row_f[g] == ik
            wk = wk + jnp.sum(jnp.where(hit, scores[g], 0.0), axis=0, keepdims=True)
            cand[g] = jnp.where(hit, neg, cand[g])
            chosen[g] = jnp.where(hit, 1.0, chosen[g])
        ws.append(wk)
        idxs.append(ik)
    for g in range(N_GROUPS):
        cnt_ref[0, g * GROUP_SIZE:(g + 1) * GROUP_SIZE, :] = jnp.sum(chosen[g], axis=1, keepdims=True)
    wsum = ws[0]
    for k in range(1, TOP_K):
        wsum = wsum + ws[k]
    eidx_ref[...] = jnp.concatenate(idxs, axis=0).astype(jnp.int32)
    w_ref[...] = jnp.concatenate([wk / wsum * ROUTED_SCALE for wk in ws], axis=0)


def _route(logits, bias_col):
    n, ne = logits.shape
    tm = ROUTE_TM
    col = lambda i: (0, i)
    return pl.pallas_call(
        _route_kernel,
        grid=(n // tm,),
        in_specs=[pl.BlockSpec((tm, ne), lambda i: (i, 0)), pl.BlockSpec((ne, 1), lambda i: (0, 0))],
        out_specs=[pl.BlockSpec((TOP_K, tm), col), pl.BlockSpec((TOP_K, tm), col),
                   pl.BlockSpec((1, ne, 1), lambda i: (i, 0, 0))],
        out_shape=[jax.ShapeDtypeStruct((TOP_K, n), jnp.int32), jax.ShapeDtypeStruct((TOP_K, n), F32),
                   jax.ShapeDtypeStruct((n // tm, ne, 1), F32)],
        compiler_params=_cparams(("arbitrary",)),
        name="route",
    )(logits, bias_col)


def _issue_rows(lo, hi, issue_one):
    n_full = (hi - lo) // ISSUE_UNROLL

    def chunk(c, carry):
        for u in range(ISSUE_UNROLL):
            issue_one(lo + c * ISSUE_UNROLL + u)
        return carry

    def tail(r, carry):
        issue_one(r)
        return carry

    lax.fori_loop(0, n_full, chunk, 0)
    lax.fori_loop(lo + n_full * ISSUE_UNROLL, hi, tail, 0)


def _hbm_slab(ref, row):
    return ref.at[pl.ds(pl.multiple_of(row * TOKEN_ROWS, TOKEN_ROWS), TOKEN_ROWS), :]


def _moe_kernel(be_ref, new_ref, nexte_ref, epar_ref, nused_ref, tgt_hbm, h_hbm, wg_hbm, wu_hbm, wd_hbm, ys_hbm,
                idx_s, xbuf, ybuf, wg_f, wu_f, wd_f, wg_s, wu_s, wd_s, sem_i, sem_g, sem_s, sem_w, *, n_tok):
    s = pl.program_id(0)
    tm = MOE_TM
    per = IDX_CHUNK // tm
    per_log2 = per.bit_length() - 1
    n_used = nused_ref[0]
    slab = TOKEN_ROWS
    buf_rows = tm * SLAB_PITCH
    dump_row0 = TOP_K * n_tok
    slot = s & 1

    def staged(buf, base, r):
        return buf.at[pl.ds(pl.multiple_of(base + r * SLAB_PITCH, SUBLANES), slab), :]

    def idx_copy(c):
        return pltpu.make_async_copy(
            tgt_hbm.at[pl.ds(pl.multiple_of(c * IDX_CHUNK, IDX_CHUNK), IDX_CHUNK)],
            idx_s.at[pl.ds(pl.multiple_of((c & 1) * IDX_CHUNK, IDX_CHUNK), IDX_CHUNK)], sem_i)

    def idx_base(b):
        return ((b >> per_log2) & 1) * IDX_CHUNK + (b & (per - 1)) * tm

    def weight_copies(e, p):
        return (pltpu.make_async_copy(wg_hbm.at[e], wg_f.at[p], sem_w.at[p]),
                pltpu.make_async_copy(wu_hbm.at[e], wu_f.at[p], sem_w.at[p]),
                pltpu.make_async_copy(wd_hbm.at[e], wd_f.at[p], sem_w.at[p]))

    def for_rows(inline, body):
        if inline:
            for r in range(tm):
                body(r, r % 2)
        else:
            def pair(c, carry):
                body(2 * c, 0)
                body(2 * c + 1, 1)
                return carry
            lax.fori_loop(0, tm // 2, pair, 0)

    def issue_gather(b, to_slot, inline):
        ibase = idx_base(b)
        xbase = to_slot * buf_rows

        def one(r, prio):
            tok = idx_s[ibase + r] & (n_tok - 1)
            pltpu.make_async_copy(_hbm_slab(h_hbm, tok), staged(xbuf, xbase, r),
                                  sem_g.at[to_slot]).start(priority=prio)
        for_rows(inline, one)

    def issue_scatter(b, from_slot, to_dump, inline):
        ibase = idx_base(b)
        ybase = from_slot * buf_rows

        def one(r, prio):
            tgt = jnp.where(to_dump, dump_row0 + tm + r, idx_s[ibase + r])
            pltpu.make_async_copy(staged(ybuf, ybase, r), _hbm_slab(ys_hbm, tgt),
                                  sem_s.at[from_slot]).start(priority=prio)
        for_rows(inline, one)

    def wait_gather(at_slot):
        v = xbuf.at[pl.ds(pl.multiple_of(at_slot * buf_rows, SUBLANES), buf_rows), :]
        pltpu.make_async_copy(h_hbm.at[pl.ds(0, buf_rows), :], v, sem_g.at[at_slot]).wait()

    def wait_scatter(at_slot):
        v = ybuf.at[pl.ds(pl.multiple_of(at_slot * buf_rows, SUBLANES), buf_rows), :]
        pltpu.make_async_copy(v, ys_hbm.at[pl.ds(0, buf_rows), :], sem_s.at[at_slot]).wait()

    @pl.when(s == 0)
    def _():
        ybuf[...] = jnp.zeros(ybuf.shape, U32)
        first = idx_copy(0)
        first.start()
        first.wait()
        init = pltpu.make_async_copy(ybuf, ys_hbm.at[pl.ds(pl.multiple_of(dump_row0 * slab, slab), 2 * buf_rows), :],
                                     sem_i)
        init.start()
        init.wait()
        issue_gather(0, 0, inline=False)
        for cp in weight_copies(be_ref[0], 0):
            cp.start()

    @pl.when(s < n_used)
    def _():
        @pl.when((s & (per - 1)) == 1)
        def _():
            idx_copy((s >> per_log2) + 1).start()

        @pl.when(((s + 1) & (per - 1)) == 0)
        def _():
            idx_copy((s + 1) >> per_log2).wait()

        @pl.when(new_ref[s] == 1)
        def _():
            p = epar_ref[s]

            @pl.when(nexte_ref[s] >= 0)
            def _():
                for cp in weight_copies(nexte_ref[s], 1 - p):
                    cp.start()

            for cp in weight_copies(be_ref[s], p):
                cp.wait()
            wg_s[...] = wg_f[p].astype(BF16)
            wu_s[...] = wu_f[p].astype(BF16)
            wd_s[...] = wd_f[p].astype(BF16)

        wait_gather(slot)

        @pl.when(s >= 1)
        def _():
            wait_scatter(slot)

        issue_gather(s + 1, 1 - slot, inline=True)
        issue_scatter(jnp.maximum(s - 1, 0), 1 - slot, s == 0, inline=True)
        base = slot * buf_rows
        x_lo, x_hi = _unpack_pairs(_slab_rows_to_matrix(xbuf, base, tm, SLAB_PITCH))
        x = jnp.concatenate([x_lo.astype(BF16), x_hi.astype(BF16)], axis=-1)
        gate = jnp.dot(x, wg_s[...], preferred_element_type=F32)
        up = jnp.dot(x, wu_s[...], preferred_element_type=F32)
        act = (_silu(gate) * up).astype(BF16)
        y = jnp.dot(act, wd_s[...], preferred_element_type=F32)
        half = slab * LANES
        words = _pack_pairs(y[:, :half], y[:, half:])
        for j in range(slab):
            ybuf[pl.ds(base + j, tm, stride=SLAB_PITCH), :] = words[:, j * LANES:(j + 1) * LANES]

    @pl.when(s == n_used)
    def _():
        last = (s - 1) & (per - 1)

        @pl.when((last == 1) | (last == 2))
        def _():
            idx_copy(((s - 1) >> per_log2) + 1).wait()

        wait_gather(slot)
        wait_scatter(slot)
        issue_scatter(s - 1, 1 - slot, False, inline=False)
        wait_scatter(1 - slot)


def _moe(block_e, block_new, block_nexte, block_epar, n_used, row_tgt, h2d, w_gate, w_up, w_down):
    n = h2d.shape[0] // TOKEN_ROWS
    assert n & (n - 1) == 0
    d = 2 * TOKEN_ROWS * LANES
    de = w_gate.shape[2]
    tm = MOE_TM
    n_blk = block_e.shape[0]
    grid_spec = pltpu.PrefetchScalarGridSpec(
        num_scalar_prefetch=5,
        grid=(n_blk,),
        in_specs=[pl.BlockSpec(memory_space=pl.ANY)] * 5,
        out_specs=pl.BlockSpec(memory_space=pl.ANY),
        scratch_shapes=[pltpu.SMEM((2 * IDX_CHUNK,), jnp.int32),
                        pltpu.VMEM((2 * tm * SLAB_PITCH, LANES), U32),
                        pltpu.VMEM((2 * tm * SLAB_PITCH, LANES), U32),
                        pltpu.VMEM((2, d, de), F32),
                        pltpu.VMEM((2, d, de), F32),
                        pltpu.VMEM((2, de, d), F32),
                        pltpu.VMEM((d, de), BF16),
                        pltpu.VMEM((d, de), BF16),
                        pltpu.VMEM((de, d), BF16),
                        pltpu.SemaphoreType.DMA,
                        pltpu.SemaphoreType.DMA((2,)),
                        pltpu.SemaphoreType.DMA((2,)),
                        pltpu.SemaphoreType.DMA((2,))])
    return pl.pallas_call(
        functools.partial(_moe_kernel, n_tok=n),
        grid_spec=grid_spec,
        out_shape=jax.ShapeDtypeStruct(((TOP_K * n + 2 * tm) * TOKEN_ROWS, LANES), U32),
        compiler_params=_cparams(("arbitrary",)),
        name="moe",
    )(block_e, block_new, block_nexte, block_epar, n_used, row_tgt, h2d, w_gate, w_up, w_down)


def _final_kernel(gw_hbm, *refs):
    ys_refs = refs[:TOP_K]
    h_ref, x1_ref, mod_ref, wg_ref, wu_ref, wd_ref, g_ref, b_ref, o_ref, gw_s, acc_ref, sem_i = refs[TOP_K:]
    i = pl.program_id(0)
    tm = x1_ref.shape[0]
    per_step = tm * TOP_K
    cp = pltpu.make_async_copy(gw_hbm.at[pl.ds(pl.multiple_of(i * per_step, per_step), per_step)], gw_s, sem_i)
    cp.start()
    cp.wait()

    hi_base = tm * SLAB_PITCH

    def combine(t, carry):
        rows = pl.ds(pl.multiple_of(t * TOKEN_ROWS, TOKEN_ROWS), TOKEN_ROWS)
        acc_lo = acc_hi = None
        for k in range(TOP_K):
            lo, hi = _unpack_pairs(ys_refs[k][rows, :])
            g = gw_s[t * TOP_K + k]
            acc_lo = g * lo if acc_lo is None else acc_lo + g * lo
            acc_hi = g * hi if acc_hi is None else acc_hi + g * hi
        acc_ref[pl.ds(pl.multiple_of(t * SLAB_PITCH, SUBLANES), TOKEN_ROWS), :] = acc_lo
        acc_ref[pl.ds(pl.multiple_of(hi_base + t * SLAB_PITCH, SUBLANES), TOKEN_ROWS), :] = acc_hi
        return carry

    lax.fori_loop(0, tm, combine, 0, unroll=4)
    moe = jnp.concatenate([_slab_rows_to_matrix(acc_ref, 0, tm, SLAB_PITCH),
                           _slab_rows_to_matrix(acc_ref, hi_base, tm, SLAB_PITCH)], axis=-1)
    h_lo, h_hi = _unpack_pairs(_slab_rows_to_matrix(h_ref, 0, tm, TOKEN_ROWS))
    h = jnp.concatenate([h_lo.astype(BF16), h_hi.astype(BF16)], axis=-1)
    gate = jnp.dot(h, wg_ref[...], preferred_element_type=F32)
    up = jnp.dot(h, wu_ref[...], preferred_element_type=F32)
    shared = jnp.dot((_silu(gate) * up).astype(BF16), wd_ref[...], preferred_element_type=F32)
    m = mod_ref[0]
    y = ALPHA * x1_ref[...] + (1.0 + m[5:6]) * (moe + shared)
    o_ref[...] = _layer_norm(y, g_ref[...], b_ref[...])


def _final(gate_w, ys, h2d, x1, mod3, wsg, wsu, wsd, ln_g, ln_b, seq):
    n, d = x1.shape
    de = wsg.shape[1]
    tm = FINAL_TM
    per_step = tm * TOP_K
    assert per_step % IDX_CHUNK == 0
    row = lambda i: (i, 0)
    const = lambda i: (0, 0)
    slot_rows = lambda k: (lambda i: (k * (n // tm) + i, 0))
    return pl.pallas_call(
        _final_kernel,
        grid=(n // tm,),
        in_specs=[pl.BlockSpec(memory_space=pl.ANY)]
                 + [pl.BlockSpec((tm * TOKEN_ROWS, LANES), slot_rows(k)) for k in range(TOP_K)]
                 + [pl.BlockSpec((tm * TOKEN_ROWS, LANES), row),
                  pl.BlockSpec((tm, d), row),
                  pl.BlockSpec((1, 6, d), lambda i: (i * tm // seq, 0, 0)),
                  pl.BlockSpec((d, de), const),
                  pl.BlockSpec((d, de), const),
                  pl.BlockSpec((de, d), const),
                  pl.BlockSpec((1, d), const),
                  pl.BlockSpec((1, d), const)],
        out_specs=pl.BlockSpec((tm, d), row),
        out_shape=jax.ShapeDtypeStruct((n, d), F32),
        scratch_shapes=[pltpu.SMEM((per_step,), F32),
                        pltpu.VMEM((2 * tm * SLAB_PITCH, LANES), F32),
                        pltpu.SemaphoreType.DMA],
        compiler_params=_cparams(("arbitrary",)),
        name="final",
    )(gate_w, *([ys] * TOP_K), h2d, x1, mod3, wsg, wsu, wsd, ln_g, ln_b)


def _dispatch_tables(eidx, tile_counts, n):
    tm = MOE_TM
    a = n * TOP_K
    i32 = jnp.int32
    experts = jnp.arange(N_EXPERTS, dtype=i32)
    counts = jnp.sum(tile_counts, axis=(0, 2)).astype(i32)
    padded = (counts + tm - 1) // tm * tm
    pad_end = jnp.cumsum(padded)
    starts = pad_end - padded
    n_blk = a // tm + N_EXPERTS + 1
    blk_start = jnp.arange(n_blk, dtype=i32) * tm
    n_used = pad_end[-1] // tm
    in_use = jnp.arange(n_blk) < n_used
    raw_e = jnp.minimum(jnp.sum((pad_end[None, :] <= blk_start[:, None]).astype(i32), axis=1), N_EXPERTS - 1)
    last_e = jnp.sum(jnp.where(jnp.arange(n_blk) == n_used - 1, raw_e, 0))
    block_e = jnp.where(in_use, raw_e, last_e)
    onehot = block_e[:, None] == experts[None, :]
    block_new = jnp.concatenate([jnp.ones((1,), i32), (block_e[1:] != block_e[:-1]).astype(i32)])
    has_rows = counts > 0
    later = (experts[None, :] > experts[:, None]) & has_rows[None, :]
    next_e = jnp.min(jnp.where(later, experts[None, :], N_EXPERTS), axis=1)
    next_e = jnp.where(next_e == N_EXPERTS, -1, next_e)
    parity_e = (jnp.cumsum(has_rows.astype(i32)) - 1) & 1
    block_nexte = jnp.sum(jnp.where(onehot, next_e[None, :], 0), axis=1).astype(i32)
    block_epar = jnp.sum(jnp.where(onehot, parity_e[None, :], 0), axis=1).astype(i32)
    cnt_b = jnp.sum(jnp.where(onehot, counts[None, :], 0), axis=1)
    start_b = jnp.sum(jnp.where(onehot, starts[None, :], 0), axis=1)
    block_nv = jnp.where(in_use, jnp.clip(cnt_b - (blk_start - start_b), 0, tm), 0)
    dummy_keys = jnp.where(jnp.arange(tm, dtype=i32)[None, :] < (padded - counts)[:, None],
                           experts[:, None], N_EXPERTS).reshape(-1)
    keys = jnp.concatenate([eidx.reshape(-1), dummy_keys, jnp.full((tm,), N_EXPERTS, i32)])
    tgt = jnp.arange(a, dtype=i32)
    bits = (a - 1).bit_length() + 1
    low = (1 << bits) - 1
    payload = jnp.concatenate([tgt, jnp.full((n_blk * tm - a,), low, i32)])
    row_tgt = lax.sort(keys * (1 << bits) + payload) & low
    r = jnp.arange(tm, dtype=i32)[None, :]
    dump = a + (jnp.arange(n_blk, dtype=i32)[:, None] & 1) * tm + r
    row_tgt = jnp.where(r >= block_nv[:, None], dump, row_tgt.reshape(n_blk, tm)).reshape(-1)
    row_tgt = jnp.pad(row_tgt, (0, -(n_blk * tm) % IDX_CHUNK))
    return block_e.astype(i32), block_new, block_nexte, block_epar, n_used.astype(i32).reshape(1), row_tgt


def kernel(x, c, w_mod, b_mod, w_in, conv_w, attn_sinks, w_out, ln1_g, ln1_b, w_router, router_bias,
           w_gate, w_up, w_down, ws_gate, ws_up, ws_down, ln2_g, ln2_b):
    b, s, d = x.shape
    n = b * s
    attn_w = N_Q_HEADS * HEAD_DIM
    kv_w = N_KV_HEADS * HEAD_DIM
    conv_wd = d - attn_w
    in_w = attn_w + 2 * kv_w + 3 * conv_wd
    x2 = x.reshape(n, d)
    c8 = jnp.zeros((SUBLANES, d), F32).at[:b].set(c)
    for l in range(DEPTH):
        mod = _mod(c8, w_mod[l], b_mod[l].reshape(1, -1))[:b]
        mod3 = mod.reshape(b, 6, d)
        proj = _inproj(x2, mod3, w_in[l].astype(BF16), s)
        sink_col = jnp.repeat(attn_sinks[l], Q_BLOCK).reshape(-1, 1)
        mix = _mixer(proj, sink_col, conv_w[l], b, s, attn_w, kv_w, conv_wd)
        x1, h2d, logits = _outproj(mix, x2, mod3, w_out[l].astype(BF16), ln1_g[l].reshape(1, -1),
                                   ln1_b[l].reshape(1, -1), w_router[l], s)
        eidx, gate_w, tile_counts = _route(logits, router_bias[l].reshape(-1, 1))
        block_e, block_new, block_nexte, block_epar, n_used, row_tgt = _dispatch_tables(eidx, tile_counts, n)
        ys = _moe(block_e, block_new, block_nexte, block_epar, n_used, row_tgt, h2d,
                  w_gate[l], w_up[l], w_down[l])
        x2 = _final(gate_w.T.reshape(-1), ys, h2d, x1, mod3,
                    ws_gate[l].astype(BF16), ws_up[l].astype(BF16), ws_down[l].astype(BF16),
                    ln2_g[l].reshape(1, -1), ln2_b[l].reshape(1, -1), s)
    return x2.reshape(b, s, d)
```

```python
import functools

import jax
import jax.numpy as jnp
from jax import lax
from jax.experimental import pallas as pl
from jax.experimental.pallas import tpu as pltpu

HEAD_DIM = 64
N_Q_HEADS = 16
N_KV_HEADS = 4
GQA = N_Q_HEADS // N_KV_HEADS
CONV_K = 3
WINDOW = 128
Q_BLOCK = 128
N_EXPERTS = 64
TOP_K = 8
N_GROUPS = 8
GROUP_SIZE = N_EXPERTS // N_GROUPS
TOPK_GROUPS = 4
ROUTED_SCALE = 2.5
DEPTH = 1
ALPHA = (2.0 * DEPTH) ** 0.25
LN_EPS = 1e-5

LANES = 128
SUBLANES = 8
TOKEN_ROWS = 8
SLAB_PITCH = 8
VMEM_LIMIT = 56 * 1024 * 1024

MOD_TN = 1024
INPROJ_TM = 512
INPROJ_TN = 1536
OUTPROJ_TM = 256
ROUTE_TM = 512
MOE_TM = 256
FINAL_TM = 256
IDX_CHUNK = 1024
ISSUE_UNROLL = 8

F32 = jnp.float32
BF16 = jnp.bfloat16


def _cparams(sem):
    return pltpu.CompilerParams(dimension_semantics=sem, vmem_limit_bytes=VMEM_LIMIT)


def _silu(v):
    return v * jax.nn.sigmoid(v)


U32 = jnp.uint32
HI_MASK = 0xFFFF0000


def _pack_pairs(lo, hi):
    lo_bits = lax.bitcast_convert_type(lo.astype(BF16).astype(F32), U32) >> 16
    hi_bits = lax.bitcast_convert_type(hi.astype(BF16).astype(F32), U32) & U32(HI_MASK)
    return lo_bits | hi_bits


def _unpack_pairs(w):
    return (lax.bitcast_convert_type(w << 16, F32), lax.bitcast_convert_type(w & U32(HI_MASK), F32))


def _slab_rows_to_matrix(ref, base, tm, pitch):
    return jnp.concatenate([ref[pl.ds(base + s, tm, stride=pitch), :] for s in range(TOKEN_ROWS)], axis=-1)


def _layer_norm(y, g, b):
    mu = jnp.mean(y, axis=-1, keepdims=True)
    yc = y - mu
    var = jnp.mean(yc * yc, axis=-1, keepdims=True)
    return yc * lax.rsqrt(var + LN_EPS) * g + b


def _mod_kernel(c_ref, w_ref, b_ref, o_ref):
    cs = _silu(c_ref[...]).astype(BF16)
    o_ref[...] = jnp.dot(cs, w_ref[...].astype(BF16), preferred_element_type=F32) + b_ref[...]


def _mod(c8, w_mod, b_mod):
    d, n = w_mod.shape
    return pl.pallas_call(
        _mod_kernel,
        grid=(n // MOD_TN,),
        in_specs=[pl.BlockSpec((SUBLANES, d), lambda j: (0, 0)),
                  pl.BlockSpec((d, MOD_TN), lambda j: (0, j)),
                  pl.BlockSpec((1, MOD_TN), lambda j: (0, j))],
        out_specs=pl.BlockSpec((SUBLANES, MOD_TN), lambda j: (0, j)),
        out_shape=jax.ShapeDtypeStruct((SUBLANES, n), F32),
        compiler_params=_cparams(("arbitrary",)),
        name="mod",
    )(c8, w_mod, b_mod)


def _inproj_kernel(x_ref, mod_ref, w_ref, o_ref, h_ref):
    j = pl.program_id(1)

    @pl.when(j == 0)
    def _():
        m = mod_ref[0]
        h_ref[...] = (x_ref[...] * (1.0 + m[1:2]) + m[0:1]).astype(BF16)

    o_ref[...] = jnp.dot(h_ref[...], w_ref[...], preferred_element_type=F32).astype(BF16)


def _inproj(x2, mod3, w_in_bf, seq):
    n, d = x2.shape
    in_w = w_in_bf.shape[1]
    tm, tn = INPROJ_TM, INPROJ_TN
    return pl.pallas_call(
        _inproj_kernel,
        grid=(n // tm, in_w // tn),
        in_specs=[pl.BlockSpec((tm, d), lambda i, j: (i, 0)),
                  pl.BlockSpec((1, 6, d), lambda i, j: (i * tm // seq, 0, 0)),
                  pl.BlockSpec((d, tn), lambda i, j: (0, j))],
        out_specs=pl.BlockSpec((tm, tn), lambda i, j: (i, j)),
        out_shape=jax.ShapeDtypeStruct((n, in_w), BF16),
        scratch_shapes=[pltpu.VMEM((tm, d), BF16)],
        compiler_params=_cparams(("arbitrary", "arbitrary")),
        name="inproj",
    )(x2, mod3, w_in_bf)


def _mixer_kernel(cur_ref, pk_ref, pv_ref, prow_ref, sink_ref, cw_ref, o_ref, *, attn_w, kv_w, conv_w):
    nblk = pl.program_id(1)
    has_prev = nblk > 0
    qb = Q_BLOCK
    cur = cur_ref[...]
    k_cur = cur[:, attn_w:attn_w + kv_w]
    v_cur = cur[:, attn_w + kv_w:attn_w + 2 * kv_w]
    k_all = jnp.concatenate([pk_ref[...], k_cur], axis=0)
    v_all = jnp.concatenate([pv_ref[...], v_cur], axis=0)

    rows = GQA * qb
    qi = lax.broadcasted_iota(jnp.int32, (rows, 2 * qb), 0) % qb
    kj = lax.broadcasted_iota(jnp.int32, (rows, 2 * qb), 1)
    dist = qi + qb - kj
    kmin = jnp.where(has_prev, 0, qb)
    valid = (dist >= 0) & (dist < WINDOW) & (kj >= kmin)
    distf = dist.astype(F32)
    head_in_group = lax.broadcasted_iota(jnp.int32, (rows, 1), 0) // qb
    sinks = sink_ref[...]

    outs = []
    for g in range(N_KV_HEADS):
        q4 = jnp.concatenate(
            [cur[:, (g * GQA + j) * HEAD_DIM:(g * GQA + j + 1) * HEAD_DIM] for j in range(GQA)], axis=0)
        kg = k_all[:, g * HEAD_DIM:(g + 1) * HEAD_DIM]
        vg = v_all[:, g * HEAD_DIM:(g + 1) * HEAD_DIM]
        s = lax.dot_general(q4, kg, (((1,), (1,)), ((), ())), preferred_element_type=F32)
        s = s * (HEAD_DIM ** -0.5)
        slope = jnp.zeros((rows, 1), F32)
        sink = jnp.zeros((rows, 1), F32)
        for j in range(GQA):
            h = g * GQA + j
            sel = head_in_group == j
            slope = jnp.where(sel, 2.0 ** (-8.0 * (h + 1) / N_Q_HEADS), slope)
            sink = jnp.where(sel, sinks[:, h:h + 1], sink)
        s = jnp.where(valid, s - slope * distf, -jnp.inf)
        m = jnp.maximum(jnp.max(s, axis=-1, keepdims=True), sink)
        p = jnp.exp(s - m)
        denom = jnp.sum(p, axis=-1, keepdims=True) + jnp.exp(sink - m)
        o4 = jnp.dot(p.astype(BF16), vg, preferred_element_type=F32) / denom
        outs.extend(o4[j * qb:(j + 1) * qb] for j in range(GQA))
    attn = jnp.concatenate(outs, axis=-1)

    c0 = attn_w + 2 * kv_w
    cb = cur[:, c0:c0 + conv_w].astype(F32)
    u = cur[:, c0 + conv_w:c0 + 2 * conv_w].astype(F32) * cur[:, c0 + 2 * conv_w:c0 + 3 * conv_w].astype(F32)
    prow = prow_ref[...]
    up = prow[:, c0 + conv_w:c0 + 2 * conv_w].astype(F32) * prow[:, c0 + 2 * conv_w:c0 + 3 * conv_w].astype(F32)
    up = up * jnp.where(has_prev, 1.0, 0.0)
    pm1 = up[15:16]
    pm2 = up[14:15]
    ri = lax.broadcasted_iota(jnp.int32, u.shape, 0)
    u1 = jnp.where(ri == 0, pm1, pltpu.roll(u, 1, 0))
    u2 = jnp.where(ri == 0, pm2, jnp.where(ri == 1, pm1, pltpu.roll(u, 2, 0)))
    cw = cw_ref[...]
    conv = cb * (cw[0:1] * u2 + cw[1:2] * u1 + cw[2:3] * u)
    o_ref[...] = jnp.concatenate([attn, conv], axis=-1).astype(BF16)


def _mixer(proj, sinks2, conv_w, batch, seq, attn_w, kv_w, conv_wd):
    n, in_w = proj.shape
    nb = seq // Q_BLOCK
    kv_blk0 = attn_w // kv_w
    sub16 = Q_BLOCK // 16

    def cur_map(b, i):
        return (b * nb + i, 0)

    def prev_map(col):
        return lambda b, i: (b * nb + jnp.maximum(i - 1, 0), col)

    def prow_map(b, i):
        return (jnp.maximum((b * nb + i) * sub16 - 1, 0), 0)

    kern = functools.partial(_mixer_kernel, attn_w=attn_w, kv_w=kv_w, conv_w=conv_wd)
    return pl.pallas_call(
        kern,
        grid=(batch, nb),
        in_specs=[pl.BlockSpec((Q_BLOCK, in_w), cur_map),
                  pl.BlockSpec((Q_BLOCK, kv_w), prev_map(kv_blk0)),
                  pl.BlockSpec((Q_BLOCK, kv_w), prev_map(kv_blk0 + 1)),
                  pl.BlockSpec((16, in_w), prow_map),
                  pl.BlockSpec((1, N_Q_HEADS), lambda b, i: (0, 0)),
                  pl.BlockSpec((CONV_K, conv_wd), lambda b, i: (0, 0))],
        out_specs=pl.BlockSpec((Q_BLOCK, attn_w + conv_wd), cur_map),
        out_shape=jax.ShapeDtypeStruct((n, attn_w + conv_wd), BF16),
        compiler_params=_cparams(("arbitrary", "arbitrary")),
        name="mixer",
    )(proj, proj, proj, proj, sinks2, conv_w)


def _split_bf16(v):
    hi = v.astype(BF16)
    lo = (v - hi.astype(F32)).astype(BF16)
    return hi, lo


def _outproj_kernel(mix_ref, x_ref, mod_ref, w_ref, g_ref, b_ref, wr_ref, x1_ref, h2_ref, lg_ref):
    m = mod_ref[0]
    mix = jnp.dot(mix_ref[...], w_ref[...], preferred_element_type=F32)
    x1 = _layer_norm(ALPHA * x_ref[...] + (1.0 + m[2:3]) * mix, g_ref[...], b_ref[...])
    x1_ref[...] = x1
    h2 = x1 * (1.0 + m[4:5]) + m[3:4]
    tm, d = h2.shape
    words = _pack_pairs(h2[:, :d // 2], h2[:, d // 2:])
    for s in range(TOKEN_ROWS):
        h2_ref[pl.ds(s, tm, stride=TOKEN_ROWS), :] = words[:, s * LANES:(s + 1) * LANES]
    h_hi, h_lo = _split_bf16(h2)
    w_hi, w_lo = _split_bf16(wr_ref[...])
    lg_ref[...] = (jnp.dot(h_hi, w_hi, preferred_element_type=F32)
                   + (jnp.dot(h_hi, w_lo, preferred_element_type=F32)
                      + jnp.dot(h_lo, w_hi, preferred_element_type=F32)))


def _outproj(mix, x2, mod3, w_out_bf, ln_g, ln_b, w_router, seq):
    n, d = x2.shape
    tm = OUTPROJ_TM
    ne = w_router.shape[1]
    row = lambda i: (i, 0)
    const = lambda i: (0, 0)
    return pl.pallas_call(
        _outproj_kernel,
        grid=(n // tm,),
        in_specs=[pl.BlockSpec((tm, d), row),
                  pl.BlockSpec((tm, d), row),
                  pl.BlockSpec((1, 6, d), lambda i: (i * tm // seq, 0, 0)),
                  pl.BlockSpec((d, d), const),
                  pl.BlockSpec((1, d), const),
                  pl.BlockSpec((1, d), const),
                  pl.BlockSpec((d, ne), const)],
        out_specs=[pl.BlockSpec((tm, d), row),
                   pl.BlockSpec((tm * TOKEN_ROWS, LANES), row),
                   pl.BlockSpec((tm, ne), row)],
        out_shape=[jax.ShapeDtypeStruct((n, d), F32),
                   jax.ShapeDtypeStruct((n * TOKEN_ROWS, LANES), U32),
                   jax.ShapeDtypeStruct((n, ne), F32)],
        compiler_params=_cparams(("arbitrary",)),
        name="outproj",
    )(mix, x2, mod3, w_out_bf, ln_g, ln_b, w_router)


def _col_argmax(tiles, row_f):
    m = tiles[0]
    for t in tiles[1:]:
        m = jnp.maximum(m, t)
    m = jnp.max(m, axis=0, keepdims=True)
    idx = None
    for t, r in zip(tiles, row_f):
        c = jnp.where(t == m, r, float(N_EXPERTS))
        idx = c if idx is None else jnp.minimum(idx, c)
    return m, jnp.min(idx, axis=0, keepdims=True)


def _route_kernel(lg_ref, bias_ref, eidx_ref, w_ref, cnt_ref):
    lt = lg_ref[...].T
    tm = lt.shape[1]
    neg = -jnp.inf
    sub = lax.broadcasted_iota(jnp.int32, (GROUP_SIZE, tm), 0).astype(F32)
    row_f = [sub + float(g * GROUP_SIZE) for g in range(N_GROUPS)]
    scores = [jax.nn.sigmoid(lt[g * GROUP_SIZE:(g + 1) * GROUP_SIZE]) for g in range(N_GROUPS)]
    sel = [scores[g] + bias_ref[g * GROUP_SIZE:(g + 1) * GROUP_SIZE, :] for g in range(N_GROUPS)]
    gs = []
    for g in range(N_GROUPS):
        m1, i1 = _col_argmax([sel[g]], [row_f[g]])
        m2 = jnp.max(jnp.where(row_f[g] == i1, neg, sel[g]), axis=0, keepdims=True)
        gs.append(m1 + m2)
    cand = []
    for g in range(N_GROUPS):
        rank = jnp.zeros((1, tm), F32)
        for o in range(N_GROUPS):
            if o == g:
                continue
            ahead = (gs[o] >= gs[g]) if o < g else (gs[o] > gs[g])
            rank = rank + jnp.where(ahead, 1.0, 0.0)
        cand.append(jnp.where(rank < TOPK_GROUPS, sel[g], neg))
    idxs, ws = [], []
    chosen = [jnp.zeros((GROUP_SIZE, tm), F32) for _ in range(N_GROUPS)]
    for _ in range(TOP_K):
        _, ik = _col_argmax(cand, row_f)
        wk = jnp.zeros((1, tm), F32)
        for g in range(N_GROUPS):
            hit = row_f[g] == ik
            wk = wk + jnp.sum(jnp.where(hit, scores[g], 0.0), axis=0, keepdims=True)
            cand[g] = jnp.where(hit, neg, cand[g])
            chosen[g] = jnp.where(hit, 1.0, chosen[g])
        ws.append(wk)
        idxs.append(ik)
    for g in range(N_GROUPS):
        cnt_ref[0, g * GROUP_SIZE:(g + 1) * GROUP_SIZE, :] = jnp.sum(chosen[g], axis=1, keepdims=True)
    wsum = ws[0]
    for k in range(1, TOP_K):
        wsum = wsum + ws[k]
    eidx_ref[...] = jnp.concatenate(idxs, axis=0).astype(jnp.int32)
    w_ref[...] = jnp.concatenate([wk / wsum * ROUTED_SCALE for wk in ws], axis=0)


def _route(logits, bias_col):
    n, ne = logits.shape
    tm = ROUTE_TM
    col = lambda i: (0, i)
    return pl.pallas_call(
        _route_kernel,
        grid=(n // tm,),
        in_specs=[pl.BlockSpec((tm, ne), lambda i: (i, 0)), pl.BlockSpec((ne, 1), lambda i: (0, 0))],
        out_specs=[pl.BlockSpec((TOP_K, tm), col), pl.BlockSpec((TOP_K, tm), col),
                   pl.BlockSpec((1, ne, 1), lambda i: (i, 0, 0))],
        out_shape=[jax.ShapeDtypeStruct((TOP_K, n), jnp.int32), jax.ShapeDtypeStruct((TOP_K, n), F32),
                   jax.ShapeDtypeStruct((n // tm, ne, 1), F32)],
        compiler_params=_cparams(("arbitrary",)),
        name="route",
    )(logits, bias_col)


def _issue_rows(lo, hi, issue_one):
    n_full = (hi - lo) // ISSUE_UNROLL

    def chunk(c, carry):
        for u in range(ISSUE_UNROLL):
            issue_one(lo + c * ISSUE_UNROLL + u)
        return carry

    def tail(r, carry):
        issue_one(r)
        return carry

    lax.fori_loop(0, n_full, chunk, 0)
    lax.fori_loop(lo + n_full * ISSUE_UNROLL, hi, tail, 0)


def _hbm_slab(ref, row):
    return ref.at[pl.ds(pl.multiple_of(row * TOKEN_ROWS, TOKEN_ROWS), TOKEN_ROWS), :]


def _moe_kernel(be_ref, new_ref, nexte_ref, epar_ref, nused_ref, tgt_hbm, h_hbm, wg_hbm, wu_hbm, wd_hbm, ys_hbm,
                idx_s, xbuf, ybuf, wg_f, wu_f, wd_f, wg_s, wu_s, wd_s, sem_i, sem_g, sem_s, sem_w, *, n_tok):
    s = pl.program_id(0)
    tm = MOE_TM
    per = IDX_CHUNK // tm
    per_log2 = per.bit_length() - 1
    n_used = nused_ref[0]
    slab = TOKEN_ROWS
    buf_rows = tm * SLAB_PITCH
    dump_row0 = TOP_K * n_tok
    slot = s & 1

    def staged(buf, base, r):
        return buf.at[pl.ds(pl.multiple_of(base + r * SLAB_PITCH, SUBLANES), slab), :]

    def idx_copy(c):
        return pltpu.make_async_copy(
            tgt_hbm.at[pl.ds(pl.multiple_of(c * IDX_CHUNK, IDX_CHUNK), IDX_CHUNK)],
            idx_s.at[pl.ds(pl.multiple_of((c & 1) * IDX_CHUNK, IDX_CHUNK), IDX_CHUNK)], sem_i)

    def idx_base(b):
        return ((b >> per_log2) & 1) * IDX_CHUNK + (b & (per - 1)) * tm

    def weight_copies(e, p):
        return (pltpu.make_async_copy(wg_hbm.at[e], wg_f.at[p], sem_w.at[p]),
                pltpu.make_async_copy(wu_hbm.at[e], wu_f.at[p], sem_w.at[p]),
                pltpu.make_async_copy(wd_hbm.at[e], wd_f.at[p], sem_w.at[p]))

    def for_rows(inline, body):
        if inline:
            for r in range(tm):
                body(r, r % 2)
        else:
            def pair(c, carry):
                body(2 * c, 0)
                body(2 * c + 1, 1)
                return carry
            lax.fori_loop(0, tm // 2, pair, 0)

    def issue_gather(b, to_slot, inline):
        ibase = idx_base(b)
        xbase = to_slot * buf_rows

        def one(r, prio):
            tok = idx_s[ibase + r] & (n_tok - 1)
            pltpu.make_async_copy(_hbm_slab(h_hbm, tok), staged(xbuf, xbase, r),
                                  sem_g.at[to_slot]).start(priority=0)
        for_rows(inline, one)

    def issue_scatter(b, from_slot, to_dump, inline):
        ibase = idx_base(b)
        ybase = from_slot * buf_rows

        def one(r, prio):
            tgt = jnp.where(to_dump, dump_row0 + tm + r, idx_s[ibase + r])
            pltpu.make_async_copy(staged(ybuf, ybase, r), _hbm_slab(ys_hbm, tgt),
                                  sem_s.at[from_slot]).start(priority=prio)
        for_rows(inline, one)

    def wait_gather(at_slot):
        v = xbuf.at[pl.ds(pl.multiple_of(at_slot * buf_rows, SUBLANES), buf_rows), :]
        pltpu.make_async_copy(h_hbm.at[pl.ds(0, buf_rows), :], v, sem_g.at[at_slot]).wait()

    def wait_scatter(at_slot):
        v = ybuf.at[pl.ds(pl.multiple_of(at_slot * buf_rows, SUBLANES), buf_rows), :]
        pltpu.make_async_copy(v, ys_hbm.at[pl.ds(0, buf_rows), :], sem_s.at[at_slot]).wait()

    @pl.when(s == 0)
    def _():
        ybuf[...] = jnp.zeros(ybuf.shape, U32)
        first = idx_copy(0)
        first.start()
        first.wait()
        init = pltpu.make_async_copy(ybuf, ys_hbm.at[pl.ds(pl.multiple_of(dump_row0 * slab, slab), 2 * buf_rows), :],
                                     sem_i)
        init.start()
        init.wait()
        issue_gather(0, 0, inline=False)
        for cp in weight_copies(be_ref[0], 0):
            cp.start()

    @pl.when(s < n_used)
    def _():
        @pl.when((s & (per - 1)) == 1)
        def _():
            idx_copy((s >> per_log2) + 1).start()

        @pl.when(((s + 1) & (per - 1)) == 0)
        def _():
            idx_copy((s + 1) >> per_log2).wait()

        @pl.when(new_ref[s] == 1)
        def _():
            p = epar_ref[s]

            @pl.when(nexte_ref[s] >= 0)
            def _():
                for cp in weight_copies(nexte_ref[s], 1 - p):
                    cp.start(priority=1)

            for cp in weight_copies(be_ref[s], p):
                cp.wait()
            wg_s[...] = wg_f[p].astype(BF16)
            wu_s[...] = wu_f[p].astype(BF16)
            wd_s[...] = wd_f[p].astype(BF16)

        wait_gather(slot)

        @pl.when(s >= 1)
        def _():
            wait_scatter(slot)

        issue_gather(s + 1, 1 - slot, inline=True)
        issue_scatter(jnp.maximum(s - 1, 0), 1 - slot, s == 0, inline=True)
        base = slot * buf_rows
        x_lo, x_hi = _unpack_pairs(_slab_rows_to_matrix(xbuf, base, tm, SLAB_PITCH))
        x = jnp.concatenate([x_lo.astype(BF16), x_hi.astype(BF16)], axis=-1)
        gate = jnp.dot(x, wg_s[...], preferred_element_type=F32)
        up = jnp.dot(x, wu_s[...], preferred_element_type=F32)
        act = (_silu(gate) * up).astype(BF16)
        y = jnp.dot(act, wd_s[...], preferred_element_type=F32)
        half = slab * LANES
        words = _pack_pairs(y[:, :half], y[:, half:])
        for j in range(slab):
            ybuf[pl.ds(base + j, tm, stride=SLAB_PITCH), :] = words[:, j * LANES:(j + 1) * LANES]

    @pl.when(s == n_used)
    def _():
        last = (s - 1) & (per - 1)

        @pl.when((last == 1) | (last == 2))
        def _():
            idx_copy(((s - 1) >> per_log2) + 1).wait()

        wait_gather(slot)
        wait_scatter(slot)
        issue_scatter(s - 1, 1 - slot, False, inline=False)
        wait_scatter(1 - slot)


def _moe(block_e, block_new, block_nexte, block_epar, n_used, row_tgt, h2d, w_gate, w_up, w_down):
    n = h2d.shape[0] // TOKEN_ROWS
    assert n & (n - 1) == 0
    d = 2 * TOKEN_ROWS * LANES
    de = w_gate.shape[2]
    tm = MOE_TM
    n_blk = block_e.shape[0]
    grid_spec = pltpu.PrefetchScalarGridSpec(
        num_scalar_prefetch=5,
        grid=(n_blk,),
        in_specs=[pl.BlockSpec(memory_space=pl.ANY)] * 5,
        out_specs=pl.BlockSpec(memory_space=pl.ANY),
        scratch_shapes=[pltpu.SMEM((2 * IDX_CHUNK,), jnp.int32),
                        pltpu.VMEM((2 * tm * SLAB_PITCH, LANES), U32),
                        pltpu.VMEM((2 * tm * SLAB_PITCH, LANES), U32),
                        pltpu.VMEM((2, d, de), F32),
                        pltpu.VMEM((2, d, de), F32),
                        pltpu.VMEM((2, de, d), F32),
                        pltpu.VMEM((d, de), BF16),
                        pltpu.VMEM((d, de), BF16),
                        pltpu.VMEM((de, d), BF16),
                        pltpu.SemaphoreType.DMA,
                        pltpu.SemaphoreType.DMA((2,)),
                        pltpu.SemaphoreType.DMA((2,)),
                        pltpu.SemaphoreType.DMA((2,))])
    return pl.pallas_call(
        functools.partial(_moe_kernel, n_tok=n),
        grid_spec=grid_spec,
        out_shape=jax.ShapeDtypeStruct(((TOP_K * n + 2 * tm) * TOKEN_ROWS, LANES), U32),
        compiler_params=_cparams(("arbitrary",)),
        name="moe",
    )(block_e, block_new, block_nexte, block_epar, n_used, row_tgt, h2d, w_gate, w_up, w_down)


def _final_kernel(gw_hbm, *refs):
    ys_refs = refs[:TOP_K]
    h_ref, x1_ref, mod_ref, wg_ref, wu_ref, wd_ref, g_ref, b_ref, o_ref, gw_s, acc_ref, sem_i = refs[TOP_K:]
    i = pl.program_id(0)
    tm = x1_ref.shape[0]
    per_step = tm * TOP_K
    cp = pltpu.make_async_copy(gw_hbm.at[pl.ds(pl.multiple_of(i * per_step, per_step), per_step)], gw_s, sem_i)
    cp.start()
    cp.wait()

    hi_base = tm * SLAB_PITCH

    def combine(t, carry):
        rows = pl.ds(pl.multiple_of(t * TOKEN_ROWS, TOKEN_ROWS), TOKEN_ROWS)
        acc_lo = acc_hi = None
        for k in range(TOP_K):
            lo, hi = _unpack_pairs(ys_refs[k][rows, :])
            g = gw_s[t * TOP_K + k]
            acc_lo = g * lo if acc_lo is None else acc_lo + g * lo
            acc_hi = g * hi if acc_hi is None else acc_hi + g * hi
        acc_ref[pl.ds(pl.multiple_of(t * SLAB_PITCH, SUBLANES), TOKEN_ROWS), :] = acc_lo
        acc_ref[pl.ds(pl.multiple_of(hi_base + t * SLAB_PITCH, SUBLANES), TOKEN_ROWS), :] = acc_hi
        return carry

    lax.fori_loop(0, tm, combine, 0, unroll=4)
    moe = jnp.concatenate([_slab_rows_to_matrix(acc_ref, 0, tm, SLAB_PITCH),
                           _slab_rows_to_matrix(acc_ref, hi_base, tm, SLAB_PITCH)], axis=-1)
    h_lo, h_hi = _unpack_pairs(_slab_rows_to_matrix(h_ref, 0, tm, TOKEN_ROWS))
    h = jnp.concatenate([h_lo.astype(BF16), h_hi.astype(BF16)], axis=-1)
    gate = jnp.dot(h, wg_ref[...], preferred_element_type=F32)
    up = jnp.dot(h, wu_ref[...], preferred_element_type=F32)
    shared = jnp.dot((_silu(gate) * up).astype(BF16), wd_ref[...], preferred_element_type=F32)
    m = mod_ref[0]
    y = ALPHA * x1_ref[...] + (1.0 + m[5:6]) * (moe + shared)
    o_ref[...] = _layer_norm(y, g_ref[...], b_ref[...])


def _final(gate_w, ys, h2d, x1, mod3, wsg, wsu, wsd, ln_g, ln_b, seq):
    n, d = x1.shape
    de = wsg.shape[1]
    tm = FINAL_TM
    per_step = tm * TOP_K
    assert per_step % IDX_CHUNK == 0
    row = lambda i: (i, 0)
    const = lambda i: (0, 0)
    slot_rows = lambda k: (lambda i: (k * (n // tm) + i, 0))
    return pl.pallas_call(
        _final_kernel,
        grid=(n // tm,),
        in_specs=[pl.BlockSpec(memory_space=pl.ANY)]
                 + [pl.BlockSpec((tm * TOKEN_ROWS, LANES), slot_rows(k)) for k in range(TOP_K)]
                 + [pl.BlockSpec((tm * TOKEN_ROWS, LANES), row),
                  pl.BlockSpec((tm, d), row),
                  pl.BlockSpec((1, 6, d), lambda i: (i * tm // seq, 0, 0)),
                  pl.BlockSpec((d, de), const),
                  pl.BlockSpec((d, de), const),
                  pl.BlockSpec((de, d), const),
                  pl.BlockSpec((1, d), const),
                  pl.BlockSpec((1, d), const)],
        out_specs=pl.BlockSpec((tm, d), row),
        out_shape=jax.ShapeDtypeStruct((n, d), F32),
        scratch_shapes=[pltpu.SMEM((per_step,), F32),
                        pltpu.VMEM((2 * tm * SLAB_PITCH, LANES), F32),
                        pltpu.SemaphoreType.DMA],
        compiler_params=_cparams(("arbitrary",)),
        name="final",
    )(gate_w, *([ys] * TOP_K), h2d, x1, mod3, wsg, wsu, wsd, ln_g, ln_b)


def _dispatch_tables(eidx, tile_counts, n):
    tm = MOE_TM
    a = n * TOP_K
    i32 = jnp.int32
    experts = jnp.arange(N_EXPERTS, dtype=i32)
    counts = jnp.sum(tile_counts, axis=(0, 2)).astype(i32)
    padded = (counts + tm - 1) // tm * tm
    pad_end = jnp.cumsum(padded)
    starts = pad_end - padded
    n_blk = a // tm + N_EXPERTS + 1
    blk_start = jnp.arange(n_blk, dtype=i32) * tm
    n_used = pad_end[-1] // tm
    in_use = jnp.arange(n_blk) < n_used
    raw_e = jnp.minimum(jnp.sum((pad_end[None, :] <= blk_start[:, None]).astype(i32), axis=1), N_EXPERTS - 1)
    last_e = jnp.sum(jnp.where(jnp.arange(n_blk) == n_used - 1, raw_e, 0))
    block_e = jnp.where(in_use, raw_e, last_e)
    onehot = block_e[:, None] == experts[None, :]
    block_new = jnp.concatenate([jnp.ones((1,), i32), (block_e[1:] != block_e[:-1]).astype(i32)])
    has_rows = counts > 0
    later = (experts[None, :] > experts[:, None]) & has_rows[None, :]
    next_e = jnp.min(jnp.where(later, experts[None, :], N_EXPERTS), axis=1)
    next_e = jnp.where(next_e == N_EXPERTS, -1, next_e)
    parity_e = (jnp.cumsum(has_rows.astype(i32)) - 1) & 1
    block_nexte = jnp.sum(jnp.where(onehot, next_e[None, :], 0), axis=1).astype(i32)
    block_epar = jnp.sum(jnp.where(onehot, parity_e[None, :], 0), axis=1).astype(i32)
    cnt_b = jnp.sum(jnp.where(onehot, counts[None, :], 0), axis=1)
    start_b = jnp.sum(jnp.where(onehot, starts[None, :], 0), axis=1)
    block_nv = jnp.where(in_use, jnp.clip(cnt_b - (blk_start - start_b), 0, tm), 0)
    dummy_keys = jnp.where(jnp.arange(tm, dtype=i32)[None, :] < (padded - counts)[:, None],
                           experts[:, None], N_EXPERTS).reshape(-1)
    keys = jnp.concatenate([eidx.reshape(-1), dummy_keys, jnp.full((tm,), N_EXPERTS, i32)])
    tgt = jnp.arange(a, dtype=i32)
    bits = (a - 1).bit_length() + 1
    low = (1 << bits) - 1
    payload = jnp.concatenate([tgt, jnp.full((n_blk * tm - a,), low, i32)])
    row_tgt = lax.sort(keys * (1 << bits) + payload) & low
    r = jnp.arange(tm, dtype=i32)[None, :]
    dump = a + (jnp.arange(n_blk, dtype=i32)[:, None] & 1) * tm + r
    row_tgt = jnp.where(r >= block_nv[:, None], dump, row_tgt.reshape(n_blk, tm)).reshape(-1)
    row_tgt = jnp.pad(row_tgt, (0, -(n_blk * tm) % IDX_CHUNK))
    return block_e.astype(i32), block_new, block_nexte, block_epar, n_used.astype(i32).reshape(1), row_tgt


def kernel(x, c, w_mod, b_mod, w_in, conv_w, attn_sinks, w_out, ln1_g, ln1_b, w_router, router_bias,
           w_gate, w_up, w_down, ws_gate, ws_up, ws_down, ln2_g, ln2_b):
    b, s, d = x.shape
    n = b * s
    attn_w = N_Q_HEADS * HEAD_DIM
    kv_w = N_KV_HEADS * HEAD_DIM
    conv_wd = d - attn_w
    in_w = attn_w + 2 * kv_w + 3 * conv_wd
    x2 = x.reshape(n, d)
    c8 = jnp.zeros((SUBLANES, d), F32).at[:b].set(c)
    for l in range(DEPTH):
        mod = _mod(c8, w_mod[l], b_mod[l].reshape(1, -1))[:b]
        mod3 = mod.reshape(b, 6, d)
        proj = _inproj(x2, mod3, w_in[l].astype(BF16), s)
        mix = _mixer(proj, attn_sinks[l].reshape(1, -1), conv_w[l], b, s, attn_w, kv_w, conv_wd)
        x1, h2d, logits = _outproj(mix, x2, mod3, w_out[l].astype(BF16), ln1_g[l].reshape(1, -1),
                                   ln1_b[l].reshape(1, -1), w_router[l], s)
        eidx, gate_w, tile_counts = _route(logits, router_bias[l].reshape(-1, 1))
        block_e, block_new, block_nexte, block_epar, n_used, row_tgt = _dispatch_tables(eidx, tile_counts, n)
        ys = _moe(block_e, block_new, block_nexte, block_epar, n_used, row_tgt, h2d,
                  w_gate[l], w_up[l], w_down[l])
        x2 = _final(gate_w.T.reshape(-1), ys, h2d, x1, mod3,
                    ws_gate[l].astype(BF16), ws_up[l].astype(BF16), ws_down[l].astype(BF16),
                    ln2_g[l].reshape(1, -1), ln2_b[l].reshape(1, -1), s)
    return x2.reshape(b, s, d)
```

```python
import functools

import jax
import jax.numpy as jnp
from jax import lax
from jax.experimental import pallas as pl
from jax.experimental.pallas import tpu as pltpu

HEAD_DIM = 64
N_Q_HEADS = 16
N_KV_HEADS = 4
GQA = N_Q_HEADS // N_KV_HEADS
CONV_K = 3
WINDOW = 128
Q_BLOCK = 128
N_EXPERTS = 64
TOP_K = 8
N_GROUPS = 8
GROUP_SIZE = N_EXPERTS // N_GROUPS
TOPK_GROUPS = 4
ROUTED_SCALE = 2.5
DEPTH = 1
ALPHA = (2.0 * DEPTH) ** 0.25
LN_EPS = 1e-5

LANES = 128
SUBLANES = 8
TOKEN_ROWS = 8
SLAB_PITCH = 8
VMEM_LIMIT = 56 * 1024 * 1024

MOD_TN = 1024
INPROJ_TM = 1024
INPROJ_TN = 1536
OUTPROJ_TM = 128
ROUTE_TM = 512
MOE_TM = 256
FINAL_TM = 256
IDX_CHUNK = 1024
ISSUE_UNROLL = 8

F32 = jnp.float32
BF16 = jnp.bfloat16


def _cparams(sem):
    return pltpu.CompilerParams(dimension_semantics=sem, vmem_limit_bytes=VMEM_LIMIT)


def _silu(v):
    return v * jax.nn.sigmoid(v)


U32 = jnp.uint32
HI_MASK = 0xFFFF0000


def _pack_pairs(lo, hi):
    lo_bits = lax.bitcast_convert_type(lo.astype(BF16).astype(F32), U32) >> 16
    hi_bits = lax.bitcast_convert_type(hi.astype(BF16).astype(F32), U32) & U32(HI_MASK)
    return lo_bits | hi_bits


def _unpack_pairs(w):
    return (lax.bitcast_convert_type(w << 16, F32), lax.bitcast_convert_type(w & U32(HI_MASK), F32))


def _slab_rows_to_matrix(ref, base, tm, pitch):
    return jnp.concatenate([ref[pl.ds(base + s, tm, stride=pitch), :] for s in range(TOKEN_ROWS)], axis=-1)


def _layer_norm(y, g, b):
    mu = jnp.mean(y, axis=-1, keepdims=True)
    yc = y - mu
    var = jnp.mean(yc * yc, axis=-1, keepdims=True)
    return yc * lax.rsqrt(var + LN_EPS) * g + b


def _mod_kernel(c_ref, w_ref, b_ref, o_ref):
    cs = _silu(c_ref[...]).astype(BF16)
    o_ref[...] = jnp.dot(cs, w_ref[...].astype(BF16), preferred_element_type=F32) + b_ref[...]


def _mod(c8, w_mod, b_mod):
    d, n = w_mod.shape
    return pl.pallas_call(
        _mod_kernel,
        grid=(n // MOD_TN,),
        in_specs=[pl.BlockSpec((SUBLANES, d), lambda j: (0, 0)),
                  pl.BlockSpec((d, MOD_TN), lambda j: (0, j)),
                  pl.BlockSpec((1, MOD_TN), lambda j: (0, j))],
        out_specs=pl.BlockSpec((SUBLANES, MOD_TN), lambda j: (0, j)),
        out_shape=jax.ShapeDtypeStruct((SUBLANES, n), F32),
        compiler_params=_cparams(("arbitrary",)),
        name="mod",
    )(c8, w_mod, b_mod)


def _inproj_kernel(x_ref, mod_ref, w_ref, o_ref, h_ref):
    j = pl.program_id(1)

    @pl.when(j == 0)
    def _():
        m = mod_ref[0]
        h_ref[...] = (x_ref[...] * (1.0 + m[1:2]) + m[0:1]).astype(BF16)

    o_ref[...] = jnp.dot(h_ref[...], w_ref[...], preferred_element_type=F32).astype(BF16)


def _inproj(x2, mod3, w_in_bf, seq):
    n, d = x2.shape
    in_w = w_in_bf.shape[1]
    tm, tn = INPROJ_TM, INPROJ_TN
    return pl.pallas_call(
        _inproj_kernel,
        grid=(n // tm, in_w // tn),
        in_specs=[pl.BlockSpec((tm, d), lambda i, j: (i, 0)),
                  pl.BlockSpec((1, 6, d), lambda i, j: (i * tm // seq, 0, 0)),
                  pl.BlockSpec((d, tn), lambda i, j: (0, j))],
        out_specs=pl.BlockSpec((tm, tn), lambda i, j: (i, j)),
        out_shape=jax.ShapeDtypeStruct((n, in_w), BF16),
        scratch_shapes=[pltpu.VMEM((tm, d), BF16)],
        compiler_params=_cparams(("arbitrary", "arbitrary")),
        name="inproj",
    )(x2, mod3, w_in_bf)


def _mixer_kernel(cur_ref, pk_ref, pv_ref, prow_ref, sink_ref, cw_ref, o_ref, *, attn_w, kv_w, conv_w):
    nblk = pl.program_id(1)
    has_prev = nblk > 0
    qb = Q_BLOCK
    cur = cur_ref[...]
    k_cur = cur[:, attn_w:attn_w + kv_w]
    v_cur = cur[:, attn_w + kv_w:attn_w + 2 * kv_w]
    k_all = jnp.concatenate([pk_ref[...], k_cur], axis=0)
    v_all = jnp.concatenate([pv_ref[...], v_cur], axis=0)

    rows = GQA * qb
    qi = lax.broadcasted_iota(jnp.int32, (rows, 2 * qb), 0) % qb
    kj = lax.broadcasted_iota(jnp.int32, (rows, 2 * qb), 1)
    dist = qi + qb - kj
    kmin = jnp.where(has_prev, 0, qb)
    valid = (dist >= 0) & (dist < WINDOW) & (kj >= kmin)
    distf = dist.astype(F32)
    head_in_group = lax.broadcasted_iota(jnp.int32, (rows, 1), 0) // qb
    sinks = sink_ref[...]

    outs = []
    for g in range(N_KV_HEADS):
        q4 = jnp.concatenate(
            [cur[:, (g * GQA + j) * HEAD_DIM:(g * GQA + j + 1) * HEAD_DIM] for j in range(GQA)], axis=0)
        kg = k_all[:, g * HEAD_DIM:(g + 1) * HEAD_DIM]
        vg = v_all[:, g * HEAD_DIM:(g + 1) * HEAD_DIM]
        s = lax.dot_general(q4, kg, (((1,), (1,)), ((), ())), preferred_element_type=F32)
        s = s * (HEAD_DIM ** -0.5)
        slope = jnp.zeros((rows, 1), F32)
        sink = jnp.zeros((rows, 1), F32)
        for j in range(GQA):
            h = g * GQA + j
            sel = head_in_group == j
            slope = jnp.where(sel, 2.0 ** (-8.0 * (h + 1) / N_Q_HEADS), slope)
            sink = jnp.where(sel, sinks[:, h:h + 1], sink)
        s = jnp.where(valid, s - slope * distf, -jnp.inf)
        m = jnp.maximum(jnp.max(s, axis=-1, keepdims=True), sink)
        p = jnp.exp(s - m)
        denom = jnp.sum(p, axis=-1, keepdims=True) + jnp.exp(sink - m)
        o4 = jnp.dot(p.astype(BF16), vg, preferred_element_type=F32) / denom
        outs.extend(o4[j * qb:(j + 1) * qb] for j in range(GQA))
    attn = jnp.concatenate(outs, axis=-1)

    c0 = attn_w + 2 * kv_w
    cb = cur[:, c0:c0 + conv_w].astype(F32)
    u = cur[:, c0 + conv_w:c0 + 2 * conv_w].astype(F32) * cur[:, c0 + 2 * conv_w:c0 + 3 * conv_w].astype(F32)
    prow = prow_ref[...]
    up = prow[:, c0 + conv_w:c0 + 2 * conv_w].astype(F32) * prow[:, c0 + 2 * conv_w:c0 + 3 * conv_w].astype(F32)
    up = up * jnp.where(has_prev, 1.0, 0.0)
    pm1 = up[15:16]
    pm2 = up[14:15]
    ri = lax.broadcasted_iota(jnp.int32, u.shape, 0)
    u1 = jnp.where(ri == 0, pm1, pltpu.roll(u, 1, 0))
    u2 = jnp.where(ri == 0, pm2, jnp.where(ri == 1, pm1, pltpu.roll(u, 2, 0)))
    cw = cw_ref[...]
    conv = cb * (cw[0:1] * u2 + cw[1:2] * u1 + cw[2:3] * u)
    o_ref[...] = jnp.concatenate([attn, conv], axis=-1).astype(BF16)


def _mixer(proj, sinks2, conv_w, batch, seq, attn_w, kv_w, conv_wd):
    n, in_w = proj.shape
    nb = seq // Q_BLOCK
    kv_blk0 = attn_w // kv_w
    sub16 = Q_BLOCK // 16

    def cur_map(b, i):
        return (b * nb + i, 0)

    def prev_map(col):
        return lambda b, i: (b * nb + jnp.maximum(i - 1, 0), col)

    def prow_map(b, i):
        return (jnp.maximum((b * nb + i) * sub16 - 1, 0), 0)

    kern = functools.partial(_mixer_kernel, attn_w=attn_w, kv_w=kv_w, conv_w=conv_wd)
    return pl.pallas_call(
        kern,
        grid=(batch, nb),
        in_specs=[pl.BlockSpec((Q_BLOCK, in_w), cur_map),
                  pl.BlockSpec((Q_BLOCK, kv_w), prev_map(kv_blk0)),
                  pl.BlockSpec((Q_BLOCK, kv_w), prev_map(kv_blk0 + 1)),
                  pl.BlockSpec((16, in_w), prow_map),
                  pl.BlockSpec((1, N_Q_HEADS), lambda b, i: (0, 0)),
                  pl.BlockSpec((CONV_K, conv_wd), lambda b, i: (0, 0))],
        out_specs=pl.BlockSpec((Q_BLOCK, attn_w + conv_wd), cur_map),
        out_shape=jax.ShapeDtypeStruct((n, attn_w + conv_wd), BF16),
        compiler_params=_cparams(("arbitrary", "arbitrary")),
        name="mixer",
    )(proj, proj, proj, proj, sinks2, conv_w)


def _split_bf16(v):
    hi = v.astype(BF16)
    lo = (v - hi.astype(F32)).astype(BF16)
    return hi, lo


def _outproj_kernel(mix_ref, x_ref, mod_ref, w_ref, g_ref, b_ref, wr_ref, x1_ref, h2_ref, lg_ref):
    m = mod_ref[0]
    mix = jnp.dot(mix_ref[...], w_ref[...], preferred_element_type=F32)
    x1 = _layer_norm(ALPHA * x_ref[...] + (1.0 + m[2:3]) * mix, g_ref[...], b_ref[...])
    x1_ref[...] = x1
    h2 = x1 * (1.0 + m[4:5]) + m[3:4]
    tm, d = h2.shape
    words = _pack_pairs(h2[:, :d // 2], h2[:, d // 2:])
    for s in range(TOKEN_ROWS):
        h2_ref[pl.ds(s, tm, stride=TOKEN_ROWS), :] = words[:, s * LANES:(s + 1) * LANES]
    h_hi, h_lo = _split_bf16(h2)
    w_hi, w_lo = _split_bf16(wr_ref[...])
    lg_ref[...] = (jnp.dot(h_hi, w_hi, preferred_element_type=F32)
                   + (jnp.dot(h_hi, w_lo, preferred_element_type=F32)
                      + jnp.dot(h_lo, w_hi, preferred_element_type=F32)))


def _outproj(mix, x2, mod3, w_out_bf, ln_g, ln_b, w_router, seq):
    n, d = x2.shape
    tm = OUTPROJ_TM
    ne = w_router.shape[1]
    row = lambda i: (i, 0)
    const = lambda i: (0, 0)
    return pl.pallas_call(
        _outproj_kernel,
        grid=(n // tm,),
        in_specs=[pl.BlockSpec((tm, d), row),
                  pl.BlockSpec((tm, d), row),
                  pl.BlockSpec((1, 6, d), lambda i: (i * tm // seq, 0, 0)),
                  pl.BlockSpec((d, d), const),
                  pl.BlockSpec((1, d), const),
                  pl.BlockSpec((1, d), const),
                  pl.BlockSpec((d, ne), const)],
        out_specs=[pl.BlockSpec((tm, d), row),
                   pl.BlockSpec((tm * TOKEN_ROWS, LANES), row),
                   pl.BlockSpec((tm, ne), row)],
        out_shape=[jax.ShapeDtypeStruct((n, d), F32),
                   jax.ShapeDtypeStruct((n * TOKEN_ROWS, LANES), U32),
                   jax.ShapeDtypeStruct((n, ne), F32)],
        compiler_params=_cparams(("arbitrary",)),
        name="outproj",
    )(mix, x2, mod3, w_out_bf, ln_g, ln_b, w_router)


def _col_argmax(tiles, row_f):
    m = tiles[0]
    for t in tiles[1:]:
        m = jnp.maximum(m, t)
    m = jnp.max(m, axis=0, keepdims=True)
    idx = None
    for t, r in zip(tiles, row_f):
        c = jnp.where(t == m, r, float(N_EXPERTS))
        idx = c if idx is None else jnp.minimum(idx, c)
    return m, jnp.min(idx, axis=0, keepdims=True)


def _route_kernel(lg_ref, bias_ref, eidx_ref, w_ref, cnt_ref):
    lt = lg_ref[...].T
    tm = lt.shape[1]
    neg = -jnp.inf
    sub = lax.broadcasted_iota(jnp.int32, (GROUP_SIZE, tm), 0).astype(F32)
    row_f = [sub + float(g * GROUP_SIZE) for g in range(N_GROUPS)]
    scores = [jax.nn.sigmoid(lt[g * GROUP_SIZE:(g + 1) * GROUP_SIZE]) for g in range(N_GROUPS)]
    sel = [scores[g] + bias_ref[g * GROUP_SIZE:(g + 1) * GROUP_SIZE, :] for g in range(N_GROUPS)]
    gs = []
    for g in range(N_GROUPS):
        m1, i1 = _col_argmax([sel[g]], [row_f[g]])
        m2 = jnp.max(jnp.where(row_f[g] == i1, neg, sel[g]), axis=0, keepdims=True)
        gs.append(m1 + m2)
    cand = []
    for g in range(N_GROUPS):
        rank = jnp.zeros((1, tm), F32)
        for o in range(N_GROUPS):
            if o == g:
                continue
            ahead = (gs[o] >= gs[g]) if o < g else (gs[o] > gs[g])
            rank = rank + jnp.where(ahead, 1.0, 0.0)
        cand.append(jnp.where(rank < TOPK_GROUPS, sel[g], neg))
    idxs, ws = [], []
    chosen = [jnp.zeros((GROUP_SIZE, tm), F32) for _ in range(N_GROUPS)]
    for _ in range(TOP_K):
        _, ik = _col_argmax(cand, row_f)
        wk = jnp.zeros((1, tm), F32)
        for g in range(N_GROUPS):
            hit = row_f[g] == ik
            wk = wk + jnp.sum(jnp.where(hit, scores[g], 0.0), axis=0, keepdims=True)
            cand[g] = jnp.where(hit, neg, cand[g])
            chosen[g] = jnp.where(hit, 1.0, chosen[g])
        ws.append(wk)
        idxs.append(ik)
    for g in range(N_GROUPS):
        cnt_ref[0, g * GROUP_SIZE:(g + 1) * GROUP_SIZE, :] = jnp.sum(chosen[g], axis=1, keepdims=True)
    wsum = ws[0]
    for k in range(1, TOP_K):
        wsum = wsum + ws[k]
    eidx_ref[...] = jnp.concatenate(idxs, axis=0).astype(jnp.int32)
    w_ref[...] = jnp.concatenate([wk / wsum * ROUTED_SCALE for wk in ws], axis=0)


def _route(logits, bias_col):
    n, ne = logits.shape
    tm = ROUTE_TM
    col = lambda i: (0, i)
    return pl.pallas_call(
        _route_kernel,
        grid=(n // tm,),
        in_specs=[pl.BlockSpec((tm, ne), lambda i: (i, 0)), pl.BlockSpec((ne, 1), lambda i: (0, 0))],
        out_specs=[pl.BlockSpec((TOP_K, tm), col), pl.BlockSpec((TOP_K, tm), col),
                   pl.BlockSpec((1, ne, 1), lambda i: (i, 0, 0))],
        out_shape=[jax.ShapeDtypeStruct((TOP_K, n), jnp.int32), jax.ShapeDtypeStruct((TOP_K, n), F32),
                   jax.ShapeDtypeStruct((n // tm, ne, 1), F32)],
        compiler_params=_cparams(("arbitrary",)),
        name="route",
    )(logits, bias_col)


def _issue_rows(lo, hi, issue_one):
    n_full = (hi - lo) // ISSUE_UNROLL

    def chunk(c, carry):
        for u in range(ISSUE_UNROLL):
            issue_one(lo + c * ISSUE_UNROLL + u)
        return carry

    def tail(r, carry):
        issue_one(r)
        return carry

    lax.fori_loop(0, n_full, chunk, 0)
    lax.fori_loop(lo + n_full * ISSUE_UNROLL, hi, tail, 0)


def _hbm_slab(ref, row):
    return ref.at[pl.ds(pl.multiple_of(row * TOKEN_ROWS, TOKEN_ROWS), TOKEN_ROWS), :]


def _moe_kernel(be_ref, new_ref, nexte_ref, epar_ref, nused_ref, tgt_hbm, h_hbm, wg_hbm, wu_hbm, wd_hbm, ys_hbm,
                idx_s, xbuf, ybuf, wg_f, wu_f, wd_f, wg_s, wu_s, wd_s, sem_i, sem_g, sem_s, sem_w, *, n_tok):
    s = pl.program_id(0)
    tm = MOE_TM
    per = IDX_CHUNK // tm
    per_log2 = per.bit_length() - 1
    n_used = nused_ref[0]
    slab = TOKEN_ROWS
    buf_rows = tm * SLAB_PITCH
    dump_row0 = TOP_K * n_tok
    slot = s & 1

    def staged(buf, base, r):
        return buf.at[pl.ds(pl.multiple_of(base + r * SLAB_PITCH, SUBLANES), slab), :]

    def idx_copy(c):
        return pltpu.make_async_copy(
            tgt_hbm.at[pl.ds(pl.multiple_of(c * IDX_CHUNK, IDX_CHUNK), IDX_CHUNK)],
            idx_s.at[pl.ds(pl.multiple_of((c & 1) * IDX_CHUNK, IDX_CHUNK), IDX_CHUNK)], sem_i)

    def idx_base(b):
        return ((b >> per_log2) & 1) * IDX_CHUNK + (b & (per - 1)) * tm

    def weight_copies(e, p):
        return (pltpu.make_async_copy(wg_hbm.at[e], wg_f.at[p], sem_w.at[p]),
                pltpu.make_async_copy(wu_hbm.at[e], wu_f.at[p], sem_w.at[p]),
                pltpu.make_async_copy(wd_hbm.at[e], wd_f.at[p], sem_w.at[p]))

    def for_rows(inline, body):
        if inline:
            for r in range(tm):
                body(r, r % 2)
        else:
            def pair(c, carry):
                body(2 * c, 0)
                body(2 * c + 1, 1)
                return carry
            lax.fori_loop(0, tm // 2, pair, 0)

    def issue_gather(b, to_slot, inline):
        ibase = idx_base(b)
        xbase = to_slot * buf_rows

        def one(r, prio):
            tok = idx_s[ibase + r] & (n_tok - 1)
            pltpu.make_async_copy(_hbm_slab(h_hbm, tok), staged(xbuf, xbase, r),
                                  sem_g.at[to_slot]).start(priority=0)
        for_rows(inline, one)

    def issue_scatter(b, from_slot, to_dump, inline):
        ibase = idx_base(b)
        ybase = from_slot * buf_rows

        def one(r, prio):
            tgt = jnp.where(to_dump, dump_row0 + tm + r, idx_s[ibase + r])
            pltpu.make_async_copy(staged(ybuf, ybase, r), _hbm_slab(ys_hbm, tgt),
                                  sem_s.at[from_slot]).start(priority=prio)
        for_rows(inline, one)

    def wait_gather(at_slot):
        v = xbuf.at[pl.ds(pl.multiple_of(at_slot * buf_rows, SUBLANES), buf_rows), :]
        pltpu.make_async_copy(h_hbm.at[pl.ds(0, buf_rows), :], v, sem_g.at[at_slot]).wait()

    def wait_scatter(at_slot):
        v = ybuf.at[pl.ds(pl.multiple_of(at_slot * buf_rows, SUBLANES), buf_rows), :]
        pltpu.make_async_copy(v, ys_hbm.at[pl.ds(0, buf_rows), :], sem_s.at[at_slot]).wait()

    @pl.when(s == 0)
    def _():
        ybuf[...] = jnp.zeros(ybuf.shape, U32)
        first = idx_copy(0)
        first.start()
        first.wait()
        init = pltpu.make_async_copy(ybuf, ys_hbm.at[pl.ds(pl.multiple_of(dump_row0 * slab, slab), 2 * buf_rows), :],
                                     sem_i)
        init.start()
        init.wait()
        issue_gather(0, 0, inline=False)
        for cp in weight_copies(be_ref[0], 0):
            cp.start()

    @pl.when(s < n_used)
    def _():
        @pl.when((s & (per - 1)) == 1)
        def _():
            idx_copy((s >> per_log2) + 1).start()

        @pl.when(((s + 1) & (per - 1)) == 0)
        def _():
            idx_copy((s + 1) >> per_log2).wait()

        @pl.when(new_ref[s] == 1)
        def _():
            p = epar_ref[s]

            @pl.when(nexte_ref[s] >= 0)
            def _():
                for cp in weight_copies(nexte_ref[s], 1 - p):
                    cp.start(priority=1)

            for cp in weight_copies(be_ref[s], p):
                cp.wait()
            wg_s[...] = wg_f[p].astype(BF16)
            wu_s[...] = wu_f[p].astype(BF16)
            wd_s[...] = wd_f[p].astype(BF16)

        wait_gather(slot)

        @pl.when(s >= 1)
        def _():
            wait_scatter(slot)

        issue_gather(s + 1, 1 - slot, inline=True)
        issue_scatter(jnp.maximum(s - 1, 0), 1 - slot, s == 0, inline=True)
        base = slot * buf_rows
        x_lo, x_hi = _unpack_pairs(_slab_rows_to_matrix(xbuf, base, tm, SLAB_PITCH))
        x = jnp.concatenate([x_lo.astype(BF16), x_hi.astype(BF16)], axis=-1)
        gate = jnp.dot(x, wg_s[...], preferred_element_type=F32)
        up = jnp.dot(x, wu_s[...], preferred_element_type=F32)
        act = (_silu(gate) * up).astype(BF16)
        y = jnp.dot(act, wd_s[...], preferred_element_type=F32)
        half = slab * LANES
        words = _pack_pairs(y[:, :half], y[:, half:])
        for j in range(slab):
            ybuf[pl.ds(base + j, tm, stride=SLAB_PITCH), :] = words[:, j * LANES:(j + 1) * LANES]

    @pl.when(s == n_used)
    def _():
        last = (s - 1) & (per - 1)

        @pl.when((last == 1) | (last == 2))
        def _():
            idx_copy(((s - 1) >> per_log2) + 1).wait()

        wait_gather(slot)
        wait_scatter(slot)
        issue_scatter(s - 1, 1 - slot, False, inline=False)
        wait_scatter(1 - slot)


def _moe(block_e, block_new, block_nexte, block_epar, n_used, row_tgt, h2d, w_gate, w_up, w_down):
    n = h2d.shape[0] // TOKEN_ROWS
    assert n & (n - 1) == 0
    d = 2 * TOKEN_ROWS * LANES
    de = w_gate.shape[2]
    tm = MOE_TM
    n_blk = block_e.shape[0]
    grid_spec = pltpu.PrefetchScalarGridSpec(
        num_scalar_prefetch=5,
        grid=(n_blk,),
        in_specs=[pl.BlockSpec(memory_space=pl.ANY)] * 5,
        out_specs=pl.BlockSpec(memory_space=pl.ANY),
        scratch_shapes=[pltpu.SMEM((2 * IDX_CHUNK,), jnp.int32),
                        pltpu.VMEM((2 * tm * SLAB_PITCH, LANES), U32),
                        pltpu.VMEM((2 * tm * SLAB_PITCH, LANES), U32),
                        pltpu.VMEM((2, d, de), F32),
                        pltpu.VMEM((2, d, de), F32),
                        pltpu.VMEM((2, de, d), F32),
                        pltpu.VMEM((d, de), BF16),
                        pltpu.VMEM((d, de), BF16),
                        pltpu.VMEM((de, d), BF16),
                        pltpu.SemaphoreType.DMA,
                        pltpu.SemaphoreType.DMA((2,)),
                        pltpu.SemaphoreType.DMA((2,)),
                        pltpu.SemaphoreType.DMA((2,))])
    return pl.pallas_call(
        functools.partial(_moe_kernel, n_tok=n),
        grid_spec=grid_spec,
        out_shape=jax.ShapeDtypeStruct(((TOP_K * n + 2 * tm) * TOKEN_ROWS, LANES), U32),
        compiler_params=_cparams(("arbitrary",)),
        name="moe",
    )(block_e, block_new, block_nexte, block_epar, n_used, row_tgt, h2d, w_gate, w_up, w_down)


def _final_kernel(gw_hbm, *refs):
    ys_refs = refs[:TOP_K]
    h_ref, x1_ref, mod_ref, wg_ref, wu_ref, wd_ref, g_ref, b_ref, o_ref, gw_s, acc_ref, sem_i = refs[TOP_K:]
    i = pl.program_id(0)
    tm = x1_ref.shape[0]
    per_step = tm * TOP_K
    cp = pltpu.make_async_copy(gw_hbm.at[pl.ds(pl.multiple_of(i * per_step, per_step), per_step)], gw_s, sem_i)
    cp.start()
    cp.wait()

    hi_base = tm * SLAB_PITCH

    def combine(t, carry):
        rows = pl.ds(pl.multiple_of(t * TOKEN_ROWS, TOKEN_ROWS), TOKEN_ROWS)
        acc_lo = acc_hi = None
        for k in range(TOP_K):
            lo, hi = _unpack_pairs(ys_refs[k][rows, :])
            g = gw_s[t * TOP_K + k]
            acc_lo = g * lo if acc_lo is None else acc_lo + g * lo
            acc_hi = g * hi if acc_hi is None else acc_hi + g * hi
        acc_ref[pl.ds(pl.multiple_of(t * SLAB_PITCH, SUBLANES), TOKEN_ROWS), :] = acc_lo
        acc_ref[pl.ds(pl.multiple_of(hi_base + t * SLAB_PITCH, SUBLANES), TOKEN_ROWS), :] = acc_hi
        return carry

    lax.fori_loop(0, tm, combine, 0, unroll=8)
    moe = jnp.concatenate([_slab_rows_to_matrix(acc_ref, 0, tm, SLAB_PITCH),
                           _slab_rows_to_matrix(acc_ref, hi_base, tm, SLAB_PITCH)], axis=-1)
    h_lo, h_hi = _unpack_pairs(_slab_rows_to_matrix(h_ref, 0, tm, TOKEN_ROWS))
    h = jnp.concatenate([h_lo.astype(BF16), h_hi.astype(BF16)], axis=-1)
    gate = jnp.dot(h, wg_ref[...], preferred_element_type=F32)
    up = jnp.dot(h, wu_ref[...], preferred_element_type=F32)
    shared = jnp.dot((_silu(gate) * up).astype(BF16), wd_ref[...], preferred_element_type=F32)
    m = mod_ref[0]
    y = ALPHA * x1_ref[...] + (1.0 + m[5:6]) * (moe + shared)
    o_ref[...] = _layer_norm(y, g_ref[...], b_ref[...])


def _final(gate_w, ys, h2d, x1, mod3, wsg, wsu, wsd, ln_g, ln_b, seq):
    n, d = x1.shape
    de = wsg.shape[1]
    tm = FINAL_TM
    per_step = tm * TOP_K
    assert per_step % IDX_CHUNK == 0
    row = lambda i: (i, 0)
    const = lambda i: (0, 0)
    slot_rows = lambda k: (lambda i: (k * (n // tm) + i, 0))
    return pl.pallas_call(
        _final_kernel,
        grid=(n // tm,),
        in_specs=[pl.BlockSpec(memory_space=pl.ANY)]
                 + [pl.BlockSpec((tm * TOKEN_ROWS, LANES), slot_rows(k)) for k in range(TOP_K)]
                 + [pl.BlockSpec((tm * TOKEN_ROWS, LANES), row),
                  pl.BlockSpec((tm, d), row),
                  pl.BlockSpec((1, 6, d), lambda i: (i * tm // seq, 0, 0)),
                  pl.BlockSpec((d, de), const),
                  pl.BlockSpec((d, de), const),
                  pl.BlockSpec((de, d), const),
                  pl.BlockSpec((1, d), const),
                  pl.BlockSpec((1, d), const)],
        out_specs=pl.BlockSpec((tm, d), row),
        out_shape=jax.ShapeDtypeStruct((n, d), F32),
        scratch_shapes=[pltpu.SMEM((per_step,), F32),
                        pltpu.VMEM((2 * tm * SLAB_PITCH, LANES), F32),
                        pltpu.SemaphoreType.DMA],
        compiler_params=_cparams(("arbitrary",)),
        name="final",
    )(gate_w, *([ys] * TOP_K), h2d, x1, mod3, wsg, wsu, wsd, ln_g, ln_b)


def _dispatch_tables(eidx, tile_counts, n):
    tm = MOE_TM
    a = n * TOP_K
    i32 = jnp.int32
    experts = jnp.arange(N_EXPERTS, dtype=i32)
    counts = jnp.sum(tile_counts, axis=(0, 2)).astype(i32)
    padded = (counts + tm - 1) // tm * tm
    pad_end = jnp.cumsum(padded)
    starts = pad_end - padded
    n_blk = a // tm + N_EXPERTS + 1
    blk_start = jnp.arange(n_blk, dtype=i32) * tm
    n_used = pad_end[-1] // tm
    in_use = jnp.arange(n_blk) < n_used
    raw_e = jnp.minimum(jnp.sum((pad_end[None, :] <= blk_start[:, None]).astype(i32), axis=1), N_EXPERTS - 1)
    last_e = jnp.sum(jnp.where(jnp.arange(n_blk) == n_used - 1, raw_e, 0))
    block_e = jnp.where(in_use, raw_e, last_e)
    onehot = block_e[:, None] == experts[None, :]
    block_new = jnp.concatenate([jnp.ones((1,), i32), (block_e[1:] != block_e[:-1]).astype(i32)])
    has_rows = counts > 0
    later = (experts[None, :] > experts[:, None]) & has_rows[None, :]
    next_e = jnp.min(jnp.where(later, experts[None, :], N_EXPERTS), axis=1)
    next_e = jnp.where(next_e == N_EXPERTS, -1, next_e)
    parity_e = (jnp.cumsum(has_rows.astype(i32)) - 1) & 1
    block_nexte = jnp.sum(jnp.where(onehot, next_e[None, :], 0), axis=1).astype(i32)
    block_epar = jnp.sum(jnp.where(onehot, parity_e[None, :], 0), axis=1).astype(i32)
    cnt_b = jnp.sum(jnp.where(onehot, counts[None, :], 0), axis=1)
    start_b = jnp.sum(jnp.where(onehot, starts[None, :], 0), axis=1)
    block_nv = jnp.where(in_use, jnp.clip(cnt_b - (blk_start - start_b), 0, tm), 0)
    dummy_keys = jnp.where(jnp.arange(tm, dtype=i32)[None, :] < (padded - counts)[:, None],
                           experts[:, None], N_EXPERTS).reshape(-1)
    keys = jnp.concatenate([eidx.reshape(-1), dummy_keys, jnp.full((tm,), N_EXPERTS, i32)])
    tgt = jnp.arange(a, dtype=i32)
    bits = (a - 1).bit_length() + 1
    low = (1 << bits) - 1
    payload = jnp.concatenate([tgt, jnp.full((n_blk * tm - a,), low, i32)])
    row_tgt = lax.sort(keys * (1 << bits) + payload) & low
    r = jnp.arange(tm, dtype=i32)[None, :]
    dump = a + (jnp.arange(n_blk, dtype=i32)[:, None] & 1) * tm + r
    row_tgt = jnp.where(r >= block_nv[:, None], dump, row_tgt.reshape(n_blk, tm)).reshape(-1)
    row_tgt = jnp.pad(row_tgt, (0, -(n_blk * tm) % IDX_CHUNK))
    return block_e.astype(i32), block_new, block_nexte, block_epar, n_used.astype(i32).reshape(1), row_tgt


def kernel(x, c, w_mod, b_mod, w_in, conv_w, attn_sinks, w_out, ln1_g, ln1_b, w_router, router_bias,
           w_gate, w_up, w_down, ws_gate, ws_up, ws_down, ln2_g, ln2_b):
    b, s, d = x.shape
    n = b * s
    attn_w = N_Q_HEADS * HEAD_DIM
    kv_w = N_KV_HEADS * HEAD_DIM
    conv_wd = d - attn_w
    in_w = attn_w + 2 * kv_w + 3 * conv_wd
    x2 = x.reshape(n, d)
    c8 = jnp.zeros((SUBLANES, d), F32).at[:b].set(c)
    for l in range(DEPTH):
        mod = _mod(c8, w_mod[l], b_mod[l].reshape(1, -1))[:b]
        mod3 = mod.reshape(b, 6, d)
        proj = _inproj(x2, mod3, w_in[l].astype(BF16), s)
        mix = _mixer(proj, attn_sinks[l].reshape(1, -1), conv_w[l], b, s, attn_w, kv_w, conv_wd)
        x1, h2d, logits = _outproj(mix, x2, mod3, w_out[l].astype(BF16), ln1_g[l].reshape(1, -1),
                                   ln1_b[l].reshape(1, -1), w_router[l], s)
        eidx, gate_w, tile_counts = _route(logits, router_bias[l].reshape(-1, 1))
        block_e, block_new, block_nexte, block_epar, n_used, row_tgt = _dispatch_tables(eidx, tile_counts, n)
        ys = _moe(block_e, block_new, block_nexte, block_epar, n_used, row_tgt, h2d,
                  w_gate[l], w_up[l], w_down[l])
        x2 = _final(gate_w.T.reshape(-1), ys, h2d, x1, mod3,
                    ws_gate[l].astype(BF16), ws_up[l].astype(BF16), ws_down[l].astype(BF16),
                    ln2_g[l].reshape(1, -1), ln2_b[l].reshape(1, -1), s)
    return x2.reshape(b, s, d)
```

```python
import functools

import jax
import jax.numpy as jnp
from jax import lax
from jax.experimental import pallas as pl
from jax.experimental.pallas import tpu as pltpu

HEAD_DIM = 64
N_Q_HEADS = 16
N_KV_HEADS = 4
GQA = N_Q_HEADS // N_KV_HEADS
CONV_K = 3
WINDOW = 128
Q_BLOCK = 128
N_EXPERTS = 64
TOP_K = 8
N_GROUPS = 8
GROUP_SIZE = N_EXPERTS // N_GROUPS
TOPK_GROUPS = 4
ROUTED_SCALE = 2.5
DEPTH = 1
ALPHA = (2.0 * DEPTH) ** 0.25
LN_EPS = 1e-5

LANES = 128
SUBLANES = 8
TOKEN_ROWS = 8
SLAB_PITCH = 8
VMEM_LIMIT = 56 * 1024 * 1024

MOD_TN = 1024
INPROJ_TM = 1024
INPROJ_TN = 2304
OUTPROJ_TM = 256
ROUTE_TM = 512
MOE_TM = 256
FINAL_TM = 256
IDX_CHUNK = 1024
ISSUE_UNROLL = 8

F32 = jnp.float32
BF16 = jnp.bfloat16


def _cparams(sem):
    return pltpu.CompilerParams(dimension_semantics=sem, vmem_limit_bytes=VMEM_LIMIT)


def _silu(v):
    return v * jax.nn.sigmoid(v)


U32 = jnp.uint32
HI_MASK = 0xFFFF0000


def _pack_pairs(lo, hi):
    lo_bits = lax.bitcast_convert_type(lo.astype(BF16).astype(F32), U32) >> 16
    hi_bits = lax.bitcast_convert_type(hi.astype(BF16).astype(F32), U32) & U32(HI_MASK)
    return lo_bits | hi_bits


def _unpack_pairs(w):
    return (lax.bitcast_convert_type(w << 16, F32), lax.bitcast_convert_type(w & U32(HI_MASK), F32))


def _slab_rows_to_matrix(ref, base, tm, pitch):
    return jnp.concatenate([ref[pl.ds(base + s, tm, stride=pitch), :] for s in range(TOKEN_ROWS)], axis=-1)


def _layer_norm(y, g, b):
    mu = jnp.mean(y, axis=-1, keepdims=True)
    yc = y - mu
    var = jnp.mean(yc * yc, axis=-1, keepdims=True)
    return yc * lax.rsqrt(var + LN_EPS) * g + b


def _mod_kernel(c_ref, w_ref, b_ref, o_ref):
    cs = _silu(c_ref[...]).astype(BF16)
    o_ref[...] = jnp.dot(cs, w_ref[...].astype(BF16), preferred_element_type=F32) + b_ref[...]


def _mod(c8, w_mod, b_mod):
    d, n = w_mod.shape
    return pl.pallas_call(
        _mod_kernel,
        grid=(n // MOD_TN,),
        in_specs=[pl.BlockSpec((SUBLANES, d), lambda j: (0, 0)),
                  pl.BlockSpec((d, MOD_TN), lambda j: (0, j)),
                  pl.BlockSpec((1, MOD_TN), lambda j: (0, j))],
        out_specs=pl.BlockSpec((SUBLANES, MOD_TN), lambda j: (0, j)),
        out_shape=jax.ShapeDtypeStruct((SUBLANES, n), F32),
        compiler_params=_cparams(("arbitrary",)),
        name="mod",
    )(c8, w_mod, b_mod)


def _inproj_kernel(x_ref, mod_ref, w_ref, o_ref, h_ref):
    j = pl.program_id(1)

    @pl.when(j == 0)
    def _():
        m = mod_ref[0]
        h_ref[...] = (x_ref[...] * (1.0 + m[1:2]) + m[0:1]).astype(BF16)

    o_ref[...] = jnp.dot(h_ref[...], w_ref[...], preferred_element_type=F32).astype(BF16)


def _inproj(x2, mod3, w_in_bf, seq):
    n, d = x2.shape
    in_w = w_in_bf.shape[1]
    tm, tn = INPROJ_TM, INPROJ_TN
    return pl.pallas_call(
        _inproj_kernel,
        grid=(n // tm, in_w // tn),
        in_specs=[pl.BlockSpec((tm, d), lambda i, j: (i, 0)),
                  pl.BlockSpec((1, 6, d), lambda i, j: (i * tm // seq, 0, 0)),
                  pl.BlockSpec((d, tn), lambda i, j: (0, j))],
        out_specs=pl.BlockSpec((tm, tn), lambda i, j: (i, j)),
        out_shape=jax.ShapeDtypeStruct((n, in_w), BF16),
        scratch_shapes=[pltpu.VMEM((tm, d), BF16)],
        compiler_params=_cparams(("arbitrary", "arbitrary")),
        name="inproj",
    )(x2, mod3, w_in_bf)


def _mixer_kernel(cur_ref, pk_ref, pv_ref, prow_ref, sink_ref, cw_ref, o_ref, *, attn_w, kv_w, conv_w):
    nblk = pl.program_id(1)
    has_prev = nblk > 0
    qb = Q_BLOCK
    cur = cur_ref[...]
    k_cur = cur[:, attn_w:attn_w + kv_w]
    v_cur = cur[:, attn_w + kv_w:attn_w + 2 * kv_w]
    k_all = jnp.concatenate([pk_ref[...], k_cur], axis=0)
    v_all = jnp.concatenate([pv_ref[...], v_cur], axis=0)

    rows = GQA * qb
    qi = lax.broadcasted_iota(jnp.int32, (rows, 2 * qb), 0) % qb
    kj = lax.broadcasted_iota(jnp.int32, (rows, 2 * qb), 1)
    dist = qi + qb - kj
    kmin = jnp.where(has_prev, 0, qb)
    valid = (dist >= 0) & (dist < WINDOW) & (kj >= kmin)
    distf = dist.astype(F32)
    head_in_group = lax.broadcasted_iota(jnp.int32, (rows, 1), 0) // qb
    sinks = sink_ref[...]

    outs = []
    for g in range(N_KV_HEADS):
        q4 = jnp.concatenate(
            [cur[:, (g * GQA + j) * HEAD_DIM:(g * GQA + j + 1) * HEAD_DIM] for j in range(GQA)], axis=0)
        kg = k_all[:, g * HEAD_DIM:(g + 1) * HEAD_DIM]
        vg = v_all[:, g * HEAD_DIM:(g + 1) * HEAD_DIM]
        s = lax.dot_general(q4, kg, (((1,), (1,)), ((), ())), preferred_element_type=F32)
        s = s * (HEAD_DIM ** -0.5)
        slope = jnp.zeros((rows, 1), F32)
        sink = jnp.zeros((rows, 1), F32)
        for j in range(GQA):
            h = g * GQA + j
            sel = head_in_group == j
            slope = jnp.where(sel, 2.0 ** (-8.0 * (h + 1) / N_Q_HEADS), slope)
            sink = jnp.where(sel, sinks[:, h:h + 1], sink)
        s = jnp.where(valid, s - slope * distf, -jnp.inf)
        m = jnp.maximum(jnp.max(s, axis=-1, keepdims=True), sink)
        p = jnp.exp(s - m)
        denom = jnp.sum(p, axis=-1, keepdims=True) + jnp.exp(sink - m)
        o4 = jnp.dot(p.astype(BF16), vg, preferred_element_type=F32) / denom
        outs.extend(o4[j * qb:(j + 1) * qb] for j in range(GQA))
    attn = jnp.concatenate(outs, axis=-1)

    c0 = attn_w + 2 * kv_w
    cb = cur[:, c0:c0 + conv_w].astype(F32)
    u = cur[:, c0 + conv_w:c0 + 2 * conv_w].astype(F32) * cur[:, c0 + 2 * conv_w:c0 + 3 * conv_w].astype(F32)
    prow = prow_ref[...]
    up = prow[:, c0 + conv_w:c0 + 2 * conv_w].astype(F32) * prow[:, c0 + 2 * conv_w:c0 + 3 * conv_w].astype(F32)
    up = up * jnp.where(has_prev, 1.0, 0.0)
    pm1 = up[15:16]
    pm2 = up[14:15]
    ri = lax.broadcasted_iota(jnp.int32, u.shape, 0)
    u1 = jnp.where(ri == 0, pm1, pltpu.roll(u, 1, 0))
    u2 = jnp.where(ri == 0, pm2, jnp.where(ri == 1, pm1, pltpu.roll(u, 2, 0)))
    cw = cw_ref[...]
    conv = cb * (cw[0:1] * u2 + cw[1:2] * u1 + cw[2:3] * u)
    o_ref[...] = jnp.concatenate([attn, conv], axis=-1).astype(BF16)


def _mixer(proj, sinks2, conv_w, batch, seq, attn_w, kv_w, conv_wd):
    n, in_w = proj.shape
    nb = seq // Q_BLOCK
    kv_blk0 = attn_w // kv_w
    sub16 = Q_BLOCK // 16

    def cur_map(b, i):
        return (b * nb + i, 0)

    def prev_map(col):
        return lambda b, i: (b * nb + jnp.maximum(i - 1, 0), col)

    def prow_map(b, i):
        return (jnp.maximum((b * nb + i) * sub16 - 1, 0), 0)

    kern = functools.partial(_mixer_kernel, attn_w=attn_w, kv_w=kv_w, conv_w=conv_wd)
    return pl.pallas_call(
        kern,
        grid=(batch, nb),
        in_specs=[pl.BlockSpec((Q_BLOCK, in_w), cur_map),
                  pl.BlockSpec((Q_BLOCK, kv_w), prev_map(kv_blk0)),
                  pl.BlockSpec((Q_BLOCK, kv_w), prev_map(kv_blk0 + 1)),
                  pl.BlockSpec((16, in_w), prow_map),
                  pl.BlockSpec((1, N_Q_HEADS), lambda b, i: (0, 0)),
                  pl.BlockSpec((CONV_K, conv_wd), lambda b, i: (0, 0))],
        out_specs=pl.BlockSpec((Q_BLOCK, attn_w + conv_wd), cur_map),
        out_shape=jax.ShapeDtypeStruct((n, attn_w + conv_wd), BF16),
        compiler_params=_cparams(("arbitrary", "arbitrary")),
        name="mixer",
    )(proj, proj, proj, proj, sinks2, conv_w)


def _split_bf16(v):
    hi = v.astype(BF16)
    lo = (v - hi.astype(F32)).astype(BF16)
    return hi, lo


def _outproj_kernel(mix_ref, x_ref, mod_ref, w_ref, g_ref, b_ref, wr_ref, x1_ref, h2_ref, lg_ref):
    m = mod_ref[0]
    mix = jnp.dot(mix_ref[...], w_ref[...], preferred_element_type=F32)
    x1 = _layer_norm(ALPHA * x_ref[...] + (1.0 + m[2:3]) * mix, g_ref[...], b_ref[...])
    x1_ref[...] = x1
    h2 = x1 * (1.0 + m[4:5]) + m[3:4]
    tm, d = h2.shape
    words = _pack_pairs(h2[:, :d // 2], h2[:, d // 2:])
    for s in range(TOKEN_ROWS):
        h2_ref[pl.ds(s, tm, stride=TOKEN_ROWS), :] = words[:, s * LANES:(s + 1) * LANES]
    h_hi, h_lo = _split_bf16(h2)
    w_hi, w_lo = _split_bf16(wr_ref[...])
    lg_ref[...] = (jnp.dot(h_hi, w_hi, preferred_element_type=F32)
                   + (jnp.dot(h_hi, w_lo, preferred_element_type=F32)
                      + jnp.dot(h_lo, w_hi, preferred_element_type=F32)))


def _outproj(mix, x2, mod3, w_out_bf, ln_g, ln_b, w_router, seq):
    n, d = x2.shape
    tm = OUTPROJ_TM
    ne = w_router.shape[1]
    row = lambda i: (i, 0)
    const = lambda i: (0, 0)
    return pl.pallas_call(
        _outproj_kernel,
        grid=(n // tm,),
        in_specs=[pl.BlockSpec((tm, d), row),
                  pl.BlockSpec((tm, d), row),
                  pl.BlockSpec((1, 6, d), lambda i: (i * tm // seq, 0, 0)),
                  pl.BlockSpec((d, d), const),
                  pl.BlockSpec((1, d), const),
                  pl.BlockSpec((1, d), const),
                  pl.BlockSpec((d, ne), const)],
        out_specs=[pl.BlockSpec((tm, d), row),
                   pl.BlockSpec((tm * TOKEN_ROWS, LANES), row),
                   pl.BlockSpec((tm, ne), row)],
        out_shape=[jax.ShapeDtypeStruct((n, d), F32),
                   jax.ShapeDtypeStruct((n * TOKEN_ROWS, LANES), U32),
                   jax.ShapeDtypeStruct((n, ne), F32)],
        compiler_params=_cparams(("arbitrary",)),
        name="outproj",
    )(mix, x2, mod3, w_out_bf, ln_g, ln_b, w_router)


def _col_argmax(tiles, row_f):
    m = tiles[0]
    for t in tiles[1:]:
        m = jnp.maximum(m, t)
    m = jnp.max(m, axis=0, keepdims=True)
    idx = None
    for t, r in zip(tiles, row_f):
        c = jnp.where(t == m, r, float(N_EXPERTS))
        idx = c if idx is None else jnp.minimum(idx, c)
    return m, jnp.min(idx, axis=0, keepdims=True)


def _route_kernel(lg_ref, bias_ref, eidx_ref, w_ref, cnt_ref):
    lt = lg_ref[...].T
    tm = lt.shape[1]
    neg = -jnp.inf
    sub = lax.broadcasted_iota(jnp.int32, (GROUP_SIZE, tm), 0).astype(F32)
    row_f = [sub + float(g * GROUP_SIZE) for g in range(N_GROUPS)]
    scores = [jax.nn.sigmoid(lt[g * GROUP_SIZE:(g + 1) * GROUP_SIZE]) for g in range(N_GROUPS)]
    sel = [scores[g] + bias_ref[g * GROUP_SIZE:(g + 1) * GROUP_SIZE, :] for g in range(N_GROUPS)]
    gs = []
    for g in range(N_GROUPS):
        m1, i1 = _col_argmax([sel[g]], [row_f[g]])
        m2 = jnp.max(jnp.where(row_f[g] == i1, neg, sel[g]), axis=0, keepdims=True)
        gs.append(m1 + m2)
    cand = []
    for g in range(N_GROUPS):
        rank = jnp.zeros((1, tm), F32)
        for o in range(N_GROUPS):
            if o == g:
                continue
            ahead = (gs[o] >= gs[g]) if o < g else (gs[o] > gs[g])
            rank = rank + jnp.where(ahead, 1.0, 0.0)
        cand.append(jnp.where(rank < TOPK_GROUPS, sel[g], neg))
    idxs, ws = [], []
    chosen = [jnp.zeros((GROUP_SIZE, tm), F32) for _ in range(N_GROUPS)]
    for _ in range(TOP_K):
        _, ik = _col_argmax(cand, row_f)
        wk = jnp.zeros((1, tm), F32)
        for g in range(N_GROUPS):
            hit = row_f[g] == ik
            wk = wk + jnp.sum(jnp.where(hit, scores[g], 0.0), axis=0, keepdims=True)
            cand[g] = jnp.where(hit, neg, cand[g])
            chosen[g] = jnp.where(hit, 1.0, chosen[g])
        ws.append(wk)
        idxs.append(ik)
    for g in range(N_GROUPS):
        cnt_ref[0, g * GROUP_SIZE:(g + 1) * GROUP_SIZE, :] = jnp.sum(chosen[g], axis=1, keepdims=True)
    wsum = ws[0]
    for k in range(1, TOP_K):
        wsum = wsum + ws[k]
    eidx_ref[...] = jnp.concatenate(idxs, axis=0).astype(jnp.int32)
    w_ref[...] = jnp.concatenate([wk / wsum * ROUTED_SCALE for wk in ws], axis=0)


def _route(logits, bias_col):
    n, ne = logits.shape
    tm = ROUTE_TM
    col = lambda i: (0, i)
    return pl.pallas_call(
        _route_kernel,
        grid=(n // tm,),
        in_specs=[pl.BlockSpec((tm, ne), lambda i: (i, 0)), pl.BlockSpec((ne, 1), lambda i: (0, 0))],
        out_specs=[pl.BlockSpec((TOP_K, tm), col), pl.BlockSpec((TOP_K, tm), col),
                   pl.BlockSpec((1, ne, 1), lambda i: (i, 0, 0))],
        out_shape=[jax.ShapeDtypeStruct((TOP_K, n), jnp.int32), jax.ShapeDtypeStruct((TOP_K, n), F32),
                   jax.ShapeDtypeStruct((n // tm, ne, 1), F32)],
        compiler_params=_cparams(("arbitrary",)),
        name="route",
    )(logits, bias_col)


def _issue_rows(lo, hi, issue_one):
    n_full = (hi - lo) // ISSUE_UNROLL

    def chunk(c, carry):
        for u in range(ISSUE_UNROLL):
            issue_one(lo + c * ISSUE_UNROLL + u)
        return carry

    def tail(r, carry):
        issue_one(r)
        return carry

    lax.fori_loop(0, n_full, chunk, 0)
    lax.fori_loop(lo + n_full * ISSUE_UNROLL, hi, tail, 0)


def _hbm_slab(ref, row):
    return ref.at[pl.ds(pl.multiple_of(row * TOKEN_ROWS, TOKEN_ROWS), TOKEN_ROWS), :]


def _moe_kernel(be_ref, new_ref, nexte_ref, epar_ref, nused_ref, tgt_hbm, h_hbm, wg_hbm, wu_hbm, wd_hbm, ys_hbm,
                idx_s, xbuf, ybuf, wg_f, wu_f, wd_f, wg_s, wu_s, wd_s, sem_i, sem_g, sem_s, sem_w, *, n_tok):
    s = pl.program_id(0)
    tm = MOE_TM
    per = IDX_CHUNK // tm
    per_log2 = per.bit_length() - 1
    n_used = nused_ref[0]
    slab = TOKEN_ROWS
    buf_rows = tm * SLAB_PITCH
    dump_row0 = TOP_K * n_tok
    slot = s & 1

    def staged(buf, base, r):
        return buf.at[pl.ds(pl.multiple_of(base + r * SLAB_PITCH, SUBLANES), slab), :]

    def idx_copy(c):
        return pltpu.make_async_copy(
            tgt_hbm.at[pl.ds(pl.multiple_of(c * IDX_CHUNK, IDX_CHUNK), IDX_CHUNK)],
            idx_s.at[pl.ds(pl.multiple_of((c & 1) * IDX_CHUNK, IDX_CHUNK), IDX_CHUNK)], sem_i)

    def idx_base(b):
        return ((b >> per_log2) & 1) * IDX_CHUNK + (b & (per - 1)) * tm

    def weight_copies(e, p):
        return (pltpu.make_async_copy(wg_hbm.at[e], wg_f.at[p], sem_w.at[p]),
                pltpu.make_async_copy(wu_hbm.at[e], wu_f.at[p], sem_w.at[p]),
                pltpu.make_async_copy(wd_hbm.at[e], wd_f.at[p], sem_w.at[p]))

    def for_rows(inline, body):
        if inline:
            for r in range(tm):
                body(r, r % 2)
        else:
            def pair(c, carry):
                body(2 * c, 0)
                body(2 * c + 1, 1)
                return carry
            lax.fori_loop(0, tm // 2, pair, 0)

    def issue_gather(b, to_slot, inline):
        ibase = idx_base(b)
        xbase = to_slot * buf_rows

        def one(r, prio):
            tok = idx_s[ibase + r] & (n_tok - 1)
            pltpu.make_async_copy(_hbm_slab(h_hbm, tok), staged(xbuf, xbase, r),
                                  sem_g.at[to_slot]).start(priority=0)
        for_rows(inline, one)

    def issue_scatter(b, from_slot, to_dump, inline):
        ibase = idx_base(b)
        ybase = from_slot * buf_rows

        def one(r, prio):
            tgt = jnp.where(to_dump, dump_row0 + tm + r, idx_s[ibase + r])
            pltpu.make_async_copy(staged(ybuf, ybase, r), _hbm_slab(ys_hbm, tgt),
                                  sem_s.at[from_slot]).start(priority=prio)
        for_rows(inline, one)

    def wait_gather(at_slot):
        v = xbuf.at[pl.ds(pl.multiple_of(at_slot * buf_rows, SUBLANES), buf_rows), :]
        pltpu.make_async_copy(h_hbm.at[pl.ds(0, buf_rows), :], v, sem_g.at[at_slot]).wait()

    def wait_scatter(at_slot):
        v = ybuf.at[pl.ds(pl.multiple_of(at_slot * buf_rows, SUBLANES), buf_rows), :]
        pltpu.make_async_copy(v, ys_hbm.at[pl.ds(0, buf_rows), :], sem_s.at[at_slot]).wait()

    @pl.when(s == 0)
    def _():
        ybuf[...] = jnp.zeros(ybuf.shape, U32)
        first = idx_copy(0)
        first.start()
        first.wait()
        init = pltpu.make_async_copy(ybuf, ys_hbm.at[pl.ds(pl.multiple_of(dump_row0 * slab, slab), 2 * buf_rows), :],
                                     sem_i)
        init.start()
        init.wait()
        issue_gather(0, 0, inline=False)
        for cp in weight_copies(be_ref[0], 0):
            cp.start()

    @pl.when(s < n_used)
    def _():
        @pl.when((s & (per - 1)) == 1)
        def _():
            idx_copy((s >> per_log2) + 1).start()

        @pl.when(((s + 1) & (per - 1)) == 0)
        def _():
            idx_copy((s + 1) >> per_log2).wait()

        @pl.when(new_ref[s] == 1)
        def _():
            p = epar_ref[s]

            @pl.when(nexte_ref[s] >= 0)
            def _():
                for cp in weight_copies(nexte_ref[s], 1 - p):
                    cp.start(priority=1)

            for cp in weight_copies(be_ref[s], p):
                cp.wait()
            wg_s[...] = wg_f[p].astype(BF16)
            wu_s[...] = wu_f[p].astype(BF16)
            wd_s[...] = wd_f[p].astype(BF16)

        wait_gather(slot)

        @pl.when(s >= 1)
        def _():
            wait_scatter(slot)

        issue_gather(s + 1, 1 - slot, inline=True)
        issue_scatter(jnp.maximum(s - 1, 0), 1 - slot, s == 0, inline=True)
        base = slot * buf_rows
        x_lo, x_hi = _unpack_pairs(_slab_rows_to_matrix(xbuf, base, tm, SLAB_PITCH))
        x = jnp.concatenate([x_lo.astype(BF16), x_hi.astype(BF16)], axis=-1)
        gate = jnp.dot(x, wg_s[...], preferred_element_type=F32)
        up = jnp.dot(x, wu_s[...], preferred_element_type=F32)
        act = (_silu(gate) * up).astype(BF16)
        y = jnp.dot(act, wd_s[...], preferred_element_type=F32)
        half = slab * LANES
        words = _pack_pairs(y[:, :half], y[:, half:])
        for j in range(slab):
            ybuf[pl.ds(base + j, tm, stride=SLAB_PITCH), :] = words[:, j * LANES:(j + 1) * LANES]

    @pl.when(s == n_used)
    def _():
        last = (s - 1) & (per - 1)

        @pl.when((last == 1) | (last == 2))
        def _():
            idx_copy(((s - 1) >> per_log2) + 1).wait()

        wait_gather(slot)
        wait_scatter(slot)
        issue_scatter(s - 1, 1 - slot, False, inline=False)
        wait_scatter(1 - slot)


def _moe(block_e, block_new, block_nexte, block_epar, n_used, row_tgt, h2d, w_gate, w_up, w_down):
    n = h2d.shape[0] // TOKEN_ROWS
    assert n & (n - 1) == 0
    d = 2 * TOKEN_ROWS * LANES
    de = w_gate.shape[2]
    tm = MOE_TM
    n_blk = block_e.shape[0]
    grid_spec = pltpu.PrefetchScalarGridSpec(
        num_scalar_prefetch=5,
        grid=(n_blk,),
        in_specs=[pl.BlockSpec(memory_space=pl.ANY)] * 5,
        out_specs=pl.BlockSpec(memory_space=pl.ANY),
        scratch_shapes=[pltpu.SMEM((2 * IDX_CHUNK,), jnp.int32),
                        pltpu.VMEM((2 * tm * SLAB_PITCH, LANES), U32),
                        pltpu.VMEM((2 * tm * SLAB_PITCH, LANES), U32),
                        pltpu.VMEM((2, d, de), F32),
                        pltpu.VMEM((2, d, de), F32),
                        pltpu.VMEM((2, de, d), F32),
                        pltpu.VMEM((d, de), BF16),
                        pltpu.VMEM((d, de), BF16),
                        pltpu.VMEM((de, d), BF16),
                        pltpu.SemaphoreType.DMA,
                        pltpu.SemaphoreType.DMA((2,)),
                        pltpu.SemaphoreType.DMA((2,)),
                        pltpu.SemaphoreType.DMA((2,))])
    return pl.pallas_call(
        functools.partial(_moe_kernel, n_tok=n),
        grid_spec=grid_spec,
        out_shape=jax.ShapeDtypeStruct(((TOP_K * n + 2 * tm) * TOKEN_ROWS, LANES), U32),
        compiler_params=_cparams(("arbitrary",)),
        name="moe",
    )(block_e, block_new, block_nexte, block_epar, n_used, row_tgt, h2d, w_gate, w_up, w_down)


def _final_kernel(gw_hbm, *refs):
    ys_refs = refs[:TOP_K]
    h_ref, x1_ref, mod_ref, wg_ref, wu_ref, wd_ref, g_ref, b_ref, o_ref, gw_s, acc_ref, sem_i = refs[TOP_K:]
    i = pl.program_id(0)
    tm = x1_ref.shape[0]
    per_step = tm * TOP_K
    cp = pltpu.make_async_copy(gw_hbm.at[pl.ds(pl.multiple_of(i * per_step, per_step), per_step)], gw_s, sem_i)
    cp.start()
    cp.wait()

    hi_base = tm * SLAB_PITCH

    def combine(t, carry):
        rows = pl.ds(pl.multiple_of(t * TOKEN_ROWS, TOKEN_ROWS), TOKEN_ROWS)
        acc_lo = acc_hi = None
        for k in range(TOP_K):
            lo, hi = _unpack_pairs(ys_refs[k][rows, :])
            g = gw_s[t * TOP_K + k]
            acc_lo = g * lo if acc_lo is None else acc_lo + g * lo
            acc_hi = g * hi if acc_hi is None else acc_hi + g * hi
        acc_ref[pl.ds(pl.multiple_of(t * SLAB_PITCH, SUBLANES), TOKEN_ROWS), :] = acc_lo
        acc_ref[pl.ds(pl.multiple_of(hi_base + t * SLAB_PITCH, SUBLANES), TOKEN_ROWS), :] = acc_hi
        return carry

    lax.fori_loop(0, tm, combine, 0, unroll=16)
    moe = jnp.concatenate([_slab_rows_to_matrix(acc_ref, 0, tm, SLAB_PITCH),
                           _slab_rows_to_matrix(acc_ref, hi_base, tm, SLAB_PITCH)], axis=-1)
    h_lo, h_hi = _unpack_pairs(_slab_rows_to_matrix(h_ref, 0, tm, TOKEN_ROWS))
    h = jnp.concatenate([h_lo.astype(BF16), h_hi.astype(BF16)], axis=-1)
    gate = jnp.dot(h, wg_ref[...], preferred_element_type=F32)
    up = jnp.dot(h, wu_ref[...], preferred_element_type=F32)
    shared = jnp.dot((_silu(gate) * up).astype(BF16), wd_ref[...], preferred_element_type=F32)
    m = mod_ref[0]
    y = ALPHA * x1_ref[...] + (1.0 + m[5:6]) * (moe + shared)
    o_ref[...] = _layer_norm(y, g_ref[...], b_ref[...])


def _final(gate_w, ys, h2d, x1, mod3, wsg, wsu, wsd, ln_g, ln_b, seq):
    n, d = x1.shape
    de = wsg.shape[1]
    tm = FINAL_TM
    per_step = tm * TOP_K
    assert per_step % IDX_CHUNK == 0
    row = lambda i: (i, 0)
    const = lambda i: (0, 0)
    slot_rows = lambda k: (lambda i: (k * (n // tm) + i, 0))
    return pl.pallas_call(
        _final_kernel,
        grid=(n // tm,),
        in_specs=[pl.BlockSpec(memory_space=pl.ANY)]
                 + [pl.BlockSpec((tm * TOKEN_ROWS, LANES), slot_rows(k)) for k in range(TOP_K)]
                 + [pl.BlockSpec((tm * TOKEN_ROWS, LANES), row),
                  pl.BlockSpec((tm, d), row),
                  pl.BlockSpec((1, 6, d), lambda i: (i * tm // seq, 0, 0)),
                  pl.BlockSpec((d, de), const),
                  pl.BlockSpec((d, de), const),
                  pl.BlockSpec((de, d), const),
                  pl.BlockSpec((1, d), const),
                  pl.BlockSpec((1, d), const)],
        out_specs=pl.BlockSpec((tm, d), row),
        out_shape=jax.ShapeDtypeStruct((n, d), F32),
        scratch_shapes=[pltpu.SMEM((per_step,), F32),
                        pltpu.VMEM((2 * tm * SLAB_PITCH, LANES), F32),
                        pltpu.SemaphoreType.DMA],
        compiler_params=_cparams(("arbitrary",)),
        name="final",
    )(gate_w, *([ys] * TOP_K), h2d, x1, mod3, wsg, wsu, wsd, ln_g, ln_b)


def _dispatch_tables(eidx, tile_counts, n):
    tm = MOE_TM
    a = n * TOP_K
    i32 = jnp.int32
    experts = jnp.arange(N_EXPERTS, dtype=i32)
    counts = jnp.sum(tile_counts, axis=(0, 2)).astype(i32)
    padded = (counts + tm - 1) // tm * tm
    pad_end = jnp.cumsum(padded)
    starts = pad_end - padded
    n_blk = a // tm + N_EXPERTS + 1
    blk_start = jnp.arange(n_blk, dtype=i32) * tm
    n_used = pad_end[-1] // tm
    in_use = jnp.arange(n_blk) < n_used
    raw_e = jnp.minimum(jnp.sum((pad_end[None, :] <= blk_start[:, None]).astype(i32), axis=1), N_EXPERTS - 1)
    last_e = jnp.sum(jnp.where(jnp.arange(n_blk) == n_used - 1, raw_e, 0))
    block_e = jnp.where(in_use, raw_e, last_e)
    onehot = block_e[:, None] == experts[None, :]
    block_new = jnp.concatenate([jnp.ones((1,), i32), (block_e[1:] != block_e[:-1]).astype(i32)])
    has_rows = counts > 0
    later = (experts[None, :] > experts[:, None]) & has_rows[None, :]
    next_e = jnp.min(jnp.where(later, experts[None, :], N_EXPERTS), axis=1)
    next_e = jnp.where(next_e == N_EXPERTS, -1, next_e)
    parity_e = (jnp.cumsum(has_rows.astype(i32)) - 1) & 1
    block_nexte = jnp.sum(jnp.where(onehot, next_e[None, :], 0), axis=1).astype(i32)
    block_epar = jnp.sum(jnp.where(onehot, parity_e[None, :], 0), axis=1).astype(i32)
    cnt_b = jnp.sum(jnp.where(onehot, counts[None, :], 0), axis=1)
    start_b = jnp.sum(jnp.where(onehot, starts[None, :], 0), axis=1)
    block_nv = jnp.where(in_use, jnp.clip(cnt_b - (blk_start - start_b), 0, tm), 0)
    dummy_keys = jnp.where(jnp.arange(tm, dtype=i32)[None, :] < (padded - counts)[:, None],
                           experts[:, None], N_EXPERTS).reshape(-1)
    keys = jnp.concatenate([eidx.reshape(-1), dummy_keys, jnp.full((tm,), N_EXPERTS, i32)])
    tgt = jnp.arange(a, dtype=i32)
    bits = (a - 1).bit_length() + 1
    low = (1 << bits) - 1
    payload = jnp.concatenate([tgt, jnp.full((n_blk * tm - a,), low, i32)])
    row_tgt = lax.sort(keys * (1 << bits) + payload) & low
    r = jnp.arange(tm, dtype=i32)[None, :]
    dump = a + (jnp.arange(n_blk, dtype=i32)[:, None] & 1) * tm + r
    row_tgt = jnp.where(r >= block_nv[:, None], dump, row_tgt.reshape(n_blk, tm)).reshape(-1)
    row_tgt = jnp.pad(row_tgt, (0, -(n_blk * tm) % IDX_CHUNK))
    return block_e.astype(i32), block_new, block_nexte, block_epar, n_used.astype(i32).reshape(1), row_tgt


def kernel(x, c, w_mod, b_mod, w_in, conv_w, attn_sinks, w_out, ln1_g, ln1_b, w_router, router_bias,
           w_gate, w_up, w_down, ws_gate, ws_up, ws_down, ln2_g, ln2_b):
    b, s, d = x.shape
    n = b * s
    attn_w = N_Q_HEADS * HEAD_DIM
    kv_w = N_KV_HEADS * HEAD_DIM
    conv_wd = d - attn_w
    in_w = attn_w + 2 * kv_w + 3 * conv_wd
    x2 = x.reshape(n, d)
    c8 = jnp.zeros((SUBLANES, d), F32).at[:b].set(c)
    for l in range(DEPTH):
        mod = _mod(c8, w_mod[l], b_mod[l].reshape(1, -1))[:b]
        mod3 = mod.reshape(b, 6, d)
        proj = _inproj(x2, mod3, w_in[l].astype(BF16), s)
        mix = _mixer(proj, attn_sinks[l].reshape(1, -1), conv_w[l], b, s, attn_w, kv_w, conv_wd)
        x1, h2d, logits = _outproj(mix, x2, mod3, w_out[l].astype(BF16), ln1_g[l].reshape(1, -1),
                                   ln1_b[l].reshape(1, -1), w_router[l], s)
        eidx, gate_w, tile_counts = _route(logits, router_bias[l].reshape(-1, 1))
        block_e, block_new, block_nexte, block_epar, n_used, row_tgt = _dispatch_tables(eidx, tile_counts, n)
        ys = _moe(block_e, block_new, block_nexte, block_epar, n_used, row_tgt, h2d,
                  w_gate[l], w_up[l], w_down[l])
        x2 = _final(gate_w.T.reshape(-1), ys, h2d, x1, mod3,
                    ws_gate[l].astype(BF16), ws_up[l].astype(BF16), ws_down[l].astype(BF16),
                    ln2_g[l].reshape(1, -1), ln2_b[l].reshape(1, -1), s)
    return x2.reshape(b, s, d)
```

```python
import functools

import jax
import jax.numpy as jnp
from jax import lax
from jax.experimental import pallas as pl
from jax.experimental.pallas import tpu as pltpu

HEAD_DIM = 64
N_Q_HEADS = 16
N_KV_HEADS = 4
GQA = N_Q_HEADS // N_KV_HEADS
CONV_K = 3
WINDOW = 128
Q_BLOCK = 128
N_EXPERTS = 64
TOP_K = 8
N_GROUPS = 8
GROUP_SIZE = N_EXPERTS // N_GROUPS
TOPK_GROUPS = 4
ROUTED_SCALE = 2.5
DEPTH = 1
ALPHA = (2.0 * DEPTH) ** 0.25
LN_EPS = 1e-5

LANES = 128
SUBLANES = 8
TOKEN_ROWS = 8
SLAB_PITCH = 8
VMEM_LIMIT = 56 * 1024 * 1024

MOD_TN = 2048
INPROJ_TM = 1024
INPROJ_TN = 2304
OUTPROJ_TM = 256
ROUTE_TM = 1024
MOE_TM = 256
FINAL_TM = 256
IDX_CHUNK = 1024
ISSUE_UNROLL = 8

F32 = jnp.float32
BF16 = jnp.bfloat16


def _cparams(sem):
    return pltpu.CompilerParams(dimension_semantics=sem, vmem_limit_bytes=VMEM_LIMIT)


def _silu(v):
    return v * jax.nn.sigmoid(v)


U32 = jnp.uint32
HI_MASK = 0xFFFF0000


def _pack_pairs(lo, hi):
    lo_bits = lax.bitcast_convert_type(lo.astype(BF16).astype(F32), U32) >> 16
    hi_bits = lax.bitcast_convert_type(hi.astype(BF16).astype(F32), U32) & U32(HI_MASK)
    return lo_bits | hi_bits


def _unpack_pairs(w):
    return (lax.bitcast_convert_type(w << 16, F32), lax.bitcast_convert_type(w & U32(HI_MASK), F32))


def _slab_rows_to_matrix(ref, base, tm, pitch):
    return jnp.concatenate([ref[pl.ds(base + s, tm, stride=pitch), :] for s in range(TOKEN_ROWS)], axis=-1)


def _layer_norm(y, g, b):
    mu = jnp.mean(y, axis=-1, keepdims=True)
    yc = y - mu
    var = jnp.mean(yc * yc, axis=-1, keepdims=True)
    return yc * lax.rsqrt(var + LN_EPS) * g + b


def _mod_kernel(c_ref, w_ref, b_ref, o_ref):
    cs = _silu(c_ref[...]).astype(BF16)
    o_ref[...] = jnp.dot(cs, w_ref[...].astype(BF16), preferred_element_type=F32) + b_ref[...]


def _mod(c8, w_mod, b_mod):
    d, n = w_mod.shape
    return pl.pallas_call(
        _mod_kernel,
        grid=(n // MOD_TN,),
        in_specs=[pl.BlockSpec((SUBLANES, d), lambda j: (0, 0)),
                  pl.BlockSpec((d, MOD_TN), lambda j: (0, j)),
                  pl.BlockSpec((1, MOD_TN), lambda j: (0, j))],
        out_specs=pl.BlockSpec((SUBLANES, MOD_TN), lambda j: (0, j)),
        out_shape=jax.ShapeDtypeStruct((SUBLANES, n), F32),
        compiler_params=_cparams(("arbitrary",)),
        name="mod",
    )(c8, w_mod, b_mod)


def _inproj_kernel(x_ref, mod_ref, w_ref, o_ref, h_ref):
    j = pl.program_id(1)

    @pl.when(j == 0)
    def _():
        m = mod_ref[0]
        h_ref[...] = (x_ref[...] * (1.0 + m[1:2]) + m[0:1]).astype(BF16)

    o_ref[...] = jnp.dot(h_ref[...], w_ref[...], preferred_element_type=F32).astype(BF16)


def _inproj(x2, mod3, w_in_bf, seq):
    n, d = x2.shape
    in_w = w_in_bf.shape[1]
    tm, tn = INPROJ_TM, INPROJ_TN
    return pl.pallas_call(
        _inproj_kernel,
        grid=(n // tm, in_w // tn),
        in_specs=[pl.BlockSpec((tm, d), lambda i, j: (i, 0)),
                  pl.BlockSpec((1, 6, d), lambda i, j: (i * tm // seq, 0, 0)),
                  pl.BlockSpec((d, tn), lambda i, j: (0, j))],
        out_specs=pl.BlockSpec((tm, tn), lambda i, j: (i, j)),
        out_shape=jax.ShapeDtypeStruct((n, in_w), BF16),
        scratch_shapes=[pltpu.VMEM((tm, d), BF16)],
        compiler_params=_cparams(("arbitrary", "arbitrary")),
        name="inproj",
    )(x2, mod3, w_in_bf)


def _mixer_kernel(cur_ref, pk_ref, pv_ref, prow_ref, sink_ref, cw_ref, o_ref, *, attn_w, kv_w, conv_w):
    nblk = pl.program_id(1)
    has_prev = nblk > 0
    qb = Q_BLOCK
    cur = cur_ref[...]
    k_cur = cur[:, attn_w:attn_w + kv_w]
    v_cur = cur[:, attn_w + kv_w:attn_w + 2 * kv_w]
    k_all = jnp.concatenate([pk_ref[...], k_cur], axis=0)
    v_all = jnp.concatenate([pv_ref[...], v_cur], axis=0)

    rows = GQA * qb
    qi = lax.broadcasted_iota(jnp.int32, (rows, 2 * qb), 0) % qb
    kj = lax.broadcasted_iota(jnp.int32, (rows, 2 * qb), 1)
    dist = qi + qb - kj
    kmin = jnp.where(has_prev, 0, qb)
    valid = (dist >= 0) & (dist < WINDOW) & (kj >= kmin)
    distf = dist.astype(F32)
    head_in_group = lax.broadcasted_iota(jnp.int32, (rows, 1), 0) // qb
    sinks = sink_ref[...]

    outs = []
    for g in range(N_KV_HEADS):
        q4 = jnp.concatenate(
            [cur[:, (g * GQA + j) * HEAD_DIM:(g * GQA + j + 1) * HEAD_DIM] for j in range(GQA)], axis=0)
        kg = k_all[:, g * HEAD_DIM:(g + 1) * HEAD_DIM]
        vg = v_all[:, g * HEAD_DIM:(g + 1) * HEAD_DIM]
        s = lax.dot_general(q4, kg, (((1,), (1,)), ((), ())), preferred_element_type=F32)
        s = s * (HEAD_DIM ** -0.5)
        slope = jnp.zeros((rows, 1), F32)
        sink = jnp.zeros((rows, 1), F32)
        for j in range(GQA):
            h = g * GQA + j
            sel = head_in_group == j
            slope = jnp.where(sel, 2.0 ** (-8.0 * (h + 1) / N_Q_HEADS), slope)
            sink = jnp.where(sel, sinks[:, h:h + 1], sink)
        s = jnp.where(valid, s - slope * distf, -jnp.inf)
        m = jnp.maximum(jnp.max(s, axis=-1, keepdims=True), sink)
        p = jnp.exp(s - m)
        denom = jnp.sum(p, axis=-1, keepdims=True) + jnp.exp(sink - m)
        o4 = jnp.dot(p.astype(BF16), vg, preferred_element_type=F32) / denom
        outs.extend(o4[j * qb:(j + 1) * qb] for j in range(GQA))
    attn = jnp.concatenate(outs, axis=-1)

    c0 = attn_w + 2 * kv_w
    cb = cur[:, c0:c0 + conv_w].astype(F32)
    u = cur[:, c0 + conv_w:c0 + 2 * conv_w].astype(F32) * cur[:, c0 + 2 * conv_w:c0 + 3 * conv_w].astype(F32)
    prow = prow_ref[...]
    up = prow[:, c0 + conv_w:c0 + 2 * conv_w].astype(F32) * prow[:, c0 + 2 * conv_w:c0 + 3 * conv_w].astype(F32)
    up = up * jnp.where(has_prev, 1.0, 0.0)
    pm1 = up[15:16]
    pm2 = up[14:15]
    ri = lax.broadcasted_iota(jnp.int32, u.shape, 0)
    u1 = jnp.where(ri == 0, pm1, pltpu.roll(u, 1, 0))
    u2 = jnp.where(ri == 0, pm2, jnp.where(ri == 1, pm1, pltpu.roll(u, 2, 0)))
    cw = cw_ref[...]
    conv = cb * (cw[0:1] * u2 + cw[1:2] * u1 + cw[2:3] * u)
    o_ref[...] = jnp.concatenate([attn, conv], axis=-1).astype(BF16)


def _mixer(proj, sinks2, conv_w, batch, seq, attn_w, kv_w, conv_wd):
    n, in_w = proj.shape
    nb = seq // Q_BLOCK
    kv_blk0 = attn_w // kv_w
    sub16 = Q_BLOCK // 16

    def cur_map(b, i):
        return (b * nb + i, 0)

    def prev_map(col):
        return lambda b, i: (b * nb + jnp.maximum(i - 1, 0), col)

    def prow_map(b, i):
        return (jnp.maximum((b * nb + i) * sub16 - 1, 0), 0)

    kern = functools.partial(_mixer_kernel, attn_w=attn_w, kv_w=kv_w, conv_w=conv_wd)
    return pl.pallas_call(
        kern,
        grid=(batch, nb),
        in_specs=[pl.BlockSpec((Q_BLOCK, in_w), cur_map),
                  pl.BlockSpec((Q_BLOCK, kv_w), prev_map(kv_blk0)),
                  pl.BlockSpec((Q_BLOCK, kv_w), prev_map(kv_blk0 + 1)),
                  pl.BlockSpec((16, in_w), prow_map),
                  pl.BlockSpec((1, N_Q_HEADS), lambda b, i: (0, 0)),
                  pl.BlockSpec((CONV_K, conv_wd), lambda b, i: (0, 0))],
        out_specs=pl.BlockSpec((Q_BLOCK, attn_w + conv_wd), cur_map),
        out_shape=jax.ShapeDtypeStruct((n, attn_w + conv_wd), BF16),
        compiler_params=_cparams(("arbitrary", "arbitrary")),
        name="mixer",
    )(proj, proj, proj, proj, sinks2, conv_w)


def _split_bf16(v):
    hi = v.astype(BF16)
    lo = (v - hi.astype(F32)).astype(BF16)
    return hi, lo


def _outproj_kernel(mix_ref, x_ref, mod_ref, w_ref, g_ref, b_ref, wr_ref, x1_ref, h2_ref, lg_ref):
    m = mod_ref[0]
    mix = jnp.dot(mix_ref[...], w_ref[...], preferred_element_type=F32)
    x1 = _layer_norm(ALPHA * x_ref[...] + (1.0 + m[2:3]) * mix, g_ref[...], b_ref[...])
    x1_ref[...] = x1
    h2 = x1 * (1.0 + m[4:5]) + m[3:4]
    tm, d = h2.shape
    words = _pack_pairs(h2[:, :d // 2], h2[:, d // 2:])
    for s in range(TOKEN_ROWS):
        h2_ref[pl.ds(s, tm, stride=TOKEN_ROWS), :] = words[:, s * LANES:(s + 1) * LANES]
    h_hi, h_lo = _split_bf16(h2)
    w_hi, w_lo = _split_bf16(wr_ref[...])
    lg_ref[...] = (jnp.dot(h_hi, w_hi, preferred_element_type=F32)
                   + (jnp.dot(h_hi, w_lo, preferred_element_type=F32)
                      + jnp.dot(h_lo, w_hi, preferred_element_type=F32)))


def _outproj(mix, x2, mod3, w_out_bf, ln_g, ln_b, w_router, seq):
    n, d = x2.shape
    tm = OUTPROJ_TM
    ne = w_router.shape[1]
    row = lambda i: (i, 0)
    const = lambda i: (0, 0)
    return pl.pallas_call(
        _outproj_kernel,
        grid=(n // tm,),
        in_specs=[pl.BlockSpec((tm, d), row),
                  pl.BlockSpec((tm, d), row),
                  pl.BlockSpec((1, 6, d), lambda i: (i * tm // seq, 0, 0)),
                  pl.BlockSpec((d, d), const),
                  pl.BlockSpec((1, d), const),
                  pl.BlockSpec((1, d), const),
                  pl.BlockSpec((d, ne), const)],
        out_specs=[pl.BlockSpec((tm, d), row),
                   pl.BlockSpec((tm * TOKEN_ROWS, LANES), row),
                   pl.BlockSpec((tm, ne), row)],
        out_shape=[jax.ShapeDtypeStruct((n, d), F32),
                   jax.ShapeDtypeStruct((n * TOKEN_ROWS, LANES), U32),
                   jax.ShapeDtypeStruct((n, ne), F32)],
        compiler_params=_cparams(("arbitrary",)),
        name="outproj",
    )(mix, x2, mod3, w_out_bf, ln_g, ln_b, w_router)


def _col_argmax(tiles, row_f):
    m = tiles[0]
    for t in tiles[1:]:
        m = jnp.maximum(m, t)
    m = jnp.max(m, axis=0, keepdims=True)
    idx = None
    for t, r in zip(tiles, row_f):
        c = jnp.where(t == m, r, float(N_EXPERTS))
        idx = c if idx is None else jnp.minimum(idx, c)
    return m, jnp.min(idx, axis=0, keepdims=True)


def _route_kernel(lg_ref, bias_ref, eidx_ref, w_ref, cnt_ref):
    lt = lg_ref[...].T
    tm = lt.shape[1]
    neg = -jnp.inf
    sub = lax.broadcasted_iota(jnp.int32, (GROUP_SIZE, tm), 0).astype(F32)
    row_f = [sub + float(g * GROUP_SIZE) for g in range(N_GROUPS)]
    scores = [jax.nn.sigmoid(lt[g * GROUP_SIZE:(g + 1) * GROUP_SIZE]) for g in range(N_GROUPS)]
    sel = [scores[g] + bias_ref[g * GROUP_SIZE:(g + 1) * GROUP_SIZE, :] for g in range(N_GROUPS)]
    gs = []
    for g in range(N_GROUPS):
        m1, i1 = _col_argmax([sel[g]], [row_f[g]])
        m2 = jnp.max(jnp.where(row_f[g] == i1, neg, sel[g]), axis=0, keepdims=True)
        gs.append(m1 + m2)
    cand = []
    for g in range(N_GROUPS):
        rank = jnp.zeros((1, tm), F32)
        for o in range(N_GROUPS):
            if o == g:
                continue
            ahead = (gs[o] >= gs[g]) if o < g else (gs[o] > gs[g])
            rank = rank + jnp.where(ahead, 1.0, 0.0)
        cand.append(jnp.where(rank < TOPK_GROUPS, sel[g], neg))
    idxs, ws = [], []
    chosen = [jnp.zeros((GROUP_SIZE, tm), F32) for _ in range(N_GROUPS)]
    for _ in range(TOP_K):
        _, ik = _col_argmax(cand, row_f)
        wk = jnp.zeros((1, tm), F32)
        for g in range(N_GROUPS):
            hit = row_f[g] == ik
            wk = wk + jnp.sum(jnp.where(hit, scores[g], 0.0), axis=0, keepdims=True)
            cand[g] = jnp.where(hit, neg, cand[g])
            chosen[g] = jnp.where(hit, 1.0, chosen[g])
        ws.append(wk)
        idxs.append(ik)
    for g in range(N_GROUPS):
        cnt_ref[0, g * GROUP_SIZE:(g + 1) * GROUP_SIZE, :] = jnp.sum(chosen[g], axis=1, keepdims=True)
    wsum = ws[0]
    for k in range(1, TOP_K):
        wsum = wsum + ws[k]
    eidx_ref[...] = jnp.concatenate(idxs, axis=0).astype(jnp.int32)
    w_ref[...] = jnp.concatenate([wk / wsum * ROUTED_SCALE for wk in ws], axis=0)


def _route(logits, bias_col):
    n, ne = logits.shape
    tm = ROUTE_TM
    col = lambda i: (0, i)
    return pl.pallas_call(
        _route_kernel,
        grid=(n // tm,),
        in_specs=[pl.BlockSpec((tm, ne), lambda i: (i, 0)), pl.BlockSpec((ne, 1), lambda i: (0, 0))],
        out_specs=[pl.BlockSpec((TOP_K, tm), col), pl.BlockSpec((TOP_K, tm), col),
                   pl.BlockSpec((1, ne, 1), lambda i: (i, 0, 0))],
        out_shape=[jax.ShapeDtypeStruct((TOP_K, n), jnp.int32), jax.ShapeDtypeStruct((TOP_K, n), F32),
                   jax.ShapeDtypeStruct((n // tm, ne, 1), F32)],
        compiler_params=_cparams(("arbitrary",)),
        name="route",
    )(logits, bias_col)


def _issue_rows(lo, hi, issue_one):
    n_full = (hi - lo) // ISSUE_UNROLL

    def chunk(c, carry):
        for u in range(ISSUE_UNROLL):
            issue_one(lo + c * ISSUE_UNROLL + u)
        return carry

    def tail(r, carry):
        issue_one(r)
        return carry

    lax.fori_loop(0, n_full, chunk, 0)
    lax.fori_loop(lo + n_full * ISSUE_UNROLL, hi, tail, 0)


def _hbm_slab(ref, row):
    return ref.at[pl.ds(pl.multiple_of(row * TOKEN_ROWS, TOKEN_ROWS), TOKEN_ROWS), :]


def _moe_kernel(be_ref, new_ref, nexte_ref, epar_ref, nused_ref, tgt_hbm, h_hbm, wg_hbm, wu_hbm, wd_hbm, ys_hbm,
                idx_s, xbuf, ybuf, wg_f, wu_f, wd_f, wg_s, wu_s, wd_s, sem_i, sem_g, sem_s, sem_w, *, n_tok):
    s = pl.program_id(0)
    tm = MOE_TM
    per = IDX_CHUNK // tm
    per_log2 = per.bit_length() - 1
    n_used = nused_ref[0]
    slab = TOKEN_ROWS
    buf_rows = tm * SLAB_PITCH
    dump_row0 = TOP_K * n_tok
    slot = s & 1

    def staged(buf, base, r):
        return buf.at[pl.ds(pl.multiple_of(base + r * SLAB_PITCH, SUBLANES), slab), :]

    def idx_copy(c):
        return pltpu.make_async_copy(
            tgt_hbm.at[pl.ds(pl.multiple_of(c * IDX_CHUNK, IDX_CHUNK), IDX_CHUNK)],
            idx_s.at[pl.ds(pl.multiple_of((c & 1) * IDX_CHUNK, IDX_CHUNK), IDX_CHUNK)], sem_i)

    def idx_base(b):
        return ((b >> per_log2) & 1) * IDX_CHUNK + (b & (per - 1)) * tm

    def weight_copies(e, p):
        return (pltpu.make_async_copy(wg_hbm.at[e], wg_f.at[p], sem_w.at[p]),
                pltpu.make_async_copy(wu_hbm.at[e], wu_f.at[p], sem_w.at[p]),
                pltpu.make_async_copy(wd_hbm.at[e], wd_f.at[p], sem_w.at[p]))

    def for_rows(inline, body):
        if inline:
            for r in range(tm):
                body(r, r % 2)
        else:
            def pair(c, carry):
                body(2 * c, 0)
                body(2 * c + 1, 1)
                return carry
            lax.fori_loop(0, tm // 2, pair, 0)

    def issue_gather(b, to_slot, inline):
        ibase = idx_base(b)
        xbase = to_slot * buf_rows

        def one(r, prio):
            tok = idx_s[ibase + r] & (n_tok - 1)
            pltpu.make_async_copy(_hbm_slab(h_hbm, tok), staged(xbuf, xbase, r),
                                  sem_g.at[to_slot]).start(priority=0)
        for_rows(inline, one)

    def issue_scatter(b, from_slot, to_dump, inline):
        ibase = idx_base(b)
        ybase = from_slot * buf_rows

        def one(r, prio):
            tgt = jnp.where(to_dump, dump_row0 + tm + r, idx_s[ibase + r])
            pltpu.make_async_copy(staged(ybuf, ybase, r), _hbm_slab(ys_hbm, tgt),
                                  sem_s.at[from_slot]).start(priority=prio)
        for_rows(inline, one)

    def wait_gather(at_slot):
        v = xbuf.at[pl.ds(pl.multiple_of(at_slot * buf_rows, SUBLANES), buf_rows), :]
        pltpu.make_async_copy(h_hbm.at[pl.ds(0, buf_rows), :], v, sem_g.at[at_slot]).wait()

    def wait_scatter(at_slot):
        v = ybuf.at[pl.ds(pl.multiple_of(at_slot * buf_rows, SUBLANES), buf_rows), :]
        pltpu.make_async_copy(v, ys_hbm.at[pl.ds(0, buf_rows), :], sem_s.at[at_slot]).wait()

    @pl.when(s == 0)
    def _():
        ybuf[...] = jnp.zeros(ybuf.shape, U32)
        first = idx_copy(0)
        first.start()
        first.wait()
        init = pltpu.make_async_copy(ybuf, ys_hbm.at[pl.ds(pl.multiple_of(dump_row0 * slab, slab), 2 * buf_rows), :],
                                     sem_i)
        init.start()
        init.wait()
        issue_gather(0, 0, inline=False)
        for cp in weight_copies(be_ref[0], 0):
            cp.start()

    @pl.when(s < n_used)
    def _():
        @pl.when((s & (per - 1)) == 1)
        def _():
            idx_copy((s >> per_log2) + 1).start()

        @pl.when(((s + 1) & (per - 1)) == 0)
        def _():
            idx_copy((s + 1) >> per_log2).wait()

        @pl.when(new_ref[s] == 1)
        def _():
            p = epar_ref[s]

            @pl.when(nexte_ref[s] >= 0)
            def _():
                for cp in weight_copies(nexte_ref[s], 1 - p):
                    cp.start(priority=1)

            for cp in weight_copies(be_ref[s], p):
                cp.wait()
            wg_s[...] = wg_f[p].astype(BF16)
            wu_s[...] = wu_f[p].astype(BF16)
            wd_s[...] = wd_f[p].astype(BF16)

        wait_gather(slot)

        @pl.when(s >= 1)
        def _():
            wait_scatter(slot)

        issue_gather(s + 1, 1 - slot, inline=True)
        issue_scatter(jnp.maximum(s - 1, 0), 1 - slot, s == 0, inline=True)
        base = slot * buf_rows
        x_lo, x_hi = _unpack_pairs(_slab_rows_to_matrix(xbuf, base, tm, SLAB_PITCH))
        x = jnp.concatenate([x_lo.astype(BF16), x_hi.astype(BF16)], axis=-1)
        gate = jnp.dot(x, wg_s[...], preferred_element_type=F32)
        up = jnp.dot(x, wu_s[...], preferred_element_type=F32)
        act = (_silu(gate) * up).astype(BF16)
        y = jnp.dot(act, wd_s[...], preferred_element_type=F32)
        half = slab * LANES
        words = _pack_pairs(y[:, :half], y[:, half:])
        for j in range(slab):
            ybuf[pl.ds(base + j, tm, stride=SLAB_PITCH), :] = words[:, j * LANES:(j + 1) * LANES]

    @pl.when(s == n_used)
    def _():
        last = (s - 1) & (per - 1)

        @pl.when((last == 1) | (last == 2))
        def _():
            idx_copy(((s - 1) >> per_log2) + 1).wait()

        wait_gather(slot)
        wait_scatter(slot)
        issue_scatter(s - 1, 1 - slot, False, inline=False)
        wait_scatter(1 - slot)


def _moe(block_e, block_new, block_nexte, block_epar, n_used, row_tgt, h2d, w_gate, w_up, w_down):
    n = h2d.shape[0] // TOKEN_ROWS
    assert n & (n - 1) == 0
    d = 2 * TOKEN_ROWS * LANES
    de = w_gate.shape[2]
    tm = MOE_TM
    n_blk = block_e.shape[0]
    grid_spec = pltpu.PrefetchScalarGridSpec(
        num_scalar_prefetch=5,
        grid=(n_blk,),
        in_specs=[pl.BlockSpec(memory_space=pl.ANY)] * 5,
        out_specs=pl.BlockSpec(memory_space=pl.ANY),
        scratch_shapes=[pltpu.SMEM((2 * IDX_CHUNK,), jnp.int32),
                        pltpu.VMEM((2 * tm * SLAB_PITCH, LANES), U32),
                        pltpu.VMEM((2 * tm * SLAB_PITCH, LANES), U32),
                        pltpu.VMEM((2, d, de), F32),
                        pltpu.VMEM((2, d, de), F32),
                        pltpu.VMEM((2, de, d), F32),
                        pltpu.VMEM((d, de), BF16),
                        pltpu.VMEM((d, de), BF16),
                        pltpu.VMEM((de, d), BF16),
                        pltpu.SemaphoreType.DMA,
                        pltpu.SemaphoreType.DMA((2,)),
                        pltpu.SemaphoreType.DMA((2,)),
                        pltpu.SemaphoreType.DMA((2,))])
    return pl.pallas_call(
        functools.partial(_moe_kernel, n_tok=n),
        grid_spec=grid_spec,
        out_shape=jax.ShapeDtypeStruct(((TOP_K * n + 2 * tm) * TOKEN_ROWS, LANES), U32),
        compiler_params=_cparams(("arbitrary",)),
        name="moe",
    )(block_e, block_new, block_nexte, block_epar, n_used, row_tgt, h2d, w_gate, w_up, w_down)


def _final_kernel(gw_hbm, *refs):
    ys_refs = refs[:TOP_K]
    h_ref, x1_ref, mod_ref, wg_ref, wu_ref, wd_ref, g_ref, b_ref, o_ref, gw_s, acc_ref, sem_i = refs[TOP_K:]
    i = pl.program_id(0)
    tm = x1_ref.shape[0]
    per_step = tm * TOP_K
    cp = pltpu.make_async_copy(gw_hbm.at[pl.ds(pl.multiple_of(i * per_step, per_step), per_step)], gw_s, sem_i)
    cp.start()
    cp.wait()

    hi_base = tm * SLAB_PITCH

    def combine(t, carry):
        rows = pl.ds(pl.multiple_of(t * TOKEN_ROWS, TOKEN_ROWS), TOKEN_ROWS)
        acc_lo = acc_hi = None
        for k in range(TOP_K):
            lo, hi = _unpack_pairs(ys_refs[k][rows, :])
            g = gw_s[t * TOP_K + k]
            acc_lo = g * lo if acc_lo is None else acc_lo + g * lo
            acc_hi = g * hi if acc_hi is None else acc_hi + g * hi
        acc_ref[pl.ds(pl.multiple_of(t * SLAB_PITCH, SUBLANES), TOKEN_ROWS), :] = acc_lo
        acc_ref[pl.ds(pl.multiple_of(hi_base + t * SLAB_PITCH, SUBLANES), TOKEN_ROWS), :] = acc_hi
        return carry

    lax.fori_loop(0, tm, combine, 0, unroll=32)
    moe = jnp.concatenate([_slab_rows_to_matrix(acc_ref, 0, tm, SLAB_PITCH),
                           _slab_rows_to_matrix(acc_ref, hi_base, tm, SLAB_PITCH)], axis=-1)
    h_lo, h_hi = _unpack_pairs(_slab_rows_to_matrix(h_ref, 0, tm, TOKEN_ROWS))
    h = jnp.concatenate([h_lo.astype(BF16), h_hi.astype(BF16)], axis=-1)
    gate = jnp.dot(h, wg_ref[...], preferred_element_type=F32)
    up = jnp.dot(h, wu_ref[...], preferred_element_type=F32)
    shared = jnp.dot((_silu(gate) * up).astype(BF16), wd_ref[...], preferred_element_type=F32)
    m = mod_ref[0]
    y = ALPHA * x1_ref[...] + (1.0 + m[5:6]) * (moe + shared)
    o_ref[...] = _layer_norm(y, g_ref[...], b_ref[...])


def _final(gate_w, ys, h2d, x1, mod3, wsg, wsu, wsd, ln_g, ln_b, seq):
    n, d = x1.shape
    de = wsg.shape[1]
    tm = FINAL_TM
    per_step = tm * TOP_K
    assert per_step % IDX_CHUNK == 0
    row = lambda i: (i, 0)
    const = lambda i: (0, 0)
    slot_rows = lambda k: (lambda i: (k * (n // tm) + i, 0))
    return pl.pallas_call(
        _final_kernel,
        grid=(n // tm,),
        in_specs=[pl.BlockSpec(memory_space=pl.ANY)]
                 + [pl.BlockSpec((tm * TOKEN_ROWS, LANES), slot_rows(k)) for k in range(TOP_K)]
                 + [pl.BlockSpec((tm * TOKEN_ROWS, LANES), row),
                  pl.BlockSpec((tm, d), row),
                  pl.BlockSpec((1, 6, d), lambda i: (i * tm // seq, 0, 0)),
                  pl.BlockSpec((d, de), const),
                  pl.BlockSpec((d, de), const),
                  pl.BlockSpec((de, d), const),
                  pl.BlockSpec((1, d), const),
                  pl.BlockSpec((1, d), const)],
        out_specs=pl.BlockSpec((tm, d), row),
        out_shape=jax.ShapeDtypeStruct((n, d), F32),
        scratch_shapes=[pltpu.SMEM((per_step,), F32),
                        pltpu.VMEM((2 * tm * SLAB_PITCH, LANES), F32),
                        pltpu.SemaphoreType.DMA],
        compiler_params=_cparams(("arbitrary",)),
        name="final",
    )(gate_w, *([ys] * TOP_K), h2d, x1, mod3, wsg, wsu, wsd, ln_g, ln_b)


def _dispatch_tables(eidx, tile_counts, n):
    tm = MOE_TM
    a = n * TOP_K
    i32 = jnp.int32
    experts = jnp.arange(N_EXPERTS, dtype=i32)
    counts = jnp.sum(tile_counts, axis=(0, 2)).astype(i32)
    padded = (counts + tm - 1) // tm * tm
    pad_end = jnp.cumsum(padded)
    starts = pad_end - padded
    n_blk = a // tm + N_EXPERTS + 1
    blk_start = jnp.arange(n_blk, dtype=i32) * tm
    n_used = pad_end[-1] // tm
    in_use = jnp.arange(n_blk) < n_used
    raw_e = jnp.minimum(jnp.sum((pad_end[None, :] <= blk_start[:, None]).astype(i32), axis=1), N_EXPERTS - 1)
    last_e = jnp.sum(jnp.where(jnp.arange(n_blk) == n_used - 1, raw_e, 0))
    block_e = jnp.where(in_use, raw_e, last_e)
    onehot = block_e[:, None] == experts[None, :]
    block_new = jnp.concatenate([jnp.ones((1,), i32), (block_e[1:] != block_e[:-1]).astype(i32)])
    has_rows = counts > 0
    later = (experts[None, :] > experts[:, None]) & has_rows[None, :]
    next_e = jnp.min(jnp.where(later, experts[None, :], N_EXPERTS), axis=1)
    next_e = jnp.where(next_e == N_EXPERTS, -1, next_e)
    parity_e = (jnp.cumsum(has_rows.astype(i32)) - 1) & 1
    block_nexte = jnp.sum(jnp.where(onehot, next_e[None, :], 0), axis=1).astype(i32)
    block_epar = jnp.sum(jnp.where(onehot, parity_e[None, :], 0), axis=1).astype(i32)
    cnt_b = jnp.sum(jnp.where(onehot, counts[None, :], 0), axis=1)
    start_b = jnp.sum(jnp.where(onehot, starts[None, :], 0), axis=1)
    block_nv = jnp.where(in_use, jnp.clip(cnt_b - (blk_start - start_b), 0, tm), 0)
    dummy_keys = jnp.where(jnp.arange(tm, dtype=i32)[None, :] < (padded - counts)[:, None],
                           experts[:, None], N_EXPERTS).reshape(-1)
    keys = jnp.concatenate([eidx.reshape(-1), dummy_keys, jnp.full((tm,), N_EXPERTS, i32)])
    tgt = jnp.arange(a, dtype=i32)
    bits = (a - 1).bit_length() + 1
    low = (1 << bits) - 1
    payload = jnp.concatenate([tgt, jnp.full((n_blk * tm - a,), low, i32)])
    row_tgt = lax.sort(keys * (1 << bits) + payload) & low
    r = jnp.arange(tm, dtype=i32)[None, :]
    dump = a + (jnp.arange(n_blk, dtype=i32)[:, None] & 1) * tm + r
    row_tgt = jnp.where(r >= block_nv[:, None], dump, row_tgt.reshape(n_blk, tm)).reshape(-1)
    row_tgt = jnp.pad(row_tgt, (0, -(n_blk * tm) % IDX_CHUNK))
    return block_e.astype(i32), block_new, block_nexte, block_epar, n_used.astype(i32).reshape(1), row_tgt


def kernel(x, c, w_mod, b_mod, w_in, conv_w, attn_sinks, w_out, ln1_g, ln1_b, w_router, router_bias,
           w_gate, w_up, w_down, ws_gate, ws_up, ws_down, ln2_g, ln2_b):
    b, s, d = x.shape
    n = b * s
    attn_w = N_Q_HEADS * HEAD_DIM
    kv_w = N_KV_HEADS * HEAD_DIM
    conv_wd = d - attn_w
    in_w = attn_w + 2 * kv_w + 3 * conv_wd
    x2 = x.reshape(n, d)
    c8 = jnp.zeros((SUBLANES, d), F32).at[:b].set(c)
    for l in range(DEPTH):
        mod = _mod(c8, w_mod[l], b_mod[l].reshape(1, -1))[:b]
        mod3 = mod.reshape(b, 6, d)
        proj = _inproj(x2, mod3, w_in[l].astype(BF16), s)
        mix = _mixer(proj, attn_sinks[l].reshape(1, -1), conv_w[l], b, s, attn_w, kv_w, conv_wd)
        x1, h2d, logits = _outproj(mix, x2, mod3, w_out[l].astype(BF16), ln1_g[l].reshape(1, -1),
                                   ln1_b[l].reshape(1, -1), w_router[l], s)
        eidx, gate_w, tile_counts = _route(logits, router_bias[l].reshape(-1, 1))
        block_e, block_new, block_nexte, block_epar, n_used, row_tgt = _dispatch_tables(eidx, tile_counts, n)
        ys = _moe(block_e, block_new, block_nexte, block_epar, n_used, row_tgt, h2d,
                  w_gate[l], w_up[l], w_down[l])
        x2 = _final(gate_w.T.reshape(-1), ys, h2d, x1, mod3,
                    ws_gate[l].astype(BF16), ws_up[l].astype(BF16), ws_down[l].astype(BF16),
                    ln2_g[l].reshape(1, -1), ln2_b[l].reshape(1, -1), s)
    return x2.reshape(b, s, d)
```

```python
import functools

import jax
import jax.numpy as jnp
from jax import lax
from jax.experimental import pallas as pl
from jax.experimental.pallas import tpu as pltpu

HEAD_DIM = 64
N_Q_HEADS = 16
N_KV_HEADS = 4
GQA = N_Q_HEADS // N_KV_HEADS
CONV_K = 3
WINDOW = 128
Q_BLOCK = 128
N_EXPERTS = 64
TOP_K = 8
N_GROUPS = 8
GROUP_SIZE = N_EXPERTS // N_GROUPS
TOPK_GROUPS = 4
ROUTED_SCALE = 2.5
DEPTH = 1
ALPHA = (2.0 * DEPTH) ** 0.25
LN_EPS = 1e-5

LANES = 128
SUBLANES = 8
TOKEN_ROWS = 8
SLAB_PITCH = 8
VMEM_LIMIT = 56 * 1024 * 1024

MOD_TN = 2048
INPROJ_TM = 1024
INPROJ_TN = 2304
OUTPROJ_TM = 256
ROUTE_TM = 1024
MOE_TM = 256
FINAL_TM = 256
IDX_CHUNK = 1024
ISSUE_UNROLL = 8

F32 = jnp.float32
BF16 = jnp.bfloat16


def _cparams(sem):
    return pltpu.CompilerParams(dimension_semantics=sem, vmem_limit_bytes=VMEM_LIMIT)


def _silu(v):
    return v * jax.nn.sigmoid(v)


U32 = jnp.uint32
HI_MASK = 0xFFFF0000


def _pack_pairs(lo, hi):
    lo_bits = lax.bitcast_convert_type(lo.astype(BF16).astype(F32), U32) >> 16
    hi_bits = lax.bitcast_convert_type(hi.astype(BF16).astype(F32), U32) & U32(HI_MASK)
    return lo_bits | hi_bits


def _unpack_pairs(w):
    return (lax.bitcast_convert_type(w << 16, F32), lax.bitcast_convert_type(w & U32(HI_MASK), F32))


def _slab_rows_to_matrix(ref, base, tm, pitch):
    return jnp.concatenate([ref[pl.ds(base + s, tm, stride=pitch), :] for s in range(TOKEN_ROWS)], axis=-1)


def _layer_norm(y, g, b):
    mu = jnp.mean(y, axis=-1, keepdims=True)
    yc = y - mu
    var = jnp.mean(yc * yc, axis=-1, keepdims=True)
    return yc * lax.rsqrt(var + LN_EPS) * g + b


def _mod_kernel(c_ref, w_ref, b_ref, o_ref):
    cs = _silu(c_ref[...]).astype(BF16)
    o_ref[...] = jnp.dot(cs, w_ref[...].astype(BF16), preferred_element_type=F32) + b_ref[...]


def _mod(c8, w_mod, b_mod):
    d, n = w_mod.shape
    return pl.pallas_call(
        _mod_kernel,
        grid=(n // MOD_TN,),
        in_specs=[pl.BlockSpec((SUBLANES, d), lambda j: (0, 0)),
                  pl.BlockSpec((d, MOD_TN), lambda j: (0, j)),
                  pl.BlockSpec((1, MOD_TN), lambda j: (0, j))],
        out_specs=pl.BlockSpec((SUBLANES, MOD_TN), lambda j: (0, j)),
        out_shape=jax.ShapeDtypeStruct((SUBLANES, n), F32),
        compiler_params=_cparams(("arbitrary",)),
        name="mod",
    )(c8, w_mod, b_mod)


def _inproj_kernel(x_ref, mod_ref, w_ref, o_ref, h_ref):
    j = pl.program_id(1)

    @pl.when(j == 0)
    def _():
        m = mod_ref[0]
        h_ref[...] = (x_ref[...] * (1.0 + m[1:2]) + m[0:1]).astype(BF16)

    o_ref[...] = jnp.dot(h_ref[...], w_ref[...], preferred_element_type=F32).astype(BF16)


def _inproj(x2, mod3, w_in_bf, seq):
    n, d = x2.shape
    in_w = w_in_bf.shape[1]
    tm, tn = INPROJ_TM, INPROJ_TN
    return pl.pallas_call(
        _inproj_kernel,
        grid=(n // tm, in_w // tn),
        in_specs=[pl.BlockSpec((tm, d), lambda i, j: (i, 0)),
                  pl.BlockSpec((1, 6, d), lambda i, j: (i * tm // seq, 0, 0)),
                  pl.BlockSpec((d, tn), lambda i, j: (0, j))],
        out_specs=pl.BlockSpec((tm, tn), lambda i, j: (i, j)),
        out_shape=jax.ShapeDtypeStruct((n, in_w), BF16),
        scratch_shapes=[pltpu.VMEM((tm, d), BF16)],
        compiler_params=_cparams(("arbitrary", "arbitrary")),
        name="inproj",
    )(x2, mod3, w_in_bf)


def _mixer_kernel(cur_ref, pk_ref, pv_ref, prow_ref, sink_ref, cw_ref, o_ref, *, attn_w, kv_w, conv_w):
    nblk = pl.program_id(1)
    has_prev = nblk > 0
    qb = Q_BLOCK
    cur = cur_ref[...]
    k_cur = cur[:, attn_w:attn_w + kv_w]
    v_cur = cur[:, attn_w + kv_w:attn_w + 2 * kv_w]
    k_all = jnp.concatenate([pk_ref[...], k_cur], axis=0)
    v_all = jnp.concatenate([pv_ref[...], v_cur], axis=0)

    rows = GQA * qb
    qi = lax.broadcasted_iota(jnp.int32, (rows, 2 * qb), 0) % qb
    kj = lax.broadcasted_iota(jnp.int32, (rows, 2 * qb), 1)
    dist = qi + qb - kj
    kmin = jnp.where(has_prev, 0, qb)
    valid = (dist >= 0) & (dist < WINDOW) & (kj >= kmin)
    distf = dist.astype(F32)
    head_in_group = lax.broadcasted_iota(jnp.int32, (rows, 1), 0) // qb
    sinks = sink_ref[...]

    outs = []
    for g in range(N_KV_HEADS):
        q4 = jnp.concatenate(
            [cur[:, (g * GQA + j) * HEAD_DIM:(g * GQA + j + 1) * HEAD_DIM] for j in range(GQA)], axis=0)
        kg = k_all[:, g * HEAD_DIM:(g + 1) * HEAD_DIM]
        vg = v_all[:, g * HEAD_DIM:(g + 1) * HEAD_DIM]
        s = lax.dot_general(q4, kg, (((1,), (1,)), ((), ())), preferred_element_type=F32)
        s = s * (HEAD_DIM ** -0.5)
        slope = jnp.zeros((rows, 1), F32)
        sink = jnp.zeros((rows, 1), F32)
        for j in range(GQA):
            h = g * GQA + j
            sel = head_in_group == j
            slope = jnp.where(sel, 2.0 ** (-8.0 * (h + 1) / N_Q_HEADS), slope)
            sink = jnp.where(sel, sinks[:, h:h + 1], sink)
        s = jnp.where(valid, s - slope * distf, -jnp.inf)
        m = jnp.maximum(jnp.max(s, axis=-1, keepdims=True), sink)
        p = jnp.exp(s - m)
        denom = jnp.sum(p, axis=-1, keepdims=True) + jnp.exp(sink - m)
        o4 = jnp.dot(p.astype(BF16), vg, preferred_element_type=F32) / denom
        outs.extend(o4[j * qb:(j + 1) * qb] for j in range(GQA))
    attn = jnp.concatenate(outs, axis=-1)

    c0 = attn_w + 2 * kv_w
    cb = cur[:, c0:c0 + conv_w].astype(F32)
    u = cur[:, c0 + conv_w:c0 + 2 * conv_w].astype(F32) * cur[:, c0 + 2 * conv_w:c0 + 3 * conv_w].astype(F32)
    prow = prow_ref[...]
    up = prow[:, c0 + conv_w:c0 + 2 * conv_w].astype(F32) * prow[:, c0 + 2 * conv_w:c0 + 3 * conv_w].astype(F32)
    up = up * jnp.where(has_prev, 1.0, 0.0)
    pm1 = up[15:16]
    pm2 = up[14:15]
    ri = lax.broadcasted_iota(jnp.int32, u.shape, 0)
    u1 = jnp.where(ri == 0, pm1, pltpu.roll(u, 1, 0))
    u2 = jnp.where(ri == 0, pm2, jnp.where(ri == 1, pm1, pltpu.roll(u, 2, 0)))
    cw = cw_ref[...]
    conv = cb * (cw[0:1] * u2 + cw[1:2] * u1 + cw[2:3] * u)
    o_ref[...] = jnp.concatenate([attn, conv], axis=-1).astype(BF16)


def _mixer(proj, sinks2, conv_w, batch, seq, attn_w, kv_w, conv_wd):
    n, in_w = proj.shape
    nb = seq // Q_BLOCK
    kv_blk0 = attn_w // kv_w
    sub16 = Q_BLOCK // 16

    def cur_map(b, i):
        return (b * nb + i, 0)

    def prev_map(col):
        return lambda b, i: (b * nb + jnp.maximum(i - 1, 0), col)

    def prow_map(b, i):
        return (jnp.maximum((b * nb + i) * sub16 - 1, 0), 0)

    kern = functools.partial(_mixer_kernel, attn_w=attn_w, kv_w=kv_w, conv_w=conv_wd)
    return pl.pallas_call(
        kern,
        grid=(batch, nb),
        in_specs=[pl.BlockSpec((Q_BLOCK, in_w), cur_map),
                  pl.BlockSpec((Q_BLOCK, kv_w), prev_map(kv_blk0)),
                  pl.BlockSpec((Q_BLOCK, kv_w), prev_map(kv_blk0 + 1)),
                  pl.BlockSpec((16, in_w), prow_map),
                  pl.BlockSpec((1, N_Q_HEADS), lambda b, i: (0, 0)),
                  pl.BlockSpec((CONV_K, conv_wd), lambda b, i: (0, 0))],
        out_specs=pl.BlockSpec((Q_BLOCK, attn_w + conv_wd), cur_map),
        out_shape=jax.ShapeDtypeStruct((n, attn_w + conv_wd), BF16),
        compiler_params=_cparams(("arbitrary", "arbitrary")),
        name="mixer",
    )(proj, proj, proj, proj, sinks2, conv_w)


def _split_bf16(v):
    hi = v.astype(BF16)
    lo = (v - hi.astype(F32)).astype(BF16)
    return hi, lo


def _outproj_kernel(mix_ref, x_ref, mod_ref, w_ref, g_ref, b_ref, wr_ref, x1_ref, h2_ref, lg_ref):
    m = mod_ref[0]
    mix = jnp.dot(mix_ref[...], w_ref[...], preferred_element_type=F32)
    x1 = _layer_norm(ALPHA * x_ref[...] + (1.0 + m[2:3]) * mix, g_ref[...], b_ref[...])
    x1_ref[...] = x1
    h2 = x1 * (1.0 + m[4:5]) + m[3:4]
    tm, d = h2.shape
    words = _pack_pairs(h2[:, :d // 2], h2[:, d // 2:])
    for s in range(TOKEN_ROWS):
        h2_ref[pl.ds(s, tm, stride=TOKEN_ROWS), :] = words[:, s * LANES:(s + 1) * LANES]
    h_hi, h_lo = _split_bf16(h2)
    w_hi, w_lo = _split_bf16(wr_ref[...])
    lg_ref[...] = (jnp.dot(h_hi, w_hi, preferred_element_type=F32)
                   + (jnp.dot(h_hi, w_lo, preferred_element_type=F32)
                      + jnp.dot(h_lo, w_hi, preferred_element_type=F32)))


def _outproj(mix, x2, mod3, w_out_bf, ln_g, ln_b, w_router, seq):
    n, d = x2.shape
    tm = OUTPROJ_TM
    ne = w_router.shape[1]
    row = lambda i: (i, 0)
    const = lambda i: (0, 0)
    return pl.pallas_call(
        _outproj_kernel,
        grid=(n // tm,),
        in_specs=[pl.BlockSpec((tm, d), row),
                  pl.BlockSpec((tm, d), row),
                  pl.BlockSpec((1, 6, d), lambda i: (i * tm // seq, 0, 0)),
                  pl.BlockSpec((d, d), const),
                  pl.BlockSpec((1, d), const),
                  pl.BlockSpec((1, d), const),
                  pl.BlockSpec((d, ne), const)],
        out_specs=[pl.BlockSpec((tm, d), row),
                   pl.BlockSpec((tm * TOKEN_ROWS, LANES), row),
                   pl.BlockSpec((tm, ne), row)],
        out_shape=[jax.ShapeDtypeStruct((n, d), F32),
                   jax.ShapeDtypeStruct((n * TOKEN_ROWS, LANES), U32),
                   jax.ShapeDtypeStruct((n, ne), F32)],
        compiler_params=_cparams(("arbitrary",)),
        name="outproj",
    )(mix, x2, mod3, w_out_bf, ln_g, ln_b, w_router)


def _col_argmax(tiles, row_f):
    m = tiles[0]
    for t in tiles[1:]:
        m = jnp.maximum(m, t)
    m = jnp.max(m, axis=0, keepdims=True)
    idx = None
    for t, r in zip(tiles, row_f):
        c = jnp.where(t == m, r, float(N_EXPERTS))
        idx = c if idx is None else jnp.minimum(idx, c)
    return m, jnp.min(idx, axis=0, keepdims=True)


def _route_kernel(lg_ref, bias_ref, eidx_ref, w_ref, cnt_ref):
    lt = lg_ref[...].T
    tm = lt.shape[1]
    neg = -jnp.inf
    sub = lax.broadcasted_iota(jnp.int32, (GROUP_SIZE, tm), 0).astype(F32)
    row_f = [sub + float(g * GROUP_SIZE) for g in range(N_GROUPS)]
    scores = [jax.nn.sigmoid(lt[g * GROUP_SIZE:(g + 1) * GROUP_SIZE]) for g in range(N_GROUPS)]
    sel = [scores[g] + bias_ref[g * GROUP_SIZE:(g + 1) * GROUP_SIZE, :] for g in range(N_GROUPS)]
    gs = []
    for g in range(N_GROUPS):
        m1, i1 = _col_argmax([sel[g]], [row_f[g]])
        m2 = jnp.max(jnp.where(row_f[g] == i1, neg, sel[g]), axis=0, keepdims=True)
        gs.append(m1 + m2)
    cand = []
    for g in range(N_GROUPS):
        rank = jnp.zeros((1, tm), F32)
        for o in range(N_GROUPS):
            if o == g:
                continue
            ahead = (gs[o] >= gs[g]) if o < g else (gs[o] > gs[g])
            rank = rank + jnp.where(ahead, 1.0, 0.0)
        cand.append(jnp.where(rank < TOPK_GROUPS, sel[g], neg))
    idxs, ws = [], []
    chosen = [jnp.zeros((GROUP_SIZE, tm), F32) for _ in range(N_GROUPS)]
    for _ in range(TOP_K):
        _, ik = _col_argmax(cand, row_f)
        wk = jnp.zeros((1, tm), F32)
        for g in range(N_GROUPS):
            hit = row_f[g] == ik
            wk = wk + jnp.sum(jnp.where(hit, scores[g], 0.0), axis=0, keepdims=True)
            cand[g] = jnp.where(hit, neg, cand[g])
            chosen[g] = jnp.where(hit, 1.0, chosen[g])
        ws.append(wk)
        idxs.append(ik)
    for g in range(N_GROUPS):
        cnt_ref[0, g * GROUP_SIZE:(g + 1) * GROUP_SIZE, :] = jnp.sum(chosen[g], axis=1, keepdims=True)
    wsum = ws[0]
    for k in range(1, TOP_K):
        wsum = wsum + ws[k]
    eidx_ref[...] = jnp.concatenate(idxs, axis=0).astype(jnp.int32)
    w_ref[...] = jnp.concatenate([wk / wsum * ROUTED_SCALE for wk in ws], axis=0)


def _route(logits, bias_col):
    n, ne = logits.shape
    tm = ROUTE_TM
    col = lambda i: (0, i)
    return pl.pallas_call(
        _route_kernel,
        grid=(n // tm,),
        in_specs=[pl.BlockSpec((tm, ne), lambda i: (i, 0)), pl.BlockSpec((ne, 1), lambda i: (0, 0))],
        out_specs=[pl.BlockSpec((TOP_K, tm), col), pl.BlockSpec((TOP_K, tm), col),
                   pl.BlockSpec((1, ne, 1), lambda i: (i, 0, 0))],
        out_shape=[jax.ShapeDtypeStruct((TOP_K, n), jnp.int32), jax.ShapeDtypeStruct((TOP_K, n), F32),
                   jax.ShapeDtypeStruct((n // tm, ne, 1), F32)],
        compiler_params=_cparams(("arbitrary",)),
        name="route",
    )(logits, bias_col)


def _issue_rows(lo, hi, issue_one):
    n_full = (hi - lo) // ISSUE_UNROLL

    def chunk(c, carry):
        for u in range(ISSUE_UNROLL):
            issue_one(lo + c * ISSUE_UNROLL + u)
        return carry

    def tail(r, carry):
        issue_one(r)
        return carry

    lax.fori_loop(0, n_full, chunk, 0)
    lax.fori_loop(lo + n_full * ISSUE_UNROLL, hi, tail, 0)


def _hbm_slab(ref, row):
    return ref.at[pl.ds(pl.multiple_of(row * TOKEN_ROWS, TOKEN_ROWS), TOKEN_ROWS), :]


def _moe_kernel(be_ref, new_ref, nexte_ref, epar_ref, nused_ref, tgt_hbm, h_hbm, wg_hbm, wu_hbm, wd_hbm, ys_hbm,
                idx_s, xbuf, ybuf, wg_f, wu_f, wd_f, wg_s, wu_s, wd_s, sem_i, sem_g, sem_s, sem_w, *, n_tok):
    s = pl.program_id(0)
    tm = MOE_TM
    per = IDX_CHUNK // tm
    per_log2 = per.bit_length() - 1
    n_used = nused_ref[0]
    slab = TOKEN_ROWS
    buf_rows = tm * SLAB_PITCH
    dump_row0 = TOP_K * n_tok
    slot = s & 1

    def staged(buf, base, r):
        return buf.at[pl.ds(pl.multiple_of(base + r * SLAB_PITCH, SUBLANES), slab), :]

    def idx_copy(c):
        return pltpu.make_async_copy(
            tgt_hbm.at[pl.ds(pl.multiple_of(c * IDX_CHUNK, IDX_CHUNK), IDX_CHUNK)],
            idx_s.at[pl.ds(pl.multiple_of((c & 1) * IDX_CHUNK, IDX_CHUNK), IDX_CHUNK)], sem_i)

    def idx_base(b):
        return ((b >> per_log2) & 1) * IDX_CHUNK + (b & (per - 1)) * tm

    def weight_copies(e, p):
        return (pltpu.make_async_copy(wg_hbm.at[e], wg_f.at[p], sem_w.at[p]),
                pltpu.make_async_copy(wu_hbm.at[e], wu_f.at[p], sem_w.at[p]),
                pltpu.make_async_copy(wd_hbm.at[e], wd_f.at[p], sem_w.at[p]))

    def for_rows(inline, body):
        if inline:
            for r in range(tm):
                body(r, r % 2)
        else:
            def pair(c, carry):
                body(2 * c, 0)
                body(2 * c + 1, 1)
                return carry
            lax.fori_loop(0, tm // 2, pair, 0)

    def issue_gather(b, to_slot, inline):
        ibase = idx_base(b)
        xbase = to_slot * buf_rows

        def one(r, prio):
            tok = idx_s[ibase + r] & (n_tok - 1)
            pltpu.make_async_copy(_hbm_slab(h_hbm, tok), staged(xbuf, xbase, r),
                                  sem_g.at[to_slot]).start(priority=0)
        for_rows(inline, one)

    def issue_scatter(b, from_slot, to_dump, inline):
        ibase = idx_base(b)
        ybase = from_slot * buf_rows

        def one(r, prio):
            tgt = jnp.where(to_dump, dump_row0 + tm + r, idx_s[ibase + r])
            pltpu.make_async_copy(staged(ybuf, ybase, r), _hbm_slab(ys_hbm, tgt),
                                  sem_s.at[from_slot]).start(priority=1)
        for_rows(inline, one)

    def wait_gather(at_slot):
        v = xbuf.at[pl.ds(pl.multiple_of(at_slot * buf_rows, SUBLANES), buf_rows), :]
        pltpu.make_async_copy(h_hbm.at[pl.ds(0, buf_rows), :], v, sem_g.at[at_slot]).wait()

    def wait_scatter(at_slot):
        v = ybuf.at[pl.ds(pl.multiple_of(at_slot * buf_rows, SUBLANES), buf_rows), :]
        pltpu.make_async_copy(v, ys_hbm.at[pl.ds(0, buf_rows), :], sem_s.at[at_slot]).wait()

    @pl.when(s == 0)
    def _():
        ybuf[...] = jnp.zeros(ybuf.shape, U32)
        first = idx_copy(0)
        first.start()
        first.wait()
        init = pltpu.make_async_copy(ybuf, ys_hbm.at[pl.ds(pl.multiple_of(dump_row0 * slab, slab), 2 * buf_rows), :],
                                     sem_i)
        init.start()
        init.wait()
        issue_gather(0, 0, inline=False)
        for cp in weight_copies(be_ref[0], 0):
            cp.start()

    @pl.when(s < n_used)
    def _():
        @pl.when((s & (per - 1)) == 1)
        def _():
            idx_copy((s >> per_log2) + 1).start()

        @pl.when(((s + 1) & (per - 1)) == 0)
        def _():
            idx_copy((s + 1) >> per_log2).wait()

        @pl.when(new_ref[s] == 1)
        def _():
            p = epar_ref[s]

            @pl.when(nexte_ref[s] >= 0)
            def _():
                for cp in weight_copies(nexte_ref[s], 1 - p):
                    cp.start(priority=1)

            for cp in weight_copies(be_ref[s], p):
                cp.wait()
            wg_s[...] = wg_f[p].astype(BF16)
            wu_s[...] = wu_f[p].astype(BF16)
            wd_s[...] = wd_f[p].astype(BF16)

        wait_gather(slot)

        @pl.when(s >= 1)
        def _():
            wait_scatter(slot)

        issue_gather(s + 1, 1 - slot, inline=True)
        issue_scatter(jnp.maximum(s - 1, 0), 1 - slot, s == 0, inline=True)
        base = slot * buf_rows
        x_lo, x_hi = _unpack_pairs(_slab_rows_to_matrix(xbuf, base, tm, SLAB_PITCH))
        x = jnp.concatenate([x_lo.astype(BF16), x_hi.astype(BF16)], axis=-1)
        gate = jnp.dot(x, wg_s[...], preferred_element_type=F32)
        up = jnp.dot(x, wu_s[...], preferred_element_type=F32)
        act = (_silu(gate) * up).astype(BF16)
        y = jnp.dot(act, wd_s[...], preferred_element_type=F32)
        half = slab * LANES
        words = _pack_pairs(y[:, :half], y[:, half:])
        for j in range(slab):
            ybuf[pl.ds(base + j, tm, stride=SLAB_PITCH), :] = words[:, j * LANES:(j + 1) * LANES]

    @pl.when(s == n_used)
    def _():
        last = (s - 1) & (per - 1)

        @pl.when((last == 1) | (last == 2))
        def _():
            idx_copy(((s - 1) >> per_log2) + 1).wait()

        wait_gather(slot)
        wait_scatter(slot)
        issue_scatter(s - 1, 1 - slot, False, inline=False)
        wait_scatter(1 - slot)


def _moe(block_e, block_new, block_nexte, block_epar, n_used, row_tgt, h2d, w_gate, w_up, w_down):
    n = h2d.shape[0] // TOKEN_ROWS
    assert n & (n - 1) == 0
    d = 2 * TOKEN_ROWS * LANES
    de = w_gate.shape[2]
    tm = MOE_TM
    n_blk = block_e.shape[0]
    grid_spec = pltpu.PrefetchScalarGridSpec(
        num_scalar_prefetch=5,
        grid=(n_blk,),
        in_specs=[pl.BlockSpec(memory_space=pl.ANY)] * 5,
        out_specs=pl.BlockSpec(memory_space=pl.ANY),
        scratch_shapes=[pltpu.SMEM((2 * IDX_CHUNK,), jnp.int32),
                        pltpu.VMEM((2 * tm * SLAB_PITCH, LANES), U32),
                        pltpu.VMEM((2 * tm * SLAB_PITCH, LANES), U32),
                        pltpu.VMEM((2, d, de), F32),
                        pltpu.VMEM((2, d, de), F32),
                        pltpu.VMEM((2, de, d), F32),
                        pltpu.VMEM((d, de), BF16),
                        pltpu.VMEM((d, de), BF16),
                        pltpu.VMEM((de, d), BF16),
                        pltpu.SemaphoreType.DMA,
                        pltpu.SemaphoreType.DMA((2,)),
                        pltpu.SemaphoreType.DMA((2,)),
                        pltpu.SemaphoreType.DMA((2,))])
    return pl.pallas_call(
        functools.partial(_moe_kernel, n_tok=n),
        grid_spec=grid_spec,
        out_shape=jax.ShapeDtypeStruct(((TOP_K * n + 2 * tm) * TOKEN_ROWS, LANES), U32),
        compiler_params=_cparams(("arbitrary",)),
        name="moe",
    )(block_e, block_new, block_nexte, block_epar, n_used, row_tgt, h2d, w_gate, w_up, w_down)


def _final_kernel(gw_hbm, *refs):
    ys_refs = refs[:TOP_K]
    h_ref, x1_ref, mod_ref, wg_ref, wu_ref, wd_ref, g_ref, b_ref, o_ref, gw_s, acc_ref, sem_i = refs[TOP_K:]
    i = pl.program_id(0)
    tm = x1_ref.shape[0]
    per_step = tm * TOP_K
    cp = pltpu.make_async_copy(gw_hbm.at[pl.ds(pl.multiple_of(i * per_step, per_step), per_step)], gw_s, sem_i)
    cp.start()
    cp.wait()

    hi_base = tm * SLAB_PITCH

    def combine(t, carry):
        rows = pl.ds(pl.multiple_of(t * TOKEN_ROWS, TOKEN_ROWS), TOKEN_ROWS)
        acc_lo = acc_hi = None
        for k in range(TOP_K):
            lo, hi = _unpack_pairs(ys_refs[k][rows, :])
            g = gw_s[t * TOP_K + k]
            acc_lo = g * lo if acc_lo is None else acc_lo + g * lo
            acc_hi = g * hi if acc_hi is None else acc_hi + g * hi
        acc_ref[pl.ds(pl.multiple_of(t * SLAB_PITCH, SUBLANES), TOKEN_ROWS), :] = acc_lo
        acc_ref[pl.ds(pl.multiple_of(hi_base + t * SLAB_PITCH, SUBLANES), TOKEN_ROWS), :] = acc_hi
        return carry

    lax.fori_loop(0, tm, combine, 0, unroll=32)
    moe = jnp.concatenate([_slab_rows_to_matrix(acc_ref, 0, tm, SLAB_PITCH),
                           _slab_rows_to_matrix(acc_ref, hi_base, tm, SLAB_PITCH)], axis=-1)
    h_lo, h_hi = _unpack_pairs(_slab_rows_to_matrix(h_ref, 0, tm, TOKEN_ROWS))
    h = jnp.concatenate([h_lo.astype(BF16), h_hi.astype(BF16)], axis=-1)
    gate = jnp.dot(h, wg_ref[...], preferred_element_type=F32)
    up = jnp.dot(h, wu_ref[...], preferred_element_type=F32)
    shared = jnp.dot((_silu(gate) * up).astype(BF16), wd_ref[...], preferred_element_type=F32)
    m = mod_ref[0]
    y = ALPHA * x1_ref[...] + (1.0 + m[5:6]) * (moe + shared)
    o_ref[...] = _layer_norm(y, g_ref[...], b_ref[...])


def _final(gate_w, ys, h2d, x1, mod3, wsg, wsu, wsd, ln_g, ln_b, seq):
    n, d = x1.shape
    de = wsg.shape[1]
    tm = FINAL_TM
    per_step = tm * TOP_K
    assert per_step % IDX_CHUNK == 0
    row = lambda i: (i, 0)
    const = lambda i: (0, 0)
    slot_rows = lambda k: (lambda i: (k * (n // tm) + i, 0))
    return pl.pallas_call(
        _final_kernel,
        grid=(n // tm,),
        in_specs=[pl.BlockSpec(memory_space=pl.ANY)]
                 + [pl.BlockSpec((tm * TOKEN_ROWS, LANES), slot_rows(k)) for k in range(TOP_K)]
                 + [pl.BlockSpec((tm * TOKEN_ROWS, LANES), row),
                  pl.BlockSpec((tm, d), row),
                  pl.BlockSpec((1, 6, d), lambda i: (i * tm // seq, 0, 0)),
                  pl.BlockSpec((d, de), const),
                  pl.BlockSpec((d, de), const),
                  pl.BlockSpec((de, d), const),
                  pl.BlockSpec((1, d), const),
                  pl.BlockSpec((1, d), const)],
        out_specs=pl.BlockSpec((tm, d), row),
        out_shape=jax.ShapeDtypeStruct((n, d), F32),
        scratch_shapes=[pltpu.SMEM((per_step,), F32),
                        pltpu.VMEM((2 * tm * SLAB_PITCH, LANES), F32),
                        pltpu.SemaphoreType.DMA],
        compiler_params=_cparams(("arbitrary",)),
        name="final",
    )(gate_w, *([ys] * TOP_K), h2d, x1, mod3, wsg, wsu, wsd, ln_g, ln_b)


def _dispatch_tables(eidx, tile_counts, n):
    tm = MOE_TM
    a = n * TOP_K
    i32 = jnp.int32
    experts = jnp.arange(N_EXPERTS, dtype=i32)
    counts = jnp.sum(tile_counts, axis=(0, 2)).astype(i32)
    padded = (counts + tm - 1) // tm * tm
    pad_end = jnp.cumsum(padded)
    starts = pad_end - padded
    n_blk = a // tm + N_EXPERTS + 1
    blk_start = jnp.arange(n_blk, dtype=i32) * tm
    n_used = pad_end[-1] // tm
    in_use = jnp.arange(n_blk) < n_used
    raw_e = jnp.minimum(jnp.sum((pad_end[None, :] <= blk_start[:, None]).astype(i32), axis=1), N_EXPERTS - 1)
    last_e = jnp.sum(jnp.where(jnp.arange(n_blk) == n_used - 1, raw_e, 0))
    block_e = jnp.where(in_use, raw_e, last_e)
    onehot = block_e[:, None] == experts[None, :]
    block_new = jnp.concatenate([jnp.ones((1,), i32), (block_e[1:] != block_e[:-1]).astype(i32)])
    has_rows = counts > 0
    later = (experts[None, :] > experts[:, None]) & has_rows[None, :]
    next_e = jnp.min(jnp.where(later, experts[None, :], N_EXPERTS), axis=1)
    next_e = jnp.where(next_e == N_EXPERTS, -1, next_e)
    parity_e = (jnp.cumsum(has_rows.astype(i32)) - 1) & 1
    block_nexte = jnp.sum(jnp.where(onehot, next_e[None, :], 0), axis=1).astype(i32)
    block_epar = jnp.sum(jnp.where(onehot, parity_e[None, :], 0), axis=1).astype(i32)
    cnt_b = jnp.sum(jnp.where(onehot, counts[None, :], 0), axis=1)
    start_b = jnp.sum(jnp.where(onehot, starts[None, :], 0), axis=1)
    block_nv = jnp.where(in_use, jnp.clip(cnt_b - (blk_start - start_b), 0, tm), 0)
    dummy_keys = jnp.where(jnp.arange(tm, dtype=i32)[None, :] < (padded - counts)[:, None],
                           experts[:, None], N_EXPERTS).reshape(-1)
    keys = jnp.concatenate([eidx.reshape(-1), dummy_keys, jnp.full((tm,), N_EXPERTS, i32)])
    tgt = jnp.arange(a, dtype=i32)
    bits = (a - 1).bit_length() + 1
    low = (1 << bits) - 1
    payload = jnp.concatenate([tgt, jnp.full((n_blk * tm - a,), low, i32)])
    row_tgt = lax.sort(keys * (1 << bits) + payload) & low
    r = jnp.arange(tm, dtype=i32)[None, :]
    dump = a + (jnp.arange(n_blk, dtype=i32)[:, None] & 1) * tm + r
    row_tgt = jnp.where(r >= block_nv[:, None], dump, row_tgt.reshape(n_blk, tm)).reshape(-1)
    row_tgt = jnp.pad(row_tgt, (0, -(n_blk * tm) % IDX_CHUNK))
    return block_e.astype(i32), block_new, block_nexte, block_epar, n_used.astype(i32).reshape(1), row_tgt


def kernel(x, c, w_mod, b_mod, w_in, conv_w, attn_sinks, w_out, ln1_g, ln1_b, w_router, router_bias,
           w_gate, w_up, w_down, ws_gate, ws_up, ws_down, ln2_g, ln2_b):
    b, s, d = x.shape
    n = b * s
    attn_w = N_Q_HEADS * HEAD_DIM
    kv_w = N_KV_HEADS * HEAD_DIM
    conv_wd = d - attn_w
    in_w = attn_w + 2 * kv_w + 3 * conv_wd
    x2 = x.reshape(n, d)
    c8 = jnp.zeros((SUBLANES, d), F32).at[:b].set(c)
    for l in range(DEPTH):
        mod = _mod(c8, w_mod[l], b_mod[l].reshape(1, -1))[:b]
        mod3 = mod.reshape(b, 6, d)
        proj = _inproj(x2, mod3, w_in[l].astype(BF16), s)
        mix = _mixer(proj, attn_sinks[l].reshape(1, -1), conv_w[l], b, s, attn_w, kv_w, conv_wd)
        x1, h2d, logits = _outproj(mix, x2, mod3, w_out[l].astype(BF16), ln1_g[l].reshape(1, -1),
                                   ln1_b[l].reshape(1, -1), w_router[l], s)
        eidx, gate_w, tile_counts = _route(logits, router_bias[l].reshape(-1, 1))
        block_e, block_new, block_nexte, block_epar, n_used, row_tgt = _dispatch_tables(eidx, tile_counts, n)
        ys = _moe(block_e, block_new, block_nexte, block_epar, n_used, row_tgt, h2d,
                  w_gate[l], w_up[l], w_down[l])
        x2 = _final(gate_w.T.reshape(-1), ys, h2d, x1, mod3,
                    ws_gate[l].astype(BF16), ws_up[l].astype(BF16), ws_down[l].astype(BF16),
                    ln2_g[l].reshape(1, -1), ln2_b[l].reshape(1, -1), s)
    return x2.reshape(b, s, d)
```

```python
import functools

import jax
import jax.numpy as jnp
from jax import lax
from jax.experimental import pallas as pl
from jax.experimental.pallas import tpu as pltpu

HEAD_DIM = 64
N_Q_HEADS = 16
N_KV_HEADS = 4
GQA = N_Q_HEADS // N_KV_HEADS
CONV_K = 3
WINDOW = 128
Q_BLOCK = 128
N_EXPERTS = 64
TOP_K = 8
N_GROUPS = 8
GROUP_SIZE = N_EXPERTS // N_GROUPS
TOPK_GROUPS = 4
ROUTED_SCALE = 2.5
DEPTH = 1
ALPHA = (2.0 * DEPTH) ** 0.25
LN_EPS = 1e-5

LANES = 128
SUBLANES = 8
TOKEN_ROWS = 8
SLAB_PITCH = 8
VMEM_LIMIT = 56 * 1024 * 1024

MOD_TN = 2048
INPROJ_TM = 1024
INPROJ_TN = 2304
OUTPROJ_TM = 256
ROUTE_TM = 1024
MOE_TM = 256
FINAL_TM = 256
IDX_CHUNK = 1024
ISSUE_UNROLL = 8

F32 = jnp.float32
BF16 = jnp.bfloat16


def _cparams(sem):
    return pltpu.CompilerParams(dimension_semantics=sem, vmem_limit_bytes=VMEM_LIMIT)


def _silu(v):
    return v * jax.nn.sigmoid(v)


U32 = jnp.uint32
HI_MASK = 0xFFFF0000


def _pack_pairs(lo, hi):
    lo_bits = lax.bitcast_convert_type(lo.astype(BF16).astype(F32), U32) >> 16
    hi_bits = lax.bitcast_convert_type(hi.astype(BF16).astype(F32), U32) & U32(HI_MASK)
    return lo_bits | hi_bits


def _unpack_pairs(w):
    return (lax.bitcast_convert_type(w << 16, F32), lax.bitcast_convert_type(w & U32(HI_MASK), F32))


def _slab_rows_to_matrix(ref, base, tm, pitch):
    return jnp.concatenate([ref[pl.ds(base + s, tm, stride=pitch), :] for s in range(TOKEN_ROWS)], axis=-1)


def _layer_norm(y, g, b):
    mu = jnp.mean(y, axis=-1, keepdims=True)
    yc = y - mu
    var = jnp.mean(yc * yc, axis=-1, keepdims=True)
    return yc * lax.rsqrt(var + LN_EPS) * g + b


def _mod_kernel(c_ref, w_ref, b_ref, o_ref):
    cs = _silu(c_ref[...]).astype(BF16)
    o_ref[...] = jnp.dot(cs, w_ref[...].astype(BF16), preferred_element_type=F32) + b_ref[...]


def _mod(c8, w_mod, b_mod):
    d, n = w_mod.shape
    return pl.pallas_call(
        _mod_kernel,
        grid=(n // MOD_TN,),
        in_specs=[pl.BlockSpec((SUBLANES, d), lambda j: (0, 0)),
                  pl.BlockSpec((d, MOD_TN), lambda j: (0, j)),
                  pl.BlockSpec((1, MOD_TN), lambda j: (0, j))],
        out_specs=pl.BlockSpec((SUBLANES, MOD_TN), lambda j: (0, j)),
        out_shape=jax.ShapeDtypeStruct((SUBLANES, n), F32),
        compiler_params=_cparams(("arbitrary",)),
        name="mod",
    )(c8, w_mod, b_mod)


def _inproj_kernel(x_ref, mod_ref, w_ref, o_ref, h_ref):
    j = pl.program_id(1)

    @pl.when(j == 0)
    def _():
        m = mod_ref[0]
        h_ref[...] = (x_ref[...] * (1.0 + m[1:2]) + m[0:1]).astype(BF16)

    o_ref[...] = jnp.dot(h_ref[...], w_ref[...], preferred_element_type=F32).astype(BF16)


def _inproj(x2, mod3, w_in_bf, seq):
    n, d = x2.shape
    in_w = w_in_bf.shape[1]
    tm, tn = INPROJ_TM, INPROJ_TN
    return pl.pallas_call(
        _inproj_kernel,
        grid=(n // tm, in_w // tn),
        in_specs=[pl.BlockSpec((tm, d), lambda i, j: (i, 0)),
                  pl.BlockSpec((1, 6, d), lambda i, j: (i * tm // seq, 0, 0)),
                  pl.BlockSpec((d, tn), lambda i, j: (0, j))],
        out_specs=pl.BlockSpec((tm, tn), lambda i, j: (i, j)),
        out_shape=jax.ShapeDtypeStruct((n, in_w), BF16),
        scratch_shapes=[pltpu.VMEM((tm, d), BF16)],
        compiler_params=_cparams(("arbitrary", "arbitrary")),
        name="inproj",
    )(x2, mod3, w_in_bf)


def _mixer_kernel(cur_ref, pk_ref, pv_ref, prow_ref, sink_ref, cw_ref, o_ref, *, attn_w, kv_w, conv_w):
    nblk = pl.program_id(1)
    has_prev = nblk > 0
    qb = Q_BLOCK
    cur = cur_ref[...]
    k_cur = cur[:, attn_w:attn_w + kv_w]
    v_cur = cur[:, attn_w + kv_w:attn_w + 2 * kv_w]
    k_all = jnp.concatenate([pk_ref[...], k_cur], axis=0)
    v_all = jnp.concatenate([pv_ref[...], v_cur], axis=0)

    rows = GQA * qb
    qi = lax.broadcasted_iota(jnp.int32, (rows, 2 * qb), 0) % qb
    kj = lax.broadcasted_iota(jnp.int32, (rows, 2 * qb), 1)
    dist = qi + qb - kj
    kmin = jnp.where(has_prev, 0, qb)
    valid = (dist >= 0) & (dist < WINDOW) & (kj >= kmin)
    distf = dist.astype(F32)
    head_in_group = lax.broadcasted_iota(jnp.int32, (rows, 1), 0) // qb
    sinks = sink_ref[...]

    outs = []
    for g in range(N_KV_HEADS):
        q4 = jnp.concatenate(
            [cur[:, (g * GQA + j) * HEAD_DIM:(g * GQA + j + 1) * HEAD_DIM] for j in range(GQA)], axis=0)
        kg = k_all[:, g * HEAD_DIM:(g + 1) * HEAD_DIM]
        vg = v_all[:, g * HEAD_DIM:(g + 1) * HEAD_DIM]
        s = lax.dot_general(q4, kg, (((1,), (1,)), ((), ())), preferred_element_type=F32)
        s = s * (HEAD_DIM ** -0.5)
        slope = jnp.zeros((rows, 1), F32)
        sink = jnp.zeros((rows, 1), F32)
        for j in range(GQA):
            h = g * GQA + j
            sel = head_in_group == j
            slope = jnp.where(sel, 2.0 ** (-8.0 * (h + 1) / N_Q_HEADS), slope)
            sink = jnp.where(sel, sinks[:, h:h + 1], sink)
        s = jnp.where(valid, s - slope * distf, -jnp.inf)
        m = jnp.maximum(jnp.max(s, axis=-1, keepdims=True), sink)
        p = jnp.exp(s - m)
        denom = jnp.sum(p, axis=-1, keepdims=True) + jnp.exp(sink - m)
        o4 = jnp.dot(p.astype(BF16), vg, preferred_element_type=F32) / denom
        outs.extend(o4[j * qb:(j + 1) * qb] for j in range(GQA))
    attn = jnp.concatenate(outs, axis=-1)

    c0 = attn_w + 2 * kv_w
    cb = cur[:, c0:c0 + conv_w].astype(F32)
    u = cur[:, c0 + conv_w:c0 + 2 * conv_w].astype(F32) * cur[:, c0 + 2 * conv_w:c0 + 3 * conv_w].astype(F32)
    prow = prow_ref[...]
    up = prow[:, c0 + conv_w:c0 + 2 * conv_w].astype(F32) * prow[:, c0 + 2 * conv_w:c0 + 3 * conv_w].astype(F32)
    up = up * jnp.where(has_prev, 1.0, 0.0)
    pm1 = up[15:16]
    pm2 = up[14:15]
    ri = lax.broadcasted_iota(jnp.int32, u.shape, 0)
    u1 = jnp.where(ri == 0, pm1, pltpu.roll(u, 1, 0))
    u2 = jnp.where(ri == 0, pm2, jnp.where(ri == 1, pm1, pltpu.roll(u, 2, 0)))
    cw = cw_ref[...]
    conv = cb * (cw[0:1] * u2 + cw[1:2] * u1 + cw[2:3] * u)
    o_ref[...] = jnp.concatenate([attn, conv], axis=-1).astype(BF16)


def _mixer(proj, sinks2, conv_w, batch, seq, attn_w, kv_w, conv_wd):
    n, in_w = proj.shape
    nb = seq // Q_BLOCK
    kv_blk0 = attn_w // kv_w
    sub16 = Q_BLOCK // 16

    def cur_map(b, i):
        return (b * nb + i, 0)

    def prev_map(col):
        return lambda b, i: (b * nb + jnp.maximum(i - 1, 0), col)

    def prow_map(b, i):
        return (jnp.maximum((b * nb + i) * sub16 - 1, 0), 0)

    kern = functools.partial(_mixer_kernel, attn_w=attn_w, kv_w=kv_w, conv_w=conv_wd)
    return pl.pallas_call(
        kern,
        grid=(batch, nb),
        in_specs=[pl.BlockSpec((Q_BLOCK, in_w), cur_map),
                  pl.BlockSpec((Q_BLOCK, kv_w), prev_map(kv_blk0)),
                  pl.BlockSpec((Q_BLOCK, kv_w), prev_map(kv_blk0 + 1)),
                  pl.BlockSpec((16, in_w), prow_map),
                  pl.BlockSpec((1, N_Q_HEADS), lambda b, i: (0, 0)),
                  pl.BlockSpec((CONV_K, conv_wd), lambda b, i: (0, 0))],
        out_specs=pl.BlockSpec((Q_BLOCK, attn_w + conv_wd), cur_map),
        out_shape=jax.ShapeDtypeStruct((n, attn_w + conv_wd), BF16),
        compiler_params=_cparams(("arbitrary", "arbitrary")),
        name="mixer",
    )(proj, proj, proj, proj, sinks2, conv_w)


def _split_bf16(v):
    hi = v.astype(BF16)
    lo = (v - hi.astype(F32)).astype(BF16)
    return hi, lo


def _outproj_kernel(mix_ref, x_ref, mod_ref, w_ref, g_ref, b_ref, wr_ref, x1_ref, h2_ref, lg_ref):
    m = mod_ref[0]
    mix = jnp.dot(mix_ref[...], w_ref[...], preferred_element_type=F32)
    x1 = _layer_norm(ALPHA * x_ref[...] + (1.0 + m[2:3]) * mix, g_ref[...], b_ref[...])
    x1_ref[...] = x1
    h2 = x1 * (1.0 + m[4:5]) + m[3:4]
    tm, d = h2.shape
    words = _pack_pairs(h2[:, :d // 2], h2[:, d // 2:])
    for s in range(TOKEN_ROWS):
        h2_ref[pl.ds(s, tm, stride=TOKEN_ROWS), :] = words[:, s * LANES:(s + 1) * LANES]
    h_hi, h_lo = _split_bf16(h2)
    w_hi, w_lo = _split_bf16(wr_ref[...])
    lg_ref[...] = (jnp.dot(h_hi, w_hi, preferred_element_type=F32)
                   + (jnp.dot(h_hi, w_lo, preferred_element_type=F32)
                      + jnp.dot(h_lo, w_hi, preferred_element_type=F32)))


def _outproj(mix, x2, mod3, w_out_bf, ln_g, ln_b, w_router, seq):
    n, d = x2.shape
    tm = OUTPROJ_TM
    ne = w_router.shape[1]
    row = lambda i: (i, 0)
    const = lambda i: (0, 0)
    return pl.pallas_call(
        _outproj_kernel,
        grid=(n // tm,),
        in_specs=[pl.BlockSpec((tm, d), row),
                  pl.BlockSpec((tm, d), row),
                  pl.BlockSpec((1, 6, d), lambda i: (i * tm // seq, 0, 0)),
                  pl.BlockSpec((d, d), const),
                  pl.BlockSpec((1, d), const),
                  pl.BlockSpec((1, d), const),
                  pl.BlockSpec((d, ne), const)],
        out_specs=[pl.BlockSpec((tm, d), row),
                   pl.BlockSpec((tm * TOKEN_ROWS, LANES), row),
                   pl.BlockSpec((tm, ne), row)],
        out_shape=[jax.ShapeDtypeStruct((n, d), F32),
                   jax.ShapeDtypeStruct((n * TOKEN_ROWS, LANES), U32),
                   jax.ShapeDtypeStruct((n, ne), F32)],
        compiler_params=_cparams(("arbitrary",)),
        name="outproj",
    )(mix, x2, mod3, w_out_bf, ln_g, ln_b, w_router)


def _col_argmax(tiles, row_f):
    m = tiles[0]
    for t in tiles[1:]:
        m = jnp.maximum(m, t)
    m = jnp.max(m, axis=0, keepdims=True)
    idx = None
    for t, r in zip(tiles, row_f):
        c = jnp.where(t == m, r, float(N_EXPERTS))
        idx = c if idx is None else jnp.minimum(idx, c)
    return m, jnp.min(idx, axis=0, keepdims=True)


def _route_kernel(lg_ref, bias_ref, eidx_ref, w_ref, cnt_ref):
    lt = lg_ref[...].T
    tm = lt.shape[1]
    neg = -jnp.inf
    sub = lax.broadcasted_iota(jnp.int32, (GROUP_SIZE, tm), 0).astype(F32)
    row_f = [sub + float(g * GROUP_SIZE) for g in range(N_GROUPS)]
    scores = [jax.nn.sigmoid(lt[g * GROUP_SIZE:(g + 1) * GROUP_SIZE]) for g in range(N_GROUPS)]
    sel = [scores[g] + bias_ref[g * GROUP_SIZE:(g + 1) * GROUP_SIZE, :] for g in range(N_GROUPS)]
    gs = []
    for g in range(N_GROUPS):
        m1, i1 = _col_argmax([sel[g]], [row_f[g]])
        m2 = jnp.max(jnp.where(row_f[g] == i1, neg, sel[g]), axis=0, keepdims=True)
        gs.append(m1 + m2)
    cand = []
    for g in range(N_GROUPS):
        rank = jnp.zeros((1, tm), F32)
        for o in range(N_GROUPS):
            if o == g:
                continue
            ahead = (gs[o] >= gs[g]) if o < g else (gs[o] > gs[g])
            rank = rank + jnp.where(ahead, 1.0, 0.0)
        cand.append(jnp.where(rank < TOPK_GROUPS, sel[g], neg))
    idxs, ws = [], []
    chosen = [jnp.zeros((GROUP_SIZE, tm), F32) for _ in range(N_GROUPS)]
    for _ in range(TOP_K):
        _, ik = _col_argmax(cand, row_f)
        wk = jnp.zeros((1, tm), F32)
        for g in range(N_GROUPS):
            hit = row_f[g] == ik
            wk = wk + jnp.sum(jnp.where(hit, scores[g], 0.0), axis=0, keepdims=True)
            cand[g] = jnp.where(hit, neg, cand[g])
            chosen[g] = jnp.where(hit, 1.0, chosen[g])
        ws.append(wk)
        idxs.append(ik)
    for g in range(N_GROUPS):
        cnt_ref[0, g * GROUP_SIZE:(g + 1) * GROUP_SIZE, :] = jnp.sum(chosen[g], axis=1, keepdims=True)
    wsum = ws[0]
    for k in range(1, TOP_K):
        wsum = wsum + ws[k]
    eidx_ref[...] = jnp.concatenate(idxs, axis=0).astype(jnp.int32)
    w_ref[...] = jnp.concatenate([wk / wsum * ROUTED_SCALE for wk in ws], axis=0)


def _route(logits, bias_col):
    n, ne = logits.shape
    tm = ROUTE_TM
    col = lambda i: (0, i)
    return pl.pallas_call(
        _route_kernel,
        grid=(n // tm,),
        in_specs=[pl.BlockSpec((tm, ne), lambda i: (i, 0)), pl.BlockSpec((ne, 1), lambda i: (0, 0))],
        out_specs=[pl.BlockSpec((TOP_K, tm), col), pl.BlockSpec((TOP_K, tm), col),
                   pl.BlockSpec((1, ne, 1), lambda i: (i, 0, 0))],
        out_shape=[jax.ShapeDtypeStruct((TOP_K, n), jnp.int32), jax.ShapeDtypeStruct((TOP_K, n), F32),
                   jax.ShapeDtypeStruct((n // tm, ne, 1), F32)],
        compiler_params=_cparams(("arbitrary",)),
        name="route",
    )(logits, bias_col)


def _issue_rows(lo, hi, issue_one):
    n_full = (hi - lo) // ISSUE_UNROLL

    def chunk(c, carry):
        for u in range(ISSUE_UNROLL):
            issue_one(lo + c * ISSUE_UNROLL + u)
        return carry

    def tail(r, carry):
        issue_one(r)
        return carry

    lax.fori_loop(0, n_full, chunk, 0)
    lax.fori_loop(lo + n_full * ISSUE_UNROLL, hi, tail, 0)


def _hbm_slab(ref, row):
    return ref.at[pl.ds(pl.multiple_of(row * TOKEN_ROWS, TOKEN_ROWS), TOKEN_ROWS), :]


def _moe_kernel(be_ref, new_ref, nexte_ref, epar_ref, nused_ref, tgt_hbm, h_hbm, wg_hbm, wu_hbm, wd_hbm, ys_hbm,
                idx_s, xbuf, ybuf, wg_f, wu_f, wd_f, wg_s, wu_s, wd_s, sem_i, sem_g, sem_s, sem_w, *, n_tok):
    s = pl.program_id(0)
    tm = MOE_TM
    per = IDX_CHUNK // tm
    per_log2 = per.bit_length() - 1
    n_used = nused_ref[0]
    slab = TOKEN_ROWS
    buf_rows = tm * SLAB_PITCH
    dump_row0 = TOP_K * n_tok
    slot = s & 1

    def staged(buf, base, r):
        return buf.at[pl.ds(pl.multiple_of(base + r * SLAB_PITCH, SUBLANES), slab), :]

    def idx_copy(c):
        return pltpu.make_async_copy(
            tgt_hbm.at[pl.ds(pl.multiple_of(c * IDX_CHUNK, IDX_CHUNK), IDX_CHUNK)],
            idx_s.at[pl.ds(pl.multiple_of((c & 1) * IDX_CHUNK, IDX_CHUNK), IDX_CHUNK)], sem_i)

    def idx_base(b):
        return ((b >> per_log2) & 1) * IDX_CHUNK + (b & (per - 1)) * tm

    def weight_copies(e, p):
        return (pltpu.make_async_copy(wg_hbm.at[e], wg_f.at[p], sem_w.at[p]),
                pltpu.make_async_copy(wu_hbm.at[e], wu_f.at[p], sem_w.at[p]),
                pltpu.make_async_copy(wd_hbm.at[e], wd_f.at[p], sem_w.at[p]))

    def for_rows(inline, body):
        if inline:
            for r in range(tm):
                body(r, r % 2)
        else:
            def pair(c, carry):
                body(2 * c, 0)
                body(2 * c + 1, 1)
                return carry
            lax.fori_loop(0, tm // 2, pair, 0)

    def issue_gather(b, to_slot, inline):
        ibase = idx_base(b)
        xbase = to_slot * buf_rows

        def one(r, prio):
            tok = idx_s[ibase + r] & (n_tok - 1)
            pltpu.make_async_copy(_hbm_slab(h_hbm, tok), staged(xbuf, xbase, r),
                                  sem_g.at[to_slot]).start(priority=0)
        for_rows(inline, one)

    def issue_scatter(b, from_slot, to_dump, inline):
        ibase = idx_base(b)
        ybase = from_slot * buf_rows

        def one(r, prio):
            tgt = jnp.where(to_dump, dump_row0 + tm + r, idx_s[ibase + r])
            pltpu.make_async_copy(staged(ybuf, ybase, r), _hbm_slab(ys_hbm, tgt),
                                  sem_s.at[from_slot]).start(priority=1)
        for_rows(inline, one)

    def wait_gather(at_slot):
        v = xbuf.at[pl.ds(pl.multiple_of(at_slot * buf_rows, SUBLANES), buf_rows), :]
        pltpu.make_async_copy(h_hbm.at[pl.ds(0, buf_rows), :], v, sem_g.at[at_slot]).wait()

    def wait_scatter(at_slot):
        v = ybuf.at[pl.ds(pl.multiple_of(at_slot * buf_rows, SUBLANES), buf_rows), :]
        pltpu.make_async_copy(v, ys_hbm.at[pl.ds(0, buf_rows), :], sem_s.at[at_slot]).wait()

    @pl.when(s == 0)
    def _():
        ybuf[...] = jnp.zeros(ybuf.shape, U32)
        first = idx_copy(0)
        first.start()
        first.wait()
        init = pltpu.make_async_copy(ybuf, ys_hbm.at[pl.ds(pl.multiple_of(dump_row0 * slab, slab), 2 * buf_rows), :],
                                     sem_i)
        init.start()
        init.wait()
        issue_gather(0, 0, inline=False)
        for cp in weight_copies(be_ref[0], 0):
            cp.start()

    @pl.when(s < n_used)
    def _():
        @pl.when((s & (per - 1)) == 1)
        def _():
            idx_copy((s >> per_log2) + 1).start()

        @pl.when(((s + 1) & (per - 1)) == 0)
        def _():
            idx_copy((s + 1) >> per_log2).wait()

        @pl.when(new_ref[s] == 1)
        def _():
            p = epar_ref[s]

            @pl.when(nexte_ref[s] >= 0)
            def _():
                for cp in weight_copies(nexte_ref[s], 1 - p):
                    cp.start(priority=1)

            for cp in weight_copies(be_ref[s], p):
                cp.wait()
            wg_s[...] = wg_f[p].astype(BF16)
            wu_s[...] = wu_f[p].astype(BF16)
            wd_s[...] = wd_f[p].astype(BF16)

        wait_gather(slot)

        @pl.when(s >= 1)
        def _():
            wait_scatter(slot)

        issue_gather(s + 1, 1 - slot, inline=True)
        issue_scatter(jnp.maximum(s - 1, 0), 1 - slot, s == 0, inline=True)
        base = slot * buf_rows
        x_lo, x_hi = _unpack_pairs(_slab_rows_to_matrix(xbuf, base, tm, SLAB_PITCH))
        x = jnp.concatenate([x_lo.astype(BF16), x_hi.astype(BF16)], axis=-1)
        gate = jnp.dot(x, wg_s[...], preferred_element_type=F32)
        up = jnp.dot(x, wu_s[...], preferred_element_type=F32)
        act = (_silu(gate) * up).astype(BF16)
        y = jnp.dot(act, wd_s[...], preferred_element_type=F32)
        half = slab * LANES
        words = _pack_pairs(y[:, :half], y[:, half:])
        for j in range(slab):
            ybuf[pl.ds(base + j, tm, stride=SLAB_PITCH), :] = words[:, j * LANES:(j + 1) * LANES]

    @pl.when(s == n_used)
    def _():
        last = (s - 1) & (per - 1)

        @pl.when((last == 1) | (last == 2))
        def _():
            idx_copy(((s - 1) >> per_log2) + 1).wait()

        wait_gather(slot)
        wait_scatter(slot)
        issue_scatter(s - 1, 1 - slot, False, inline=False)
        wait_scatter(1 - slot)


def _moe(block_e, block_new, block_nexte, block_epar, n_used, row_tgt, h2d, w_gate, w_up, w_down):
    n = h2d.shape[0] // TOKEN_ROWS
    assert n & (n - 1) == 0
    d = 2 * TOKEN_ROWS * LANES
    de = w_gate.shape[2]
    tm = MOE_TM
    n_blk = block_e.shape[0]
    grid_spec = pltpu.PrefetchScalarGridSpec(
        num_scalar_prefetch=5,
        grid=(n_blk,),
        in_specs=[pl.BlockSpec(memory_space=pl.ANY)] * 5,
        out_specs=pl.BlockSpec(memory_space=pl.ANY),
        scratch_shapes=[pltpu.SMEM((2 * IDX_CHUNK,), jnp.int32),
                        pltpu.VMEM((2 * tm * SLAB_PITCH, LANES), U32),
                        pltpu.VMEM((2 * tm * SLAB_PITCH, LANES), U32),
                        pltpu.VMEM((2, d, de), F32),
                        pltpu.VMEM((2, d, de), F32),
                        pltpu.VMEM((2, de, d), F32),
                        pltpu.VMEM((d, de), BF16),
                        pltpu.VMEM((d, de), BF16),
                        pltpu.VMEM((de, d), BF16),
                        pltpu.SemaphoreType.DMA,
                        pltpu.SemaphoreType.DMA((2,)),
                        pltpu.SemaphoreType.DMA((2,)),
                        pltpu.SemaphoreType.DMA((2,))])
    return pl.pallas_call(
        functools.partial(_moe_kernel, n_tok=n),
        grid_spec=grid_spec,
        out_shape=jax.ShapeDtypeStruct(((TOP_K * n + 2 * tm) * TOKEN_ROWS, LANES), U32),
        compiler_params=_cparams(("arbitrary",)),
        name="moe",
    )(block_e, block_new, block_nexte, block_epar, n_used, row_tgt, h2d, w_gate, w_up, w_down)


def _final_kernel(gw_hbm, *refs):
    ys_refs = refs[:TOP_K]
    h_ref, x1_ref, mod_ref, wg_ref, wu_ref, wd_ref, g_ref, b_ref, o_ref, gw_s, acc_ref, sem_i = refs[TOP_K:]
    i = pl.program_id(0)
    tm = x1_ref.shape[0]
    per_step = tm * TOP_K
    slot = i & 1

    def gw_copy(step):
        half = pl.ds(pl.multiple_of((step & 1) * per_step, per_step), per_step)
        return pltpu.make_async_copy(gw_hbm.at[pl.ds(pl.multiple_of(step * per_step, per_step), per_step)],
                                     gw_s.at[half], sem_i.at[step & 1])

    @pl.when(i == 0)
    def _():
        gw_copy(0).start()

    @pl.when(i + 1 < pl.num_programs(0))
    def _():
        gw_copy(i + 1).start()

    gw_copy(i).wait()
    gbase = slot * per_step
    hi_base = tm * SLAB_PITCH

    def combine(t, carry):
        rows = pl.ds(pl.multiple_of(t * TOKEN_ROWS, TOKEN_ROWS), TOKEN_ROWS)
        acc_lo = acc_hi = None
        for k in range(TOP_K):
            lo, hi = _unpack_pairs(ys_refs[k][rows, :])
            g = gw_s[gbase + t * TOP_K + k]
            acc_lo = g * lo if acc_lo is None else acc_lo + g * lo
            acc_hi = g * hi if acc_hi is None else acc_hi + g * hi
        acc_ref[pl.ds(pl.multiple_of(t * SLAB_PITCH, SUBLANES), TOKEN_ROWS), :] = acc_lo
        acc_ref[pl.ds(pl.multiple_of(hi_base + t * SLAB_PITCH, SUBLANES), TOKEN_ROWS), :] = acc_hi
        return carry

    lax.fori_loop(0, tm, combine, 0, unroll=32)
    moe = jnp.concatenate([_slab_rows_to_matrix(acc_ref, 0, tm, SLAB_PITCH),
                           _slab_rows_to_matrix(acc_ref, hi_base, tm, SLAB_PITCH)], axis=-1)
    h_lo, h_hi = _unpack_pairs(_slab_rows_to_matrix(h_ref, 0, tm, TOKEN_ROWS))
    h = jnp.concatenate([h_lo.astype(BF16), h_hi.astype(BF16)], axis=-1)
    gate = jnp.dot(h, wg_ref[...], preferred_element_type=F32)
    up = jnp.dot(h, wu_ref[...], preferred_element_type=F32)
    shared = jnp.dot((_silu(gate) * up).astype(BF16), wd_ref[...], preferred_element_type=F32)
    m = mod_ref[0]
    y = ALPHA * x1_ref[...] + (1.0 + m[5:6]) * (moe + shared)
    o_ref[...] = _layer_norm(y, g_ref[...], b_ref[...])


def _final(gate_w, ys, h2d, x1, mod3, wsg, wsu, wsd, ln_g, ln_b, seq):
    n, d = x1.shape
    de = wsg.shape[1]
    tm = FINAL_TM
    per_step = tm * TOP_K
    assert per_step % IDX_CHUNK == 0
    row = lambda i: (i, 0)
    const = lambda i: (0, 0)
    slot_rows = lambda k: (lambda i: (k * (n // tm) + i, 0))
    return pl.pallas_call(
        _final_kernel,
        grid=(n // tm,),
        in_specs=[pl.BlockSpec(memory_space=pl.ANY)]
                 + [pl.BlockSpec((tm * TOKEN_ROWS, LANES), slot_rows(k)) for k in range(TOP_K)]
                 + [pl.BlockSpec((tm * TOKEN_ROWS, LANES), row),
                  pl.BlockSpec((tm, d), row),
                  pl.BlockSpec((1, 6, d), lambda i: (i * tm // seq, 0, 0)),
                  pl.BlockSpec((d, de), const),
                  pl.BlockSpec((d, de), const),
                  pl.BlockSpec((de, d), const),
                  pl.BlockSpec((1, d), const),
                  pl.BlockSpec((1, d), const)],
        out_specs=pl.BlockSpec((tm, d), row),
        out_shape=jax.ShapeDtypeStruct((n, d), F32),
        scratch_shapes=[pltpu.SMEM((2 * per_step,), F32),
                        pltpu.VMEM((2 * tm * SLAB_PITCH, LANES), F32),
                        pltpu.SemaphoreType.DMA((2,))],
        compiler_params=_cparams(("arbitrary",)),
        name="final",
    )(gate_w, *([ys] * TOP_K), h2d, x1, mod3, wsg, wsu, wsd, ln_g, ln_b)


def _dispatch_tables(eidx, tile_counts, n):
    tm = MOE_TM
    a = n * TOP_K
    i32 = jnp.int32
    experts = jnp.arange(N_EXPERTS, dtype=i32)
    counts = jnp.sum(tile_counts, axis=(0, 2)).astype(i32)
    padded = (counts + tm - 1) // tm * tm
    pad_end = jnp.cumsum(padded)
    starts = pad_end - padded
    n_blk = a // tm + N_EXPERTS + 1
    blk_start = jnp.arange(n_blk, dtype=i32) * tm
    n_used = pad_end[-1] // tm
    in_use = jnp.arange(n_blk) < n_used
    raw_e = jnp.minimum(jnp.sum((pad_end[None, :] <= blk_start[:, None]).astype(i32), axis=1), N_EXPERTS - 1)
    last_e = jnp.sum(jnp.where(jnp.arange(n_blk) == n_used - 1, raw_e, 0))
    block_e = jnp.where(in_use, raw_e, last_e)
    onehot = block_e[:, None] == experts[None, :]
    block_new = jnp.concatenate([jnp.ones((1,), i32), (block_e[1:] != block_e[:-1]).astype(i32)])
    has_rows = counts > 0
    later = (experts[None, :] > experts[:, None]) & has_rows[None, :]
    next_e = jnp.min(jnp.where(later, experts[None, :], N_EXPERTS), axis=1)
    next_e = jnp.where(next_e == N_EXPERTS, -1, next_e)
    parity_e = (jnp.cumsum(has_rows.astype(i32)) - 1) & 1
    block_nexte = jnp.sum(jnp.where(onehot, next_e[None, :], 0), axis=1).astype(i32)
    block_epar = jnp.sum(jnp.where(onehot, parity_e[None, :], 0), axis=1).astype(i32)
    cnt_b = jnp.sum(jnp.where(onehot, counts[None, :], 0), axis=1)
    start_b = jnp.sum(jnp.where(onehot, starts[None, :], 0), axis=1)
    block_nv = jnp.where(in_use, jnp.clip(cnt_b - (blk_start - start_b), 0, tm), 0)
    dummy_keys = jnp.where(jnp.arange(tm, dtype=i32)[None, :] < (padded - counts)[:, None],
                           experts[:, None], N_EXPERTS).reshape(-1)
    keys = jnp.concatenate([eidx.reshape(-1), dummy_keys, jnp.full((tm,), N_EXPERTS, i32)])
    tgt = jnp.arange(a, dtype=i32)
    bits = (a - 1).bit_length() + 1
    low = (1 << bits) - 1
    payload = jnp.concatenate([tgt, jnp.full((n_blk * tm - a,), low, i32)])
    row_tgt = lax.sort(keys * (1 << bits) + payload) & low
    r = jnp.arange(tm, dtype=i32)[None, :]
    dump = a + (jnp.arange(n_blk, dtype=i32)[:, None] & 1) * tm + r
    row_tgt = jnp.where(r >= block_nv[:, None], dump, row_tgt.reshape(n_blk, tm)).reshape(-1)
    row_tgt = jnp.pad(row_tgt, (0, -(n_blk * tm) % IDX_CHUNK))
    return block_e.astype(i32), block_new, block_nexte, block_epar, n_used.astype(i32).reshape(1), row_tgt


def kernel(x, c, w_mod, b_mod, w_in, conv_w, attn_sinks, w_out, ln1_g, ln1_b, w_router, router_bias,
           w_gate, w_up, w_down, ws_gate, ws_up, ws_down, ln2_g, ln2_b):
    b, s, d = x.shape
    n = b * s
    attn_w = N_Q_HEADS * HEAD_DIM
    kv_w = N_KV_HEADS * HEAD_DIM
    conv_wd = d - attn_w
    in_w = attn_w + 2 * kv_w + 3 * conv_wd
    x2 = x.reshape(n, d)
    c8 = jnp.zeros((SUBLANES, d), F32).at[:b].set(c)
    for l in range(DEPTH):
        mod = _mod(c8, w_mod[l], b_mod[l].reshape(1, -1))[:b]
        mod3 = mod.reshape(b, 6, d)
        proj = _inproj(x2, mod3, w_in[l].astype(BF16), s)
        mix = _mixer(proj, attn_sinks[l].reshape(1, -1), conv_w[l], b, s, attn_w, kv_w, conv_wd)
        x1, h2d, logits = _outproj(mix, x2, mod3, w_out[l].astype(BF16), ln1_g[l].reshape(1, -1),
                                   ln1_b[l].reshape(1, -1), w_router[l], s)
        eidx, gate_w, tile_counts = _route(logits, router_bias[l].reshape(-1, 1))
        block_e, block_new, block_nexte, block_epar, n_used, row_tgt = _dispatch_tables(eidx, tile_counts, n)
        ys = _moe(block_e, block_new, block_nexte, block_epar, n_used, row_tgt, h2d,
                  w_gate[l], w_up[l], w_down[l])
        x2 = _final(gate_w.T.reshape(-1), ys, h2d, x1, mod3,
                    ws_gate[l].astype(BF16), ws_up[l].astype(BF16), ws_down[l].astype(BF16),
                    ln2_g[l].reshape(1, -1), ln2_b[l].reshape(1, -1), s)
    return x2.reshape(b, s, d)
```

```python
import functools

import jax
import jax.numpy as jnp
from jax import lax
from jax.experimental import pallas as pl
from jax.experimental.pallas import tpu as pltpu

HEAD_DIM = 64
N_Q_HEADS = 16
N_KV_HEADS = 4
GQA = N_Q_HEADS // N_KV_HEADS
CONV_K = 3
WINDOW = 128
Q_BLOCK = 128
N_EXPERTS = 64
TOP_K = 8
N_GROUPS = 8
GROUP_SIZE = N_EXPERTS // N_GROUPS
TOPK_GROUPS = 4
ROUTED_SCALE = 2.5
DEPTH = 1
ALPHA = (2.0 * DEPTH) ** 0.25
LN_EPS = 1e-5

LANES = 128
SUBLANES = 8
TOKEN_ROWS = 8
SLAB_PITCH = 8
VMEM_LIMIT = 56 * 1024 * 1024

MOD_TN = 2048
INPROJ_TM = 1024
INPROJ_TN = 2304
OUTPROJ_TM = 256
ROUTE_TM = 1024
MOE_TM = 256
FINAL_TM = 256
IDX_CHUNK = 1024
ISSUE_UNROLL = 8

F32 = jnp.float32
BF16 = jnp.bfloat16


def _cparams(sem):
    return pltpu.CompilerParams(dimension_semantics=sem, vmem_limit_bytes=VMEM_LIMIT)


def _silu(v):
    return v * jax.nn.sigmoid(v)


U32 = jnp.uint32
HI_MASK = 0xFFFF0000


def _pack_pairs(lo, hi):
    lo_bits = lax.bitcast_convert_type(lo.astype(BF16).astype(F32), U32) >> 16
    hi_bits = lax.bitcast_convert_type(hi.astype(BF16).astype(F32), U32) & U32(HI_MASK)
    return lo_bits | hi_bits


def _unpack_pairs(w):
    return (lax.bitcast_convert_type(w << 16, F32), lax.bitcast_convert_type(w & U32(HI_MASK), F32))


def _slab_rows_to_matrix(ref, base, tm, pitch):
    return jnp.concatenate([ref[pl.ds(base + s, tm, stride=pitch), :] for s in range(TOKEN_ROWS)], axis=-1)


def _layer_norm(y, g, b):
    mu = jnp.mean(y, axis=-1, keepdims=True)
    yc = y - mu
    var = jnp.mean(yc * yc, axis=-1, keepdims=True)
    return yc * lax.rsqrt(var + LN_EPS) * g + b


def _mod_kernel(c_ref, w_ref, b_ref, o_ref):
    cs = _silu(c_ref[...]).astype(BF16)
    o_ref[...] = jnp.dot(cs, w_ref[...].astype(BF16), preferred_element_type=F32) + b_ref[...]


def _mod(c8, w_mod, b_mod):
    d, n = w_mod.shape
    return pl.pallas_call(
        _mod_kernel,
        grid=(n // MOD_TN,),
        in_specs=[pl.BlockSpec((SUBLANES, d), lambda j: (0, 0)),
                  pl.BlockSpec((d, MOD_TN), lambda j: (0, j)),
                  pl.BlockSpec((1, MOD_TN), lambda j: (0, j))],
        out_specs=pl.BlockSpec((SUBLANES, MOD_TN), lambda j: (0, j)),
        out_shape=jax.ShapeDtypeStruct((SUBLANES, n), F32),
        compiler_params=_cparams(("arbitrary",)),
        name="mod",
    )(c8, w_mod, b_mod)


def _inproj_kernel(x_ref, mod_ref, w_ref, o_ref, h_ref):
    j = pl.program_id(1)

    @pl.when(j == 0)
    def _():
        m = mod_ref[0]
        h_ref[...] = (x_ref[...] * (1.0 + m[1:2]) + m[0:1]).astype(BF16)

    o_ref[...] = jnp.dot(h_ref[...], w_ref[...], preferred_element_type=F32).astype(BF16)


def _inproj(x2, mod3, w_in_bf, seq):
    n, d = x2.shape
    in_w = w_in_bf.shape[1]
    tm, tn = INPROJ_TM, INPROJ_TN
    return pl.pallas_call(
        _inproj_kernel,
        grid=(n // tm, in_w // tn),
        in_specs=[pl.BlockSpec((tm, d), lambda i, j: (i, 0)),
                  pl.BlockSpec((1, 6, d), lambda i, j: (i * tm // seq, 0, 0)),
                  pl.BlockSpec((d, tn), lambda i, j: (0, j))],
        out_specs=pl.BlockSpec((tm, tn), lambda i, j: (i, j)),
        out_shape=jax.ShapeDtypeStruct((n, in_w), BF16),
        scratch_shapes=[pltpu.VMEM((tm, d), BF16)],
        compiler_params=_cparams(("arbitrary", "arbitrary")),
        name="inproj",
    )(x2, mod3, w_in_bf)


def _mixer_kernel(cur_ref, pk_ref, pv_ref, prow_ref, sink_ref, cw_ref, o_ref, *, attn_w, kv_w, conv_w):
    nblk = pl.program_id(1)
    has_prev = nblk > 0
    qb = Q_BLOCK
    cur = cur_ref[...]
    k_cur = cur[:, attn_w:attn_w + kv_w]
    v_cur = cur[:, attn_w + kv_w:attn_w + 2 * kv_w]
    k_all = jnp.concatenate([pk_ref[...], k_cur], axis=0)
    v_all = jnp.concatenate([pv_ref[...], v_cur], axis=0)

    rows = GQA * qb
    qi = lax.broadcasted_iota(jnp.int32, (rows, 2 * qb), 0) % qb
    kj = lax.broadcasted_iota(jnp.int32, (rows, 2 * qb), 1)
    dist = qi + qb - kj
    kmin = jnp.where(has_prev, 0, qb)
    valid = (dist >= 0) & (dist < WINDOW) & (kj >= kmin)
    distf = dist.astype(F32)
    head_in_group = lax.broadcasted_iota(jnp.int32, (rows, 1), 0) // qb
    sinks = sink_ref[...]

    outs = []
    for g in range(N_KV_HEADS):
        q4 = jnp.concatenate(
            [cur[:, (g * GQA + j) * HEAD_DIM:(g * GQA + j + 1) * HEAD_DIM] for j in range(GQA)], axis=0)
        kg = k_all[:, g * HEAD_DIM:(g + 1) * HEAD_DIM]
        vg = v_all[:, g * HEAD_DIM:(g + 1) * HEAD_DIM]
        s = lax.dot_general(q4, kg, (((1,), (1,)), ((), ())), preferred_element_type=F32)
        s = s * (HEAD_DIM ** -0.5)
        slope = jnp.zeros((rows, 1), F32)
        sink = jnp.zeros((rows, 1), F32)
        for j in range(GQA):
            h = g * GQA + j
            sel = head_in_group == j
            slope = jnp.where(sel, 2.0 ** (-8.0 * (h + 1) / N_Q_HEADS), slope)
            sink = jnp.where(sel, sinks[:, h:h + 1], sink)
        s = jnp.where(valid, s - slope * distf, -jnp.inf)
        m = jnp.maximum(jnp.max(s, axis=-1, keepdims=True), sink)
        p = jnp.exp(s - m)
        denom = jnp.sum(p, axis=-1, keepdims=True) + jnp.exp(sink - m)
        o4 = jnp.dot(p.astype(BF16), vg, preferred_element_type=F32) / denom
        outs.extend(o4[j * qb:(j + 1) * qb] for j in range(GQA))
    attn = jnp.concatenate(outs, axis=-1)

    c0 = attn_w + 2 * kv_w
    cb = cur[:, c0:c0 + conv_w].astype(F32)
    u = cur[:, c0 + conv_w:c0 + 2 * conv_w].astype(F32) * cur[:, c0 + 2 * conv_w:c0 + 3 * conv_w].astype(F32)
    prow = prow_ref[...]
    up = prow[:, c0 + conv_w:c0 + 2 * conv_w].astype(F32) * prow[:, c0 + 2 * conv_w:c0 + 3 * conv_w].astype(F32)
    up = up * jnp.where(has_prev, 1.0, 0.0)
    pm1 = up[15:16]
    pm2 = up[14:15]
    ri = lax.broadcasted_iota(jnp.int32, u.shape, 0)
    u1 = jnp.where(ri == 0, pm1, pltpu.roll(u, 1, 0))
    u2 = jnp.where(ri == 0, pm2, jnp.where(ri == 1, pm1, pltpu.roll(u, 2, 0)))
    cw = cw_ref[...]
    conv = cb * (cw[0:1] * u2 + cw[1:2] * u1 + cw[2:3] * u)
    o_ref[...] = jnp.concatenate([attn, conv], axis=-1).astype(BF16)


def _mixer(proj, sinks2, conv_w, batch, seq, attn_w, kv_w, conv_wd):
    n, in_w = proj.shape
    nb = seq // Q_BLOCK
    kv_blk0 = attn_w // kv_w
    sub16 = Q_BLOCK // 16

    def cur_map(b, i):
        return (b * nb + i, 0)

    def prev_map(col):
        return lambda b, i: (b * nb + jnp.maximum(i - 1, 0), col)

    def prow_map(b, i):
        return (jnp.maximum((b * nb + i) * sub16 - 1, 0), 0)

    kern = functools.partial(_mixer_kernel, attn_w=attn_w, kv_w=kv_w, conv_w=conv_wd)
    return pl.pallas_call(
        kern,
        grid=(batch, nb),
        in_specs=[pl.BlockSpec((Q_BLOCK, in_w), cur_map),
                  pl.BlockSpec((Q_BLOCK, kv_w), prev_map(kv_blk0)),
                  pl.BlockSpec((Q_BLOCK, kv_w), prev_map(kv_blk0 + 1)),
                  pl.BlockSpec((16, in_w), prow_map),
                  pl.BlockSpec((1, N_Q_HEADS), lambda b, i: (0, 0)),
                  pl.BlockSpec((CONV_K, conv_wd), lambda b, i: (0, 0))],
        out_specs=pl.BlockSpec((Q_BLOCK, attn_w + conv_wd), cur_map),
        out_shape=jax.ShapeDtypeStruct((n, attn_w + conv_wd), BF16),
        compiler_params=_cparams(("arbitrary", "arbitrary")),
        name="mixer",
    )(proj, proj, proj, proj, sinks2, conv_w)


def _split_bf16(v):
    hi = v.astype(BF16)
    lo = (v - hi.astype(F32)).astype(BF16)
    return hi, lo


def _outproj_kernel(mix_ref, x_ref, mod_ref, w_ref, g_ref, b_ref, wr_ref, x1_ref, h2_ref, lg_ref):
    m = mod_ref[0]
    mix = jnp.dot(mix_ref[...], w_ref[...], preferred_element_type=F32)
    x1 = _layer_norm(ALPHA * x_ref[...] + (1.0 + m[2:3]) * mix, g_ref[...], b_ref[...])
    x1_ref[...] = x1
    h2 = x1 * (1.0 + m[4:5]) + m[3:4]
    tm, d = h2.shape
    words = _pack_pairs(h2[:, :d // 2], h2[:, d // 2:])
    for s in range(TOKEN_ROWS):
        h2_ref[pl.ds(s, tm, stride=TOKEN_ROWS), :] = words[:, s * LANES:(s + 1) * LANES]
    h_hi, h_lo = _split_bf16(h2)
    w_hi, w_lo = _split_bf16(wr_ref[...])
    lg_ref[...] = (jnp.dot(h_hi, w_hi, preferred_element_type=F32)
                   + (jnp.dot(h_hi, w_lo, preferred_element_type=F32)
                      + jnp.dot(h_lo, w_hi, preferred_element_type=F32)))


def _outproj(mix, x2, mod3, w_out_bf, ln_g, ln_b, w_router, seq):
    n, d = x2.shape
    tm = OUTPROJ_TM
    ne = w_router.shape[1]
    row = lambda i: (i, 0)
    const = lambda i: (0, 0)
    return pl.pallas_call(
        _outproj_kernel,
        grid=(n // tm,),
        in_specs=[pl.BlockSpec((tm, d), row),
                  pl.BlockSpec((tm, d), row),
                  pl.BlockSpec((1, 6, d), lambda i: (i * tm // seq, 0, 0)),
                  pl.BlockSpec((d, d), const),
                  pl.BlockSpec((1, d), const),
                  pl.BlockSpec((1, d), const),
                  pl.BlockSpec((d, ne), const)],
        out_specs=[pl.BlockSpec((tm, d), row),
                   pl.BlockSpec((tm * TOKEN_ROWS, LANES), row),
                   pl.BlockSpec((tm, ne), row)],
        out_shape=[jax.ShapeDtypeStruct((n, d), F32),
                   jax.ShapeDtypeStruct((n * TOKEN_ROWS, LANES), U32),
                   jax.ShapeDtypeStruct((n, ne), F32)],
        compiler_params=_cparams(("arbitrary",)),
        name="outproj",
    )(mix, x2, mod3, w_out_bf, ln_g, ln_b, w_router)


def _col_argmax(tiles, row_f):
    m = tiles[0]
    for t in tiles[1:]:
        m = jnp.maximum(m, t)
    m = jnp.max(m, axis=0, keepdims=True)
    idx = None
    for t, r in zip(tiles, row_f):
        c = jnp.where(t == m, r, float(N_EXPERTS))
        idx = c if idx is None else jnp.minimum(idx, c)
    return m, jnp.min(idx, axis=0, keepdims=True)


def _route_kernel(lg_ref, bias_ref, eidx_ref, w_ref, cnt_ref):
    lt = lg_ref[...].T
    tm = lt.shape[1]
    neg = -jnp.inf
    sub = lax.broadcasted_iota(jnp.int32, (GROUP_SIZE, tm), 0).astype(F32)
    row_f = [sub + float(g * GROUP_SIZE) for g in range(N_GROUPS)]
    scores = [jax.nn.sigmoid(lt[g * GROUP_SIZE:(g + 1) * GROUP_SIZE]) for g in range(N_GROUPS)]
    sel = [scores[g] + bias_ref[g * GROUP_SIZE:(g + 1) * GROUP_SIZE, :] for g in range(N_GROUPS)]
    gs = []
    for g in range(N_GROUPS):
        m1, i1 = _col_argmax([sel[g]], [row_f[g]])
        m2 = jnp.max(jnp.where(row_f[g] == i1, neg, sel[g]), axis=0, keepdims=True)
        gs.append(m1 + m2)
    cand = []
    for g in range(N_GROUPS):
        rank = jnp.zeros((1, tm), F32)
        for o in range(N_GROUPS):
            if o == g:
                continue
            ahead = (gs[o] >= gs[g]) if o < g else (gs[o] > gs[g])
            rank = rank + jnp.where(ahead, 1.0, 0.0)
        cand.append(jnp.where(rank < TOPK_GROUPS, sel[g], neg))
    idxs, ws = [], []
    chosen = [jnp.zeros((GROUP_SIZE, tm), F32) for _ in range(N_GROUPS)]
    for _ in range(TOP_K):
        _, ik = _col_argmax(cand, row_f)
        wk = jnp.zeros((1, tm), F32)
        for g in range(N_GROUPS):
            hit = row_f[g] == ik
            wk = wk + jnp.sum(jnp.where(hit, scores[g], 0.0), axis=0, keepdims=True)
            cand[g] = jnp.where(hit, neg, cand[g])
            chosen[g] = jnp.where(hit, 1.0, chosen[g])
        ws.append(wk)
        idxs.append(ik)
    for g in range(N_GROUPS):
        cnt_ref[0, g * GROUP_SIZE:(g + 1) * GROUP_SIZE, :] = jnp.sum(chosen[g], axis=1, keepdims=True)
    wsum = ws[0]
    for k in range(1, TOP_K):
        wsum = wsum + ws[k]
    eidx_ref[...] = jnp.concatenate(idxs, axis=0).astype(jnp.int32)
    w_ref[...] = jnp.concatenate([wk / wsum * ROUTED_SCALE for wk in ws], axis=0)


def _route(logits, bias_col):
    n, ne = logits.shape
    tm = ROUTE_TM
    col = lambda i: (0, i)
    return pl.pallas_call(
        _route_kernel,
        grid=(n // tm,),
        in_specs=[pl.BlockSpec((tm, ne), lambda i: (i, 0)), pl.BlockSpec((ne, 1), lambda i: (0, 0))],
        out_specs=[pl.BlockSpec((TOP_K, tm), col), pl.BlockSpec((TOP_K, tm), col),
                   pl.BlockSpec((1, ne, 1), lambda i: (i, 0, 0))],
        out_shape=[jax.ShapeDtypeStruct((TOP_K, n), jnp.int32), jax.ShapeDtypeStruct((TOP_K, n), F32),
                   jax.ShapeDtypeStruct((n // tm, ne, 1), F32)],
        compiler_params=_cparams(("arbitrary",)),
        name="route",
    )(logits, bias_col)


def _issue_rows(lo, hi, issue_one):
    n_full = (hi - lo) // ISSUE_UNROLL

    def chunk(c, carry):
        for u in range(ISSUE_UNROLL):
            issue_one(lo + c * ISSUE_UNROLL + u)
        return carry

    def tail(r, carry):
        issue_one(r)
        return carry

    lax.fori_loop(0, n_full, chunk, 0)
    lax.fori_loop(lo + n_full * ISSUE_UNROLL, hi, tail, 0)


def _hbm_slab(ref, row):
    return ref.at[pl.ds(pl.multiple_of(row * TOKEN_ROWS, TOKEN_ROWS), TOKEN_ROWS), :]


def _moe_kernel(be_ref, new_ref, nexte_ref, epar_ref, nused_ref, tgt_hbm, h_hbm, wg_hbm, wu_hbm, wd_hbm, ys_hbm,
                idx_s, xbuf, ybuf, wg_f, wu_f, wd_f, wg_s, wu_s, wd_s, sem_i, sem_g, sem_s, sem_w, *, n_tok):
    s = pl.program_id(0)
    tm = MOE_TM
    per = IDX_CHUNK // tm
    per_log2 = per.bit_length() - 1
    n_used = nused_ref[0]
    slab = TOKEN_ROWS
    buf_rows = tm * SLAB_PITCH
    dump_row0 = TOP_K * n_tok
    slot = s & 1

    def staged(buf, base, r):
        return buf.at[pl.ds(pl.multiple_of(base + r * SLAB_PITCH, SUBLANES), slab), :]

    def idx_copy(c):
        return pltpu.make_async_copy(
            tgt_hbm.at[pl.ds(pl.multiple_of(c * IDX_CHUNK, IDX_CHUNK), IDX_CHUNK)],
            idx_s.at[pl.ds(pl.multiple_of((c & 1) * IDX_CHUNK, IDX_CHUNK), IDX_CHUNK)], sem_i)

    def idx_base(b):
        return ((b >> per_log2) & 1) * IDX_CHUNK + (b & (per - 1)) * tm

    def weight_copies(e, p):
        return (pltpu.make_async_copy(wg_hbm.at[e], wg_f.at[p], sem_w.at[p]),
                pltpu.make_async_copy(wu_hbm.at[e], wu_f.at[p], sem_w.at[p]),
                pltpu.make_async_copy(wd_hbm.at[e], wd_f.at[p], sem_w.at[p]))

    def for_rows(inline, body):
        if inline:
            for r in range(tm):
                body(r, r % 2)
        else:
            def pair(c, carry):
                body(2 * c, 0)
                body(2 * c + 1, 1)
                return carry
            lax.fori_loop(0, tm // 2, pair, 0)

    def issue_gather(b, to_slot, inline):
        ibase = idx_base(b)
        xbase = to_slot * buf_rows

        def one(r, prio):
            tok = idx_s[ibase + r] & (n_tok - 1)
            pltpu.make_async_copy(_hbm_slab(h_hbm, tok), staged(xbuf, xbase, r),
                                  sem_g.at[to_slot]).start(priority=0)
        for_rows(inline, one)

    def issue_scatter(b, from_slot, to_dump, inline):
        ibase = idx_base(b)
        ybase = from_slot * buf_rows

        def one(r, prio):
            tgt = jnp.where(to_dump, dump_row0 + tm + r, idx_s[ibase + r])
            pltpu.make_async_copy(staged(ybuf, ybase, r), _hbm_slab(ys_hbm, tgt),
                                  sem_s.at[from_slot]).start(priority=1)
        for_rows(inline, one)

    def wait_gather(at_slot):
        v = xbuf.at[pl.ds(pl.multiple_of(at_slot * buf_rows, SUBLANES), buf_rows), :]
        pltpu.make_async_copy(h_hbm.at[pl.ds(0, buf_rows), :], v, sem_g.at[at_slot]).wait()

    def wait_scatter(at_slot):
        v = ybuf.at[pl.ds(pl.multiple_of(at_slot * buf_rows, SUBLANES), buf_rows), :]
        pltpu.make_async_copy(v, ys_hbm.at[pl.ds(0, buf_rows), :], sem_s.at[at_slot]).wait()

    @pl.when(s == 0)
    def _():
        ybuf[...] = jnp.zeros(ybuf.shape, U32)
        first = idx_copy(0)
        first.start()
        first.wait()
        init = pltpu.make_async_copy(ybuf, ys_hbm.at[pl.ds(pl.multiple_of(dump_row0 * slab, slab), 2 * buf_rows), :],
                                     sem_i)
        init.start()
        init.wait()
        issue_gather(0, 0, inline=False)
        for cp in weight_copies(be_ref[0], 0):
            cp.start()

    @pl.when(s < n_used)
    def _():
        @pl.when((s & (per - 1)) == 1)
        def _():
            idx_copy((s >> per_log2) + 1).start()

        @pl.when(((s + 1) & (per - 1)) == 0)
        def _():
            idx_copy((s + 1) >> per_log2).wait()

        @pl.when(new_ref[s] == 1)
        def _():
            p = epar_ref[s]

            @pl.when(nexte_ref[s] >= 0)
            def _():
                for cp in weight_copies(nexte_ref[s], 1 - p):
                    cp.start(priority=1)

            for cp in weight_copies(be_ref[s], p):
                cp.wait()
            wg_s[...] = wg_f[p].astype(BF16)
            wu_s[...] = wu_f[p].astype(BF16)
            wd_s[...] = wd_f[p].astype(BF16)

        wait_gather(slot)

        @pl.when(s >= 1)
        def _():
            wait_scatter(slot)

        issue_gather(s + 1, 1 - slot, inline=True)
        issue_scatter(jnp.maximum(s - 1, 0), 1 - slot, s == 0, inline=True)
        base = slot * buf_rows
        x_lo, x_hi = _unpack_pairs(_slab_rows_to_matrix(xbuf, base, tm, SLAB_PITCH))
        x = jnp.concatenate([x_lo.astype(BF16), x_hi.astype(BF16)], axis=-1)
        gate = jnp.dot(x, wg_s[...], preferred_element_type=F32)
        up = jnp.dot(x, wu_s[...], preferred_element_type=F32)
        hm = tm // 2
        act_a = (_silu(gate[:hm]) * up[:hm]).astype(BF16)
        act_b = (_silu(gate[hm:]) * up[hm:]).astype(BF16)
        y = jnp.concatenate([jnp.dot(act_a, wd_s[...], preferred_element_type=F32),
                             jnp.dot(act_b, wd_s[...], preferred_element_type=F32)], axis=0)
        half = slab * LANES
        words = _pack_pairs(y[:, :half], y[:, half:])
        for j in range(slab):
            ybuf[pl.ds(base + j, tm, stride=SLAB_PITCH), :] = words[:, j * LANES:(j + 1) * LANES]

    @pl.when(s == n_used)
    def _():
        last = (s - 1) & (per - 1)

        @pl.when((last == 1) | (last == 2))
        def _():
            idx_copy(((s - 1) >> per_log2) + 1).wait()

        wait_gather(slot)
        wait_scatter(slot)
        issue_scatter(s - 1, 1 - slot, False, inline=False)
        wait_scatter(1 - slot)


def _moe(block_e, block_new, block_nexte, block_epar, n_used, row_tgt, h2d, w_gate, w_up, w_down):
    n = h2d.shape[0] // TOKEN_ROWS
    assert n & (n - 1) == 0
    d = 2 * TOKEN_ROWS * LANES
    de = w_gate.shape[2]
    tm = MOE_TM
    n_blk = block_e.shape[0]
    grid_spec = pltpu.PrefetchScalarGridSpec(
        num_scalar_prefetch=5,
        grid=(n_blk,),
        in_specs=[pl.BlockSpec(memory_space=pl.ANY)] * 5,
        out_specs=pl.BlockSpec(memory_space=pl.ANY),
        scratch_shapes=[pltpu.SMEM((2 * IDX_CHUNK,), jnp.int32),
                        pltpu.VMEM((2 * tm * SLAB_PITCH, LANES), U32),
                        pltpu.VMEM((2 * tm * SLAB_PITCH, LANES), U32),
                        pltpu.VMEM((2, d, de), F32),
                        pltpu.VMEM((2, d, de), F32),
                        pltpu.VMEM((2, de, d), F32),
                        pltpu.VMEM((d, de), BF16),
                        pltpu.VMEM((d, de), BF16),
                        pltpu.VMEM((de, d), BF16),
                        pltpu.SemaphoreType.DMA,
                        pltpu.SemaphoreType.DMA((2,)),
                        pltpu.SemaphoreType.DMA((2,)),
                        pltpu.SemaphoreType.DMA((2,))])
    return pl.pallas_call(
        functools.partial(_moe_kernel, n_tok=n),
        grid_spec=grid_spec,
        out_shape=jax.ShapeDtypeStruct(((TOP_K * n + 2 * tm) * TOKEN_ROWS, LANES), U32),
        compiler_params=_cparams(("arbitrary",)),
        name="moe",
    )(block_e, block_new, block_nexte, block_epar, n_used, row_tgt, h2d, w_gate, w_up, w_down)


def _final_kernel(gw_hbm, *refs):
    ys_refs = refs[:TOP_K]
    h_ref, x1_ref, mod_ref, wg_ref, wu_ref, wd_ref, g_ref, b_ref, o_ref, gw_s, acc_ref, sem_i = refs[TOP_K:]
    i = pl.program_id(0)
    tm = x1_ref.shape[0]
    per_step = tm * TOP_K
    slot = i & 1

    def gw_copy(step):
        half = pl.ds(pl.multiple_of((step & 1) * per_step, per_step), per_step)
        return pltpu.make_async_copy(gw_hbm.at[pl.ds(pl.multiple_of(step * per_step, per_step), per_step)],
                                     gw_s.at[half], sem_i.at[step & 1])

    @pl.when(i == 0)
    def _():
        gw_copy(0).start()

    @pl.when(i + 1 < pl.num_programs(0))
    def _():
        gw_copy(i + 1).start()

    gw_copy(i).wait()
    gbase = slot * per_step
    hi_base = tm * SLAB_PITCH

    def combine(t, carry):
        rows = pl.ds(pl.multiple_of(t * TOKEN_ROWS, TOKEN_ROWS), TOKEN_ROWS)
        acc_lo = acc_hi = None
        for k in range(TOP_K):
            lo, hi = _unpack_pairs(ys_refs[k][rows, :])
            g = gw_s[gbase + t * TOP_K + k]
            acc_lo = g * lo if acc_lo is None else acc_lo + g * lo
            acc_hi = g * hi if acc_hi is None else acc_hi + g * hi
        acc_ref[pl.ds(pl.multiple_of(t * SLAB_PITCH, SUBLANES), TOKEN_ROWS), :] = acc_lo
        acc_ref[pl.ds(pl.multiple_of(hi_base + t * SLAB_PITCH, SUBLANES), TOKEN_ROWS), :] = acc_hi
        return carry

    lax.fori_loop(0, tm, combine, 0, unroll=32)
    moe = jnp.concatenate([_slab_rows_to_matrix(acc_ref, 0, tm, SLAB_PITCH),
                           _slab_rows_to_matrix(acc_ref, hi_base, tm, SLAB_PITCH)], axis=-1)
    h_lo, h_hi = _unpack_pairs(_slab_rows_to_matrix(h_ref, 0, tm, TOKEN_ROWS))
    h = jnp.concatenate([h_lo.astype(BF16), h_hi.astype(BF16)], axis=-1)
    gate = jnp.dot(h, wg_ref[...], preferred_element_type=F32)
    up = jnp.dot(h, wu_ref[...], preferred_element_type=F32)
    shared = jnp.dot((_silu(gate) * up).astype(BF16), wd_ref[...], preferred_element_type=F32)
    m = mod_ref[0]
    y = ALPHA * x1_ref[...] + (1.0 + m[5:6]) * (moe + shared)
    o_ref[...] = _layer_norm(y, g_ref[...], b_ref[...])


def _final(gate_w, ys, h2d, x1, mod3, wsg, wsu, wsd, ln_g, ln_b, seq):
    n, d = x1.shape
    de = wsg.shape[1]
    tm = FINAL_TM
    per_step = tm * TOP_K
    assert per_step % IDX_CHUNK == 0
    row = lambda i: (i, 0)
    const = lambda i: (0, 0)
    slot_rows = lambda k: (lambda i: (k * (n // tm) + i, 0))
    return pl.pallas_call(
        _final_kernel,
        grid=(n // tm,),
        in_specs=[pl.BlockSpec(memory_space=pl.ANY)]
                 + [pl.BlockSpec((tm * TOKEN_ROWS, LANES), slot_rows(k)) for k in range(TOP_K)]
                 + [pl.BlockSpec((tm * TOKEN_ROWS, LANES), row),
                  pl.BlockSpec((tm, d), row),
                  pl.BlockSpec((1, 6, d), lambda i: (i * tm // seq, 0, 0)),
                  pl.BlockSpec((d, de), const),
                  pl.BlockSpec((d, de), const),
                  pl.BlockSpec((de, d), const),
                  pl.BlockSpec((1, d), const),
                  pl.BlockSpec((1, d), const)],
        out_specs=pl.BlockSpec((tm, d), row),
        out_shape=jax.ShapeDtypeStruct((n, d), F32),
        scratch_shapes=[pltpu.SMEM((2 * per_step,), F32),
                        pltpu.VMEM((2 * tm * SLAB_PITCH, LANES), F32),
                        pltpu.SemaphoreType.DMA((2,))],
        compiler_params=_cparams(("arbitrary",)),
        name="final",
    )(gate_w, *([ys] * TOP_K), h2d, x1, mod3, wsg, wsu, wsd, ln_g, ln_b)


def _dispatch_tables(eidx, tile_counts, n):
    tm = MOE_TM
    a = n * TOP_K
    i32 = jnp.int32
    experts = jnp.arange(N_EXPERTS, dtype=i32)
    counts = jnp.sum(tile_counts, axis=(0, 2)).astype(i32)
    padded = (counts + tm - 1) // tm * tm
    pad_end = jnp.cumsum(padded)
    starts = pad_end - padded
    n_blk = a // tm + N_EXPERTS + 1
    blk_start = jnp.arange(n_blk, dtype=i32) * tm
    n_used = pad_end[-1] // tm
    in_use = jnp.arange(n_blk) < n_used
    raw_e = jnp.minimum(jnp.sum((pad_end[None, :] <= blk_start[:, None]).astype(i32), axis=1), N_EXPERTS - 1)
    last_e = jnp.sum(jnp.where(jnp.arange(n_blk) == n_used - 1, raw_e, 0))
    block_e = jnp.where(in_use, raw_e, last_e)
    onehot = block_e[:, None] == experts[None, :]
    block_new = jnp.concatenate([jnp.ones((1,), i32), (block_e[1:] != block_e[:-1]).astype(i32)])
    has_rows = counts > 0
    later = (experts[None, :] > experts[:, None]) & has_rows[None, :]
    next_e = jnp.min(jnp.where(later, experts[None, :], N_EXPERTS), axis=1)
    next_e = jnp.where(next_e == N_EXPERTS, -1, next_e)
    parity_e = (jnp.cumsum(has_rows.astype(i32)) - 1) & 1
    block_nexte = jnp.sum(jnp.where(onehot, next_e[None, :], 0), axis=1).astype(i32)
    block_epar = jnp.sum(jnp.where(onehot, parity_e[None, :], 0), axis=1).astype(i32)
    cnt_b = jnp.sum(jnp.where(onehot, counts[None, :], 0), axis=1)
    start_b = jnp.sum(jnp.where(onehot, starts[None, :], 0), axis=1)
    block_nv = jnp.where(in_use, jnp.clip(cnt_b - (blk_start - start_b), 0, tm), 0)
    dummy_keys = jnp.where(jnp.arange(tm, dtype=i32)[None, :] < (padded - counts)[:, None],
                           experts[:, None], N_EXPERTS).reshape(-1)
    keys = jnp.concatenate([eidx.reshape(-1), dummy_keys, jnp.full((tm,), N_EXPERTS, i32)])
    tgt = jnp.arange(a, dtype=i32)
    bits = (a - 1).bit_length() + 1
    low = (1 << bits) - 1
    payload = jnp.concatenate([tgt, jnp.full((n_blk * tm - a,), low, i32)])
    row_tgt = lax.sort(keys * (1 << bits) + payload) & low
    r = jnp.arange(tm, dtype=i32)[None, :]
    dump = a + (jnp.arange(n_blk, dtype=i32)[:, None] & 1) * tm + r
    row_tgt = jnp.where(r >= block_nv[:, None], dump, row_tgt.reshape(n_blk, tm)).reshape(-1)
    row_tgt = jnp.pad(row_tgt, (0, -(n_blk * tm) % IDX_CHUNK))
    return block_e.astype(i32), block_new, block_nexte, block_epar, n_used.astype(i32).reshape(1), row_tgt


def kernel(x, c, w_mod, b_mod, w_in, conv_w, attn_sinks, w_out, ln1_g, ln1_b, w_router, router_bias,
           w_gate, w_up, w_down, ws_gate, ws_up, ws_down, ln2_g, ln2_b):
    b, s, d = x.shape
    n = b * s
    attn_w = N_Q_HEADS * HEAD_DIM
    kv_w = N_KV_HEADS * HEAD_DIM
    conv_wd = d - attn_w
    in_w = attn_w + 2 * kv_w + 3 * conv_wd
    x2 = x.reshape(n, d)
    c8 = jnp.zeros((SUBLANES, d), F32).at[:b].set(c)
    for l in range(DEPTH):
        mod = _mod(c8, w_mod[l], b_mod[l].reshape(1, -1))[:b]
        mod3 = mod.reshape(b, 6, d)
        proj = _inproj(x2, mod3, w_in[l].astype(BF16), s)
        mix = _mixer(proj, attn_sinks[l].reshape(1, -1), conv_w[l], b, s, attn_w, kv_w, conv_wd)
        x1, h2d, logits = _outproj(mix, x2, mod3, w_out[l].astype(BF16), ln1_g[l].reshape(1, -1),
                                   ln1_b[l].reshape(1, -1), w_router[l], s)
        eidx, gate_w, tile_counts = _route(logits, router_bias[l].reshape(-1, 1))
        block_e, block_new, block_nexte, block_epar, n_used, row_tgt = _dispatch_tables(eidx, tile_counts, n)
        ys = _moe(block_e, block_new, block_nexte, block_epar, n_used, row_tgt, h2d,
                  w_gate[l], w_up[l], w_down[l])
        x2 = _final(gate_w.T.reshape(-1), ys, h2d, x1, mod3,
                    ws_gate[l].astype(BF16), ws_up[l].astype(BF16), ws_down[l].astype(BF16),
                    ln2_g[l].reshape(1, -1), ln2_b[l].reshape(1, -1), s)
    return x2.reshape(b, s, d)
```

```python
import functools

import jax
import jax.numpy as jnp
from jax import lax
from jax.experimental import pallas as pl
from jax.experimental.pallas import tpu as pltpu

HEAD_DIM = 64
N_Q_HEADS = 16
N_KV_HEADS = 4
GQA = N_Q_HEADS // N_KV_HEADS
CONV_K = 3
WINDOW = 128
Q_BLOCK = 128
N_EXPERTS = 64
TOP_K = 8
N_GROUPS = 8
GROUP_SIZE = N_EXPERTS // N_GROUPS
TOPK_GROUPS = 4
ROUTED_SCALE = 2.5
DEPTH = 1
ALPHA = (2.0 * DEPTH) ** 0.25
LN_EPS = 1e-5

LANES = 128
SUBLANES = 8
TOKEN_ROWS = 8
SLAB_PITCH = 8
VMEM_LIMIT = 56 * 1024 * 1024

MOD_TN = 2048
INPROJ_TM = 1024
INPROJ_TN = 2304
OUTPROJ_TM = 256
ROUTE_TM = 1024
MOE_TM = 256
FINAL_TM = 256
IDX_CHUNK = 1024
ISSUE_UNROLL = 8

F32 = jnp.float32
BF16 = jnp.bfloat16


def _cparams(sem):
    return pltpu.CompilerParams(dimension_semantics=sem, vmem_limit_bytes=VMEM_LIMIT)


def _silu(v):
    return v * jax.nn.sigmoid(v)


U32 = jnp.uint32
HI_MASK = 0xFFFF0000


def _pack_pairs(lo, hi):
    lo_bits = lax.bitcast_convert_type(lo.astype(BF16).astype(F32), U32) >> 16
    hi_bits = lax.bitcast_convert_type(hi.astype(BF16).astype(F32), U32) & U32(HI_MASK)
    return lo_bits | hi_bits


def _unpack_pairs(w):
    return (lax.bitcast_convert_type(w << 16, F32), lax.bitcast_convert_type(w & U32(HI_MASK), F32))


def _slab_rows_to_matrix(ref, base, tm, pitch):
    return jnp.concatenate([ref[pl.ds(base + s, tm, stride=pitch), :] for s in range(TOKEN_ROWS)], axis=-1)


def _layer_norm(y, g, b):
    mu = jnp.mean(y, axis=-1, keepdims=True)
    yc = y - mu
    var = jnp.mean(yc * yc, axis=-1, keepdims=True)
    return yc * lax.rsqrt(var + LN_EPS) * g + b


def _mod_kernel(c_ref, w_ref, b_ref, o_ref):
    cs = _silu(c_ref[...]).astype(BF16)
    o_ref[...] = jnp.dot(cs, w_ref[...].astype(BF16), preferred_element_type=F32) + b_ref[...]


def _mod(c8, w_mod, b_mod):
    d, n = w_mod.shape
    return pl.pallas_call(
        _mod_kernel,
        grid=(n // MOD_TN,),
        in_specs=[pl.BlockSpec((SUBLANES, d), lambda j: (0, 0)),
                  pl.BlockSpec((d, MOD_TN), lambda j: (0, j)),
                  pl.BlockSpec((1, MOD_TN), lambda j: (0, j))],
        out_specs=pl.BlockSpec((SUBLANES, MOD_TN), lambda j: (0, j)),
        out_shape=jax.ShapeDtypeStruct((SUBLANES, n), F32),
        compiler_params=_cparams(("arbitrary",)),
        name="mod",
    )(c8, w_mod, b_mod)


def _inproj_kernel(x_ref, mod_ref, w_ref, o_ref, h_ref):
    j = pl.program_id(1)

    @pl.when(j == 0)
    def _():
        m = mod_ref[0]
        h_ref[...] = (x_ref[...] * (1.0 + m[1:2]) + m[0:1]).astype(BF16)

    o_ref[...] = jnp.dot(h_ref[...], w_ref[...], preferred_element_type=F32).astype(BF16)


def _inproj(x2, mod3, w_in_bf, seq):
    n, d = x2.shape
    in_w = w_in_bf.shape[1]
    tm, tn = INPROJ_TM, INPROJ_TN
    return pl.pallas_call(
        _inproj_kernel,
        grid=(n // tm, in_w // tn),
        in_specs=[pl.BlockSpec((tm, d), lambda i, j: (i, 0)),
                  pl.BlockSpec((1, 6, d), lambda i, j: (i * tm // seq, 0, 0)),
                  pl.BlockSpec((d, tn), lambda i, j: (0, j))],
        out_specs=pl.BlockSpec((tm, tn), lambda i, j: (i, j)),
        out_shape=jax.ShapeDtypeStruct((n, in_w), BF16),
        scratch_shapes=[pltpu.VMEM((tm, d), BF16)],
        compiler_params=_cparams(("arbitrary", "arbitrary")),
        name="inproj",
    )(x2, mod3, w_in_bf)


def _mixer_kernel(cur_ref, pk_ref, pv_ref, prow_ref, sink_ref, cw_ref, o_ref, *, attn_w, kv_w, conv_w):
    nblk = pl.program_id(1)
    has_prev = nblk > 0
    qb = Q_BLOCK
    cur = cur_ref[...]
    k_cur = cur[:, attn_w:attn_w + kv_w]
    v_cur = cur[:, attn_w + kv_w:attn_w + 2 * kv_w]
    k_all = jnp.concatenate([pk_ref[...], k_cur], axis=0)
    v_all = jnp.concatenate([pv_ref[...], v_cur], axis=0)

    rows = GQA * qb
    qi = lax.broadcasted_iota(jnp.int32, (rows, 2 * qb), 0) % qb
    kj = lax.broadcasted_iota(jnp.int32, (rows, 2 * qb), 1)
    dist = qi + qb - kj
    kmin = jnp.where(has_prev, 0, qb)
    valid = (dist >= 0) & (dist < WINDOW) & (kj >= kmin)
    distf = dist.astype(F32)
    head_in_group = lax.broadcasted_iota(jnp.int32, (rows, 1), 0) // qb
    sinks = sink_ref[...]

    outs = []
    for g in range(N_KV_HEADS):
        q4 = jnp.concatenate(
            [cur[:, (g * GQA + j) * HEAD_DIM:(g * GQA + j + 1) * HEAD_DIM] for j in range(GQA)], axis=0)
        kg = k_all[:, g * HEAD_DIM:(g + 1) * HEAD_DIM]
        vg = v_all[:, g * HEAD_DIM:(g + 1) * HEAD_DIM]
        s = lax.dot_general(q4, kg, (((1,), (1,)), ((), ())), preferred_element_type=F32)
        s = s * (HEAD_DIM ** -0.5)
        slope = jnp.zeros((rows, 1), F32)
        sink = jnp.zeros((rows, 1), F32)
        for j in range(GQA):
            h = g * GQA + j
            sel = head_in_group == j
            slope = jnp.where(sel, 2.0 ** (-8.0 * (h + 1) / N_Q_HEADS), slope)
            sink = jnp.where(sel, sinks[:, h:h + 1], sink)
        s = jnp.where(valid, s - slope * distf, -jnp.inf)
        m = jnp.maximum(jnp.max(s, axis=-1, keepdims=True), sink)
        p = jnp.exp(s - m)
        denom = jnp.sum(p, axis=-1, keepdims=True) + jnp.exp(sink - m)
        o4 = jnp.dot(p.astype(BF16), vg, preferred_element_type=F32) / denom
        outs.extend(o4[j * qb:(j + 1) * qb] for j in range(GQA))
    attn = jnp.concatenate(outs, axis=-1)

    c0 = attn_w + 2 * kv_w
    cb = cur[:, c0:c0 + conv_w].astype(F32)
    u = cur[:, c0 + conv_w:c0 + 2 * conv_w].astype(F32) * cur[:, c0 + 2 * conv_w:c0 + 3 * conv_w].astype(F32)
    prow = prow_ref[...]
    up = prow[:, c0 + conv_w:c0 + 2 * conv_w].astype(F32) * prow[:, c0 + 2 * conv_w:c0 + 3 * conv_w].astype(F32)
    up = up * jnp.where(has_prev, 1.0, 0.0)
    pm1 = up[15:16]
    pm2 = up[14:15]
    ri = lax.broadcasted_iota(jnp.int32, u.shape, 0)
    u1 = jnp.where(ri == 0, pm1, pltpu.roll(u, 1, 0))
    u2 = jnp.where(ri == 0, pm2, jnp.where(ri == 1, pm1, pltpu.roll(u, 2, 0)))
    cw = cw_ref[...]
    conv = cb * (cw[0:1] * u2 + cw[1:2] * u1 + cw[2:3] * u)
    o_ref[...] = jnp.concatenate([attn, conv], axis=-1).astype(BF16)


def _mixer(proj, sinks2, conv_w, batch, seq, attn_w, kv_w, conv_wd):
    n, in_w = proj.shape
    nb = seq // Q_BLOCK
    kv_blk0 = attn_w // kv_w
    sub16 = Q_BLOCK // 16

    def cur_map(b, i):
        return (b * nb + i, 0)

    def prev_map(col):
        return lambda b, i: (b * nb + jnp.maximum(i - 1, 0), col)

    def prow_map(b, i):
        return (jnp.maximum((b * nb + i) * sub16 - 1, 0), 0)

    kern = functools.partial(_mixer_kernel, attn_w=attn_w, kv_w=kv_w, conv_w=conv_wd)
    return pl.pallas_call(
        kern,
        grid=(batch, nb),
        in_specs=[pl.BlockSpec((Q_BLOCK, in_w), cur_map),
                  pl.BlockSpec((Q_BLOCK, kv_w), prev_map(kv_blk0)),
                  pl.BlockSpec((Q_BLOCK, kv_w), prev_map(kv_blk0 + 1)),
                  pl.BlockSpec((16, in_w), prow_map),
                  pl.BlockSpec((1, N_Q_HEADS), lambda b, i: (0, 0)),
                  pl.BlockSpec((CONV_K, conv_wd), lambda b, i: (0, 0))],
        out_specs=pl.BlockSpec((Q_BLOCK, attn_w + conv_wd), cur_map),
        out_shape=jax.ShapeDtypeStruct((n, attn_w + conv_wd), BF16),
        compiler_params=_cparams(("arbitrary", "arbitrary")),
        name="mixer",
    )(proj, proj, proj, proj, sinks2, conv_w)


def _split_bf16(v):
    hi = v.astype(BF16)
    lo = (v - hi.astype(F32)).astype(BF16)
    return hi, lo


def _outproj_kernel(mix_ref, x_ref, mod_ref, w_ref, g_ref, b_ref, wr_ref, x1_ref, h2_ref, lg_ref):
    m = mod_ref[0]
    mix = jnp.dot(mix_ref[...], w_ref[...], preferred_element_type=F32)
    x1 = _layer_norm(ALPHA * x_ref[...] + (1.0 + m[2:3]) * mix, g_ref[...], b_ref[...])
    x1_ref[...] = x1
    h2 = x1 * (1.0 + m[4:5]) + m[3:4]
    tm, d = h2.shape
    words = _pack_pairs(h2[:, :d // 2], h2[:, d // 2:])
    for s in range(TOKEN_ROWS):
        h2_ref[pl.ds(s, tm, stride=TOKEN_ROWS), :] = words[:, s * LANES:(s + 1) * LANES]
    h_hi, h_lo = _split_bf16(h2)
    w_hi, w_lo = _split_bf16(wr_ref[...])
    ne = w_hi.shape[1]
    both = jnp.dot(h_hi, jnp.concatenate([w_hi, w_lo], axis=1), preferred_element_type=F32)
    lg_ref[...] = both[:, :ne] + (both[:, ne:] + jnp.dot(h_lo, w_hi, preferred_element_type=F32))


def _outproj(mix, x2, mod3, w_out_bf, ln_g, ln_b, w_router, seq):
    n, d = x2.shape
    tm = OUTPROJ_TM
    ne = w_router.shape[1]
    row = lambda i: (i, 0)
    const = lambda i: (0, 0)
    return pl.pallas_call(
        _outproj_kernel,
        grid=(n // tm,),
        in_specs=[pl.BlockSpec((tm, d), row),
                  pl.BlockSpec((tm, d), row),
                  pl.BlockSpec((1, 6, d), lambda i: (i * tm // seq, 0, 0)),
                  pl.BlockSpec((d, d), const),
                  pl.BlockSpec((1, d), const),
                  pl.BlockSpec((1, d), const),
                  pl.BlockSpec((d, ne), const)],
        out_specs=[pl.BlockSpec((tm, d), row),
                   pl.BlockSpec((tm * TOKEN_ROWS, LANES), row),
                   pl.BlockSpec((tm, ne), row)],
        out_shape=[jax.ShapeDtypeStruct((n, d), F32),
                   jax.ShapeDtypeStruct((n * TOKEN_ROWS, LANES), U32),
                   jax.ShapeDtypeStruct((n, ne), F32)],
        compiler_params=_cparams(("arbitrary",)),
        name="outproj",
    )(mix, x2, mod3, w_out_bf, ln_g, ln_b, w_router)


def _col_argmax(tiles, row_f):
    m = tiles[0]
    for t in tiles[1:]:
        m = jnp.maximum(m, t)
    m = jnp.max(m, axis=0, keepdims=True)
    idx = None
    for t, r in zip(tiles, row_f):
        c = jnp.where(t == m, r, float(N_EXPERTS))
        idx = c if idx is None else jnp.minimum(idx, c)
    return m, jnp.min(idx, axis=0, keepdims=True)


def _route_kernel(lg_ref, bias_ref, eidx_ref, w_ref, cnt_ref):
    lt = lg_ref[...].T
    tm = lt.shape[1]
    neg = -jnp.inf
    sub = lax.broadcasted_iota(jnp.int32, (GROUP_SIZE, tm), 0).astype(F32)
    row_f = [sub + float(g * GROUP_SIZE) for g in range(N_GROUPS)]
    scores = [jax.nn.sigmoid(lt[g * GROUP_SIZE:(g + 1) * GROUP_SIZE]) for g in range(N_GROUPS)]
    sel = [scores[g] + bias_ref[g * GROUP_SIZE:(g + 1) * GROUP_SIZE, :] for g in range(N_GROUPS)]
    gs = []
    for g in range(N_GROUPS):
        m1, i1 = _col_argmax([sel[g]], [row_f[g]])
        m2 = jnp.max(jnp.where(row_f[g] == i1, neg, sel[g]), axis=0, keepdims=True)
        gs.append(m1 + m2)
    cand = []
    for g in range(N_GROUPS):
        rank = jnp.zeros((1, tm), F32)
        for o in range(N_GROUPS):
            if o == g:
                continue
            ahead = (gs[o] >= gs[g]) if o < g else (gs[o] > gs[g])
            rank = rank + jnp.where(ahead, 1.0, 0.0)
        cand.append(jnp.where(rank < TOPK_GROUPS, sel[g], neg))
    idxs, ws = [], []
    chosen = [jnp.zeros((GROUP_SIZE, tm), F32) for _ in range(N_GROUPS)]
    for _ in range(TOP_K):
        _, ik = _col_argmax(cand, row_f)
        wk = jnp.zeros((1, tm), F32)
        for g in range(N_GROUPS):
            hit = row_f[g] == ik
            wk = wk + jnp.sum(jnp.where(hit, scores[g], 0.0), axis=0, keepdims=True)
            cand[g] = jnp.where(hit, neg, cand[g])
            chosen[g] = jnp.where(hit, 1.0, chosen[g])
        ws.append(wk)
        idxs.append(ik)
    for g in range(N_GROUPS):
        cnt_ref[0, g * GROUP_SIZE:(g + 1) * GROUP_SIZE, :] = jnp.sum(chosen[g], axis=1, keepdims=True)
    wsum = ws[0]
    for k in range(1, TOP_K):
        wsum = wsum + ws[k]
    eidx_ref[...] = jnp.concatenate(idxs, axis=0).astype(jnp.int32)
    w_ref[...] = jnp.concatenate([wk / wsum * ROUTED_SCALE for wk in ws], axis=0)


def _route(logits, bias_col):
    n, ne = logits.shape
    tm = ROUTE_TM
    col = lambda i: (0, i)
    return pl.pallas_call(
        _route_kernel,
        grid=(n // tm,),
        in_specs=[pl.BlockSpec((tm, ne), lambda i: (i, 0)), pl.BlockSpec((ne, 1), lambda i: (0, 0))],
        out_specs=[pl.BlockSpec((TOP_K, tm), col), pl.BlockSpec((TOP_K, tm), col),
                   pl.BlockSpec((1, ne, 1), lambda i: (i, 0, 0))],
        out_shape=[jax.ShapeDtypeStruct((TOP_K, n), jnp.int32), jax.ShapeDtypeStruct((TOP_K, n), F32),
                   jax.ShapeDtypeStruct((n // tm, ne, 1), F32)],
        compiler_params=_cparams(("arbitrary",)),
        name="route",
    )(logits, bias_col)


def _issue_rows(lo, hi, issue_one):
    n_full = (hi - lo) // ISSUE_UNROLL

    def chunk(c, carry):
        for u in range(ISSUE_UNROLL):
            issue_one(lo + c * ISSUE_UNROLL + u)
        return carry

    def tail(r, carry):
        issue_one(r)
        return carry

    lax.fori_loop(0, n_full, chunk, 0)
    lax.fori_loop(lo + n_full * ISSUE_UNROLL, hi, tail, 0)


def _hbm_slab(ref, row):
    return ref.at[pl.ds(pl.multiple_of(row * TOKEN_ROWS, TOKEN_ROWS), TOKEN_ROWS), :]


def _moe_kernel(be_ref, new_ref, nexte_ref, epar_ref, nused_ref, tgt_hbm, h_hbm, wg_hbm, wu_hbm, wd_hbm, ys_hbm,
                idx_s, xbuf, ybuf, wg_f, wu_f, wd_f, wg_s, wu_s, wd_s, sem_i, sem_g, sem_s, sem_w, *, n_tok):
    s = pl.program_id(0)
    tm = MOE_TM
    per = IDX_CHUNK // tm
    per_log2 = per.bit_length() - 1
    n_used = nused_ref[0]
    slab = TOKEN_ROWS
    buf_rows = tm * SLAB_PITCH
    dump_row0 = TOP_K * n_tok
    slot = s & 1

    def staged(buf, base, r):
        return buf.at[pl.ds(pl.multiple_of(base + r * SLAB_PITCH, SUBLANES), slab), :]

    def idx_copy(c):
        return pltpu.make_async_copy(
            tgt_hbm.at[pl.ds(pl.multiple_of(c * IDX_CHUNK, IDX_CHUNK), IDX_CHUNK)],
            idx_s.at[pl.ds(pl.multiple_of((c & 1) * IDX_CHUNK, IDX_CHUNK), IDX_CHUNK)], sem_i)

    def idx_base(b):
        return ((b >> per_log2) & 1) * IDX_CHUNK + (b & (per - 1)) * tm

    def weight_copies(e, p):
        return (pltpu.make_async_copy(wg_hbm.at[e], wg_f.at[p], sem_w.at[p]),
                pltpu.make_async_copy(wu_hbm.at[e], wu_f.at[p], sem_w.at[p]),
                pltpu.make_async_copy(wd_hbm.at[e], wd_f.at[p], sem_w.at[p]))

    def for_rows(inline, body):
        if inline:
            for r in range(tm):
                body(r, r % 2)
        else:
            def pair(c, carry):
                body(2 * c, 0)
                body(2 * c + 1, 1)
                return carry
            lax.fori_loop(0, tm // 2, pair, 0)

    def issue_gather(b, to_slot, inline):
        ibase = idx_base(b)
        xbase = to_slot * buf_rows

        def one(r, prio):
            tok = idx_s[ibase + r] & (n_tok - 1)
            pltpu.make_async_copy(_hbm_slab(h_hbm, tok), staged(xbuf, xbase, r),
                                  sem_g.at[to_slot]).start(priority=0)
        for_rows(inline, one)

    def issue_scatter(b, from_slot, to_dump, inline):
        ibase = idx_base(b)
        ybase = from_slot * buf_rows

        def one(r, prio):
            tgt = jnp.where(to_dump, dump_row0 + tm + r, idx_s[ibase + r])
            pltpu.make_async_copy(staged(ybuf, ybase, r), _hbm_slab(ys_hbm, tgt),
                                  sem_s.at[from_slot]).start(priority=1)
        for_rows(inline, one)

    def wait_gather(at_slot):
        v = xbuf.at[pl.ds(pl.multiple_of(at_slot * buf_rows, SUBLANES), buf_rows), :]
        pltpu.make_async_copy(h_hbm.at[pl.ds(0, buf_rows), :], v, sem_g.at[at_slot]).wait()

    def wait_scatter(at_slot):
        v = ybuf.at[pl.ds(pl.multiple_of(at_slot * buf_rows, SUBLANES), buf_rows), :]
        pltpu.make_async_copy(v, ys_hbm.at[pl.ds(0, buf_rows), :], sem_s.at[at_slot]).wait()

    @pl.when(s == 0)
    def _():
        ybuf[...] = jnp.zeros(ybuf.shape, U32)
        first = idx_copy(0)
        first.start()
        first.wait()
        init = pltpu.make_async_copy(ybuf, ys_hbm.at[pl.ds(pl.multiple_of(dump_row0 * slab, slab), 2 * buf_rows), :],
                                     sem_i)
        init.start()
        init.wait()
        issue_gather(0, 0, inline=False)
        for cp in weight_copies(be_ref[0], 0):
            cp.start()

    @pl.when(s < n_used)
    def _():
        @pl.when((s & (per - 1)) == 1)
        def _():
            idx_copy((s >> per_log2) + 1).start()

        @pl.when(((s + 1) & (per - 1)) == 0)
        def _():
            idx_copy((s + 1) >> per_log2).wait()

        @pl.when(new_ref[s] == 1)
        def _():
            p = epar_ref[s]

            @pl.when(nexte_ref[s] >= 0)
            def _():
                for cp in weight_copies(nexte_ref[s], 1 - p):
                    cp.start(priority=1)

            for cp in weight_copies(be_ref[s], p):
                cp.wait()
            wg_s[...] = wg_f[p].astype(BF16)
            wu_s[...] = wu_f[p].astype(BF16)
            wd_s[...] = wd_f[p].astype(BF16)

        wait_gather(slot)

        @pl.when(s >= 1)
        def _():
            wait_scatter(slot)

        issue_gather(s + 1, 1 - slot, inline=True)
        issue_scatter(jnp.maximum(s - 1, 0), 1 - slot, s == 0, inline=True)
        base = slot * buf_rows
        x_lo, x_hi = _unpack_pairs(_slab_rows_to_matrix(xbuf, base, tm, SLAB_PITCH))
        x = jnp.concatenate([x_lo.astype(BF16), x_hi.astype(BF16)], axis=-1)
        gate = jnp.dot(x, wg_s[...], preferred_element_type=F32)
        up = jnp.dot(x, wu_s[...], preferred_element_type=F32)
        act = (_silu(gate) * up).astype(BF16)
        y = jnp.dot(act, wd_s[...], preferred_element_type=F32)
        half = slab * LANES
        words = _pack_pairs(y[:, :half], y[:, half:])
        for j in range(slab):
            ybuf[pl.ds(base + j, tm, stride=SLAB_PITCH), :] = words[:, j * LANES:(j + 1) * LANES]

    @pl.when(s == n_used)
    def _():
        last = (s - 1) & (per - 1)

        @pl.when((last == 1) | (last == 2))
        def _():
            idx_copy(((s - 1) >> per_log2) + 1).wait()

        wait_gather(slot)
        wait_scatter(slot)
        issue_scatter(s - 1, 1 - slot, False, inline=False)
        wait_scatter(1 - slot)


def _moe(block_e, block_new, block_nexte, block_epar, n_used, row_tgt, h2d, w_gate, w_up, w_down):
    n = h2d.shape[0] // TOKEN_ROWS
    assert n & (n - 1) == 0
    d = 2 * TOKEN_ROWS * LANES
    de = w_gate.shape[2]
    tm = MOE_TM
    n_blk = block_e.shape[0]
    grid_spec = pltpu.PrefetchScalarGridSpec(
        num_scalar_prefetch=5,
        grid=(n_blk,),
        in_specs=[pl.BlockSpec(memory_space=pl.ANY)] * 5,
        out_specs=pl.BlockSpec(memory_space=pl.ANY),
        scratch_shapes=[pltpu.SMEM((2 * IDX_CHUNK,), jnp.int32),
                        pltpu.VMEM((2 * tm * SLAB_PITCH, LANES), U32),
                        pltpu.VMEM((2 * tm * SLAB_PITCH, LANES), U32),
                        pltpu.VMEM((2, d, de), F32),
                        pltpu.VMEM((2, d, de), F32),
                        pltpu.VMEM((2, de, d), F32),
                        pltpu.VMEM((d, de), BF16),
                        pltpu.VMEM((d, de), BF16),
                        pltpu.VMEM((de, d), BF16),
                        pltpu.SemaphoreType.DMA,
                        pltpu.SemaphoreType.DMA((2,)),
                        pltpu.SemaphoreType.DMA((2,)),
                        pltpu.SemaphoreType.DMA((2,))])
    return pl.pallas_call(
        functools.partial(_moe_kernel, n_tok=n),
        grid_spec=grid_spec,
        out_shape=jax.ShapeDtypeStruct(((TOP_K * n + 2 * tm) * TOKEN_ROWS, LANES), U32),
        compiler_params=_cparams(("arbitrary",)),
        name="moe",
    )(block_e, block_new, block_nexte, block_epar, n_used, row_tgt, h2d, w_gate, w_up, w_down)


def _final_kernel(gw_hbm, *refs):
    ys_refs = refs[:TOP_K]
    h_ref, x1_ref, mod_ref, wg_ref, wu_ref, wd_ref, g_ref, b_ref, o_ref, gw_s, acc_ref, sem_i = refs[TOP_K:]
    i = pl.program_id(0)
    tm = x1_ref.shape[0]
    per_step = tm * TOP_K
    slot = i & 1

    def gw_copy(step):
        half = pl.ds(pl.multiple_of((step & 1) * per_step, per_step), per_step)
        return pltpu.make_async_copy(gw_hbm.at[pl.ds(pl.multiple_of(step * per_step, per_step), per_step)],
                                     gw_s.at[half], sem_i.at[step & 1])

    @pl.when(i == 0)
    def _():
        gw_copy(0).start()

    @pl.when(i + 1 < pl.num_programs(0))
    def _():
        gw_copy(i + 1).start()

    gw_copy(i).wait()
    gbase = slot * per_step
    hi_base = tm * SLAB_PITCH

    def combine(t, carry):
        rows = pl.ds(pl.multiple_of(t * TOKEN_ROWS, TOKEN_ROWS), TOKEN_ROWS)
        acc_lo = acc_hi = None
        for k in range(TOP_K):
            lo, hi = _unpack_pairs(ys_refs[k][rows, :])
            g = gw_s[gbase + t * TOP_K + k]
            acc_lo = g * lo if acc_lo is None else acc_lo + g * lo
            acc_hi = g * hi if acc_hi is None else acc_hi + g * hi
        acc_ref[pl.ds(pl.multiple_of(t * SLAB_PITCH, SUBLANES), TOKEN_ROWS), :] = acc_lo
        acc_ref[pl.ds(pl.multiple_of(hi_base + t * SLAB_PITCH, SUBLANES), TOKEN_ROWS), :] = acc_hi
        return carry

    lax.fori_loop(0, tm, combine, 0, unroll=32)
    moe = jnp.concatenate([_slab_rows_to_matrix(acc_ref, 0, tm, SLAB_PITCH),
                           _slab_rows_to_matrix(acc_ref, hi_base, tm, SLAB_PITCH)], axis=-1)
    h_lo, h_hi = _unpack_pairs(_slab_rows_to_matrix(h_ref, 0, tm, TOKEN_ROWS))
    h = jnp.concatenate([h_lo.astype(BF16), h_hi.astype(BF16)], axis=-1)
    gate = jnp.dot(h, wg_ref[...], preferred_element_type=F32)
    up = jnp.dot(h, wu_ref[...], preferred_element_type=F32)
    shared = jnp.dot((_silu(gate) * up).astype(BF16), wd_ref[...], preferred_element_type=F32)
    m = mod_ref[0]
    y = ALPHA * x1_ref[...] + (1.0 + m[5:6]) * (moe + shared)
    o_ref[...] = _layer_norm(y, g_ref[...], b_ref[...])


def _final(gate_w, ys, h2d, x1, mod3, wsg, wsu, wsd, ln_g, ln_b, seq):
    n, d = x1.shape
    de = wsg.shape[1]
    tm = FINAL_TM
    per_step = tm * TOP_K
    assert per_step % IDX_CHUNK == 0
    row = lambda i: (i, 0)
    const = lambda i: (0, 0)
    slot_rows = lambda k: (lambda i: (k * (n // tm) + i, 0))
    return pl.pallas_call(
        _final_kernel,
        grid=(n // tm,),
        in_specs=[pl.BlockSpec(memory_space=pl.ANY)]
                 + [pl.BlockSpec((tm * TOKEN_ROWS, LANES), slot_rows(k)) for k in range(TOP_K)]
                 + [pl.BlockSpec((tm * TOKEN_ROWS, LANES), row),
                  pl.BlockSpec((tm, d), row),
                  pl.BlockSpec((1, 6, d), lambda i: (i * tm // seq, 0, 0)),
                  pl.BlockSpec((d, de), const),
                  pl.BlockSpec((d, de), const),
                  pl.BlockSpec((de, d), const),
                  pl.BlockSpec((1, d), const),
                  pl.BlockSpec((1, d), const)],
        out_specs=pl.BlockSpec((tm, d), row),
        out_shape=jax.ShapeDtypeStruct((n, d), F32),
        scratch_shapes=[pltpu.SMEM((2 * per_step,), F32),
                        pltpu.VMEM((2 * tm * SLAB_PITCH, LANES), F32),
                        pltpu.SemaphoreType.DMA((2,))],
        compiler_params=_cparams(("arbitrary",)),
        name="final",
    )(gate_w, *([ys] * TOP_K), h2d, x1, mod3, wsg, wsu, wsd, ln_g, ln_b)


def _dispatch_tables(eidx, tile_counts, n):
    tm = MOE_TM
    a = n * TOP_K
    i32 = jnp.int32
    experts = jnp.arange(N_EXPERTS, dtype=i32)
    counts = jnp.sum(tile_counts, axis=(0, 2)).astype(i32)
    padded = (counts + tm - 1) // tm * tm
    pad_end = jnp.cumsum(padded)
    starts = pad_end - padded
    n_blk = a // tm + N_EXPERTS + 1
    blk_start = jnp.arange(n_blk, dtype=i32) * tm
    n_used = pad_end[-1] // tm
    in_use = jnp.arange(n_blk) < n_used
    raw_e = jnp.minimum(jnp.sum((pad_end[None, :] <= blk_start[:, None]).astype(i32), axis=1), N_EXPERTS - 1)
    last_e = jnp.sum(jnp.where(jnp.arange(n_blk) == n_used - 1, raw_e, 0))
    block_e = jnp.where(in_use, raw_e, last_e)
    onehot = block_e[:, None] == experts[None, :]
    block_new = jnp.concatenate([jnp.ones((1,), i32), (block_e[1:] != block_e[:-1]).astype(i32)])
    has_rows = counts > 0
    later = (experts[None, :] > experts[:, None]) & has_rows[None, :]
    next_e = jnp.min(jnp.where(later, experts[None, :], N_EXPERTS), axis=1)
    next_e = jnp.where(next_e == N_EXPERTS, -1, next_e)
    parity_e = (jnp.cumsum(has_rows.astype(i32)) - 1) & 1
    block_nexte = jnp.sum(jnp.where(onehot, next_e[None, :], 0), axis=1).astype(i32)
    block_epar = jnp.sum(jnp.where(onehot, parity_e[None, :], 0), axis=1).astype(i32)
    cnt_b = jnp.sum(jnp.where(onehot, counts[None, :], 0), axis=1)
    start_b = jnp.sum(jnp.where(onehot, starts[None, :], 0), axis=1)
    block_nv = jnp.where(in_use, jnp.clip(cnt_b - (blk_start - start_b), 0, tm), 0)
    dummy_keys = jnp.where(jnp.arange(tm, dtype=i32)[None, :] < (padded - counts)[:, None],
                           experts[:, None], N_EXPERTS).reshape(-1)
    keys = jnp.concatenate([eidx.reshape(-1), dummy_keys, jnp.full((tm,), N_EXPERTS, i32)])
    tgt = jnp.arange(a, dtype=i32)
    bits = (a - 1).bit_length() + 1
    low = (1 << bits) - 1
    payload = jnp.concatenate([tgt, jnp.full((n_blk * tm - a,), low, i32)])
    row_tgt = lax.sort(keys * (1 << bits) + payload) & low
    r = jnp.arange(tm, dtype=i32)[None, :]
    dump = a + (jnp.arange(n_blk, dtype=i32)[:, None] & 1) * tm + r
    row_tgt = jnp.where(r >= block_nv[:, None], dump, row_tgt.reshape(n_blk, tm)).reshape(-1)
    row_tgt = jnp.pad(row_tgt, (0, -(n_blk * tm) % IDX_CHUNK))
    return block_e.astype(i32), block_new, block_nexte, block_epar, n_used.astype(i32).reshape(1), row_tgt


def kernel(x, c, w_mod, b_mod, w_in, conv_w, attn_sinks, w_out, ln1_g, ln1_b, w_router, router_bias,
           w_gate, w_up, w_down, ws_gate, ws_up, ws_down, ln2_g, ln2_b):
    b, s, d = x.shape
    n = b * s
    attn_w = N_Q_HEADS * HEAD_DIM
    kv_w = N_KV_HEADS * HEAD_DIM
    conv_wd = d - attn_w
    in_w = attn_w + 2 * kv_w + 3 * conv_wd
    x2 = x.reshape(n, d)
    c8 = jnp.zeros((SUBLANES, d), F32).at[:b].set(c)
    for l in range(DEPTH):
        mod = _mod(c8, w_mod[l], b_mod[l].reshape(1, -1))[:b]
        mod3 = mod.reshape(b, 6, d)
        proj = _inproj(x2, mod3, w_in[l].astype(BF16), s)
        mix = _mixer(proj, attn_sinks[l].reshape(1, -1), conv_w[l], b, s, attn_w, kv_w, conv_wd)
        x1, h2d, logits = _outproj(mix, x2, mod3, w_out[l].astype(BF16), ln1_g[l].reshape(1, -1),
                                   ln1_b[l].reshape(1, -1), w_router[l], s)
        eidx, gate_w, tile_counts = _route(logits, router_bias[l].reshape(-1, 1))
        block_e, block_new, block_nexte, block_epar, n_used, row_tgt = _dispatch_tables(eidx, tile_counts, n)
        ys = _moe(block_e, block_new, block_nexte, block_epar, n_used, row_tgt, h2d,
                  w_gate[l], w_up[l], w_down[l])
        x2 = _final(gate_w.T.reshape(-1), ys, h2d, x1, mod3,
                    ws_gate[l].astype(BF16), ws_up[l].astype(BF16), ws_down[l].astype(BF16),
                    ln2_g[l].reshape(1, -1), ln2_b[l].reshape(1, -1), s)
    return x2.reshape(b, s, d)
```
